```python
import jax, jax.numpy as jnp
from jax import lax
import numpy as np

D_MODEL = 1024
BATCH = 8
SEQ = 8192
DEPTH = 2

D_FF = 2816
D_MIX = D_MODEL
POOL_CH = D_MIX // 2
POOL_WINDOWS = (2, 4, 8, 16)
POOL_GROUPS = len(POOL_WINDOWS)
POOL_GC = POOL_CH // POOL_GROUPS
CONV_CH = D_MIX - POOL_CH
CONV_WIDTH = 31
AB_IN = POOL_CH + 2 * CONV_CH
SGU_CH = D_MIX
SGU_HEADS = 8
SGU_HC = SGU_CH // SGU_HEADS
CHUNK = 128
N_EVEN = (DEPTH + 1) // 2
N_ODD = DEPTH // 2
EPS = 1e-6

kernel_name = "hybrid_pool_conv_sgu_macaron"


def rms_norm(x, g):
    xf = x.astype(jnp.float32)
    y = xf * lax.rsqrt(jnp.mean(xf * xf, axis=-1, keepdims=True) + EPS)
    return (y * g.astype(jnp.float32)).astype(x.dtype)


def layer_norm(x, g, b):
    xf = x.astype(jnp.float32)
    mu = jnp.mean(xf, axis=-1, keepdims=True)
    var = jnp.mean(jnp.square(xf - mu), axis=-1, keepdims=True)
    y = (xf - mu) * lax.rsqrt(var + EPS)
    return (y * g.astype(jnp.float32) + b.astype(jnp.float32)).astype(x.dtype)


def swiglu_ffn(x, w_in, w_out):
    gate, up = jnp.split(x @ w_in, 2, axis=-1)
    return (jax.nn.silu(gate) * up) @ w_out


def pool_mixer(u, w, b, scale):
    bsz, t_len, _ = u.shape
    csum = jnp.cumsum(u.astype(jnp.float32), axis=1)
    pos = jnp.arange(t_len, dtype=jnp.int32)
    means = []
    for g, win in enumerate(POOL_WINDOWS):
        cg = csum[..., g * POOL_GC:(g + 1) * POOL_GC]
        shifted = jnp.pad(cg[:, :t_len - win], ((0, 0), (win, 0), (0, 0)))
        count = jnp.minimum(pos + 1, win).astype(jnp.float32)[None, :, None]
        means.append((cg - shifted) / count)
    pooled = jnp.concatenate(means, axis=-1).astype(u.dtype) - u
    pooled = pooled.reshape(bsz, t_len, POOL_GROUPS, POOL_GC)
    mixed = jnp.einsum('btgc,gcd->btgd', pooled, w) + b
    return mixed.reshape(bsz, t_len, POOL_CH) * scale


def conv_module(h, conv_w, conv_b, ln_g, ln_b):
    a, gate = jnp.split(h, 2, axis=-1)
    g = a * jax.nn.sigmoid(gate)
    y = lax.conv_general_dilated(
        g, conv_w[:, None, :], window_strides=(1,), padding=((CONV_WIDTH - 1, 0),),
        dimension_numbers=('NWC', 'WIO', 'NWC'), feature_group_count=CONV_CH) + conv_b
    return jax.nn.silu(layer_norm(y, ln_g, ln_b))


def pool_conv_mixer(xn, w_in, pool_w, pool_b, pool_scale, conv_w, conv_b, ln_g, ln_b, w_out):
    h = xn @ w_in
    ya = pool_mixer(h[..., :POOL_CH], pool_w, pool_b, pool_scale)
    yb = conv_module(h[..., POOL_CH:], conv_w, conv_b, ln_g, ln_b)
    return jnp.concatenate([ya, yb], axis=-1) @ w_out


def sgu_mixer(xn, w_in, ln_g, ln_b, w_s, b_s, w_out):
    bsz, t_len, _ = xn.shape
    z = jax.nn.gelu(xn @ w_in, approximate=False)
    u, v = jnp.split(z, 2, axis=-1)
    v = layer_norm(v, ln_g, ln_b)
    v = v.reshape(bsz, t_len // CHUNK, CHUNK, SGU_HEADS, SGU_HC)
    mask = jnp.tril(jnp.ones((CHUNK, CHUNK), dtype=w_s.dtype))
    w_masked = w_s * mask
    v = jnp.einsum('hst,bnthc->bnshc', w_masked, v) + b_s.T[None, None, :, :, None]
    v = v.reshape(bsz, t_len, SGU_CH)
    return (u * v) @ w_out


def _fwd_setup_inputs(seed: int = 0) -> dict:
    key = jax.random.key(seed)
    ks = iter(jax.random.split(key, 32))

    def nrm(shape, scale):
        return jax.random.normal(next(ks), shape, jnp.float32) * scale

    def gain(shape):
        return 1.0 + nrm(shape, 0.05)

    return {
        "x": nrm((BATCH, SEQ, D_MODEL), 1.0),
        "ffn1_norm": gain((DEPTH, D_MODEL)),
        "ffn1_w_in": nrm((DEPTH, D_MODEL, 2 * D_FF), D_MODEL ** -0.5),
        "ffn1_w_out": nrm((DEPTH, D_FF, D_MODEL), D_FF ** -0.5),
        "mix_norm": gain((DEPTH, D_MODEL)),
        "ffn2_norm": gain((DEPTH, D_MODEL)),
        "ffn2_w_in": nrm((DEPTH, D_MODEL, 2 * D_FF), D_MODEL ** -0.5),
        "ffn2_w_out": nrm((DEPTH, D_FF, D_MODEL), D_FF ** -0.5),
        "ab_w_in": nrm((N_EVEN, D_MODEL, AB_IN), D_MODEL ** -0.5),
        "pool_w": nrm((N_EVEN, POOL_GROUPS, POOL_GC, POOL_GC), POOL_GC ** -0.5),
        "pool_b": nrm((N_EVEN, POOL_GROUPS, POOL_GC), 0.02),
        "pool_scale": 0.5 + nrm((N_EVEN, POOL_CH), 0.05),
        "conv_w": nrm((N_EVEN, CONV_WIDTH, CONV_CH), CONV_WIDTH ** -0.5),
        "conv_b": nrm((N_EVEN, CONV_CH), 0.02),
        "conv_ln_g": gain((N_EVEN, CONV_CH)),
        "conv_ln_b": nrm((N_EVEN, CONV_CH), 0.02),
        "ab_w_out": nrm((N_EVEN, D_MIX, D_MODEL), D_MIX ** -0.5),
        "sgu_w_in": nrm((N_ODD, D_MODEL, 2 * SGU_CH), D_MODEL ** -0.5),
        "sgu_ln_g": gain((N_ODD, SGU_CH)),
        "sgu_ln_b": nrm((N_ODD, SGU_CH), 0.02),
        "sgu_w": nrm((N_ODD, SGU_HEADS, CHUNK, CHUNK), CHUNK ** -0.5),
        "sgu_b": 1.0 + nrm((N_ODD, SGU_HEADS, CHUNK), 0.05),
        "sgu_w_out": nrm((N_ODD, SGU_CH, D_MODEL), SGU_CH ** -0.5),
        "final_norm": gain((D_MODEL,)),
    }


def _fwd_reference(x, ffn1_norm, ffn1_w_in, ffn1_w_out, mix_norm, ffn2_norm, ffn2_w_in, ffn2_w_out,
              ab_w_in, pool_w, pool_b, pool_scale, conv_w, conv_b, conv_ln_g, conv_ln_b, ab_w_out,
              sgu_w_in, sgu_ln_g, sgu_ln_b, sgu_w, sgu_b, sgu_w_out, final_norm):
    for i in range(DEPTH):
        x = x + 0.5 * swiglu_ffn(rms_norm(x, ffn1_norm[i]), ffn1_w_in[i], ffn1_w_out[i])
        xn = rms_norm(x, mix_norm[i])
        if i % 2 == 0:
            j = i // 2
            y = pool_conv_mixer(xn, ab_w_in[j], pool_w[j], pool_b[j], pool_scale[j],
                                conv_w[j], conv_b[j], conv_ln_g[j], conv_ln_b[j], ab_w_out[j])
        else:
            j = i // 2
            y = sgu_mixer(xn, sgu_w_in[j], sgu_ln_g[j], sgu_ln_b[j], sgu_w[j], sgu_b[j], sgu_w_out[j])
        x = x + y
        x = x + 0.5 * swiglu_ffn(rms_norm(x, ffn2_norm[i]), ffn2_w_in[i], ffn2_w_out[i])
    return rms_norm(x, final_norm)


import jax as _jax
import jax.numpy as _jnp

TWIN_FORMAT = 'train_step'
FWD_PARAMS = ['x', 'ffn1_norm', 'ffn1_w_in', 'ffn1_w_out', 'mix_norm', 'ffn2_norm', 'ffn2_w_in', 'ffn2_w_out', 'ab_w_in', 'pool_w', 'pool_b', 'pool_scale', 'conv_w', 'conv_b', 'conv_ln_g', 'conv_ln_b', 'ab_w_out', 'sgu_w_in', 'sgu_ln_g', 'sgu_ln_b', 'sgu_w', 'sgu_b', 'sgu_w_out', 'final_norm']
TWIN_WEIGHTS = ['ffn1_norm', 'ffn1_w_in', 'ffn1_w_out', 'mix_norm', 'ffn2_norm', 'ffn2_w_in', 'ffn2_w_out', 'ab_w_in', 'pool_w', 'pool_b', 'pool_scale', 'conv_w', 'conv_b', 'conv_ln_g', 'conv_ln_b', 'ab_w_out', 'sgu_w_in', 'sgu_ln_g', 'sgu_ln_b', 'sgu_w', 'sgu_b', 'sgu_w_out', 'final_norm']
TWIN_DIFF_INPUT = 'x'
TWIN_INPUTS = ['x', 'ffn1_norm', 'ffn1_w_in', 'ffn1_w_out', 'mix_norm', 'ffn2_norm', 'ffn2_w_in', 'ffn2_w_out', 'ab_w_in', 'pool_w', 'pool_b', 'pool_scale', 'conv_w', 'conv_b', 'conv_ln_g', 'conv_ln_b', 'ab_w_out', 'sgu_w_in', 'sgu_ln_g', 'sgu_ln_b', 'sgu_w', 'sgu_b', 'sgu_w_out', 'final_norm', 'loss_target', 'm_ffn1_norm', 'm_ffn1_w_in', 'm_ffn1_w_out', 'm_mix_norm', 'm_ffn2_norm', 'm_ffn2_w_in', 'm_ffn2_w_out', 'm_ab_w_in', 'm_pool_w', 'm_pool_b', 'm_pool_scale', 'm_conv_w', 'm_conv_b', 'm_conv_ln_g', 'm_conv_ln_b', 'm_ab_w_out', 'm_sgu_w_in', 'm_sgu_ln_g', 'm_sgu_ln_b', 'm_sgu_w', 'm_sgu_b', 'm_sgu_w_out', 'm_final_norm', 'v_ffn1_norm', 'v_ffn1_w_in', 'v_ffn1_w_out', 'v_mix_norm', 'v_ffn2_norm', 'v_ffn2_w_in', 'v_ffn2_w_out', 'v_ab_w_in', 'v_pool_w', 'v_pool_b', 'v_pool_scale', 'v_conv_w', 'v_conv_b', 'v_conv_ln_g', 'v_conv_ln_b', 'v_ab_w_out', 'v_sgu_w_in', 'v_sgu_ln_g', 'v_sgu_ln_b', 'v_sgu_w', 'v_sgu_b', 'v_sgu_w_out', 'v_final_norm']
TWIN_OUTPUTS = ['loss', 'grad_x', 'grad_ffn1_norm', 'grad_ffn1_w_in', 'grad_ffn1_w_out', 'grad_mix_norm', 'grad_ffn2_norm', 'grad_ffn2_w_in', 'grad_ffn2_w_out', 'grad_ab_w_in', 'grad_pool_w', 'grad_pool_b', 'grad_pool_scale', 'grad_conv_w', 'grad_conv_b', 'grad_conv_ln_g', 'grad_conv_ln_b', 'grad_ab_w_out', 'grad_sgu_w_in', 'grad_sgu_ln_g', 'grad_sgu_ln_b', 'grad_sgu_w', 'grad_sgu_b', 'grad_sgu_w_out', 'grad_final_norm', 'delta_ffn1_norm', 'delta_ffn1_w_in', 'delta_ffn1_w_out', 'delta_mix_norm', 'delta_ffn2_norm', 'delta_ffn2_w_in', 'delta_ffn2_w_out', 'delta_ab_w_in', 'delta_pool_w', 'delta_pool_b', 'delta_pool_scale', 'delta_conv_w', 'delta_conv_b', 'delta_conv_ln_g', 'delta_conv_ln_b', 'delta_ab_w_out', 'delta_sgu_w_in', 'delta_sgu_ln_g', 'delta_sgu_ln_b', 'delta_sgu_w', 'delta_sgu_b', 'delta_sgu_w_out', 'delta_final_norm', 'new_m_ffn1_norm', 'new_m_ffn1_w_in', 'new_m_ffn1_w_out', 'new_m_mix_norm', 'new_m_ffn2_norm', 'new_m_ffn2_w_in', 'new_m_ffn2_w_out', 'new_m_ab_w_in', 'new_m_pool_w', 'new_m_pool_b', 'new_m_pool_scale', 'new_m_conv_w', 'new_m_conv_b', 'new_m_conv_ln_g', 'new_m_conv_ln_b', 'new_m_ab_w_out', 'new_m_sgu_w_in', 'new_m_sgu_ln_g', 'new_m_sgu_ln_b', 'new_m_sgu_w', 'new_m_sgu_b', 'new_m_sgu_w_out', 'new_m_final_norm', 'new_v_ffn1_norm', 'new_v_ffn1_w_in', 'new_v_ffn1_w_out', 'new_v_mix_norm', 'new_v_ffn2_norm', 'new_v_ffn2_w_in', 'new_v_ffn2_w_out', 'new_v_ab_w_in', 'new_v_pool_w', 'new_v_pool_b', 'new_v_pool_scale', 'new_v_conv_w', 'new_v_conv_b', 'new_v_conv_ln_g', 'new_v_conv_ln_b', 'new_v_ab_w_out', 'new_v_sgu_w_in', 'new_v_sgu_ln_g', 'new_v_sgu_ln_b', 'new_v_sgu_w', 'new_v_sgu_b', 'new_v_sgu_w_out', 'new_v_final_norm']
TWIN_LEAF_KINDS = {'loss': 'loss', 'grad_x': 'grad_x', 'grad_ffn1_norm': 'grad_w', 'grad_ffn1_w_in': 'grad_w', 'grad_ffn1_w_out': 'grad_w', 'grad_mix_norm': 'grad_w', 'grad_ffn2_norm': 'grad_w', 'grad_ffn2_w_in': 'grad_w', 'grad_ffn2_w_out': 'grad_w', 'grad_ab_w_in': 'grad_w', 'grad_pool_w': 'grad_w', 'grad_pool_b': 'grad_w', 'grad_pool_scale': 'grad_w', 'grad_conv_w': 'grad_w', 'grad_conv_b': 'grad_w', 'grad_conv_ln_g': 'grad_w', 'grad_conv_ln_b': 'grad_w', 'grad_ab_w_out': 'grad_w', 'grad_sgu_w_in': 'grad_w', 'grad_sgu_ln_g': 'grad_w', 'grad_sgu_ln_b': 'grad_w', 'grad_sgu_w': 'grad_w', 'grad_sgu_b': 'grad_w', 'grad_sgu_w_out': 'grad_w', 'grad_final_norm': 'grad_w', 'delta_ffn1_norm': 'delta_w', 'delta_ffn1_w_in': 'delta_w', 'delta_ffn1_w_out': 'delta_w', 'delta_mix_norm': 'delta_w', 'delta_ffn2_norm': 'delta_w', 'delta_ffn2_w_in': 'delta_w', 'delta_ffn2_w_out': 'delta_w', 'delta_ab_w_in': 'delta_w', 'delta_pool_w': 'delta_w', 'delta_pool_b': 'delta_w', 'delta_pool_scale': 'delta_w', 'delta_conv_w': 'delta_w', 'delta_conv_b': 'delta_w', 'delta_conv_ln_g': 'delta_w', 'delta_conv_ln_b': 'delta_w', 'delta_ab_w_out': 'delta_w', 'delta_sgu_w_in': 'delta_w', 'delta_sgu_ln_g': 'delta_w', 'delta_sgu_ln_b': 'delta_w', 'delta_sgu_w': 'delta_w', 'delta_sgu_b': 'delta_w', 'delta_sgu_w_out': 'delta_w', 'delta_final_norm': 'delta_w', 'new_m_ffn1_norm': 'new_m', 'new_m_ffn1_w_in': 'new_m', 'new_m_ffn1_w_out': 'new_m', 'new_m_mix_norm': 'new_m', 'new_m_ffn2_norm': 'new_m', 'new_m_ffn2_w_in': 'new_m', 'new_m_ffn2_w_out': 'new_m', 'new_m_ab_w_in': 'new_m', 'new_m_pool_w': 'new_m', 'new_m_pool_b': 'new_m', 'new_m_pool_scale': 'new_m', 'new_m_conv_w': 'new_m', 'new_m_conv_b': 'new_m', 'new_m_conv_ln_g': 'new_m', 'new_m_conv_ln_b': 'new_m', 'new_m_ab_w_out': 'new_m', 'new_m_sgu_w_in': 'new_m', 'new_m_sgu_ln_g': 'new_m', 'new_m_sgu_ln_b': 'new_m', 'new_m_sgu_w': 'new_m', 'new_m_sgu_b': 'new_m', 'new_m_sgu_w_out': 'new_m', 'new_m_final_norm': 'new_m', 'new_v_ffn1_norm': 'new_v', 'new_v_ffn1_w_in': 'new_v', 'new_v_ffn1_w_out': 'new_v', 'new_v_mix_norm': 'new_v', 'new_v_ffn2_norm': 'new_v', 'new_v_ffn2_w_in': 'new_v', 'new_v_ffn2_w_out': 'new_v', 'new_v_ab_w_in': 'new_v', 'new_v_pool_w': 'new_v', 'new_v_pool_b': 'new_v', 'new_v_pool_scale': 'new_v', 'new_v_conv_w': 'new_v', 'new_v_conv_b': 'new_v', 'new_v_conv_ln_g': 'new_v', 'new_v_conv_ln_b': 'new_v', 'new_v_ab_w_out': 'new_v', 'new_v_sgu_w_in': 'new_v', 'new_v_sgu_ln_g': 'new_v', 'new_v_sgu_ln_b': 'new_v', 'new_v_sgu_w': 'new_v', 'new_v_sgu_b': 'new_v', 'new_v_sgu_w_out': 'new_v', 'new_v_final_norm': 'new_v'}


def _forward(args):
    return _fwd_reference(*[args[k] for k in FWD_PARAMS])


def _output_shape():
    def fwd():
        inp = _fwd_setup_inputs(0)
        return _fwd_reference(*[inp[k] for k in FWD_PARAMS])
    out = _jax.eval_shape(fwd)
    return out.shape, out.dtype

N_MICROBATCH = 1
ADAM_LR = 0.001
ADAM_B1 = 0.9
ADAM_B2 = 0.999
ADAM_EPS = 1e-08
ADAM_WD = 0.01
ADAM_STEP = 10
PER_EXAMPLE_BATCH_AXIS = {'x': 0, 'loss_target': 0}
SHARED_INPUTS = []
_WEIGHT_DTYPES = {'ffn1_norm': _jnp.float32, 'ffn1_w_in': _jnp.float32, 'ffn1_w_out': _jnp.float32, 'mix_norm': _jnp.float32, 'ffn2_norm': _jnp.float32, 'ffn2_w_in': _jnp.float32, 'ffn2_w_out': _jnp.float32, 'ab_w_in': _jnp.float32, 'pool_w': _jnp.float32, 'pool_b': _jnp.float32, 'pool_scale': _jnp.float32, 'conv_w': _jnp.float32, 'conv_b': _jnp.float32, 'conv_ln_g': _jnp.float32, 'conv_ln_b': _jnp.float32, 'ab_w_out': _jnp.float32, 'sgu_w_in': _jnp.float32, 'sgu_ln_g': _jnp.float32, 'sgu_ln_b': _jnp.float32, 'sgu_w': _jnp.float32, 'sgu_b': _jnp.float32, 'sgu_w_out': _jnp.float32, 'final_norm': _jnp.float32}
MOMENT_SCALE = {'ffn1_norm': 1.160601e-01, 'ffn1_w_in': 4.706569e-02, 'ffn1_w_out': 7.703429e-02, 'mix_norm': 1.667206e-01, 'ffn2_norm': 9.420494e-02, 'ffn2_w_in': 3.801635e-02, 'ffn2_w_out': 6.280643e-02, 'ab_w_in': 1.175868e-01, 'pool_w': 1.140470e-01, 'pool_b': 4.047458e-01, 'pool_scale': 2.339624e-01, 'conv_w': 1.615965e-01, 'conv_b': 6.030882e-01, 'conv_ln_g': 2.779037e-01, 'conv_ln_b': 3.782648e-01, 'ab_w_out': 1.603831e-01, 'sgu_w_in': 1.254702e-01, 'sgu_ln_g': 8.269843e-02, 'sgu_ln_b': 8.407278e-02, 'sgu_w': 7.962839e-02, 'sgu_b': 1.144002e-01, 'sgu_w_out': 2.181644e-01, 'final_norm': 6.416450e+01}


def _to_microbatches(a, axis):
    t = _jnp.moveaxis(a, axis, 0)
    t = t.reshape((N_MICROBATCH, t.shape[0] // N_MICROBATCH) + t.shape[1:])
    return _jnp.moveaxis(t, 1, axis + 1)


def setup_inputs(seed: int = 0) -> dict:
    inp = _fwd_setup_inputs(seed)
    key = _jax.random.fold_in(_jax.random.key(seed), 7919)
    shape, _ = _output_shape()
    out = dict(inp)
    out["loss_target"] = _jax.random.normal(_jax.random.fold_in(key, 0), shape, _jnp.float32)
    for i, name in enumerate(TWIN_WEIGHTS):
        w = inp[name].astype(_jnp.float32)
        if MOMENT_SCALE is None:
            s = _jnp.sqrt(_jnp.mean(_jnp.square(w)) + 1e-30)
        else:
            s = MOMENT_SCALE[name]
        km, kv = _jax.random.split(_jax.random.fold_in(key, i + 1))
        out[name] = w
        out["m_" + name] = s * _jax.random.normal(km, w.shape, _jnp.float32)
        out["v_" + name] = (s * s) * _jax.random.uniform(kv, w.shape, _jnp.float32, 0.5, 1.5)
    if N_MICROBATCH > 1:
        for name, axis in PER_EXAMPLE_BATCH_AXIS.items():
            out[name] = _to_microbatches(out[name], axis)
    return {'x': out['x'], 'ffn1_norm': out['ffn1_norm'], 'ffn1_w_in': out['ffn1_w_in'], 'ffn1_w_out': out['ffn1_w_out'], 'mix_norm': out['mix_norm'], 'ffn2_norm': out['ffn2_norm'], 'ffn2_w_in': out['ffn2_w_in'], 'ffn2_w_out': out['ffn2_w_out'], 'ab_w_in': out['ab_w_in'], 'pool_w': out['pool_w'], 'pool_b': out['pool_b'], 'pool_scale': out['pool_scale'], 'conv_w': out['conv_w'], 'conv_b': out['conv_b'], 'conv_ln_g': out['conv_ln_g'], 'conv_ln_b': out['conv_ln_b'], 'ab_w_out': out['ab_w_out'], 'sgu_w_in': out['sgu_w_in'], 'sgu_ln_g': out['sgu_ln_g'], 'sgu_ln_b': out['sgu_ln_b'], 'sgu_w': out['sgu_w'], 'sgu_b': out['sgu_b'], 'sgu_w_out': out['sgu_w_out'], 'final_norm': out['final_norm'], 'loss_target': out['loss_target'], 'm_ffn1_norm': out['m_ffn1_norm'], 'm_ffn1_w_in': out['m_ffn1_w_in'], 'm_ffn1_w_out': out['m_ffn1_w_out'], 'm_mix_norm': out['m_mix_norm'], 'm_ffn2_norm': out['m_ffn2_norm'], 'm_ffn2_w_in': out['m_ffn2_w_in'], 'm_ffn2_w_out': out['m_ffn2_w_out'], 'm_ab_w_in': out['m_ab_w_in'], 'm_pool_w': out['m_pool_w'], 'm_pool_b': out['m_pool_b'], 'm_pool_scale': out['m_pool_scale'], 'm_conv_w': out['m_conv_w'], 'm_conv_b': out['m_conv_b'], 'm_conv_ln_g': out['m_conv_ln_g'], 'm_conv_ln_b': out['m_conv_ln_b'], 'm_ab_w_out': out['m_ab_w_out'], 'm_sgu_w_in': out['m_sgu_w_in'], 'm_sgu_ln_g': out['m_sgu_ln_g'], 'm_sgu_ln_b': out['m_sgu_ln_b'], 'm_sgu_w': out['m_sgu_w'], 'm_sgu_b': out['m_sgu_b'], 'm_sgu_w_out': out['m_sgu_w_out'], 'm_final_norm': out['m_final_norm'], 'v_ffn1_norm': out['v_ffn1_norm'], 'v_ffn1_w_in': out['v_ffn1_w_in'], 'v_ffn1_w_out': out['v_ffn1_w_out'], 'v_mix_norm': out['v_mix_norm'], 'v_ffn2_norm': out['v_ffn2_norm'], 'v_ffn2_w_in': out['v_ffn2_w_in'], 'v_ffn2_w_out': out['v_ffn2_w_out'], 'v_ab_w_in': out['v_ab_w_in'], 'v_pool_w': out['v_pool_w'], 'v_pool_b': out['v_pool_b'], 'v_pool_scale': out['v_pool_scale'], 'v_conv_w': out['v_conv_w'], 'v_conv_b': out['v_conv_b'], 'v_conv_ln_g': out['v_conv_ln_g'], 'v_conv_ln_b': out['v_conv_ln_b'], 'v_ab_w_out': out['v_ab_w_out'], 'v_sgu_w_in': out['v_sgu_w_in'], 'v_sgu_ln_g': out['v_sgu_ln_g'], 'v_sgu_ln_b': out['v_sgu_ln_b'], 'v_sgu_w': out['v_sgu_w'], 'v_sgu_b': out['v_sgu_b'], 'v_sgu_w_out': out['v_sgu_w_out'], 'v_final_norm': out['v_final_norm']}


def _loss(weights, diff, rest, loss_target):
    with _jax.named_scope("forward"):
        args = {**rest, TWIN_DIFF_INPUT: diff, **{k: w.astype(_WEIGHT_DTYPES[k]) for k, w in weights.items()}}
        y = _forward(args)
    with _jax.named_scope("loss_head"):
        err = _jnp.square(y.astype(_jnp.float32) - loss_target)
        return 0.5 * _jnp.sum(_jnp.mean(err, axis=-1)) if err.ndim else 0.5 * err


def _adamw(w, g, m, v):
    m = ADAM_B1 * m + (1.0 - ADAM_B1) * g
    v = ADAM_B2 * v + (1.0 - ADAM_B2) * _jnp.square(g)
    m_hat = m / (1.0 - ADAM_B1 ** ADAM_STEP)
    v_hat = v / (1.0 - ADAM_B2 ** ADAM_STEP)
    delta = -ADAM_LR * (m_hat / (_jnp.sqrt(v_hat) + ADAM_EPS) + ADAM_WD * w)
    return delta, m, v


def reference(x, ffn1_norm, ffn1_w_in, ffn1_w_out, mix_norm, ffn2_norm, ffn2_w_in, ffn2_w_out, ab_w_in, pool_w, pool_b, pool_scale, conv_w, conv_b, conv_ln_g, conv_ln_b, ab_w_out, sgu_w_in, sgu_ln_g, sgu_ln_b, sgu_w, sgu_b, sgu_w_out, final_norm, loss_target, m_ffn1_norm, m_ffn1_w_in, m_ffn1_w_out, m_mix_norm, m_ffn2_norm, m_ffn2_w_in, m_ffn2_w_out, m_ab_w_in, m_pool_w, m_pool_b, m_pool_scale, m_conv_w, m_conv_b, m_conv_ln_g, m_conv_ln_b, m_ab_w_out, m_sgu_w_in, m_sgu_ln_g, m_sgu_ln_b, m_sgu_w, m_sgu_b, m_sgu_w_out, m_final_norm, v_ffn1_norm, v_ffn1_w_in, v_ffn1_w_out, v_mix_norm, v_ffn2_norm, v_ffn2_w_in, v_ffn2_w_out, v_ab_w_in, v_pool_w, v_pool_b, v_pool_scale, v_conv_w, v_conv_b, v_conv_ln_g, v_conv_ln_b, v_ab_w_out, v_sgu_w_in, v_sgu_ln_g, v_sgu_ln_b, v_sgu_w, v_sgu_b, v_sgu_w_out, v_final_norm):
    given = dict(x=x, ffn1_norm=ffn1_norm, ffn1_w_in=ffn1_w_in, ffn1_w_out=ffn1_w_out, mix_norm=mix_norm, ffn2_norm=ffn2_norm, ffn2_w_in=ffn2_w_in, ffn2_w_out=ffn2_w_out, ab_w_in=ab_w_in, pool_w=pool_w, pool_b=pool_b, pool_scale=pool_scale, conv_w=conv_w, conv_b=conv_b, conv_ln_g=conv_ln_g, conv_ln_b=conv_ln_b, ab_w_out=ab_w_out, sgu_w_in=sgu_w_in, sgu_ln_g=sgu_ln_g, sgu_ln_b=sgu_ln_b, sgu_w=sgu_w, sgu_b=sgu_b, sgu_w_out=sgu_w_out, final_norm=final_norm, loss_target=loss_target, m_ffn1_norm=m_ffn1_norm, m_ffn1_w_in=m_ffn1_w_in, m_ffn1_w_out=m_ffn1_w_out, m_mix_norm=m_mix_norm, m_ffn2_norm=m_ffn2_norm, m_ffn2_w_in=m_ffn2_w_in, m_ffn2_w_out=m_ffn2_w_out, m_ab_w_in=m_ab_w_in, m_pool_w=m_pool_w, m_pool_b=m_pool_b, m_pool_scale=m_pool_scale, m_conv_w=m_conv_w, m_conv_b=m_conv_b, m_conv_ln_g=m_conv_ln_g, m_conv_ln_b=m_conv_ln_b, m_ab_w_out=m_ab_w_out, m_sgu_w_in=m_sgu_w_in, m_sgu_ln_g=m_sgu_ln_g, m_sgu_ln_b=m_sgu_ln_b, m_sgu_w=m_sgu_w, m_sgu_b=m_sgu_b, m_sgu_w_out=m_sgu_w_out, m_final_norm=m_final_norm, v_ffn1_norm=v_ffn1_norm, v_ffn1_w_in=v_ffn1_w_in, v_ffn1_w_out=v_ffn1_w_out, v_mix_norm=v_mix_norm, v_ffn2_norm=v_ffn2_norm, v_ffn2_w_in=v_ffn2_w_in, v_ffn2_w_out=v_ffn2_w_out, v_ab_w_in=v_ab_w_in, v_pool_w=v_pool_w, v_pool_b=v_pool_b, v_pool_scale=v_pool_scale, v_conv_w=v_conv_w, v_conv_b=v_conv_b, v_conv_ln_g=v_conv_ln_g, v_conv_ln_b=v_conv_ln_b, v_ab_w_out=v_ab_w_out, v_sgu_w_in=v_sgu_w_in, v_sgu_ln_g=v_sgu_ln_g, v_sgu_ln_b=v_sgu_ln_b, v_sgu_w=v_sgu_w, v_sgu_b=v_sgu_b, v_sgu_w_out=v_sgu_w_out, v_final_norm=v_final_norm)
    weights = {n: given[n] for n in TWIN_WEIGHTS}
    shared = {n: given[n] for n in SHARED_INPUTS}
    per_example = {n: given[n] for n in ['x']}
    grad_fn = _jax.value_and_grad(_loss, argnums=(0, 1))

    def one_microbatch(ex, loss_target):
        ex = dict(ex)
        diff = ex.pop(TWIN_DIFF_INPUT)
        return grad_fn(weights, diff, {**shared, **ex}, loss_target)

    if N_MICROBATCH == 1:
        loss, (grad_w, grad_x) = one_microbatch(per_example, given["loss_target"])
    else:
        def body(carry, xs):
            loss_sum, grad_sum = carry
            l_k, (gw_k, gx_k) = one_microbatch(xs[0], xs[1])
            with _jax.named_scope("update"):
                return (loss_sum + l_k, _jax.tree.map(_jnp.add, grad_sum, gw_k)), gx_k

        init = (_jnp.zeros((), _jnp.float32), _jax.tree.map(_jnp.zeros_like, weights))
        (loss, grad_w), grad_x = _jax.lax.scan(body, init, (per_example, given["loss_target"]))
    with _jax.named_scope("update"):
        delta_w, new_m, new_v = {}, {}, {}
        for n in TWIN_WEIGHTS:
            delta_w[n], new_m[n], new_v[n] = _adamw(weights[n], grad_w[n], given["m_" + n], given["v_" + n])
    return (loss, grad_x, *[grad_w[n] for n in TWIN_WEIGHTS], *[delta_w[n] for n in TWIN_WEIGHTS],
            *[new_m[n] for n in TWIN_WEIGHTS], *[new_v[n] for n in TWIN_WEIGHTS])
```

```python
import functools

import jax
import jax.numpy as jnp
from jax import lax
from jax.experimental import pallas as pl
from jax.experimental.pallas import tpu as pltpu

F32 = jnp.float32
BF16 = jnp.bfloat16
EPS = 1e-6
N_DEV = 8
POOL_WINDOWS = (2, 4, 8, 16)
CONV_WIDTH = 31
HALO = 32
GROUP = 128
TOKEN_TILE = 512
ADAM_LR, ADAM_B1, ADAM_B2, ADAM_EPS, ADAM_WD, ADAM_STEP = 0.001, 0.9, 0.999, 1e-08, 0.01, 10
VMEM_LIMIT = 56 * 1024 * 1024

NT = (((1,), (1,)), ((), ()))
TN = (((0,), (0,)), ((), ()))


def _call(body, **kw):
    return pl.pallas_call(body, **kw)


def _params():
    return pltpu.CompilerParams(vmem_limit_bytes=VMEM_LIMIT)


def _sigmoid(x):
    return 1.0 / (1.0 + jnp.exp(-x))


def _tile(t):
    return min(TOKEN_TILE, t)


def _ffn_fwd(name, x, gamma, wgu, wout, layer):
    t, d = x.shape
    fb = wgu.shape[-1]
    nk = wgu.shape[1]
    tm = _tile(t)

    def body(x_ref, gam_ref, wgu_ref, wo_ref, xo_ref, xn_ref, gu_ref, xn_s, acc):
        k = pl.program_id(1)

        @pl.when(k == 0)
        def _():
            xv = x_ref[...]
            r = lax.rsqrt(jnp.mean(xv * xv, axis=-1, keepdims=True) + EPS)
            xn = (xv * r * gam_ref[...]).astype(BF16)
            xn_s[...] = xn
            xn_ref[...] = xn
            acc[...] = jnp.zeros_like(acc)

        xn = xn_s[...]
        g = jnp.dot(xn, wgu_ref[0], preferred_element_type=F32)
        u = jnp.dot(xn, wgu_ref[1], preferred_element_type=F32)
        gu_ref[0] = g.astype(BF16)
        gu_ref[1] = u.astype(BF16)
        h = (g * _sigmoid(g) * u).astype(BF16)
        acc[...] += jnp.dot(h, wo_ref[...].reshape(fb, d), preferred_element_type=F32)

        @pl.when(k == nk - 1)
        def _():
            xo_ref[...] = x_ref[...] + 0.5 * acc[...]

    return _call(
        body, name=name, grid=(t // tm, nk),
        in_specs=[
            pl.BlockSpec((tm, d), lambda i, k: (i, 0)),
            pl.BlockSpec((1, d), lambda i, k: (0, 0)),
            pl.BlockSpec((2, None, None, d, fb), lambda i, k: (0, k, layer, 0, 0)),
            pl.BlockSpec((2, None, fb // 2, d), lambda i, k: (k, layer, 0, 0)),
        ],
        out_specs=[
            pl.BlockSpec((tm, d), lambda i, k: (i, 0)),
            pl.BlockSpec((tm, d), lambda i, k: (i, 0)),
            pl.BlockSpec((2, None, tm, fb), lambda i, k: (0, k, i, 0)),
        ],
        out_shape=[
            jax.ShapeDtypeStruct((t, d), F32),
            jax.ShapeDtypeStruct((t, d), BF16),
            jax.ShapeDtypeStruct((2, nk, t, fb), BF16),
        ],
        scratch_shapes=[pltpu.VMEM((tm, d), BF16), pltpu.VMEM((tm, d), F32)],
        compiler_params=_params(),
    )(x, gamma, wgu, wout)


def _ffn_bwd(name, dy, x, gamma, gu, wgu, wout, layer):
    t, d = x.shape
    fb = wgu.shape[-1]
    nk = wgu.shape[1]
    tm = _tile(t)

    def body(dy_ref, x_ref, gam_ref, gu_ref, wgu_ref, wo_ref,
             dx_ref, dgam_ref, dyh_ref, h_ref, dgu_ref, dyb_s, acc):
        i = pl.program_id(0)
        k = pl.program_id(1)

        @pl.when(k == 0)
        def _():
            dyb = (0.5 * dy_ref[...]).astype(BF16)
            dyb_s[...] = dyb
            dyh_ref[...] = dyb
            acc[...] = jnp.zeros_like(acc)

        @pl.when((i == 0) & (k == 0))
        def _():
            dgam_ref[...] = jnp.zeros_like(dgam_ref)

        dyb = dyb_s[...]
        dh = lax.dot_general(dyb, wo_ref[...].reshape(fb, d), NT, preferred_element_type=F32)
        g = gu_ref[0].astype(F32)
        u = gu_ref[1].astype(F32)
        sig = _sigmoid(g)
        silu = g * sig
        h_ref[...] = (silu * u).astype(BF16)
        dg = (dh * u * (sig * (1.0 + g * (1.0 - sig)))).astype(BF16)
        du = (dh * silu).astype(BF16)
        dgu_ref[0] = dg
        dgu_ref[1] = du
        acc[...] += (lax.dot_general(dg, wgu_ref[0], NT, preferred_element_type=F32)
                     + lax.dot_general(du, wgu_ref[1], NT, preferred_element_type=F32))

        @pl.when(k == nk - 1)
        def _():
            xv = x_ref[...]
            r = lax.rsqrt(jnp.mean(xv * xv, axis=-1, keepdims=True) + EPS)
            yv = xv * r
            dxn = acc[...]
            dgam_ref[...] += jnp.sum(dxn * yv, axis=0, keepdims=True)
            dyn = dxn * gam_ref[...]
            dx_ref[...] = dy_ref[...] + r * (dyn - yv * jnp.mean(dyn * yv, axis=-1, keepdims=True))

    return _call(
        body, name=name, grid=(t // tm, nk),
        in_specs=[
            pl.BlockSpec((tm, d), lambda i, k: (i, 0)),
            pl.BlockSpec((tm, d), lambda i, k: (i, 0)),
            pl.BlockSpec((1, d), lambda i, k: (0, 0)),
            pl.BlockSpec((2, None, tm, fb), lambda i, k: (0, k, i, 0)),
            pl.BlockSpec((2, None, None, d, fb), lambda i, k: (0, k, layer, 0, 0)),
            pl.BlockSpec((2, None, fb // 2, d), lambda i, k: (k, layer, 0, 0)),
        ],
        out_specs=[
            pl.BlockSpec((tm, d), lambda i, k: (i, 0)),
            pl.BlockSpec((1, d), lambda i, k: (0, 0)),
            pl.BlockSpec((tm, d), lambda i, k: (i, 0)),
            pl.BlockSpec((None, tm, fb), lambda i, k: (k, i, 0)),
            pl.BlockSpec((2, None, tm, fb), lambda i, k: (0, k, i, 0)),
        ],
        out_shape=[
            jax.ShapeDtypeStruct((t, d), F32),
            jax.ShapeDtypeStruct((1, d), F32),
            jax.ShapeDtypeStruct((t, d), BF16),
            jax.ShapeDtypeStruct((nk, t, fb), BF16),
            jax.ShapeDtypeStruct((2, nk, t, fb), BF16),
        ],
        scratch_shapes=[pltpu.VMEM((tm, d), BF16), pltpu.VMEM((tm, d), F32)],
        compiler_params=_params(),
    )(dy, x, gamma, gu, wgu, wout)


def _matmul_tn(name, a, b, out_dtype=BF16):
    a_b = a.ndim == 3
    b_b = b.ndim == 3
    nb = a.shape[0] if a_b else b.shape[0] if b_b else 1
    t, m = a.shape[-2:]
    n = b.shape[-1]
    tk = _tile(t)
    nt = t // tk

    def body(a_ref, b_ref, o_ref, acc):
        s = pl.program_id(1)

        @pl.when(s == 0)
        def _():
            acc[...] = jnp.zeros_like(acc)

        acc[...] += lax.dot_general(a_ref[...], b_ref[...], TN, preferred_element_type=F32)

        @pl.when(s == nt - 1)
        def _():
            o_ref[...] = acc[...].astype(o_ref.dtype)

    a_spec = (pl.BlockSpec((None, tk, m), lambda j, s: (j, s, 0)) if a_b
              else pl.BlockSpec((tk, m), lambda j, s: (s, 0)))
    b_spec = (pl.BlockSpec((None, tk, n), lambda j, s: (j, s, 0)) if b_b
              else pl.BlockSpec((tk, n), lambda j, s: (s, 0)))
    return _call(
        body, name=name, grid=(nb, nt),
        in_specs=[a_spec, b_spec],
        out_specs=pl.BlockSpec((None, m, n), lambda j, s: (j, 0, 0)),
        out_shape=jax.ShapeDtypeStruct((nb, m, n), out_dtype),
        scratch_shapes=[pltpu.VMEM((m, n), F32)],
        compiler_params=_params(),
    )(a, b)


def _rms_matmul(name, x, gamma, w):
    t, d = x.shape
    n = w.shape[1]
    tm = _tile(t)

    def body(x_ref, gam_ref, w_ref, xn_ref, h_ref):
        xv = x_ref[...]
        r = lax.rsqrt(jnp.mean(xv * xv, axis=-1, keepdims=True) + EPS)
        xn = (xv * r * gam_ref[...]).astype(BF16)
        xn_ref[...] = xn
        h_ref[...] = jnp.dot(xn, w_ref[...], preferred_element_type=F32)

    return _call(
        body, name=name, grid=(t // tm,),
        in_specs=[pl.BlockSpec((tm, d), lambda i: (i, 0)),
                  pl.BlockSpec((1, d), lambda i: (0, 0)),
                  pl.BlockSpec((d, n), lambda i: (0, 0))],
        out_specs=[pl.BlockSpec((tm, d), lambda i: (i, 0)),
                   pl.BlockSpec((tm, n), lambda i: (i, 0))],
        out_shape=[jax.ShapeDtypeStruct((t, d), BF16), jax.ShapeDtypeStruct((t, n), F32)],
        compiler_params=_params(),
    )(x, gamma, w)


def _matmul_residual(name, a, w, res):
    t, kdim = a.shape
    n = w.shape[1]
    tm = _tile(t)

    def body(a_ref, w_ref, r_ref, o_ref):
        o_ref[...] = r_ref[...] + jnp.dot(a_ref[...], w_ref[...], preferred_element_type=F32)

    return _call(
        body, name=name, grid=(t // tm,),
        in_specs=[pl.BlockSpec((tm, kdim), lambda i: (i, 0)),
                  pl.BlockSpec((kdim, n), lambda i: (0, 0)),
                  pl.BlockSpec((tm, n), lambda i: (i, 0))],
        out_specs=pl.BlockSpec((tm, n), lambda i: (i, 0)),
        out_shape=jax.ShapeDtypeStruct((t, n), F32),
        compiler_params=_params(),
    )(a, w, res)


def _matmul_nt(name, dy, w):
    t, n = dy.shape
    kdim = w.shape[0]
    tm = _tile(t)

    def body(dy_ref, w_ref, da_ref, dyb_ref):
        dyb = dy_ref[...].astype(BF16)
        dyb_ref[...] = dyb
        da_ref[...] = lax.dot_general(dyb, w_ref[...], NT, preferred_element_type=F32)

    return _call(
        body, name=name, grid=(t // tm,),
        in_specs=[pl.BlockSpec((tm, n), lambda i: (i, 0)),
                  pl.BlockSpec((kdim, n), lambda i: (0, 0))],
        out_specs=[pl.BlockSpec((tm, kdim), lambda i: (i, 0)),
                   pl.BlockSpec((tm, n), lambda i: (i, 0))],
        out_shape=[jax.ShapeDtypeStruct((t, kdim), F32), jax.ShapeDtypeStruct((t, n), BF16)],
        compiler_params=_params(),
    )(dy, w)


def _matmul_nt_rms_bwd(name, dz, w, dres, x, gamma):
    t, kdim = dz.shape
    d = w.shape[0]
    tm = _tile(t)

    def body(dz_ref, w_ref, dres_ref, x_ref, gam_ref, dx_ref, dgam_ref):
        i = pl.program_id(0)

        @pl.when(i == 0)
        def _():
            dgam_ref[...] = jnp.zeros_like(dgam_ref)

        dxn = lax.dot_general(dz_ref[...], w_ref[...], NT, preferred_element_type=F32)
        xv = x_ref[...]
        r = lax.rsqrt(jnp.mean(xv * xv, axis=-1, keepdims=True) + EPS)
        yv = xv * r
        dgam_ref[...] += jnp.sum(dxn * yv, axis=0, keepdims=True)
        dyn = dxn * gam_ref[...]
        dx_ref[...] = dres_ref[...] + r * (dyn - yv * jnp.mean(dyn * yv, axis=-1, keepdims=True))

    return _call(
        body, name=name, grid=(t // tm,),
        in_specs=[pl.BlockSpec((tm, kdim), lambda i: (i, 0)),
                  pl.BlockSpec((d, kdim), lambda i: (0, 0)),
                  pl.BlockSpec((tm, d), lambda i: (i, 0)),
                  pl.BlockSpec((tm, d), lambda i: (i, 0)),
                  pl.BlockSpec((1, d), lambda i: (0, 0))],
        out_specs=[pl.BlockSpec((tm, d), lambda i: (i, 0)),
                   pl.BlockSpec((1, d), lambda i: (0, 0))],
        out_shape=[jax.ShapeDtypeStruct((t, d), F32), jax.ShapeDtypeStruct((1, d), F32)],
        compiler_params=_params(),
    )(dz, w, dres, x, gamma)


def _pool_means(uext_ref, pos, tm, g, win):
    cols = slice(g * GROUP, (g + 1) * GROUP)
    acc = uext_ref[pl.ds(HALO, tm), cols]
    for j in range(1, win):
        acc = acc + uext_ref[pl.ds(HALO - j, tm), cols]
    cnt = jnp.minimum(pos + 1, win).astype(F32)
    return acc / cnt - uext_ref[pl.ds(HALO, tm), cols]


def _conv_taps(gext_ref, cw_ref, cb_ref, start, rows):
    y = cb_ref[...] + cw_ref[0:1, :] * gext_ref[pl.ds(start, rows), :]
    for k in range(1, CONV_WIDTH):
        y = y + cw_ref[k:k + 1, :] * gext_ref[pl.ds(start + k, rows), :]
    return y


def _pool_conv_fwd(name, h, pool_w, pool_b, pool_scale, conv_w, conv_b, ln_g, ln_b):
    t, hw = h.shape
    pc = len(POOL_WINDOWS) * GROUP
    cc = (hw - pc) // 2
    tm = _tile(t)
    per = tm // HALO

    def body(h_ref, hp_ref, pw_ref, pb_ref, ps_ref, cw_ref, cb_ref, lg_ref, lb_ref, cat_ref, uext, gext):
        i = pl.program_id(0)
        keep = (i > 0).astype(F32)
        hp = hp_ref[...] * keep
        uext[0:HALO, :] = hp[:, :pc]
        uext[HALO:, :] = h_ref[:, :pc]
        gext[0:HALO, :] = hp[:, pc:pc + cc] * _sigmoid(hp[:, pc + cc:])
        gext[HALO:, :] = h_ref[:, pc:pc + cc] * _sigmoid(h_ref[:, pc + cc:])
        pos = i * tm + lax.broadcasted_iota(jnp.int32, (tm, 1), 0)
        for g, win in enumerate(POOL_WINDOWS):
            cols = slice(g * GROUP, (g + 1) * GROUP)
            pooled = _pool_means(uext, pos, tm, g, win)
            mixed = jnp.dot(pooled.astype(BF16), pw_ref[g].astype(BF16),
                            preferred_element_type=F32) + pb_ref[g:g + 1, :]
            cat_ref[:, cols] = (mixed * ps_ref[:, cols]).astype(BF16)
        y = _conv_taps(gext, cw_ref, cb_ref, HALO - (CONV_WIDTH - 1), tm)
        mu = jnp.mean(y, axis=-1, keepdims=True)
        dv = y - mu
        rstd = lax.rsqrt(jnp.mean(dv * dv, axis=-1, keepdims=True) + EPS)
        ln = dv * rstd * lg_ref[...] + lb_ref[...]
        cat_ref[:, pc:] = (ln * _sigmoid(ln)).astype(BF16)

    small = lambda a: pl.BlockSpec(a.shape, lambda i: (0,) * a.ndim)
    return _call(
        body, name=name, grid=(t // tm,),
        in_specs=[pl.BlockSpec((tm, hw), lambda i: (i, 0)),
                  pl.BlockSpec((HALO, hw), lambda i: (jnp.maximum(i * per - 1, 0), 0)),
                  small(pool_w), small(pool_b), small(pool_scale), small(conv_w), small(conv_b),
                  small(ln_g), small(ln_b)],
        out_specs=pl.BlockSpec((tm, pc + cc), lambda i: (i, 0)),
        out_shape=jax.ShapeDtypeStruct((t, pc + cc), BF16),
        scratch_shapes=[pltpu.VMEM((HALO + tm, pc), F32), pltpu.VMEM((HALO + tm, cc), F32)],
        compiler_params=_params(),
    )(h, h, pool_w, pool_b, pool_scale, conv_w, conv_b, ln_g, ln_b)


def _pool_conv_bwd(name, h, dcat, pool_w, pool_b, pool_scale, conv_w, conv_b, ln_g, ln_b):
    t, hw = h.shape
    pc = len(POOL_WINDOWS) * GROUP
    cc = (hw - pc) // 2
    ng = len(POOL_WINDOWS)
    tm = _tile(t)
    per = tm // HALO
    nt = t // tm
    r2 = tm + HALO
    taps = CONV_WIDTH - 1

    def body(h_ref, hp_ref, hn_ref, dc_ref, dcn_ref, pw_ref, pb_ref, ps_ref, cw_ref, cb_ref, lg_ref, lb_ref,
             dh_ref, dpw_ref, dpb_ref, dps_ref, dcw_ref, dcb_ref, dlg_ref, dlb_ref,
             uext, gext, dcext, dqext, dycext):
        i = pl.program_id(0)

        @pl.when(i == 0)
        def _():
            for ref in (dpw_ref, dpb_ref, dps_ref, dcw_ref, dcb_ref, dlg_ref, dlb_ref):
                ref[...] = jnp.zeros_like(ref)

        keep_p = (i > 0).astype(F32)
        keep_n = (i < nt - 1).astype(F32)
        hp = hp_ref[...] * keep_p
        hn = hn_ref[...] * keep_n
        uext[0:HALO, :] = hp[:, :pc]
        uext[HALO:, :] = h_ref[:, :pc]
        gext[0:HALO, :] = hp[:, pc:pc + cc] * _sigmoid(hp[:, pc + cc:])
        gext[pl.ds(HALO, tm), :] = h_ref[:, pc:pc + cc] * _sigmoid(h_ref[:, pc + cc:])
        gext[pl.ds(HALO + tm, HALO), :] = hn[:, pc:pc + cc] * _sigmoid(hn[:, pc + cc:])
        dcext[0:tm, :] = dc_ref[...]
        dcext[pl.ds(tm, HALO), :] = dcn_ref[...] * keep_n

        pos = i * tm + lax.broadcasted_iota(jnp.int32, (tm, 1), 0)
        pos2 = i * tm + lax.broadcasted_iota(jnp.int32, (r2, 1), 0)
        for g, win in enumerate(POOL_WINDOWS):
            cols = slice(g * GROUP, (g + 1) * GROUP)
            wg = pw_ref[g].astype(BF16)
            dya = dcext[:, cols]
            dmixed = dya * ps_ref[:, cols]
            dpooled = lax.dot_general(dmixed.astype(BF16), wg, NT, preferred_element_type=F32)
            cnt2 = jnp.minimum(pos2 + 1, win).astype(F32)
            dqext[:, cols] = dpooled / cnt2
            du = -dpooled[0:tm]
            for j in range(win):
                du = du + dqext[pl.ds(j, tm), cols]
            dh_ref[:, cols] = du.astype(BF16)
            pooled = _pool_means(uext, pos, tm, g, win)
            pooled_b = pooled.astype(BF16)
            mixed = jnp.dot(pooled_b, wg, preferred_element_type=F32) + pb_ref[g:g + 1, :]
            dps_ref[:, cols] += jnp.sum(dya[0:tm] * mixed, axis=0, keepdims=True)
            dpb_ref[g:g + 1, :] += jnp.sum(dmixed[0:tm], axis=0, keepdims=True)
            dpw_ref[g] += lax.dot_general(pooled_b, dmixed[0:tm].astype(BF16), TN, preferred_element_type=F32)

        y = _conv_taps(gext, cw_ref, cb_ref, HALO - taps, r2)
        mu = jnp.mean(y, axis=-1, keepdims=True)
        dv = y - mu
        rstd = lax.rsqrt(jnp.mean(dv * dv, axis=-1, keepdims=True) + EPS)
        norm = dv * rstd
        ln = norm * lg_ref[...] + lb_ref[...]
        sig = _sigmoid(ln)
        dln = dcext[:, pc:] * (sig * (1.0 + ln * (1.0 - sig)))
        dnorm = dln * lg_ref[...]
        dyc = rstd * (dnorm - jnp.mean(dnorm, axis=-1, keepdims=True)
                      - norm * jnp.mean(dnorm * norm, axis=-1, keepdims=True))
        dycext[...] = dyc
        dlg_ref[...] += jnp.sum((dln * norm)[0:tm], axis=0, keepdims=True)
        dlb_ref[...] += jnp.sum(dln[0:tm], axis=0, keepdims=True)
        dcb_ref[...] += jnp.sum(dyc[0:tm], axis=0, keepdims=True)
        dyc_t = dycext[0:tm, :]
        dg = jnp.zeros((tm, cc), F32)
        for k in range(CONV_WIDTH):
            dcw_ref[k:k + 1, :] += jnp.sum(dyc_t * gext[pl.ds(HALO - taps + k, tm), :], axis=0, keepdims=True)
            dg = dg + cw_ref[k:k + 1, :] * dycext[pl.ds(taps - k, tm), :]
        a = h_ref[:, pc:pc + cc]
        sg = _sigmoid(h_ref[:, pc + cc:])
        dh_ref[:, pc:pc + cc] = (dg * sg).astype(BF16)
        dh_ref[:, pc + cc:] = (dg * a * sg * (1.0 - sg)).astype(BF16)

    small = lambda a: pl.BlockSpec(a.shape, lambda i: (0,) * a.ndim)
    smalls = (pool_w, pool_b, pool_scale, conv_w, conv_b, ln_g, ln_b)
    return _call(
        body, name=name, grid=(nt,),
        in_specs=[pl.BlockSpec((tm, hw), lambda i: (i, 0)),
                  pl.BlockSpec((HALO, hw), lambda i: (jnp.maximum(i * per - 1, 0), 0)),
                  pl.BlockSpec((HALO, hw), lambda i: (jnp.minimum((i + 1) * per, t // HALO - 1), 0)),
                  pl.BlockSpec((tm, pc + cc), lambda i: (i, 0)),
                  pl.BlockSpec((HALO, pc + cc), lambda i: (jnp.minimum((i + 1) * per, t // HALO - 1), 0)),
                  ] + [small(a) for a in smalls],
        out_specs=[pl.BlockSpec((tm, hw), lambda i: (i, 0))] + [small(a) for a in smalls],
        out_shape=[jax.ShapeDtypeStruct((t, hw), BF16)] + [jax.ShapeDtypeStruct(a.shape, F32) for a in smalls],
        scratch_shapes=[pltpu.VMEM((HALO + tm, pc), F32), pltpu.VMEM((HALO + tm + HALO, cc), F32),
                        pltpu.VMEM((r2, pc + cc), F32), pltpu.VMEM((r2, pc), F32), pltpu.VMEM((r2, cc), F32)],
        compiler_params=_params(),
    )(h, h, h, dcat, dcat, *smalls)


SQRT_HALF = 0.7071067811865476
INV_SQRT_2PI = 0.3989422804014327


def _sgu_core(zp_ref, lg_ref, lb_ref, ws_ref, bs_ref, vo_s, tm, sc, heads):
    zp = zp_ref[...]
    z = 0.5 * zp * (1.0 + lax.erf(zp * SQRT_HALF))
    u = z[:, :sc]
    v = z[:, sc:]
    mu = jnp.mean(v, axis=-1, keepdims=True)
    dv = v - mu
    rstd = lax.rsqrt(jnp.mean(dv * dv, axis=-1, keepdims=True) + EPS)
    norm = dv * rstd
    vb = (norm * lg_ref[...] + lb_ref[...]).astype(BF16)
    row = lax.broadcasted_iota(jnp.int32, (GROUP, GROUP), 0)
    col = lax.broadcasted_iota(jnp.int32, (GROUP, GROUP), 1)
    mask = (col <= row).astype(F32)
    wm = [ws_ref[hd] * mask for hd in range(heads)]
    for hd in range(heads):
        cols = slice(hd * GROUP, (hd + 1) * GROUP)
        wb = wm[hd].astype(BF16)
        for n in range(tm // GROUP):
            rows = slice(n * GROUP, (n + 1) * GROUP)
            vo_s[rows, cols] = jnp.dot(wb, vb[rows, cols], preferred_element_type=F32) + bs_ref[hd]
    return zp, u, norm, rstd, vb, wm, mask


def _sgu_fwd(name, zp, ln_g, ln_b, w_s, b_s):
    t, two_sc = zp.shape
    sc = two_sc // 2
    heads = sc // GROUP
    tm = _tile(t)

    def body(zp_ref, lg_ref, lb_ref, ws_ref, bs_ref, q_ref, vo_s):
        _, u, _, _, _, _, _ = _sgu_core(zp_ref, lg_ref, lb_ref, ws_ref, bs_ref, vo_s, tm, sc, heads)
        q_ref[...] = (u * vo_s[...]).astype(BF16)

    small = lambda a: pl.BlockSpec(a.shape, lambda i: (0,) * a.ndim)
    return _call(
        body, name=name, grid=(t // tm,),
        in_specs=[pl.BlockSpec((tm, two_sc), lambda i: (i, 0)), small(ln_g), small(ln_b), small(w_s), small(b_s)],
        out_specs=pl.BlockSpec((tm, sc), lambda i: (i, 0)),
        out_shape=jax.ShapeDtypeStruct((t, sc), BF16),
        scratch_shapes=[pltpu.VMEM((tm, sc), F32)],
        compiler_params=_params(),
    )(zp, ln_g, ln_b, w_s, b_s)


def _sgu_bwd(name, zp, dq, ln_g, ln_b, w_s, b_s):
    t, two_sc = zp.shape
    sc = two_sc // 2
    heads = sc // GROUP
    tm = _tile(t)
    nt = t // tm

    def body(zp_ref, dq_ref, lg_ref, lb_ref, ws_ref, bs_ref,
             dzp_ref, dlg_ref, dlb_ref, dws_ref, dbs_ref, vo_s, dvl_s, dws_acc):
        i = pl.program_id(0)

        @pl.when(i == 0)
        def _():
            dlg_ref[...] = jnp.zeros_like(dlg_ref)
            dlb_ref[...] = jnp.zeros_like(dlb_ref)
            dbs_ref[...] = jnp.zeros_like(dbs_ref)
            dws_acc[...] = jnp.zeros_like(dws_acc)

        zp, u, norm, rstd, vb, wm, mask = _sgu_core(zp_ref, lg_ref, lb_ref, ws_ref, bs_ref, vo_s, tm, sc, heads)
        dq = dq_ref[...]
        du = dq * vo_s[...]
        dvo = dq * u
        dvob = dvo.astype(BF16)
        for hd in range(heads):
            cols = slice(hd * GROUP, (hd + 1) * GROUP)
            wtb = jnp.transpose(wm[hd]).astype(BF16)
            for n in range(tm // GROUP):
                rows = slice(n * GROUP, (n + 1) * GROUP)
                blk = dvob[rows, cols]
                dws_acc[hd] += lax.dot_general(blk, vb[rows, cols], NT, preferred_element_type=F32)
                dvl_s[rows, cols] = jnp.dot(wtb, blk, preferred_element_type=F32)
                dbs_ref[hd] += jnp.sum(dvo[rows, cols], axis=-1, keepdims=True)
        dvl = dvl_s[...]
        dlg_ref[...] += jnp.sum(dvl * norm, axis=0, keepdims=True)
        dlb_ref[...] += jnp.sum(dvl, axis=0, keepdims=True)
        dnorm = dvl * lg_ref[...]
        dv = rstd * (dnorm - jnp.mean(dnorm, axis=-1, keepdims=True)
                     - norm * jnp.mean(dnorm * norm, axis=-1, keepdims=True))
        dgelu = 0.5 * (1.0 + lax.erf(zp * SQRT_HALF)) + zp * (INV_SQRT_2PI * jnp.exp(-0.5 * zp * zp))
        dzp_ref[:, :sc] = (du * dgelu[:, :sc]).astype(BF16)
        dzp_ref[:, sc:] = (dv * dgelu[:, sc:]).astype(BF16)

        @pl.when(i == nt - 1)
        def _():
            for hd in range(heads):
                dws_ref[hd] = dws_acc[hd] * mask

    small = lambda a: pl.BlockSpec(a.shape, lambda i: (0,) * a.ndim)
    smalls = (ln_g, ln_b, w_s, b_s)
    return _call(
        body, name=name, grid=(nt,),
        in_specs=[pl.BlockSpec((tm, two_sc), lambda i: (i, 0)), pl.BlockSpec((tm, sc), lambda i: (i, 0))]
                 + [small(a) for a in smalls],
        out_specs=[pl.BlockSpec((tm, two_sc), lambda i: (i, 0))] + [small(a) for a in smalls],
        out_shape=[jax.ShapeDtypeStruct((t, two_sc), BF16)] + [jax.ShapeDtypeStruct(a.shape, F32) for a in smalls],
        scratch_shapes=[pltpu.VMEM((tm, sc), F32), pltpu.VMEM((tm, sc), F32), pltpu.VMEM(w_s.shape, F32)],
        compiler_params=_params(),
    )(zp, dq, ln_g, ln_b, w_s, b_s)


def _loss_head(name, x, gamma, target):
    t, d = x.shape
    tm = _tile(t)

    def body(x_ref, gam_ref, tg_ref, loss_ref, dx_ref, dgam_ref):
        i = pl.program_id(0)

        @pl.when(i == 0)
        def _():
            loss_ref[...] = jnp.zeros_like(loss_ref)
            dgam_ref[...] = jnp.zeros_like(dgam_ref)

        xv = x_ref[...]
        r = lax.rsqrt(jnp.mean(xv * xv, axis=-1, keepdims=True) + EPS)
        yv = xv * r
        err = yv * gam_ref[...] - tg_ref[...]
        row = jnp.sum(err * err, axis=-1, keepdims=True)
        loss_ref[...] += (0.5 / d) * jnp.sum(row, axis=0, keepdims=True)
        dout = err * (1.0 / d)
        dgam_ref[...] += jnp.sum(dout * yv, axis=0, keepdims=True)
        dyn = dout * gam_ref[...]
        dx_ref[...] = r * (dyn - yv * jnp.mean(dyn * yv, axis=-1, keepdims=True))

    return _call(
        body, name=name, grid=(t // tm,),
        in_specs=[pl.BlockSpec((tm, d), lambda i: (i, 0)),
                  pl.BlockSpec((1, d), lambda i: (0, 0)),
                  pl.BlockSpec((tm, d), lambda i: (i, 0))],
        out_specs=[pl.BlockSpec((1, 1), lambda i: (0, 0)),
                   pl.BlockSpec((tm, d), lambda i: (i, 0)),
                   pl.BlockSpec((1, d), lambda i: (0, 0))],
        out_shape=[jax.ShapeDtypeStruct((1, 1), F32), jax.ShapeDtypeStruct((t, d), F32),
                   jax.ShapeDtypeStruct((1, d), F32)],
        compiler_params=_params(),
    )(x, gamma, target)


def _exchange(name, gathers, scatters, recv_shapes):
    ng, ns, nr = len(gathers), len(scatters), len(recv_shapes)
    n_item = ng + ns
    meta = [(rid, layer) for (_, rid, layer) in scatters]

    def body(*refs):
        g_in = refs[:ng]
        s_in = refs[ng:n_item]
        g_out = refs[n_item:n_item + ng]
        r_out = refs[n_item + ng:n_item + ng + nr]
        send_sems, recv_sems, local_sems = refs[n_item + ng + nr:]
        x, y, c = lax.axis_index("x"), lax.axis_index("y"), lax.axis_index("c")
        me = 4 * x + 2 * y + c

        def src_dst(a, pid):
            if a < ng:
                return g_in[a], g_out[a].at[me]
            rid, layer = meta[a - ng]
            return s_in[a - ng].at[pid], r_out[rid].at[me, layer]

        copies = []
        for a in range(n_item):
            src, dst = src_dst(a, me)
            cp = pltpu.make_async_copy(src, dst, local_sems.at[a])
            cp.start()
            copies.append(cp)
        for k in (1, 4, 2, 6, 5, 3, 7):
            peer = ((1 - x) if k & 4 else x, (1 - y) if k & 2 else y, (1 - c) if k & 1 else c)
            pid = 4 * peer[0] + 2 * peer[1] + peer[2]
            for a in range(n_item):
                src, dst = src_dst(a, pid)
                cp = pltpu.make_async_remote_copy(
                    src_ref=src, dst_ref=dst, send_sem=send_sems.at[a, k - 1], recv_sem=recv_sems.at[a, k - 1],
                    device_id=peer, device_id_type=pl.DeviceIdType.MESH)
                cp.start()
                copies.append(cp)
        for cp in copies:
            cp.wait()

    hbm = pl.BlockSpec(memory_space=pl.ANY)
    out_shape = ([jax.ShapeDtypeStruct((N_DEV,) + g.shape, g.dtype) for g in gathers]
                 + [jax.ShapeDtypeStruct(s, dt) for (s, dt) in recv_shapes])
    outs = _call(
        body, name=name,
        in_specs=[hbm] * n_item, out_specs=[hbm] * (ng + nr), out_shape=out_shape,
        scratch_shapes=[pltpu.SemaphoreType.DMA((n_item, N_DEV - 1)), pltpu.SemaphoreType.DMA((n_item, N_DEV - 1)),
                        pltpu.SemaphoreType.DMA((n_item,))],
        compiler_params=pltpu.CompilerParams(has_side_effects=True),
    )(*gathers, *[s for (s, _, _) in scatters])
    return outs[:ng], outs[ng:]


def _adamw(name, parts, w, m, v, rows):
    s_n, l_n, r_n, c_n = parts.shape
    tr = min(rows, r_n)
    c1 = 1.0 - ADAM_B1 ** ADAM_STEP
    c2 = 1.0 - ADAM_B2 ** ADAM_STEP

    def body(p_ref, w_ref, m_ref, v_ref, g_ref, d_ref, mo_ref, vo_ref):
        g = p_ref[0].astype(F32)
        for s in range(1, s_n):
            g = g + p_ref[s].astype(F32)
        wv = w_ref[...]
        mn = ADAM_B1 * m_ref[...] + (1.0 - ADAM_B1) * g
        vn = ADAM_B2 * v_ref[...] + (1.0 - ADAM_B2) * (g * g)
        m_hat = mn / c1
        v_hat = vn / c2
        g_ref[...] = g
        d_ref[...] = -ADAM_LR * (m_hat / (jnp.sqrt(v_hat) + ADAM_EPS) + ADAM_WD * wv)
        mo_ref[...] = mn
        vo_ref[...] = vn

    blk = pl.BlockSpec((None, tr, c_n), lambda l, i: (l, i, 0))
    return _call(
        body, name=name, grid=(l_n, r_n // tr),
        in_specs=[pl.BlockSpec((s_n, None, tr, c_n), lambda l, i: (0, l, i, 0)), blk, blk, blk],
        out_specs=[blk] * 4,
        out_shape=[jax.ShapeDtypeStruct((l_n, r_n, c_n), F32)] * 4,
        compiler_params=_params(),
    )(parts, w, m, v)


def _local_step(x, target, big, small):
    t, d = x.shape
    n_layers = small["ffn1_norm"].shape[0]
    pc = len(POOL_WINDOWS) * GROUP

    def row(a, l):
        return a[l:l + 1]

    saved = []
    xs = x
    for l in range(n_layers):
        rec = {"x_ffn1": xs}
        xs, rec["xn_ffn1"], rec["gu_ffn1"] = _ffn_fwd(
            f"ffn1_fwd_l{l}", xs, row(small["ffn1_norm"], l), big["ffn1_w_in"], big["ffn1_w_out"], l)
        rec["x_mix"] = xs
        if l % 2 == 0:
            rec["xn_mix"], rec["h"] = _rms_matmul(f"ab_in_l{l}", xs, row(small["mix_norm"], l), big["ab_w_in"])
            rec["cat"] = _pool_conv_fwd(f"pool_conv_fwd_l{l}", rec["h"], small["pool_w"], small["pool_b"],
                                        small["pool_scale"], small["conv_w"], small["conv_b"],
                                        small["conv_ln_g"], small["conv_ln_b"])
            xs = _matmul_residual(f"ab_out_l{l}", rec["cat"], big["ab_w_out"], xs)
        else:
            rec["xn_mix"], rec["zp"] = _rms_matmul(f"sgu_in_l{l}", xs, row(small["mix_norm"], l), big["sgu_w_in"])
            rec["q"] = _sgu_fwd(f"sgu_fwd_l{l}", rec["zp"], small["sgu_ln_g"], small["sgu_ln_b"],
                                small["sgu_w"], small["sgu_b"])
            xs = _matmul_residual(f"sgu_out_l{l}", rec["q"], big["sgu_w_out"], xs)
        rec["x_ffn2"] = xs
        xs, rec["xn_ffn2"], rec["gu_ffn2"] = _ffn_fwd(
            f"ffn2_fwd_l{l}", xs, row(small["ffn2_norm"], l), big["ffn2_w_in"], big["ffn2_w_out"], l)
        saved.append(rec)

    loss, dx, d_final = _loss_head("loss_head", xs, small["final_norm"], target)

    gb = {}
    gs = {"final_norm": d_final}
    norm_rows = {"ffn1_norm": [None] * n_layers, "mix_norm": [None] * n_layers, "ffn2_norm": [None] * n_layers}

    def ffn_backward(tag, l, dx, rec):
        dx, dgam, dyh, hh, dgu = _ffn_bwd(f"{tag}_bwd_l{l}", dx, rec[f"x_{tag}"], row(small[f"{tag}_norm"], l),
                                          rec[f"gu_{tag}"], big[f"{tag}_w_in"], big[f"{tag}_w_out"], l)
        norm_rows[f"{tag}_norm"][l] = dgam
        fb = dgu.shape[-1]
        gb[f"{tag}_w_in", l] = _matmul_tn(f"{tag}_dwin_l{l}", rec[f"xn_{tag}"], dgu.reshape(-1, t, fb))
        gb[f"{tag}_w_out", l] = _matmul_tn(f"{tag}_dwout_l{l}", hh, dyh).reshape(N_DEV, fb // 2, d)
        return dx

    for l in reversed(range(n_layers)):
        rec = saved[l]
        dx = ffn_backward("ffn2", l, dx, rec)
        if l % 2 == 0:
            dcat, dxb = _matmul_nt(f"ab_out_bwd_l{l}", dx, big["ab_w_out"])
            gb["ab_w_out", 0] = _matmul_tn(f"ab_dwout_l{l}", rec["cat"], dxb)
            dh, gs["pool_w"], gs["pool_b"], gs["pool_scale"], gs["conv_w"], gs["conv_b"], gs["conv_ln_g"], \
                gs["conv_ln_b"] = _pool_conv_bwd(
                    f"pool_conv_bwd_l{l}", rec["h"], dcat, small["pool_w"], small["pool_b"], small["pool_scale"],
                    small["conv_w"], small["conv_b"], small["conv_ln_g"], small["conv_ln_b"])
            gb["ab_w_in", 0] = _matmul_tn(f"ab_dwin_l{l}", rec["xn_mix"], dh)
            dx, dgam = _matmul_nt_rms_bwd(f"ab_in_bwd_l{l}", dh, big["ab_w_in"], dx, rec["x_mix"],
                                          row(small["mix_norm"], l))
        else:
            dq, dxb = _matmul_nt(f"sgu_out_bwd_l{l}", dx, big["sgu_w_out"])
            gb["sgu_w_out", 0] = _matmul_tn(f"sgu_dwout_l{l}", rec["q"], dxb)
            dzp, gs["sgu_ln_g"], gs["sgu_ln_b"], gs["sgu_w"], gs["sgu_b"] = _sgu_bwd(
                f"sgu_bwd_l{l}", rec["zp"], dq, small["sgu_ln_g"], small["sgu_ln_b"], small["sgu_w"], small["sgu_b"])
            gb["sgu_w_in", 0] = _matmul_tn(f"sgu_dwin_l{l}", rec["xn_mix"], dzp)
            dx, dgam = _matmul_nt_rms_bwd(f"sgu_in_bwd_l{l}", dzp, big["sgu_w_in"], dx, rec["x_mix"],
                                          row(small["mix_norm"], l))
        norm_rows["mix_norm"][l] = dgam
        dx = ffn_backward("ffn1", l, dx, rec)

    for k, rows in norm_rows.items():
        gs[k] = jnp.concatenate(rows, axis=0)
    return loss, dx, gb, gs


BIG = ("ffn1_w_in", "ffn1_w_out", "ffn2_w_in", "ffn2_w_out", "ab_w_in", "ab_w_out", "sgu_w_in", "sgu_w_out")
SHARDED_SMALL = ("conv_w", "sgu_ln_g", "sgu_ln_b")
REPLICATED = ("ffn1_norm", "mix_norm", "ffn2_norm", "pool_w", "pool_b", "pool_scale", "conv_b", "conv_ln_g",
              "conv_ln_b", "sgu_w", "sgu_b", "final_norm")
WEIGHTS = ("ffn1_norm", "ffn1_w_in", "ffn1_w_out", "mix_norm", "ffn2_norm", "ffn2_w_in", "ffn2_w_out", "ab_w_in",
           "pool_w", "pool_b", "pool_scale", "conv_w", "conv_b", "conv_ln_g", "conv_ln_b", "ab_w_out", "sgu_w_in",
           "sgu_ln_g", "sgu_ln_b", "sgu_w", "sgu_b", "sgu_w_out", "final_norm")
LANES = 128


def _interleave_cols(g):
    n, k, c = g.shape
    return jnp.transpose(g, (1, 0, 2)).reshape(k, n * c)


def _split_cols(a):
    k, nc = a.shape
    return jnp.transpose(a.reshape(k, N_DEV, nc // N_DEV), (1, 0, 2))


def _as3(a):
    if a.ndim == 1:
        return a.reshape(1, 1, -1)
    if a.ndim == 2:
        return a.reshape(a.shape[0], 1, a.shape[1])
    return a.reshape(a.shape[0], -1, a.shape[-1])


def _pack_rows(a):
    flat = a.reshape(-1)
    pad = (-flat.shape[0]) % LANES
    if pad:
        flat = jnp.concatenate([flat, jnp.zeros((pad,), flat.dtype)])
    return flat.reshape(-1, LANES)


def kernel(x, ffn1_norm, ffn1_w_in, ffn1_w_out, mix_norm, ffn2_norm, ffn2_w_in, ffn2_w_out, ab_w_in, pool_w, pool_b, pool_scale, conv_w, conv_b, conv_ln_g, conv_ln_b, ab_w_out, sgu_w_in, sgu_ln_g, sgu_ln_b, sgu_w, sgu_b, sgu_w_out, final_norm, loss_target, m_ffn1_norm, m_ffn1_w_in, m_ffn1_w_out, m_mix_norm, m_ffn2_norm, m_ffn2_w_in, m_ffn2_w_out, m_ab_w_in, m_pool_w, m_pool_b, m_pool_scale, m_conv_w, m_conv_b, m_conv_ln_g, m_conv_ln_b, m_ab_w_out, m_sgu_w_in, m_sgu_ln_g, m_sgu_ln_b, m_sgu_w, m_sgu_b, m_sgu_w_out, m_final_norm, v_ffn1_norm, v_ffn1_w_in, v_ffn1_w_out, v_mix_norm, v_ffn2_norm, v_ffn2_w_in, v_ffn2_w_out, v_ab_w_in, v_pool_w, v_pool_b, v_pool_scale, v_conv_w, v_conv_b, v_conv_ln_g, v_conv_ln_b, v_ab_w_out, v_sgu_w_in, v_sgu_ln_g, v_sgu_ln_b, v_sgu_w, v_sgu_b, v_sgu_w_out, v_final_norm):
    args = dict(locals())
    w = {n: args[n] for n in WEIGHTS}
    m = {n: args["m_" + n] for n in WEIGHTS}
    v = {n: args["v_" + n] for n in WEIGHTS}
    d = x.shape[-1]

    g_list = [w[n].astype(BF16) for n in BIG] + [w[n] for n in SHARDED_SMALL]
    gathered, _ = _exchange("gather_weights", g_list, [], [])
    gw = dict(zip(BIG + SHARDED_SMALL, gathered))

    fb = gw["ffn1_w_in"].shape[-1]
    n_layers = ffn1_norm.shape[0]
    big = {
        "ffn1_w_in": gw["ffn1_w_in"].reshape(2, N_DEV // 2, n_layers, d, fb),
        "ffn2_w_in": gw["ffn2_w_in"].reshape(2, N_DEV // 2, n_layers, d, fb),
        "ffn1_w_out": gw["ffn1_w_out"],
        "ffn2_w_out": gw["ffn2_w_out"],
        "ab_w_in": _interleave_cols(gw["ab_w_in"][:, 0]),
        "ab_w_out": gw["ab_w_out"].reshape(-1, d),
        "sgu_w_in": _interleave_cols(gw["sgu_w_in"][:, 0]),
        "sgu_w_out": gw["sgu_w_out"].reshape(-1, d),
    }
    conv_w_full = _interleave_cols(gw["conv_w"][:, 0])
    small = {
        "ffn1_norm": ffn1_norm, "mix_norm": mix_norm, "ffn2_norm": ffn2_norm, "final_norm": final_norm.reshape(1, -1),
        "pool_w": pool_w[0], "pool_b": pool_b[0], "pool_scale": pool_scale,
        "conv_w": jnp.concatenate([conv_w_full, jnp.zeros((1, conv_w_full.shape[1]), F32)], axis=0),
        "conv_b": conv_b, "conv_ln_g": conv_ln_g, "conv_ln_b": conv_ln_b,
        "sgu_ln_g": gw["sgu_ln_g"].reshape(1, -1), "sgu_ln_b": gw["sgu_ln_b"].reshape(1, -1),
        "sgu_w": sgu_w[0], "sgu_b": sgu_b[0][:, :, None],
    }

    loss, grad_x, gb, gs = _local_step(x[0], loss_target[0], big, small)

    scatters, recv_shapes, recv_names = [], [], []

    def add_scatter(name, send, layer, n_l):
        if name not in recv_names:
            recv_names.append(name)
            recv_shapes.append(((N_DEV, n_l) + send.shape[1:], send.dtype))
        scatters.append((send, recv_names.index(name), layer))

    for n in ("ffn1_w_in", "ffn1_w_out", "ffn2_w_in", "ffn2_w_out"):
        for l in range(n_layers):
            add_scatter(n, gb[n, l], l, n_layers)
    add_scatter("ab_w_in", _split_cols(gb["ab_w_in", 0][0]), 0, 1)
    add_scatter("ab_w_out", gb["ab_w_out", 0][0].reshape(N_DEV, -1, d), 0, 1)
    add_scatter("sgu_w_in", _split_cols(gb["sgu_w_in", 0][0]), 0, 1)
    add_scatter("sgu_w_out", gb["sgu_w_out", 0][0].reshape(N_DEV, -1, d), 0, 1)
    add_scatter("conv_w", _split_cols(gs["conv_w"][:CONV_WIDTH]), 0, 1)
    add_scatter("sgu_ln_g", gs["sgu_ln_g"].reshape(N_DEV, 1, -1), 0, 1)
    add_scatter("sgu_ln_b", gs["sgu_ln_b"].reshape(N_DEV, 1, -1), 0, 1)

    rep_grads = {
        "ffn1_norm": gs["ffn1_norm"], "mix_norm": gs["mix_norm"], "ffn2_norm": gs["ffn2_norm"],
        "pool_w": gs["pool_w"], "pool_b": gs["pool_b"], "pool_scale": gs["pool_scale"],
        "conv_b": gs["conv_b"], "conv_ln_g": gs["conv_ln_g"], "conv_ln_b": gs["conv_ln_b"],
        "sgu_w": gs["sgu_w"], "sgu_b": gs["sgu_b"], "final_norm": gs["final_norm"],
    }
    packs = [_pack_rows(rep_grads[n]) for n in REPLICATED] + [_pack_rows(loss)]
    offsets = [0]
    for p in packs:
        offsets.append(offsets[-1] + p.shape[0])
    pad_rows = (-offsets[-1]) % 8
    zero_tail = [jnp.zeros((pad_rows, LANES), F32)] if pad_rows else []
    packed = jnp.concatenate(packs + zero_tail, axis=0)
    (packed_all,), recvs = _exchange("reduce_gradients", [packed], scatters, recv_shapes)
    recv = dict(zip(recv_names, recvs))

    out = {}
    rows_for = {"ffn1_w_in": 256, "ffn2_w_in": 256, "ffn1_w_out": 176, "ffn2_w_out": 176}
    for n in BIG + SHARDED_SMALL:
        w3 = _as3(w[n])
        parts = recv[n].reshape((N_DEV,) + w3.shape)
        res = _adamw(f"adamw_{n}", parts, w3, _as3(m[n]), _as3(v[n]), rows_for.get(n, 512))
        out[n] = [r.reshape(w[n].shape) for r in res]

    def pack_rep(src):
        tail = [jnp.zeros((offsets[-1] - offsets[-2] + pad_rows, LANES), F32)]
        return jnp.concatenate([_pack_rows(src[n]) for n in REPLICATED] + tail, axis=0)[None]

    n_rows = packed.shape[0]
    res = _adamw("adamw_replicated", packed_all.reshape(N_DEV, 1, n_rows, LANES),
                 pack_rep(w), pack_rep(m), pack_rep(v), n_rows)
    for i, n in enumerate(REPLICATED):
        size = w[n].size
        out[n] = [r[0, offsets[i]:offsets[i + 1]].reshape(-1)[:size].reshape(w[n].shape) for r in res]
    loss_sum = res[0][0, offsets[-2], 0]

    return (loss_sum, grad_x[None],
            *[out[n][0] for n in WEIGHTS], *[out[n][1] for n in WEIGHTS],
            *[out[n][2] for n in WEIGHTS], *[out[n][3] for n in WEIGHTS])
```

```python
import functools

import jax
import jax.numpy as jnp
from jax import lax
from jax.experimental import pallas as pl
from jax.experimental.pallas import tpu as pltpu

F32 = jnp.float32
BF16 = jnp.bfloat16
EPS = 1e-6
N_DEV = 8
POOL_WINDOWS = (2, 4, 8, 16)
CONV_WIDTH = 31
HALO = 32
GROUP = 128
TOKEN_TILE = 512
ADAM_LR, ADAM_B1, ADAM_B2, ADAM_EPS, ADAM_WD, ADAM_STEP = 0.001, 0.9, 0.999, 1e-08, 0.01, 10
VMEM_LIMIT = 56 * 1024 * 1024

NT = (((1,), (1,)), ((), ()))
TN = (((0,), (0,)), ((), ()))


def _pallas(body, side_effects, **kw):
    params = pltpu.CompilerParams(vmem_limit_bytes=VMEM_LIMIT, has_side_effects=side_effects)
    return pl.pallas_call(body, compiler_params=params, **kw)


def _call(body, comm=None, **kw):
    if comm is None:
        return _pallas(body, False, **kw)
    in_specs = list(kw.pop("in_specs"))
    out_specs = kw.pop("out_specs")
    out_shape = kw.pop("out_shape")
    scratch = list(kw.pop("scratch_shapes", []))
    single = not isinstance(out_shape, (list, tuple))
    if single:
        out_specs, out_shape = [out_specs], [out_shape]
    n_in, n_out, n_scr = len(in_specs), len(out_shape), len(scratch)
    n_ci, n_co = len(comm.inputs), len(comm.out_shapes)
    grid = tuple(kw.get("grid", ()))

    def wrapped(*refs):
        pos = 0
        parts = []
        for n in (n_in, n_ci, n_out, n_co, n_scr, 3):
            parts.append(refs[pos:pos + n])
            pos += n
        a_in, c_in, a_out, c_out, a_scr, sems = parts
        if grid:
            ids = [pl.program_id(ax) for ax in range(len(grid))]
            first = functools.reduce(lambda p, q: p & q, [i == 0 for i in ids])
            last = functools.reduce(lambda p, q: p & q, [i == g - 1 for i, g in zip(ids, grid)])
            pl.when(first)(lambda: comm.start(c_in, c_out, sems))
            body(*a_in, *a_out, *a_scr)
            pl.when(last)(lambda: comm.wait(c_in, c_out, sems))
        else:
            comm.start(c_in, c_out, sems)
            body(*a_in, *a_out, *a_scr)
            comm.wait(c_in, c_out, sems)

    hbm = pl.BlockSpec(memory_space=pl.ANY)
    fn = _pallas(wrapped, True, in_specs=in_specs + [hbm] * n_ci, out_specs=list(out_specs) + [hbm] * n_co,
                 out_shape=list(out_shape) + list(comm.out_shapes), scratch_shapes=scratch + comm.semaphores(), **kw)

    def run(*operands):
        outs = fn(*operands, *comm.inputs)
        res = outs[:n_out]
        return (res[0] if single else res), outs[n_out:]

    return run


class _Comm:
    def __init__(self):
        self.inputs, self.sel, self.kinds, self.out_shapes = [], [], [], []

    def gather(self, arr, sel=None):
        block = arr.shape if sel is None else arr.shape[1:]
        self.inputs.append(arr)
        self.sel.append(sel)
        self.kinds.append("gather")
        self.out_shapes.append(jax.ShapeDtypeStruct((N_DEV,) + tuple(block), arr.dtype))
        return len(self.inputs) - 1

    def scatter(self, arr):
        self.inputs.append(arr)
        self.sel.append(None)
        self.kinds.append("scatter")
        self.out_shapes.append(jax.ShapeDtypeStruct(arr.shape, arr.dtype))
        return len(self.inputs) - 1

    def semaphores(self):
        n = len(self.inputs)
        return [pltpu.SemaphoreType.DMA((n, N_DEV - 1)), pltpu.SemaphoreType.DMA((n, N_DEV - 1)),
                pltpu.SemaphoreType.DMA((n,))]

    def _copies(self, ins, outs, sems, with_passed=True):
        send_sems, recv_sems, local_sems = sems
        x, y, c = lax.axis_index("x"), lax.axis_index("y"), lax.axis_index("c")
        me = 4 * x + 2 * y + c
        sibling = (x, y, 1 - c)
        chips = [(1 - x, y), (x, 1 - y), (1 - x, 1 - y)]
        items = []
        for a, kind in enumerate(self.kinds):
            def remote(src, dst, k, to, a=a):
                return pltpu.make_async_remote_copy(
                    src_ref=src, dst_ref=dst, send_sem=send_sems.at[a, k], recv_sem=recv_sems.at[a, k],
                    device_id=to, device_id_type=pl.DeviceIdType.MESH)
            if kind == "gather":
                src = ins[a] if self.sel[a] is None else ins[a].at[self.sel[a]]
                mine = outs[a].at[me]
                local = pltpu.make_async_copy(src, mine, local_sems.at[a])
                first = [remote(src, mine, 0, sibling)]
                first += [remote(src, mine, 1 + j, (*chip, c)) for j, chip in enumerate(chips)]
                passed = []
                for j, chip in enumerate(chips if with_passed else []):
                    got = outs[a].at[4 * chip[0] + 2 * chip[1] + c]
                    passed.append(remote(got, got, 4 + j, sibling))
            else:
                local = pltpu.make_async_copy(ins[a].at[me], outs[a].at[me], local_sems.at[a])
                first, passed = [], []
                for k in (1, 4, 2, 6, 5, 3, 7):
                    peer = ((1 - x) if k & 4 else x, (1 - y) if k & 2 else y, (1 - c) if k & 1 else c)
                    pid = 4 * peer[0] + 2 * peer[1] + peer[2]
                    first.append(remote(ins[a].at[pid], outs[a].at[me], k - 1, peer))
            items.append((local, first, passed))
        return items

    def start(self, ins, outs, sems):
        for local, first, _ in self._copies(ins, outs, sems, with_passed=False):
            local.start()
            for cp in first:
                cp.start()

    def wait(self, ins, outs, sems):
        items = self._copies(ins, outs, sems)
        for _, first, passed in items:
            for j, cp in enumerate(passed):
                first[1 + j].wait_recv()
                cp.start()
        for local, first, passed in items:
            if passed:
                first[0].wait_recv()
                for cp in passed:
                    cp.wait_recv()
                for cp in first + passed:
                    cp.wait_send()
            else:
                for cp in first:
                    cp.wait()
            local.wait()


def _exchange(name, comm):
    _, outs = _call(lambda: None, comm=comm, name=name, in_specs=[], out_specs=[], out_shape=[])()
    return outs


def _sigmoid(x):
    return 1.0 / (1.0 + jnp.exp(-x))


def _tile(t):
    return min(TOKEN_TILE, t)


def _ffn_fwd(name, x, gamma, wgu, wout, comm=None):
    t, d = x.shape
    fb = wgu.shape[-2]
    nk = wgu.shape[1]
    tm = _tile(t)

    def body(x_ref, gam_ref, wgu_ref, wo_ref, xo_ref, xn_ref, gu_ref, xn_s, acc):
        k = pl.program_id(1)

        @pl.when(k == 0)
        def _():
            xv = x_ref[...]
            r = lax.rsqrt(jnp.mean(xv * xv, axis=-1, keepdims=True) + EPS)
            xn = (xv * r * gam_ref[...]).astype(BF16)
            xn_s[...] = xn
            xn_ref[...] = xn
            acc[...] = jnp.zeros_like(acc)

        xn = xn_s[...]
        g = lax.dot_general(xn, wgu_ref[0], NT, preferred_element_type=F32)
        u = lax.dot_general(xn, wgu_ref[1], NT, preferred_element_type=F32)
        gu_ref[0] = g.astype(BF16)
        gu_ref[1] = u.astype(BF16)
        h = (g * _sigmoid(g) * u).astype(BF16)
        acc[...] += jnp.dot(h, wo_ref[...].reshape(fb, d), preferred_element_type=F32)

        @pl.when(k == nk - 1)
        def _():
            xo_ref[...] = x_ref[...] + 0.5 * acc[...]

    return _call(
        body, comm=comm, name=name, grid=(t // tm, nk),
        in_specs=[
            pl.BlockSpec((tm, d), lambda i, k: (i, 0)),
            pl.BlockSpec((1, d), lambda i, k: (0, 0)),
            pl.BlockSpec((2, None, fb, d), lambda i, k: (0, k, 0, 0)),
            pl.BlockSpec((2, fb // 2, d), lambda i, k: (k, 0, 0)),
        ],
        out_specs=[
            pl.BlockSpec((tm, d), lambda i, k: (i, 0)),
            pl.BlockSpec((tm, d), lambda i, k: (i, 0)),
            pl.BlockSpec((2, None, tm, fb), lambda i, k: (0, k, i, 0)),
        ],
        out_shape=[
            jax.ShapeDtypeStruct((t, d), F32),
            jax.ShapeDtypeStruct((t, d), BF16),
            jax.ShapeDtypeStruct((2, nk, t, fb), BF16),
        ],
        scratch_shapes=[pltpu.VMEM((tm, d), BF16), pltpu.VMEM((tm, d), F32)],
    )(x, gamma, wgu, wout)


def _ffn_bwd(name, dy, x, gamma, gu, wgu, wout, comm=None):
    t, d = x.shape
    fb = wgu.shape[-2]
    nk = wgu.shape[1]
    tm = _tile(t)

    def body(dy_ref, x_ref, gam_ref, gu_ref, wgu_ref, wo_ref,
             dx_ref, dgam_ref, dyh_ref, h_ref, dgu_ref, dyb_s, acc):
        i = pl.program_id(0)
        k = pl.program_id(1)

        @pl.when(k == 0)
        def _():
            dyb = (0.5 * dy_ref[...]).astype(BF16)
            dyb_s[...] = dyb
            dyh_ref[...] = dyb
            acc[...] = jnp.zeros_like(acc)

        @pl.when((i == 0) & (k == 0))
        def _():
            dgam_ref[...] = jnp.zeros_like(dgam_ref)

        dyb = dyb_s[...]
        dh = lax.dot_general(dyb, wo_ref[...].reshape(fb, d), NT, preferred_element_type=F32)
        g = gu_ref[0].astype(F32)
        u = gu_ref[1].astype(F32)
        sig = _sigmoid(g)
        silu = g * sig
        h_ref[...] = (silu * u).astype(BF16)
        dg = (dh * u * (sig * (1.0 + g * (1.0 - sig)))).astype(BF16)
        du = (dh * silu).astype(BF16)
        dgu_ref[0] = dg
        dgu_ref[1] = du
        acc[...] += (jnp.dot(dg, wgu_ref[0], preferred_element_type=F32)
                     + jnp.dot(du, wgu_ref[1], preferred_element_type=F32))

        @pl.when(k == nk - 1)
        def _():
            xv = x_ref[...]
            r = lax.rsqrt(jnp.mean(xv * xv, axis=-1, keepdims=True) + EPS)
            yv = xv * r
            dxn = acc[...]
            dgam_ref[...] += jnp.sum(dxn * yv, axis=0, keepdims=True)
            dyn = dxn * gam_ref[...]
            dx_ref[...] = dy_ref[...] + r * (dyn - yv * jnp.mean(dyn * yv, axis=-1, keepdims=True))

    return _call(
        body, comm=comm, name=name, grid=(t // tm, nk),
        in_specs=[
            pl.BlockSpec((tm, d), lambda i, k: (i, 0)),
            pl.BlockSpec((tm, d), lambda i, k: (i, 0)),
            pl.BlockSpec((1, d), lambda i, k: (0, 0)),
            pl.BlockSpec((2, None, tm, fb), lambda i, k: (0, k, i, 0)),
            pl.BlockSpec((2, None, fb, d), lambda i, k: (0, k, 0, 0)),
            pl.BlockSpec((2, fb // 2, d), lambda i, k: (k, 0, 0)),
        ],
        out_specs=[
            pl.BlockSpec((tm, d), lambda i, k: (i, 0)),
            pl.BlockSpec((1, d), lambda i, k: (0, 0)),
            pl.BlockSpec((tm, d), lambda i, k: (i, 0)),
            pl.BlockSpec((None, tm, fb), lambda i, k: (k, i, 0)),
            pl.BlockSpec((2, None, tm, fb), lambda i, k: (0, k, i, 0)),
        ],
        out_shape=[
            jax.ShapeDtypeStruct((t, d), F32),
            jax.ShapeDtypeStruct((1, d), F32),
            jax.ShapeDtypeStruct((t, d), BF16),
            jax.ShapeDtypeStruct((nk, t, fb), BF16),
            jax.ShapeDtypeStruct((2, nk, t, fb), BF16),
        ],
        scratch_shapes=[pltpu.VMEM((tm, d), BF16), pltpu.VMEM((tm, d), F32)],
    )(dy, x, gamma, gu, wgu, wout)


def _matmul_tn(name, a, b, out_dtype=BF16, comm=None):
    a_b = a.ndim == 3
    b_b = b.ndim == 3
    nb = a.shape[0] if a_b else b.shape[0] if b_b else 1
    t, m = a.shape[-2:]
    n = b.shape[-1]
    tk = _tile(t)
    nt = t // tk

    def body(a_ref, b_ref, o_ref, acc):
        s = pl.program_id(1)

        @pl.when(s == 0)
        def _():
            acc[...] = jnp.zeros_like(acc)

        acc[...] += lax.dot_general(a_ref[...], b_ref[...], TN, preferred_element_type=F32)

        @pl.when(s == nt - 1)
        def _():
            o_ref[...] = acc[...].astype(o_ref.dtype)

    a_spec = (pl.BlockSpec((None, tk, m), lambda j, s: (j, s, 0)) if a_b
              else pl.BlockSpec((tk, m), lambda j, s: (s, 0)))
    b_spec = (pl.BlockSpec((None, tk, n), lambda j, s: (j, s, 0)) if b_b
              else pl.BlockSpec((tk, n), lambda j, s: (s, 0)))
    return _call(
        body, comm=comm, name=name, grid=(nb, nt),
        in_specs=[a_spec, b_spec],
        out_specs=pl.BlockSpec((None, m, n), lambda j, s: (j, 0, 0)),
        out_shape=jax.ShapeDtypeStruct((nb, m, n), out_dtype),
        scratch_shapes=[pltpu.VMEM((m, n), F32)],
    )(a, b)


def _rms_matmul(name, x, gamma, w):
    t, d = x.shape
    n = w.shape[1]
    tm = _tile(t)

    def body(x_ref, gam_ref, w_ref, xn_ref, h_ref):
        xv = x_ref[...]
        r = lax.rsqrt(jnp.mean(xv * xv, axis=-1, keepdims=True) + EPS)
        xn = (xv * r * gam_ref[...]).astype(BF16)
        xn_ref[...] = xn
        h_ref[...] = jnp.dot(xn, w_ref[...], preferred_element_type=F32)

    return _call(
        body, name=name, grid=(t // tm,),
        in_specs=[pl.BlockSpec((tm, d), lambda i: (i, 0)),
                  pl.BlockSpec((1, d), lambda i: (0, 0)),
                  pl.BlockSpec((d, n), lambda i: (0, 0))],
        out_specs=[pl.BlockSpec((tm, d), lambda i: (i, 0)),
                   pl.BlockSpec((tm, n), lambda i: (i, 0))],
        out_shape=[jax.ShapeDtypeStruct((t, d), BF16), jax.ShapeDtypeStruct((t, n), F32)],
    )(x, gamma, w)


def _matmul_residual(name, a, w, res):
    t, kdim = a.shape
    n = w.shape[1]
    tm = _tile(t)

    def body(a_ref, w_ref, r_ref, o_ref):
        o_ref[...] = r_ref[...] + jnp.dot(a_ref[...], w_ref[...], preferred_element_type=F32)

    return _call(
        body, name=name, grid=(t // tm,),
        in_specs=[pl.BlockSpec((tm, kdim), lambda i: (i, 0)),
                  pl.BlockSpec((kdim, n), lambda i: (0, 0)),
                  pl.BlockSpec((tm, n), lambda i: (i, 0))],
        out_specs=pl.BlockSpec((tm, n), lambda i: (i, 0)),
        out_shape=jax.ShapeDtypeStruct((t, n), F32),
    )(a, w, res)


def _matmul_nt(name, dy, w):
    t, n = dy.shape
    kdim = w.shape[0]
    tm = _tile(t)

    def body(dy_ref, w_ref, da_ref, dyb_ref):
        dyb = dy_ref[...].astype(BF16)
        dyb_ref[...] = dyb
        da_ref[...] = lax.dot_general(dyb, w_ref[...], NT, preferred_element_type=F32)

    return _call(
        body, name=name, grid=(t // tm,),
        in_specs=[pl.BlockSpec((tm, n), lambda i: (i, 0)),
                  pl.BlockSpec((kdim, n), lambda i: (0, 0))],
        out_specs=[pl.BlockSpec((tm, kdim), lambda i: (i, 0)),
                   pl.BlockSpec((tm, n), lambda i: (i, 0))],
        out_shape=[jax.ShapeDtypeStruct((t, kdim), F32), jax.ShapeDtypeStruct((t, n), BF16)],
    )(dy, w)


def _matmul_nt_rms_bwd(name, dz, w, dres, x, gamma):
    t, kdim = dz.shape
    d = w.shape[0]
    tm = _tile(t)

    def body(dz_ref, w_ref, dres_ref, x_ref, gam_ref, dx_ref, dgam_ref):
        i = pl.program_id(0)

        @pl.when(i == 0)
        def _():
            dgam_ref[...] = jnp.zeros_like(dgam_ref)

        dxn = lax.dot_general(dz_ref[...], w_ref[...], NT, preferred_element_type=F32)
        xv = x_ref[...]
        r = lax.rsqrt(jnp.mean(xv * xv, axis=-1, keepdims=True) + EPS)
        yv = xv * r
        dgam_ref[...] += jnp.sum(dxn * yv, axis=0, keepdims=True)
        dyn = dxn * gam_ref[...]
        dx_ref[...] = dres_ref[...] + r * (dyn - yv * jnp.mean(dyn * yv, axis=-1, keepdims=True))

    return _call(
        body, name=name, grid=(t // tm,),
        in_specs=[pl.BlockSpec((tm, kdim), lambda i: (i, 0)),
                  pl.BlockSpec((d, kdim), lambda i: (0, 0)),
                  pl.BlockSpec((tm, d), lambda i: (i, 0)),
                  pl.BlockSpec((tm, d), lambda i: (i, 0)),
                  pl.BlockSpec((1, d), lambda i: (0, 0))],
        out_specs=[pl.BlockSpec((tm, d), lambda i: (i, 0)),
                   pl.BlockSpec((1, d), lambda i: (0, 0))],
        out_shape=[jax.ShapeDtypeStruct((t, d), F32), jax.ShapeDtypeStruct((1, d), F32)],
    )(dz, w, dres, x, gamma)


def _pool_means(uext_ref, pos, tm, g, win):
    cols = slice(g * GROUP, (g + 1) * GROUP)
    acc = uext_ref[pl.ds(HALO, tm), cols]
    for j in range(1, win):
        acc = acc + uext_ref[pl.ds(HALO - j, tm), cols]
    cnt = jnp.minimum(pos + 1, win).astype(F32)
    return acc / cnt - uext_ref[pl.ds(HALO, tm), cols]


def _conv_taps(gext_ref, cw_ref, cb_ref, start, rows):
    y = cb_ref[...] + cw_ref[0:1, :] * gext_ref[pl.ds(start, rows), :]
    for k in range(1, CONV_WIDTH):
        y = y + cw_ref[k:k + 1, :] * gext_ref[pl.ds(start + k, rows), :]
    return y


def _pool_conv_fwd(name, h, pool_w, pool_b, pool_scale, conv_w, conv_b, ln_g, ln_b):
    t, hw = h.shape
    pc = len(POOL_WINDOWS) * GROUP
    cc = (hw - pc) // 2
    tm = _tile(t)
    per = tm // HALO

    def body(h_ref, hp_ref, pw_ref, pb_ref, ps_ref, cw_ref, cb_ref, lg_ref, lb_ref, cat_ref, uext, gext):
        i = pl.program_id(0)
        keep = (i > 0).astype(F32)
        hp = hp_ref[...] * keep
        uext[0:HALO, :] = hp[:, :pc]
        uext[HALO:, :] = h_ref[:, :pc]
        gext[0:HALO, :] = hp[:, pc:pc + cc] * _sigmoid(hp[:, pc + cc:])
        gext[HALO:, :] = h_ref[:, pc:pc + cc] * _sigmoid(h_ref[:, pc + cc:])
        pos = i * tm + lax.broadcasted_iota(jnp.int32, (tm, 1), 0)
        for g, win in enumerate(POOL_WINDOWS):
            cols = slice(g * GROUP, (g + 1) * GROUP)
            pooled = _pool_means(uext, pos, tm, g, win)
            mixed = jnp.dot(pooled.astype(BF16), pw_ref[g].astype(BF16),
                            preferred_element_type=F32) + pb_ref[g:g + 1, :]
            cat_ref[:, cols] = (mixed * ps_ref[:, cols]).astype(BF16)
        y = _conv_taps(gext, cw_ref, cb_ref, HALO - (CONV_WIDTH - 1), tm)
        mu = jnp.mean(y, axis=-1, keepdims=True)
        dv = y - mu
        rstd = lax.rsqrt(jnp.mean(dv * dv, axis=-1, keepdims=True) + EPS)
        ln = dv * rstd * lg_ref[...] + lb_ref[...]
        cat_ref[:, pc:] = (ln * _sigmoid(ln)).astype(BF16)

    small = lambda a: pl.BlockSpec(a.shape, lambda i: (0,) * a.ndim)
    return _call(
        body, name=name, grid=(t // tm,),
        in_specs=[pl.BlockSpec((tm, hw), lambda i: (i, 0)),
                  pl.BlockSpec((HALO, hw), lambda i: (jnp.maximum(i * per - 1, 0), 0)),
                  small(pool_w), small(pool_b), small(pool_scale), small(conv_w), small(conv_b),
                  small(ln_g), small(ln_b)],
        out_specs=pl.BlockSpec((tm, pc + cc), lambda i: (i, 0)),
        out_shape=jax.ShapeDtypeStruct((t, pc + cc), BF16),
        scratch_shapes=[pltpu.VMEM((HALO + tm, pc), F32), pltpu.VMEM((HALO + tm, cc), F32)],
    )(h, h, pool_w, pool_b, pool_scale, conv_w, conv_b, ln_g, ln_b)


def _pool_conv_bwd(name, h, dcat, pool_w, pool_b, pool_scale, conv_w, conv_b, ln_g, ln_b, comm=None):
    t, hw = h.shape
    pc = len(POOL_WINDOWS) * GROUP
    cc = (hw - pc) // 2
    ng = len(POOL_WINDOWS)
    tm = _tile(t)
    per = tm // HALO
    nt = t // tm
    r2 = tm + HALO
    taps = CONV_WIDTH - 1

    def body(h_ref, hp_ref, hn_ref, dc_ref, dcn_ref, pw_ref, pb_ref, ps_ref, cw_ref, cb_ref, lg_ref, lb_ref,
             dh_ref, dpw_ref, dpb_ref, dps_ref, dcw_ref, dcb_ref, dlg_ref, dlb_ref,
             uext, gext, dcext, dqext, dycext):
        i = pl.program_id(0)

        @pl.when(i == 0)
        def _():
            for ref in (dpw_ref, dpb_ref, dps_ref, dcw_ref, dcb_ref, dlg_ref, dlb_ref):
                ref[...] = jnp.zeros_like(ref)

        keep_p = (i > 0).astype(F32)
        keep_n = (i < nt - 1).astype(F32)
        hp = hp_ref[...] * keep_p
        hn = hn_ref[...] * keep_n
        uext[0:HALO, :] = hp[:, :pc]
        uext[HALO:, :] = h_ref[:, :pc]
        gext[0:HALO, :] = hp[:, pc:pc + cc] * _sigmoid(hp[:, pc + cc:])
        gext[pl.ds(HALO, tm), :] = h_ref[:, pc:pc + cc] * _sigmoid(h_ref[:, pc + cc:])
        gext[pl.ds(HALO + tm, HALO), :] = hn[:, pc:pc + cc] * _sigmoid(hn[:, pc + cc:])
        dcext[0:tm, :] = dc_ref[...]
        dcext[pl.ds(tm, HALO), :] = dcn_ref[...] * keep_n

        pos = i * tm + lax.broadcasted_iota(jnp.int32, (tm, 1), 0)
        pos2 = i * tm + lax.broadcasted_iota(jnp.int32, (r2, 1), 0)
        for g, win in enumerate(POOL_WINDOWS):
            cols = slice(g * GROUP, (g + 1) * GROUP)
            wg = pw_ref[g].astype(BF16)
            dya = dcext[:, cols]
            dmixed = dya * ps_ref[:, cols]
            dpooled = lax.dot_general(dmixed.astype(BF16), wg, NT, preferred_element_type=F32)
            cnt2 = jnp.minimum(pos2 + 1, win).astype(F32)
            dqext[:, cols] = dpooled / cnt2
            du = -dpooled[0:tm]
            for j in range(win):
                du = du + dqext[pl.ds(j, tm), cols]
            dh_ref[:, cols] = du.astype(BF16)
            pooled = _pool_means(uext, pos, tm, g, win)
            pooled_b = pooled.astype(BF16)
            mixed = jnp.dot(pooled_b, wg, preferred_element_type=F32) + pb_ref[g:g + 1, :]
            dps_ref[:, cols] += jnp.sum(dya[0:tm] * mixed, axis=0, keepdims=True)
            dpb_ref[g:g + 1, :] += jnp.sum(dmixed[0:tm], axis=0, keepdims=True)
            dpw_ref[g] += lax.dot_general(pooled_b, dmixed[0:tm].astype(BF16), TN, preferred_element_type=F32)

        y = _conv_taps(gext, cw_ref, cb_ref, HALO - taps, r2)
        mu = jnp.mean(y, axis=-1, keepdims=True)
        dv = y - mu
        rstd = lax.rsqrt(jnp.mean(dv * dv, axis=-1, keepdims=True) + EPS)
        norm = dv * rstd
        ln = norm * lg_ref[...] + lb_ref[...]
        sig = _sigmoid(ln)
        dln = dcext[:, pc:] * (sig * (1.0 + ln * (1.0 - sig)))
        dnorm = dln * lg_ref[...]
        dyc = rstd * (dnorm - jnp.mean(dnorm, axis=-1, keepdims=True)
                      - norm * jnp.mean(dnorm * norm, axis=-1, keepdims=True))
        dycext[...] = dyc
        dlg_ref[...] += jnp.sum((dln * norm)[0:tm], axis=0, keepdims=True)
        dlb_ref[...] += jnp.sum(dln[0:tm], axis=0, keepdims=True)
        dcb_ref[...] += jnp.sum(dyc[0:tm], axis=0, keepdims=True)
        dyc_t = dycext[0:tm, :]
        dg = jnp.zeros((tm, cc), F32)
        for k in range(CONV_WIDTH):
            dcw_ref[k:k + 1, :] += jnp.sum(dyc_t * gext[pl.ds(HALO - taps + k, tm), :], axis=0, keepdims=True)
            dg = dg + cw_ref[k:k + 1, :] * dycext[pl.ds(taps - k, tm), :]
        a = h_ref[:, pc:pc + cc]
        sg = _sigmoid(h_ref[:, pc + cc:])
        dh_ref[:, pc:pc + cc] = (dg * sg).astype(BF16)
        dh_ref[:, pc + cc:] = (dg * a * sg * (1.0 - sg)).astype(BF16)

    small = lambda a: pl.BlockSpec(a.shape, lambda i: (0,) * a.ndim)
    smalls = (pool_w, pool_b, pool_scale, conv_w, conv_b, ln_g, ln_b)
    return _call(
        body, comm=comm, name=name, grid=(nt,),
        in_specs=[pl.BlockSpec((tm, hw), lambda i: (i, 0)),
                  pl.BlockSpec((HALO, hw), lambda i: (jnp.maximum(i * per - 1, 0), 0)),
                  pl.BlockSpec((HALO, hw), lambda i: (jnp.minimum((i + 1) * per, t // HALO - 1), 0)),
                  pl.BlockSpec((tm, pc + cc), lambda i: (i, 0)),
                  pl.BlockSpec((HALO, pc + cc), lambda i: (jnp.minimum((i + 1) * per, t // HALO - 1), 0)),
                  ] + [small(a) for a in smalls],
        out_specs=[pl.BlockSpec((tm, hw), lambda i: (i, 0))] + [small(a) for a in smalls],
        out_shape=[jax.ShapeDtypeStruct((t, hw), BF16)] + [jax.ShapeDtypeStruct(a.shape, F32) for a in smalls],
        scratch_shapes=[pltpu.VMEM((HALO + tm, pc), F32), pltpu.VMEM((HALO + tm + HALO, cc), F32),
                        pltpu.VMEM((r2, pc + cc), F32), pltpu.VMEM((r2, pc), F32), pltpu.VMEM((r2, cc), F32)],
    )(h, h, h, dcat, dcat, *smalls)


SQRT_HALF = 0.7071067811865476
INV_SQRT_2PI = 0.3989422804014327


def _sgu_core(zp_ref, lg_ref, lb_ref, ws_ref, bs_ref, vo_s, tm, sc, heads):
    zp = zp_ref[...]
    z = 0.5 * zp * (1.0 + lax.erf(zp * SQRT_HALF))
    u = z[:, :sc]
    v = z[:, sc:]
    mu = jnp.mean(v, axis=-1, keepdims=True)
    dv = v - mu
    rstd = lax.rsqrt(jnp.mean(dv * dv, axis=-1, keepdims=True) + EPS)
    norm = dv * rstd
    vb = (norm * lg_ref[...] + lb_ref[...]).astype(BF16)
    row = lax.broadcasted_iota(jnp.int32, (GROUP, GROUP), 0)
    col = lax.broadcasted_iota(jnp.int32, (GROUP, GROUP), 1)
    mask = (col <= row).astype(F32)
    wm = [ws_ref[hd] * mask for hd in range(heads)]
    for hd in range(heads):
        cols = slice(hd * GROUP, (hd + 1) * GROUP)
        wb = wm[hd].astype(BF16)
        for n in range(tm // GROUP):
            rows = slice(n * GROUP, (n + 1) * GROUP)
            vo_s[rows, cols] = jnp.dot(wb, vb[rows, cols], preferred_element_type=F32) + bs_ref[hd]
    return zp, u, norm, rstd, vb, wm, mask


def _sgu_fwd(name, zp, ln_g, ln_b, w_s, b_s):
    t, two_sc = zp.shape
    sc = two_sc // 2
    heads = sc // GROUP
    tm = _tile(t)

    def body(zp_ref, lg_ref, lb_ref, ws_ref, bs_ref, q_ref, vo_s):
        _, u, _, _, _, _, _ = _sgu_core(zp_ref, lg_ref, lb_ref, ws_ref, bs_ref, vo_s, tm, sc, heads)
        q_ref[...] = (u * vo_s[...]).astype(BF16)

    small = lambda a: pl.BlockSpec(a.shape, lambda i: (0,) * a.ndim)
    return _call(
        body, name=name, grid=(t // tm,),
        in_specs=[pl.BlockSpec((tm, two_sc), lambda i: (i, 0)), small(ln_g), small(ln_b), small(w_s), small(b_s)],
        out_specs=pl.BlockSpec((tm, sc), lambda i: (i, 0)),
        out_shape=jax.ShapeDtypeStruct((t, sc), BF16),
        scratch_shapes=[pltpu.VMEM((tm, sc), F32)],
    )(zp, ln_g, ln_b, w_s, b_s)


def _sgu_bwd(name, zp, dq, ln_g, ln_b, w_s, b_s):
    t, two_sc = zp.shape
    sc = two_sc // 2
    heads = sc // GROUP
    tm = _tile(t)
    nt = t // tm

    def body(zp_ref, dq_ref, lg_ref, lb_ref, ws_ref, bs_ref,
             dzp_ref, dlg_ref, dlb_ref, dws_ref, dbs_ref, vo_s, dvl_s, dws_acc):
        i = pl.program_id(0)

        @pl.when(i == 0)
        def _():
            dlg_ref[...] = jnp.zeros_like(dlg_ref)
            dlb_ref[...] = jnp.zeros_like(dlb_ref)
            dbs_ref[...] = jnp.zeros_like(dbs_ref)
            dws_acc[...] = jnp.zeros_like(dws_acc)

        zp, u, norm, rstd, vb, wm, mask = _sgu_core(zp_ref, lg_ref, lb_ref, ws_ref, bs_ref, vo_s, tm, sc, heads)
        dq = dq_ref[...]
        du = dq * vo_s[...]
        dvo = dq * u
        dvob = dvo.astype(BF16)
        for hd in range(heads):
            cols = slice(hd * GROUP, (hd + 1) * GROUP)
            wtb = jnp.transpose(wm[hd]).astype(BF16)
            for n in range(tm // GROUP):
                rows = slice(n * GROUP, (n + 1) * GROUP)
                blk = dvob[rows, cols]
                dws_acc[hd] += lax.dot_general(blk, vb[rows, cols], NT, preferred_element_type=F32)
                dvl_s[rows, cols] = jnp.dot(wtb, blk, preferred_element_type=F32)
                dbs_ref[hd] += jnp.sum(dvo[rows, cols], axis=-1, keepdims=True)
        dvl = dvl_s[...]
        dlg_ref[...] += jnp.sum(dvl * norm, axis=0, keepdims=True)
        dlb_ref[...] += jnp.sum(dvl, axis=0, keepdims=True)
        dnorm = dvl * lg_ref[...]
        dv = rstd * (dnorm - jnp.mean(dnorm, axis=-1, keepdims=True)
                     - norm * jnp.mean(dnorm * norm, axis=-1, keepdims=True))
        dgelu = 0.5 * (1.0 + lax.erf(zp * SQRT_HALF)) + zp * (INV_SQRT_2PI * jnp.exp(-0.5 * zp * zp))
        dzp_ref[:, :sc] = (du * dgelu[:, :sc]).astype(BF16)
        dzp_ref[:, sc:] = (dv * dgelu[:, sc:]).astype(BF16)

        @pl.when(i == nt - 1)
        def _():
            for hd in range(heads):
                dws_ref[hd] = dws_acc[hd] * mask

    small = lambda a: pl.BlockSpec(a.shape, lambda i: (0,) * a.ndim)
    smalls = (ln_g, ln_b, w_s, b_s)
    return _call(
        body, name=name, grid=(nt,),
        in_specs=[pl.BlockSpec((tm, two_sc), lambda i: (i, 0)), pl.BlockSpec((tm, sc), lambda i: (i, 0))]
                 + [small(a) for a in smalls],
        out_specs=[pl.BlockSpec((tm, two_sc), lambda i: (i, 0))] + [small(a) for a in smalls],
        out_shape=[jax.ShapeDtypeStruct((t, two_sc), BF16)] + [jax.ShapeDtypeStruct(a.shape, F32) for a in smalls],
        scratch_shapes=[pltpu.VMEM((tm, sc), F32), pltpu.VMEM((tm, sc), F32), pltpu.VMEM(w_s.shape, F32)],
    )(zp, dq, ln_g, ln_b, w_s, b_s)


def _loss_head(name, x, gamma, target):
    t, d = x.shape
    tm = _tile(t)

    def body(x_ref, gam_ref, tg_ref, loss_ref, dx_ref, dgam_ref):
        i = pl.program_id(0)

        @pl.when(i == 0)
        def _():
            loss_ref[...] = jnp.zeros_like(loss_ref)
            dgam_ref[...] = jnp.zeros_like(dgam_ref)

        xv = x_ref[...]
        r = lax.rsqrt(jnp.mean(xv * xv, axis=-1, keepdims=True) + EPS)
        yv = xv * r
        err = yv * gam_ref[...] - tg_ref[...]
        row = jnp.sum(err * err, axis=-1, keepdims=True)
        loss_ref[...] += (0.5 / d) * jnp.sum(row, axis=0, keepdims=True)
        dout = err * (1.0 / d)
        dgam_ref[...] += jnp.sum(dout * yv, axis=0, keepdims=True)
        dyn = dout * gam_ref[...]
        dx_ref[...] = r * (dyn - yv * jnp.mean(dyn * yv, axis=-1, keepdims=True))

    return _call(
        body, name=name, grid=(t // tm,),
        in_specs=[pl.BlockSpec((tm, d), lambda i: (i, 0)),
                  pl.BlockSpec((1, d), lambda i: (0, 0)),
                  pl.BlockSpec((tm, d), lambda i: (i, 0))],
        out_specs=[pl.BlockSpec((1, 1), lambda i: (0, 0)),
                   pl.BlockSpec((tm, d), lambda i: (i, 0)),
                   pl.BlockSpec((1, d), lambda i: (0, 0))],
        out_shape=[jax.ShapeDtypeStruct((1, 1), F32), jax.ShapeDtypeStruct((t, d), F32),
                   jax.ShapeDtypeStruct((1, d), F32)],
    )(x, gamma, target)


def _adamw(name, parts, w, m, v, rows):
    l_n, r_n, c_n = w.shape
    s_n = parts[0].shape[0]
    tr = min(rows, r_n)
    nr = r_n // tr
    c1 = 1.0 - ADAM_B1 ** ADAM_STEP
    c2 = 1.0 - ADAM_B2 ** ADAM_STEP

    def body(*refs):
        p_refs = refs[:l_n]
        w_ref, m_ref, v_ref, g_ref, d_ref, mo_ref, vo_ref = refs[l_n:]
        layer = pl.program_id(0)

        def update(p_ref):
            g = p_ref[0].astype(F32)
            for s in range(1, s_n):
                g = g + p_ref[s].astype(F32)
            mn = ADAM_B1 * m_ref[...] + (1.0 - ADAM_B1) * g
            vn = ADAM_B2 * v_ref[...] + (1.0 - ADAM_B2) * (g * g)
            m_hat = mn / c1
            v_hat = vn / c2
            g_ref[...] = g
            d_ref[...] = -ADAM_LR * (m_hat / (jnp.sqrt(v_hat) + ADAM_EPS) + ADAM_WD * w_ref[...])
            mo_ref[...] = mn
            vo_ref[...] = vn

        for j in range(l_n):
            pl.when(layer == j)(functools.partial(update, p_refs[j]))

    def part_spec(j):
        return pl.BlockSpec((s_n, tr, c_n), lambda l, i: (0, jnp.where(l == j, i, jnp.where(l < j, 0, nr - 1)), 0))

    blk = pl.BlockSpec((None, tr, c_n), lambda l, i: (l, i, 0))
    return _call(
        body, name=name, grid=(l_n, nr),
        in_specs=[part_spec(j) for j in range(l_n)] + [blk, blk, blk],
        out_specs=[blk] * 4,
        out_shape=[jax.ShapeDtypeStruct((l_n, r_n, c_n), F32)] * 4,
    )(*parts, w, m, v)


def _local_step(x, target, big, small, sched=None):
    t, d = x.shape
    n_layers = small["ffn1_norm"].shape[0]
    gb = {}
    gs = {}

    def row(a, l):
        return a[l:l + 1]

    def run(fn, name, *operands):
        comm = sched.plan(name, gb, gs) if sched is not None else None
        if comm is None:
            return fn(name, *operands)
        res, got = fn(name, *operands, comm=comm)
        sched.deliver(comm, got)
        return res

    saved = []
    xs = x
    for l in range(n_layers):
        rec = {"x_ffn1": xs}
        xs, rec["xn_ffn1"], rec["gu_ffn1"] = run(
            _ffn_fwd, f"ffn1_fwd_l{l}", xs, row(small["ffn1_norm"], l), big["ffn1_w_in", l], big["ffn1_w_out", l])
        rec["x_mix"] = xs
        if l % 2 == 0:
            rec["xn_mix"], rec["h"] = _rms_matmul(f"ab_in_l{l}", xs, row(small["mix_norm"], l), big["ab_w_in"])
            rec["cat"] = _pool_conv_fwd(f"pool_conv_fwd_l{l}", rec["h"], small["pool_w"], small["pool_b"],
                                        small["pool_scale"], small["conv_w"], small["conv_b"],
                                        small["conv_ln_g"], small["conv_ln_b"])
            xs = _matmul_residual(f"ab_out_l{l}", rec["cat"], big["ab_w_out"], xs)
        else:
            rec["xn_mix"], rec["zp"] = _rms_matmul(f"sgu_in_l{l}", xs, row(small["mix_norm"], l), big["sgu_w_in"])
            rec["q"] = _sgu_fwd(f"sgu_fwd_l{l}", rec["zp"], small["sgu_ln_g"], small["sgu_ln_b"],
                                small["sgu_w"], small["sgu_b"])
            xs = _matmul_residual(f"sgu_out_l{l}", rec["q"], big["sgu_w_out"], xs)
        rec["x_ffn2"] = xs
        xs, rec["xn_ffn2"], rec["gu_ffn2"] = run(
            _ffn_fwd, f"ffn2_fwd_l{l}", xs, row(small["ffn2_norm"], l), big["ffn2_w_in", l], big["ffn2_w_out", l])
        saved.append(rec)

    loss, dx, gs["final_norm"] = _loss_head("loss_head", xs, small["final_norm"], target)

    norm_rows = {"ffn1_norm": [None] * n_layers, "mix_norm": [None] * n_layers, "ffn2_norm": [None] * n_layers}

    def ffn_backward(tag, l, dx, rec):
        dx, dgam, dyh, hh, dgu = run(_ffn_bwd, f"{tag}_bwd_l{l}", dx, rec[f"x_{tag}"], row(small[f"{tag}_norm"], l),
                                     rec[f"gu_{tag}"], big[f"{tag}_w_in", l], big[f"{tag}_w_out", l])
        norm_rows[f"{tag}_norm"][l] = dgam
        fb = dgu.shape[-1]
        gb[f"{tag}_w_out", l] = run(_matmul_tn, f"{tag}_dwout_l{l}", hh, dyh).reshape(N_DEV, fb // 2, d)
        gb[f"{tag}_w_in", l] = run(_matmul_tn, f"{tag}_dwin_l{l}", dgu.reshape(-1, t, fb), rec[f"xn_{tag}"])
        return dx

    for l in reversed(range(n_layers)):
        rec = saved[l]
        dx = ffn_backward("ffn2", l, dx, rec)
        if l % 2 == 0:
            dcat, dxb = _matmul_nt(f"ab_out_bwd_l{l}", dx, big["ab_w_out"])
            gb["ab_w_out", 0] = _matmul_tn(f"ab_dwout_l{l}", rec["cat"], dxb)
            dh, gs["pool_w"], gs["pool_b"], gs["pool_scale"], gs["conv_w"], gs["conv_b"], gs["conv_ln_g"], \
                gs["conv_ln_b"] = run(
                    _pool_conv_bwd, f"pool_conv_bwd_l{l}", rec["h"], dcat, small["pool_w"], small["pool_b"],
                    small["pool_scale"], small["conv_w"], small["conv_b"], small["conv_ln_g"], small["conv_ln_b"])
            gb["ab_w_in", 0] = _matmul_tn(f"ab_dwin_l{l}", rec["xn_mix"], dh)
            dx, dgam = _matmul_nt_rms_bwd(f"ab_in_bwd_l{l}", dh, big["ab_w_in"], dx, rec["x_mix"],
                                          row(small["mix_norm"], l))
        else:
            dq, dxb = _matmul_nt(f"sgu_out_bwd_l{l}", dx, big["sgu_w_out"])
            gb["sgu_w_out", 0] = _matmul_tn(f"sgu_dwout_l{l}", rec["q"], dxb)
            dzp, gs["sgu_ln_g"], gs["sgu_ln_b"], gs["sgu_w"], gs["sgu_b"] = _sgu_bwd(
                f"sgu_bwd_l{l}", rec["zp"], dq, small["sgu_ln_g"], small["sgu_ln_b"], small["sgu_w"], small["sgu_b"])
            gb["sgu_w_in", 0] = _matmul_tn(f"sgu_dwin_l{l}", rec["xn_mix"], dzp)
            dx, dgam = _matmul_nt_rms_bwd(f"sgu_in_bwd_l{l}", dzp, big["sgu_w_in"], dx, rec["x_mix"],
                                          row(small["mix_norm"], l))
        norm_rows["mix_norm"][l] = dgam
        dx = ffn_backward("ffn1", l, dx, rec)

    for k, rows in norm_rows.items():
        gs[k] = jnp.concatenate(rows, axis=0)
    return loss, dx, gb, gs


SHARDED_SMALL = ("conv_w", "sgu_ln_g", "sgu_ln_b")
REPLICATED = ("ffn1_norm", "mix_norm", "ffn2_norm", "pool_w", "pool_b", "pool_scale", "conv_b", "conv_ln_g",
              "conv_ln_b", "sgu_w", "sgu_b", "final_norm")
WEIGHTS = ("ffn1_norm", "ffn1_w_in", "ffn1_w_out", "mix_norm", "ffn2_norm", "ffn2_w_in", "ffn2_w_out", "ab_w_in",
           "pool_w", "pool_b", "pool_scale", "conv_w", "conv_b", "conv_ln_g", "conv_ln_b", "ab_w_out", "sgu_w_in",
           "sgu_ln_g", "sgu_ln_b", "sgu_w", "sgu_b", "sgu_w_out", "final_norm")
LANES = 128


def _interleave_cols(g):
    n, k, c = g.shape
    return jnp.transpose(g, (1, 0, 2)).reshape(k, n * c)


def _split_cols(a):
    k, nc = a.shape
    return jnp.transpose(a.reshape(k, N_DEV, nc // N_DEV), (1, 0, 2))


def _as3(a):
    if a.ndim == 1:
        return a.reshape(1, 1, -1)
    if a.ndim == 2:
        return a.reshape(a.shape[0], 1, a.shape[1])
    return a.reshape(a.shape[0], -1, a.shape[-1])


def _pack_rows(a):
    flat = a.reshape(-1)
    pad = (-flat.shape[0]) % (8 * LANES)
    if pad:
        flat = jnp.concatenate([flat, jnp.zeros((pad,), flat.dtype)])
    return flat.reshape(-1, LANES)


FIRST_GATHER = (("ffn1_w_in", 0), ("ffn1_w_out", 0), ("conv_w", 0), ("sgu_ln_g", 0), ("sgu_ln_b", 0))
GATHER_PLAN = {
    "ffn1_fwd_l0": (("ab_w_in", 0), ("ab_w_out", 0), ("ffn2_w_in", 0), ("ffn2_w_out", 0)),
    "ffn2_fwd_l0": (("ffn1_w_in", 1), ("ffn1_w_out", 1), ("sgu_w_in", 0), ("sgu_w_out", 0)),
    "ffn1_fwd_l1": (("ffn2_w_in", 1), ("ffn2_w_out", 1)),
}
SCATTER_PLAN = {
    "ffn1_bwd_l1": (("ffn2_w_in", 1), ("ffn2_w_out", 1)),
    "ffn2_bwd_l0": (("sgu_w_in", 0), ("sgu_w_out", 0), ("sgu_ln_g", 0), ("sgu_ln_b", 0),
                    ("ffn1_w_in", 1), ("ffn1_w_out", 1)),
    "pool_conv_bwd_l0": (("ffn2_w_in", 0), ("ffn2_w_out", 0)),
    "ffn1_dwout_l0": (("ab_w_in", 0), ("ab_w_out", 0), ("conv_w", 0)),
    "ffn1_dwin_l0": (("ffn1_w_out", 0),),
}
LAST_SCATTER = (("ffn1_w_in", 0),)


class _Schedule:
    def __init__(self, shards, big, small):
        self.shards, self.big, self.small = shards, big, small
        self.recv = {}
        self.pending = {}

    def gather_comm(self, keys):
        comm = _Comm()
        for key in keys:
            comm.gather(*self.shards[key])
        self.pending[id(comm)] = ("gather", keys)
        return comm

    def scatter_comm(self, keys, gb, gs):
        comm = _Comm()
        for name, l in keys:
            if name in ("ab_w_in", "sgu_w_in"):
                send = _split_cols(gb[name, l][0])
            elif name in ("ab_w_out", "sgu_w_out"):
                send = gb[name, l][0]
                send = send.reshape(N_DEV, -1, send.shape[-1])
            elif name == "conv_w":
                send = _split_cols(gs[name][:CONV_WIDTH])
            elif name in ("sgu_ln_g", "sgu_ln_b"):
                send = gs[name].reshape(N_DEV, 1, -1)
            else:
                send = gb[name, l]
            comm.scatter(send)
        self.pending[id(comm)] = ("scatter", keys)
        return comm

    def plan(self, name, gb, gs):
        if name in GATHER_PLAN:
            return self.gather_comm(GATHER_PLAN[name])
        if name in SCATTER_PLAN:
            return self.scatter_comm(SCATTER_PLAN[name], gb, gs)
        return None

    def deliver(self, comm, got):
        kind, keys = self.pending.pop(id(comm))
        for (name, l), arr in zip(keys, got):
            if kind == "scatter":
                self.recv[name, l] = arr
            elif name in ("ffn1_w_in", "ffn2_w_in"):
                self.big[name, l] = arr.reshape((2, N_DEV // 2) + arr.shape[1:])
            elif name in ("ffn1_w_out", "ffn2_w_out"):
                self.big[name, l] = arr
            elif name in ("ab_w_in", "sgu_w_in"):
                self.big[name] = _interleave_cols(arr)
            elif name in ("ab_w_out", "sgu_w_out"):
                self.big[name] = arr.reshape(-1, arr.shape[-1])
            elif name == "conv_w":
                self.small[name] = jnp.pad(_interleave_cols(arr), ((0, 1), (0, 0)))
            else:
                self.small[name] = arr.reshape(1, -1)


def kernel(x, ffn1_norm, ffn1_w_in, ffn1_w_out, mix_norm, ffn2_norm, ffn2_w_in, ffn2_w_out, ab_w_in, pool_w, pool_b, pool_scale, conv_w, conv_b, conv_ln_g, conv_ln_b, ab_w_out, sgu_w_in, sgu_ln_g, sgu_ln_b, sgu_w, sgu_b, sgu_w_out, final_norm, loss_target, m_ffn1_norm, m_ffn1_w_in, m_ffn1_w_out, m_mix_norm, m_ffn2_norm, m_ffn2_w_in, m_ffn2_w_out, m_ab_w_in, m_pool_w, m_pool_b, m_pool_scale, m_conv_w, m_conv_b, m_conv_ln_g, m_conv_ln_b, m_ab_w_out, m_sgu_w_in, m_sgu_ln_g, m_sgu_ln_b, m_sgu_w, m_sgu_b, m_sgu_w_out, m_final_norm, v_ffn1_norm, v_ffn1_w_in, v_ffn1_w_out, v_mix_norm, v_ffn2_norm, v_ffn2_w_in, v_ffn2_w_out, v_ab_w_in, v_pool_w, v_pool_b, v_pool_scale, v_conv_w, v_conv_b, v_conv_ln_g, v_conv_ln_b, v_ab_w_out, v_sgu_w_in, v_sgu_ln_g, v_sgu_ln_b, v_sgu_w, v_sgu_b, v_sgu_w_out, v_final_norm):
    args = dict(locals())
    w = {n: args[n] for n in WEIGHTS}
    m = {n: args["m_" + n] for n in WEIGHTS}
    v = {n: args["v_" + n] for n in WEIGHTS}
    n_layers = ffn1_norm.shape[0]

    shards = {}
    for n in ("ffn1_w_in", "ffn2_w_in"):
        wt = jnp.swapaxes(w[n], 1, 2).astype(BF16)
        for l in range(n_layers):
            shards[n, l] = (wt, l)
    for n in ("ffn1_w_out", "ffn2_w_out"):
        wb = w[n].astype(BF16)
        for l in range(n_layers):
            shards[n, l] = (wb, l)
    for n in ("ab_w_in", "ab_w_out", "sgu_w_in", "sgu_w_out"):
        shards[n, 0] = (w[n][0].astype(BF16), None)
    shards["conv_w", 0] = (conv_w[0], None)
    shards["sgu_ln_g", 0] = (sgu_ln_g, None)
    shards["sgu_ln_b", 0] = (sgu_ln_b, None)

    big = {}
    small = {
        "ffn1_norm": ffn1_norm, "mix_norm": mix_norm, "ffn2_norm": ffn2_norm, "final_norm": final_norm.reshape(1, -1),
        "pool_w": pool_w[0], "pool_b": pool_b[0], "pool_scale": pool_scale,
        "conv_b": conv_b, "conv_ln_g": conv_ln_g, "conv_ln_b": conv_ln_b,
        "sgu_w": sgu_w[0], "sgu_b": sgu_b[0][:, :, None],
    }
    sched = _Schedule(shards, big, small)
    first = sched.gather_comm(FIRST_GATHER)
    sched.deliver(first, _exchange("gather_first", first))

    loss, grad_x, gb, gs = _local_step(x[0], loss_target[0], big, small, sched)

    rep_grads = {
        "ffn1_norm": gs["ffn1_norm"], "mix_norm": gs["mix_norm"], "ffn2_norm": gs["ffn2_norm"],
        "pool_w": gs["pool_w"], "pool_b": gs["pool_b"], "pool_scale": gs["pool_scale"],
        "conv_b": gs["conv_b"], "conv_ln_g": gs["conv_ln_g"], "conv_ln_b": gs["conv_ln_b"],
        "sgu_w": gs["sgu_w"], "sgu_b": gs["sgu_b"], "final_norm": gs["final_norm"],
    }
    packs = [_pack_rows(rep_grads[n]) for n in REPLICATED] + [_pack_rows(loss)]
    offsets = [0]
    for p in packs:
        offsets.append(offsets[-1] + p.shape[0])
    packed = jnp.concatenate(packs, axis=0)
    n_rows = packed.shape[0]

    last = sched.scatter_comm(LAST_SCATTER, gb, gs)
    packed_id = last.gather(packed)
    got = _exchange("reduce_last", last)
    packed_all = got[packed_id]
    sched.deliver(last, got[:packed_id])
    recv = sched.recv

    out = {}
    for n in ("ffn1_w_in", "ffn2_w_in"):
        res = _adamw(f"adamw_{n}", [recv[n, l] for l in range(n_layers)], jnp.swapaxes(w[n], 1, 2),
                     jnp.swapaxes(m[n], 1, 2), jnp.swapaxes(v[n], 1, 2), 176)
        out[n] = [jnp.swapaxes(r, 1, 2) for r in res]
    for n in ("ffn1_w_out", "ffn2_w_out"):
        out[n] = _adamw(f"adamw_{n}", [recv[n, l] for l in range(n_layers)], w[n], m[n], v[n], 176)
    for n in ("ab_w_in", "ab_w_out", "sgu_w_in", "sgu_w_out") + SHARDED_SMALL:
        w3 = _as3(w[n])
        parts = recv[n, 0].reshape((N_DEV,) + w3.shape[1:])
        res = _adamw(f"adamw_{n}", [parts], w3, _as3(m[n]), _as3(v[n]), 512)
        out[n] = [r.reshape(w[n].shape) for r in res]

    def pack_rep(src):
        tail = [jnp.zeros((offsets[-1] - offsets[-2], LANES), F32)]
        return jnp.concatenate([_pack_rows(src[n]) for n in REPLICATED] + tail, axis=0)[None]

    res = _adamw("adamw_replicated", [packed_all], pack_rep(w), pack_rep(m), pack_rep(v), n_rows)
    for i, n in enumerate(REPLICATED):
        size = w[n].size
        out[n] = [r[0, offsets[i]:offsets[i + 1]].reshape(-1)[:size].reshape(w[n].shape) for r in res]
    loss_sum = res[0][0, offsets[-2], 0]

    return (loss_sum, grad_x[None],
            *[out[n][0] for n in WEIGHTS], *[out[n][1] for n in WEIGHTS],
            *[out[n][2] for n in WEIGHTS], *[out[n][3] for n in WEIGHTS])
```

```python
import functools

import jax
import jax.numpy as jnp
from jax import lax
from jax.experimental import pallas as pl
from jax.experimental.pallas import tpu as pltpu

F32 = jnp.float32
BF16 = jnp.bfloat16
EPS = 1e-6
N_DEV = 8
POOL_WINDOWS = (2, 4, 8, 16)
CONV_WIDTH = 31
HALO = 32
GROUP = 128
TOKEN_TILE = 512
CONTRACT_TILE = 2048
HIDDEN_SPLIT = 2
FFN_BWD_TILE = 256
ADAM_LR, ADAM_B1, ADAM_B2, ADAM_EPS, ADAM_WD, ADAM_STEP = 0.001, 0.9, 0.999, 1e-08, 0.01, 10
VMEM_LIMIT = 56 * 1024 * 1024

NT = (((1,), (1,)), ((), ()))
TN = (((0,), (0,)), ((), ()))


def _pallas(body, side_effects, **kw):
    params = pltpu.CompilerParams(vmem_limit_bytes=VMEM_LIMIT, has_side_effects=side_effects)
    return pl.pallas_call(body, compiler_params=params, **kw)


def _call(body, comm=None, **kw):
    if comm is None:
        return _pallas(body, False, **kw)
    in_specs = list(kw.pop("in_specs"))
    out_specs = kw.pop("out_specs")
    out_shape = kw.pop("out_shape")
    scratch = list(kw.pop("scratch_shapes", []))
    single = not isinstance(out_shape, (list, tuple))
    if single:
        out_specs, out_shape = [out_specs], [out_shape]
    n_in, n_out, n_scr = len(in_specs), len(out_shape), len(scratch)
    n_ci, n_co = len(comm.inputs), len(comm.out_shapes)
    grid = tuple(kw.get("grid", ()))

    def wrapped(*refs):
        pos = 0
        parts = []
        for n in (n_in, n_ci, n_out, n_co, n_scr, 3):
            parts.append(refs[pos:pos + n])
            pos += n
        a_in, c_in, a_out, c_out, a_scr, sems = parts
        if grid:
            ids = [pl.program_id(ax) for ax in range(len(grid))]
            first = functools.reduce(lambda p, q: p & q, [i == 0 for i in ids])
            last = functools.reduce(lambda p, q: p & q, [i == g - 1 for i, g in zip(ids, grid)])
            pl.when(first)(lambda: comm.start(c_in, c_out, sems))
            body(*a_in, *a_out, *a_scr)
            pl.when(last)(lambda: comm.wait(c_in, c_out, sems))
        else:
            comm.start(c_in, c_out, sems)
            body(*a_in, *a_out, *a_scr)
            comm.wait(c_in, c_out, sems)

    hbm = pl.BlockSpec(memory_space=pl.ANY)
    fn = _pallas(wrapped, True, in_specs=in_specs + [hbm] * n_ci, out_specs=list(out_specs) + [hbm] * n_co,
                 out_shape=list(out_shape) + list(comm.out_shapes), scratch_shapes=scratch + comm.semaphores(), **kw)

    def run(*operands):
        outs = fn(*operands, *comm.inputs)
        res = outs[:n_out]
        return (res[0] if single else res), outs[n_out:]

    return run


class _Comm:
    def __init__(self):
        self.inputs, self.sel, self.kinds, self.out_shapes = [], [], [], []

    def gather(self, arr, sel=None):
        block = arr.shape if sel is None else arr.shape[1:]
        self.inputs.append(arr)
        self.sel.append(sel)
        self.kinds.append("gather")
        self.out_shapes.append(jax.ShapeDtypeStruct((N_DEV,) + tuple(block), arr.dtype))
        return len(self.inputs) - 1

    def scatter(self, arr):
        self.inputs.append(arr)
        self.sel.append(None)
        self.kinds.append("scatter")
        self.out_shapes.append(jax.ShapeDtypeStruct(arr.shape, arr.dtype))
        return len(self.inputs) - 1

    def semaphores(self):
        n = len(self.inputs)
        return [pltpu.SemaphoreType.DMA((n, N_DEV - 1)), pltpu.SemaphoreType.DMA((n, N_DEV - 1)),
                pltpu.SemaphoreType.DMA((n,))]

    def _copies(self, ins, outs, sems, with_passed=True):
        send_sems, recv_sems, local_sems = sems
        x, y, c = lax.axis_index("x"), lax.axis_index("y"), lax.axis_index("c")
        me = 4 * x + 2 * y + c
        sibling = (x, y, 1 - c)
        chips = [(1 - x, y), (x, 1 - y), (1 - x, 1 - y)]
        items = []
        for a, kind in enumerate(self.kinds):
            def remote(src, dst, k, to, a=a):
                return pltpu.make_async_remote_copy(
                    src_ref=src, dst_ref=dst, send_sem=send_sems.at[a, k], recv_sem=recv_sems.at[a, k],
                    device_id=to, device_id_type=pl.DeviceIdType.MESH)
            if kind == "gather":
                src = ins[a] if self.sel[a] is None else ins[a].at[self.sel[a]]
                mine = outs[a].at[me]
                local = pltpu.make_async_copy(src, mine, local_sems.at[a])
                first = [remote(src, mine, 0, sibling)]
                first += [remote(src, mine, 1 + j, (*chip, c)) for j, chip in enumerate(chips)]
                passed = []
                for j, chip in enumerate(chips if with_passed else []):
                    got = outs[a].at[4 * chip[0] + 2 * chip[1] + c]
                    passed.append(remote(got, got, 4 + j, sibling))
            else:
                local = pltpu.make_async_copy(ins[a].at[me], outs[a].at[me], local_sems.at[a])
                first, passed = [], []
                for k in (1, 4, 2, 6, 5, 3, 7):
                    peer = ((1 - x) if k & 4 else x, (1 - y) if k & 2 else y, (1 - c) if k & 1 else c)
                    pid = 4 * peer[0] + 2 * peer[1] + peer[2]
                    first.append(remote(ins[a].at[pid], outs[a].at[me], k - 1, peer))
            items.append((local, first, passed))
        return items

    def start(self, ins, outs, sems):
        for local, first, _ in self._copies(ins, outs, sems, with_passed=False):
            local.start()
            for cp in first:
                cp.start()

    def wait(self, ins, outs, sems):
        items = self._copies(ins, outs, sems)
        for _, first, passed in items:
            for j, cp in enumerate(passed):
                first[1 + j].wait_recv()
                cp.start()
        for local, first, passed in items:
            if passed:
                first[0].wait_recv()
                for cp in passed:
                    cp.wait_recv()
                for cp in first + passed:
                    cp.wait_send()
            else:
                for cp in first:
                    cp.wait()
            local.wait()


def _exchange(name, comm):
    _, outs = _call(lambda: None, comm=comm, name=name, in_specs=[], out_specs=[], out_shape=[])()
    return outs


def _sigmoid(x):
    return 1.0 / (1.0 + jnp.exp(-x))


def _tile(t):
    return min(TOKEN_TILE, t)


def _ffn_fwd(name, x, gamma, wgu, wout, comm=None):
    t, d = x.shape
    fb = wgu.shape[-2]
    nk = wgu.shape[1]
    tm = _tile(t)

    def body(x_ref, gam_ref, wgu_ref, wo_ref, xo_ref, xn_ref, gu_ref, xn_s, acc):
        k = pl.program_id(1)

        @pl.when(k == 0)
        def _():
            xv = x_ref[...]
            r = lax.rsqrt(jnp.mean(xv * xv, axis=-1, keepdims=True) + EPS)
            xn = (xv * r * gam_ref[...]).astype(BF16)
            xn_s[...] = xn
            xn_ref[...] = xn
            acc[...] = jnp.zeros_like(acc)

        xn = xn_s[...]
        g = lax.dot_general(xn, wgu_ref[0], NT, preferred_element_type=F32)
        u = lax.dot_general(xn, wgu_ref[1], NT, preferred_element_type=F32)
        gu_ref[0] = g.astype(BF16)
        gu_ref[1] = u.astype(BF16)
        h = (g * _sigmoid(g) * u).astype(BF16)
        acc[...] += jnp.dot(h, wo_ref[...], preferred_element_type=F32)

        @pl.when(k == nk - 1)
        def _():
            xo_ref[...] = x_ref[...] + 0.5 * acc[...]

    return _call(
        body, comm=comm, name=name, grid=(t // tm, nk),
        in_specs=[
            pl.BlockSpec((tm, d), lambda i, k: (i, 0)),
            pl.BlockSpec((1, d), lambda i, k: (0, 0)),
            pl.BlockSpec((2, None, fb, d), lambda i, k: (0, k, 0, 0)),
            pl.BlockSpec((fb, d), lambda i, k: (k, 0)),
        ],
        out_specs=[
            pl.BlockSpec((tm, d), lambda i, k: (i, 0)),
            pl.BlockSpec((tm, d), lambda i, k: (i, 0)),
            pl.BlockSpec((2, None, tm, fb), lambda i, k: (0, k, i, 0)),
        ],
        out_shape=[
            jax.ShapeDtypeStruct((t, d), F32),
            jax.ShapeDtypeStruct((t, d), BF16),
            jax.ShapeDtypeStruct((2, nk, t, fb), BF16),
        ],
        scratch_shapes=[pltpu.VMEM((tm, d), BF16), pltpu.VMEM((tm, d), F32)],
    )(x, gamma, wgu, wout)


def _ffn_bwd(name, dy, x, gamma, gu, wgu, wout, comm=None):
    t, d = x.shape
    fb = wgu.shape[-2]
    nk = wgu.shape[1]
    tm = min(FFN_BWD_TILE, t)

    def body(dy_ref, x_ref, gam_ref, gu_ref, wgu_ref, wo_ref,
             dx_ref, dgam_ref, dyh_ref, h_ref, dgu_ref, dyb_s, acc):
        i = pl.program_id(0)
        k = pl.program_id(1)

        @pl.when(k == 0)
        def _():
            dyb = (0.5 * dy_ref[...]).astype(BF16)
            dyb_s[...] = dyb
            dyh_ref[...] = dyb
            acc[...] = jnp.zeros_like(acc)

        @pl.when((i == 0) & (k == 0))
        def _():
            dgam_ref[...] = jnp.zeros_like(dgam_ref)

        dyb = dyb_s[...]
        dh = lax.dot_general(dyb, wo_ref[...], NT, preferred_element_type=F32)
        g = gu_ref[0].astype(F32)
        u = gu_ref[1].astype(F32)
        sig = _sigmoid(g)
        silu = g * sig
        h_ref[...] = (silu * u).astype(BF16)
        dg = (dh * u * (sig * (1.0 + g * (1.0 - sig)))).astype(BF16)
        du = (dh * silu).astype(BF16)
        dgu_ref[0] = dg
        dgu_ref[1] = du
        acc[...] += (jnp.dot(dg, wgu_ref[0], preferred_element_type=F32)
                     + jnp.dot(du, wgu_ref[1], preferred_element_type=F32))

        @pl.when(k == nk - 1)
        def _():
            xv = x_ref[...]
            r = lax.rsqrt(jnp.mean(xv * xv, axis=-1, keepdims=True) + EPS)
            yv = xv * r
            dxn = acc[...]
            dgam_ref[...] += jnp.sum(dxn * yv, axis=0, keepdims=True)
            dyn = dxn * gam_ref[...]
            dx_ref[...] = dy_ref[...] + r * (dyn - yv * jnp.mean(dyn * yv, axis=-1, keepdims=True))

    return _call(
        body, comm=comm, name=name, grid=(t // tm, nk),
        in_specs=[
            pl.BlockSpec((tm, d), lambda i, k: (i, 0)),
            pl.BlockSpec((tm, d), lambda i, k: (i, 0)),
            pl.BlockSpec((1, d), lambda i, k: (0, 0)),
            pl.BlockSpec((2, None, tm, fb), lambda i, k: (0, k, i, 0)),
            pl.BlockSpec((2, None, fb, d), lambda i, k: (0, k, 0, 0)),
            pl.BlockSpec((fb, d), lambda i, k: (k, 0)),
        ],
        out_specs=[
            pl.BlockSpec((tm, d), lambda i, k: (i, 0)),
            pl.BlockSpec((1, d), lambda i, k: (0, 0)),
            pl.BlockSpec((tm, d), lambda i, k: (i, 0)),
            pl.BlockSpec((None, tm, fb), lambda i, k: (k, i, 0)),
            pl.BlockSpec((2, None, tm, fb), lambda i, k: (0, k, i, 0)),
        ],
        out_shape=[
            jax.ShapeDtypeStruct((t, d), F32),
            jax.ShapeDtypeStruct((1, d), F32),
            jax.ShapeDtypeStruct((t, d), BF16),
            jax.ShapeDtypeStruct((nk, t, fb), BF16),
            jax.ShapeDtypeStruct((2, nk, t, fb), BF16),
        ],
        scratch_shapes=[pltpu.VMEM((tm, d), BF16), pltpu.VMEM((tm, d), F32)],
    )(dy, x, gamma, gu, wgu, wout)


def _matmul_tn(name, a, b, out_dtype=BF16, comm=None):
    a_b = a.ndim == 3
    b_b = b.ndim == 3
    nb = a.shape[0] if a_b else b.shape[0] if b_b else 1
    t, m = a.shape[-2:]
    n = b.shape[-1]
    tk = min(CONTRACT_TILE, t)
    nt = t // tk

    def body(a_ref, b_ref, o_ref, acc):
        s = pl.program_id(1)

        @pl.when(s == 0)
        def _():
            acc[...] = jnp.zeros_like(acc)

        acc[...] += lax.dot_general(a_ref[...], b_ref[...], TN, preferred_element_type=F32)

        @pl.when(s == nt - 1)
        def _():
            o_ref[...] = acc[...].astype(o_ref.dtype)

    a_spec = (pl.BlockSpec((None, tk, m), lambda j, s: (j, s, 0)) if a_b
              else pl.BlockSpec((tk, m), lambda j, s: (s, 0)))
    b_spec = (pl.BlockSpec((None, tk, n), lambda j, s: (j, s, 0)) if b_b
              else pl.BlockSpec((tk, n), lambda j, s: (s, 0)))
    return _call(
        body, comm=comm, name=name, grid=(nb, nt),
        in_specs=[a_spec, b_spec],
        out_specs=pl.BlockSpec((None, m, n), lambda j, s: (j, 0, 0)),
        out_shape=jax.ShapeDtypeStruct((nb, m, n), out_dtype),
        scratch_shapes=[pltpu.VMEM((m, n), F32)],
    )(a, b)


def _rms_matmul(name, x, gamma, w):
    t, d = x.shape
    n = w.shape[1]
    tm = _tile(t)

    def body(x_ref, gam_ref, w_ref, xn_ref, h_ref):
        xv = x_ref[...]
        r = lax.rsqrt(jnp.mean(xv * xv, axis=-1, keepdims=True) + EPS)
        xn = (xv * r * gam_ref[...]).astype(BF16)
        xn_ref[...] = xn
        h_ref[...] = jnp.dot(xn, w_ref[...], preferred_element_type=F32)

    return _call(
        body, name=name, grid=(t // tm,),
        in_specs=[pl.BlockSpec((tm, d), lambda i: (i, 0)),
                  pl.BlockSpec((1, d), lambda i: (0, 0)),
                  pl.BlockSpec((d, n), lambda i: (0, 0))],
        out_specs=[pl.BlockSpec((tm, d), lambda i: (i, 0)),
                   pl.BlockSpec((tm, n), lambda i: (i, 0))],
        out_shape=[jax.ShapeDtypeStruct((t, d), BF16), jax.ShapeDtypeStruct((t, n), F32)],
    )(x, gamma, w)


def _matmul_residual(name, a, w, res):
    t, kdim = a.shape
    n = w.shape[1]
    tm = _tile(t)

    def body(a_ref, w_ref, r_ref, o_ref):
        o_ref[...] = r_ref[...] + jnp.dot(a_ref[...], w_ref[...], preferred_element_type=F32)

    return _call(
        body, name=name, grid=(t // tm,),
        in_specs=[pl.BlockSpec((tm, kdim), lambda i: (i, 0)),
                  pl.BlockSpec((kdim, n), lambda i: (0, 0)),
                  pl.BlockSpec((tm, n), lambda i: (i, 0))],
        out_specs=pl.BlockSpec((tm, n), lambda i: (i, 0)),
        out_shape=jax.ShapeDtypeStruct((t, n), F32),
    )(a, w, res)


def _matmul_nt(name, dy, w):
    t, n = dy.shape
    kdim = w.shape[0]
    tm = _tile(t)

    def body(dy_ref, w_ref, da_ref, dyb_ref):
        dyb = dy_ref[...].astype(BF16)
        dyb_ref[...] = dyb
        da_ref[...] = lax.dot_general(dyb, w_ref[...], NT, preferred_element_type=F32)

    return _call(
        body, name=name, grid=(t // tm,),
        in_specs=[pl.BlockSpec((tm, n), lambda i: (i, 0)),
                  pl.BlockSpec((kdim, n), lambda i: (0, 0))],
        out_specs=[pl.BlockSpec((tm, kdim), lambda i: (i, 0)),
                   pl.BlockSpec((tm, n), lambda i: (i, 0))],
        out_shape=[jax.ShapeDtypeStruct((t, kdim), F32), jax.ShapeDtypeStruct((t, n), BF16)],
    )(dy, w)


def _matmul_nt_rms_bwd(name, dz, w, dres, x, gamma):
    t, kdim = dz.shape
    d = w.shape[0]
    tm = _tile(t)

    def body(dz_ref, w_ref, dres_ref, x_ref, gam_ref, dx_ref, dgam_ref):
        i = pl.program_id(0)

        @pl.when(i == 0)
        def _():
            dgam_ref[...] = jnp.zeros_like(dgam_ref)

        dxn = lax.dot_general(dz_ref[...], w_ref[...], NT, preferred_element_type=F32)
        xv = x_ref[...]
        r = lax.rsqrt(jnp.mean(xv * xv, axis=-1, keepdims=True) + EPS)
        yv = xv * r
        dgam_ref[...] += jnp.sum(dxn * yv, axis=0, keepdims=True)
        dyn = dxn * gam_ref[...]
        dx_ref[...] = dres_ref[...] + r * (dyn - yv * jnp.mean(dyn * yv, axis=-1, keepdims=True))

    return _call(
        body, name=name, grid=(t // tm,),
        in_specs=[pl.BlockSpec((tm, kdim), lambda i: (i, 0)),
                  pl.BlockSpec((d, kdim), lambda i: (0, 0)),
                  pl.BlockSpec((tm, d), lambda i: (i, 0)),
                  pl.BlockSpec((tm, d), lambda i: (i, 0)),
                  pl.BlockSpec((1, d), lambda i: (0, 0))],
        out_specs=[pl.BlockSpec((tm, d), lambda i: (i, 0)),
                   pl.BlockSpec((1, d), lambda i: (0, 0))],
        out_shape=[jax.ShapeDtypeStruct((t, d), F32), jax.ShapeDtypeStruct((1, d), F32)],
    )(dz, w, dres, x, gamma)


def _pool_means(uext_ref, pos, tm, g, win):
    cols = slice(g * GROUP, (g + 1) * GROUP)
    acc = uext_ref[pl.ds(HALO, tm), cols]
    for j in range(1, win):
        acc = acc + uext_ref[pl.ds(HALO - j, tm), cols]
    cnt = jnp.minimum(pos + 1, win).astype(F32)
    return acc / cnt - uext_ref[pl.ds(HALO, tm), cols]


def _conv_taps(gext_ref, cw_ref, cb_ref, start, rows):
    y = cb_ref[...] + cw_ref[0:1, :] * gext_ref[pl.ds(start, rows), :]
    for k in range(1, CONV_WIDTH):
        y = y + cw_ref[k:k + 1, :] * gext_ref[pl.ds(start + k, rows), :]
    return y


def _pool_conv_fwd(name, h, pool_w, pool_b, pool_scale, conv_w, conv_b, ln_g, ln_b):
    t, hw = h.shape
    pc = len(POOL_WINDOWS) * GROUP
    cc = (hw - pc) // 2
    tm = _tile(t)
    per = tm // HALO

    def body(h_ref, hp_ref, pw_ref, pb_ref, ps_ref, cw_ref, cb_ref, lg_ref, lb_ref, cat_ref, uext, gext):
        i = pl.program_id(0)
        keep = (i > 0).astype(F32)
        hp = hp_ref[...] * keep
        uext[0:HALO, :] = hp[:, :pc]
        uext[HALO:, :] = h_ref[:, :pc]
        gext[0:HALO, :] = hp[:, pc:pc + cc] * _sigmoid(hp[:, pc + cc:])
        gext[HALO:, :] = h_ref[:, pc:pc + cc] * _sigmoid(h_ref[:, pc + cc:])
        pos = i * tm + lax.broadcasted_iota(jnp.int32, (tm, 1), 0)
        for g, win in enumerate(POOL_WINDOWS):
            cols = slice(g * GROUP, (g + 1) * GROUP)
            pooled = _pool_means(uext, pos, tm, g, win)
            mixed = jnp.dot(pooled.astype(BF16), pw_ref[g].astype(BF16),
                            preferred_element_type=F32) + pb_ref[g:g + 1, :]
            cat_ref[:, cols] = (mixed * ps_ref[:, cols]).astype(BF16)
        y = _conv_taps(gext, cw_ref, cb_ref, HALO - (CONV_WIDTH - 1), tm)
        mu = jnp.mean(y, axis=-1, keepdims=True)
        dv = y - mu
        rstd = lax.rsqrt(jnp.mean(dv * dv, axis=-1, keepdims=True) + EPS)
        ln = dv * rstd * lg_ref[...] + lb_ref[...]
        cat_ref[:, pc:] = (ln * _sigmoid(ln)).astype(BF16)

    small = lambda a: pl.BlockSpec(a.shape, lambda i: (0,) * a.ndim)
    return _call(
        body, name=name, grid=(t // tm,),
        in_specs=[pl.BlockSpec((tm, hw), lambda i: (i, 0)),
                  pl.BlockSpec((HALO, hw), lambda i: (jnp.maximum(i * per - 1, 0), 0)),
                  small(pool_w), small(pool_b), small(pool_scale), small(conv_w), small(conv_b),
                  small(ln_g), small(ln_b)],
        out_specs=pl.BlockSpec((tm, pc + cc), lambda i: (i, 0)),
        out_shape=jax.ShapeDtypeStruct((t, pc + cc), BF16),
        scratch_shapes=[pltpu.VMEM((HALO + tm, pc), F32), pltpu.VMEM((HALO + tm, cc), F32)],
    )(h, h, pool_w, pool_b, pool_scale, conv_w, conv_b, ln_g, ln_b)


def _pool_conv_bwd(name, h, dcat, pool_w, pool_b, pool_scale, conv_w, conv_b, ln_g, ln_b, comm=None):
    t, hw = h.shape
    pc = len(POOL_WINDOWS) * GROUP
    cc = (hw - pc) // 2
    ng = len(POOL_WINDOWS)
    tm = _tile(t)
    per = tm // HALO
    nt = t // tm
    r2 = tm + HALO
    taps = CONV_WIDTH - 1

    def body(h_ref, hp_ref, hn_ref, dc_ref, dcn_ref, pw_ref, pb_ref, ps_ref, cw_ref, cb_ref, lg_ref, lb_ref,
             dh_ref, dpw_ref, dpb_ref, dps_ref, dcw_ref, dcb_ref, dlg_ref, dlb_ref,
             uext, gext, dcext, dqext, dycext):
        i = pl.program_id(0)

        @pl.when(i == 0)
        def _():
            for ref in (dpw_ref, dpb_ref, dps_ref, dcw_ref, dcb_ref, dlg_ref, dlb_ref):
                ref[...] = jnp.zeros_like(ref)

        keep_p = (i > 0).astype(F32)
        keep_n = (i < nt - 1).astype(F32)
        hp = hp_ref[...] * keep_p
        hn = hn_ref[...] * keep_n
        uext[0:HALO, :] = hp[:, :pc]
        uext[HALO:, :] = h_ref[:, :pc]
        gext[0:HALO, :] = hp[:, pc:pc + cc] * _sigmoid(hp[:, pc + cc:])
        gext[pl.ds(HALO, tm), :] = h_ref[:, pc:pc + cc] * _sigmoid(h_ref[:, pc + cc:])
        gext[pl.ds(HALO + tm, HALO), :] = hn[:, pc:pc + cc] * _sigmoid(hn[:, pc + cc:])
        dcext[0:tm, :] = dc_ref[...]
        dcext[pl.ds(tm, HALO), :] = dcn_ref[...] * keep_n

        pos = i * tm + lax.broadcasted_iota(jnp.int32, (tm, 1), 0)
        pos2 = i * tm + lax.broadcasted_iota(jnp.int32, (r2, 1), 0)
        for g, win in enumerate(POOL_WINDOWS):
            cols = slice(g * GROUP, (g + 1) * GROUP)
            wg = pw_ref[g].astype(BF16)
            dya = dcext[:, cols]
            dmixed = dya * ps_ref[:, cols]
            dpooled = lax.dot_general(dmixed.astype(BF16), wg, NT, preferred_element_type=F32)
            cnt2 = jnp.minimum(pos2 + 1, win).astype(F32)
            dqext[:, cols] = dpooled / cnt2
            du = -dpooled[0:tm]
            for j in range(win):
                du = du + dqext[pl.ds(j, tm), cols]
            dh_ref[:, cols] = du.astype(BF16)
            pooled = _pool_means(uext, pos, tm, g, win)
            pooled_b = pooled.astype(BF16)
            mixed = jnp.dot(pooled_b, wg, preferred_element_type=F32) + pb_ref[g:g + 1, :]
            dps_ref[:, cols] += jnp.sum(dya[0:tm] * mixed, axis=0, keepdims=True)
            dpb_ref[g:g + 1, :] += jnp.sum(dmixed[0:tm], axis=0, keepdims=True)
            dpw_ref[g] += lax.dot_general(pooled_b, dmixed[0:tm].astype(BF16), TN, preferred_element_type=F32)

        y = _conv_taps(gext, cw_ref, cb_ref, HALO - taps, r2)
        mu = jnp.mean(y, axis=-1, keepdims=True)
        dv = y - mu
        rstd = lax.rsqrt(jnp.mean(dv * dv, axis=-1, keepdims=True) + EPS)
        norm = dv * rstd
        ln = norm * lg_ref[...] + lb_ref[...]
        sig = _sigmoid(ln)
        dln = dcext[:, pc:] * (sig * (1.0 + ln * (1.0 - sig)))
        dnorm = dln * lg_ref[...]
        dyc = rstd * (dnorm - jnp.mean(dnorm, axis=-1, keepdims=True)
                      - norm * jnp.mean(dnorm * norm, axis=-1, keepdims=True))
        dycext[...] = dyc
        dlg_ref[...] += jnp.sum((dln * norm)[0:tm], axis=0, keepdims=True)
        dlb_ref[...] += jnp.sum(dln[0:tm], axis=0, keepdims=True)
        dcb_ref[...] += jnp.sum(dyc[0:tm], axis=0, keepdims=True)
        dyc_t = dycext[0:tm, :]
        dg = jnp.zeros((tm, cc), F32)
        for k in range(CONV_WIDTH):
            dcw_ref[k:k + 1, :] += jnp.sum(dyc_t * gext[pl.ds(HALO - taps + k, tm), :], axis=0, keepdims=True)
            dg = dg + cw_ref[k:k + 1, :] * dycext[pl.ds(taps - k, tm), :]
        a = h_ref[:, pc:pc + cc]
        sg = _sigmoid(h_ref[:, pc + cc:])
        dh_ref[:, pc:pc + cc] = (dg * sg).astype(BF16)
        dh_ref[:, pc + cc:] = (dg * a * sg * (1.0 - sg)).astype(BF16)

    small = lambda a: pl.BlockSpec(a.shape, lambda i: (0,) * a.ndim)
    smalls = (pool_w, pool_b, pool_scale, conv_w, conv_b, ln_g, ln_b)
    return _call(
        body, comm=comm, name=name, grid=(nt,),
        in_specs=[pl.BlockSpec((tm, hw), lambda i: (i, 0)),
                  pl.BlockSpec((HALO, hw), lambda i: (jnp.maximum(i * per - 1, 0), 0)),
                  pl.BlockSpec((HALO, hw), lambda i: (jnp.minimum((i + 1) * per, t // HALO - 1), 0)),
                  pl.BlockSpec((tm, pc + cc), lambda i: (i, 0)),
                  pl.BlockSpec((HALO, pc + cc), lambda i: (jnp.minimum((i + 1) * per, t // HALO - 1), 0)),
                  ] + [small(a) for a in smalls],
        out_specs=[pl.BlockSpec((tm, hw), lambda i: (i, 0))] + [small(a) for a in smalls],
        out_shape=[jax.ShapeDtypeStruct((t, hw), BF16)] + [jax.ShapeDtypeStruct(a.shape, F32) for a in smalls],
        scratch_shapes=[pltpu.VMEM((HALO + tm, pc), F32), pltpu.VMEM((HALO + tm + HALO, cc), F32),
                        pltpu.VMEM((r2, pc + cc), F32), pltpu.VMEM((r2, pc), F32), pltpu.VMEM((r2, cc), F32)],
    )(h, h, h, dcat, dcat, *smalls)


SQRT_HALF = 0.7071067811865476
INV_SQRT_2PI = 0.3989422804014327


def _sgu_core(zp_ref, lg_ref, lb_ref, ws_ref, bs_ref, vo_s, tm, sc, heads):
    zp = zp_ref[...]
    z = 0.5 * zp * (1.0 + lax.erf(zp * SQRT_HALF))
    u = z[:, :sc]
    v = z[:, sc:]
    mu = jnp.mean(v, axis=-1, keepdims=True)
    dv = v - mu
    rstd = lax.rsqrt(jnp.mean(dv * dv, axis=-1, keepdims=True) + EPS)
    norm = dv * rstd
    vb = (norm * lg_ref[...] + lb_ref[...]).astype(BF16)
    row = lax.broadcasted_iota(jnp.int32, (GROUP, GROUP), 0)
    col = lax.broadcasted_iota(jnp.int32, (GROUP, GROUP), 1)
    mask = (col <= row).astype(F32)
    wm = [ws_ref[hd] * mask for hd in range(heads)]
    for hd in range(heads):
        cols = slice(hd * GROUP, (hd + 1) * GROUP)
        wb = wm[hd].astype(BF16)
        for n in range(tm // GROUP):
            rows = slice(n * GROUP, (n + 1) * GROUP)
            vo_s[rows, cols] = jnp.dot(wb, vb[rows, cols], preferred_element_type=F32) + bs_ref[hd]
    return zp, u, norm, rstd, vb, wm, mask


def _sgu_fwd(name, zp, ln_g, ln_b, w_s, b_s):
    t, two_sc = zp.shape
    sc = two_sc // 2
    heads = sc // GROUP
    tm = _tile(t)

    def body(zp_ref, lg_ref, lb_ref, ws_ref, bs_ref, q_ref, vo_s):
        _, u, _, _, _, _, _ = _sgu_core(zp_ref, lg_ref, lb_ref, ws_ref, bs_ref, vo_s, tm, sc, heads)
        q_ref[...] = (u * vo_s[...]).astype(BF16)

    small = lambda a: pl.BlockSpec(a.shape, lambda i: (0,) * a.ndim)
    return _call(
        body, name=name, grid=(t // tm,),
        in_specs=[pl.BlockSpec((tm, two_sc), lambda i: (i, 0)), small(ln_g), small(ln_b), small(w_s), small(b_s)],
        out_specs=pl.BlockSpec((tm, sc), lambda i: (i, 0)),
        out_shape=jax.ShapeDtypeStruct((t, sc), BF16),
        scratch_shapes=[pltpu.VMEM((tm, sc), F32)],
    )(zp, ln_g, ln_b, w_s, b_s)


def _sgu_bwd(name, zp, dq, ln_g, ln_b, w_s, b_s):
    t, two_sc = zp.shape
    sc = two_sc // 2
    heads = sc // GROUP
    tm = _tile(t)
    nt = t // tm

    def body(zp_ref, dq_ref, lg_ref, lb_ref, ws_ref, bs_ref,
             dzp_ref, dlg_ref, dlb_ref, dws_ref, dbs_ref, vo_s, dvl_s, dws_acc):
        i = pl.program_id(0)

        @pl.when(i == 0)
        def _():
            dlg_ref[...] = jnp.zeros_like(dlg_ref)
            dlb_ref[...] = jnp.zeros_like(dlb_ref)
            dbs_ref[...] = jnp.zeros_like(dbs_ref)
            dws_acc[...] = jnp.zeros_like(dws_acc)

        zp, u, norm, rstd, vb, wm, mask = _sgu_core(zp_ref, lg_ref, lb_ref, ws_ref, bs_ref, vo_s, tm, sc, heads)
        dq = dq_ref[...]
        du = dq * vo_s[...]
        dvo = dq * u
        dvob = dvo.astype(BF16)
        for hd in range(heads):
            cols = slice(hd * GROUP, (hd + 1) * GROUP)
            wtb = jnp.transpose(wm[hd]).astype(BF16)
            for n in range(tm // GROUP):
                rows = slice(n * GROUP, (n + 1) * GROUP)
                blk = dvob[rows, cols]
                dws_acc[hd] += lax.dot_general(blk, vb[rows, cols], NT, preferred_element_type=F32)
                dvl_s[rows, cols] = jnp.dot(wtb, blk, preferred_element_type=F32)
                dbs_ref[hd] += jnp.sum(dvo[rows, cols], axis=-1, keepdims=True)
        dvl = dvl_s[...]
        dlg_ref[...] += jnp.sum(dvl * norm, axis=0, keepdims=True)
        dlb_ref[...] += jnp.sum(dvl, axis=0, keepdims=True)
        dnorm = dvl * lg_ref[...]
        dv = rstd * (dnorm - jnp.mean(dnorm, axis=-1, keepdims=True)
                     - norm * jnp.mean(dnorm * norm, axis=-1, keepdims=True))
        dgelu = 0.5 * (1.0 + lax.erf(zp * SQRT_HALF)) + zp * (INV_SQRT_2PI * jnp.exp(-0.5 * zp * zp))
        dzp_ref[:, :sc] = (du * dgelu[:, :sc]).astype(BF16)
        dzp_ref[:, sc:] = (dv * dgelu[:, sc:]).astype(BF16)

        @pl.when(i == nt - 1)
        def _():
            for hd in range(heads):
                dws_ref[hd] = dws_acc[hd] * mask

    small = lambda a: pl.BlockSpec(a.shape, lambda i: (0,) * a.ndim)
    smalls = (ln_g, ln_b, w_s, b_s)
    return _call(
        body, name=name, grid=(nt,),
        in_specs=[pl.BlockSpec((tm, two_sc), lambda i: (i, 0)), pl.BlockSpec((tm, sc), lambda i: (i, 0))]
                 + [small(a) for a in smalls],
        out_specs=[pl.BlockSpec((tm, two_sc), lambda i: (i, 0))] + [small(a) for a in smalls],
        out_shape=[jax.ShapeDtypeStruct((t, two_sc), BF16)] + [jax.ShapeDtypeStruct(a.shape, F32) for a in smalls],
        scratch_shapes=[pltpu.VMEM((tm, sc), F32), pltpu.VMEM((tm, sc), F32), pltpu.VMEM(w_s.shape, F32)],
    )(zp, dq, ln_g, ln_b, w_s, b_s)


def _loss_head(name, x, gamma, target):
    t, d = x.shape
    tm = _tile(t)

    def body(x_ref, gam_ref, tg_ref, loss_ref, dx_ref, dgam_ref):
        i = pl.program_id(0)

        @pl.when(i == 0)
        def _():
            loss_ref[...] = jnp.zeros_like(loss_ref)
            dgam_ref[...] = jnp.zeros_like(dgam_ref)

        xv = x_ref[...]
        r = lax.rsqrt(jnp.mean(xv * xv, axis=-1, keepdims=True) + EPS)
        yv = xv * r
        err = yv * gam_ref[...] - tg_ref[...]
        row = jnp.sum(err * err, axis=-1, keepdims=True)
        loss_ref[...] += (0.5 / d) * jnp.sum(row, axis=0, keepdims=True)
        dout = err * (1.0 / d)
        dgam_ref[...] += jnp.sum(dout * yv, axis=0, keepdims=True)
        dyn = dout * gam_ref[...]
        dx_ref[...] = r * (dyn - yv * jnp.mean(dyn * yv, axis=-1, keepdims=True))

    return _call(
        body, name=name, grid=(t // tm,),
        in_specs=[pl.BlockSpec((tm, d), lambda i: (i, 0)),
                  pl.BlockSpec((1, d), lambda i: (0, 0)),
                  pl.BlockSpec((tm, d), lambda i: (i, 0))],
        out_specs=[pl.BlockSpec((1, 1), lambda i: (0, 0)),
                   pl.BlockSpec((tm, d), lambda i: (i, 0)),
                   pl.BlockSpec((1, d), lambda i: (0, 0))],
        out_shape=[jax.ShapeDtypeStruct((1, 1), F32), jax.ShapeDtypeStruct((t, d), F32),
                   jax.ShapeDtypeStruct((1, d), F32)],
    )(x, gamma, target)


def _adamw(name, parts, w, m, v, rows):
    l_n, r_n, c_n = w.shape
    s_n = parts[0].shape[0]
    tr = min(rows, r_n)
    nr = r_n // tr
    c1 = 1.0 - ADAM_B1 ** ADAM_STEP
    c2 = 1.0 - ADAM_B2 ** ADAM_STEP

    def body(*refs):
        p_refs = refs[:l_n]
        w_ref, m_ref, v_ref, g_ref, d_ref, mo_ref, vo_ref = refs[l_n:]
        layer = pl.program_id(0)

        def update(p_ref):
            g = p_ref[0].astype(F32)
            for s in range(1, s_n):
                g = g + p_ref[s].astype(F32)
            mn = ADAM_B1 * m_ref[...] + (1.0 - ADAM_B1) * g
            vn = ADAM_B2 * v_ref[...] + (1.0 - ADAM_B2) * (g * g)
            m_hat = mn / c1
            v_hat = vn / c2
            g_ref[...] = g
            d_ref[...] = -ADAM_LR * (m_hat / (jnp.sqrt(v_hat) + ADAM_EPS) + ADAM_WD * w_ref[...])
            mo_ref[...] = mn
            vo_ref[...] = vn

        for j in range(l_n):
            pl.when(layer == j)(functools.partial(update, p_refs[j]))

    def part_spec(j):
        return pl.BlockSpec((s_n, tr, c_n), lambda l, i: (0, jnp.where(l == j, i, jnp.where(l < j, 0, nr - 1)), 0))

    blk = pl.BlockSpec((None, tr, c_n), lambda l, i: (l, i, 0))
    return _call(
        body, name=name, grid=(l_n, nr),
        in_specs=[part_spec(j) for j in range(l_n)] + [blk, blk, blk],
        out_specs=[blk] * 4,
        out_shape=[jax.ShapeDtypeStruct((l_n, r_n, c_n), F32)] * 4,
    )(*parts, w, m, v)


def _local_step(x, target, big, small, sched=None):
    t, d = x.shape
    n_layers = small["ffn1_norm"].shape[0]
    gb = {}
    gs = {}

    def row(a, l):
        return a[l:l + 1]

    def run(fn, name, *operands):
        comm = sched.plan(name, gb, gs) if sched is not None else None
        if comm is None:
            return fn(name, *operands)
        res, got = fn(name, *operands, comm=comm)
        sched.deliver(comm, got)
        return res

    saved = []
    xs = x
    for l in range(n_layers):
        rec = {"x_ffn1": xs}
        xs, rec["xn_ffn1"], rec["gu_ffn1"] = run(
            _ffn_fwd, f"ffn1_fwd_l{l}", xs, row(small["ffn1_norm"], l), big["ffn1_w_in", l], big["ffn1_w_out", l])
        rec["x_mix"] = xs
        if l % 2 == 0:
            rec["xn_mix"], rec["h"] = _rms_matmul(f"ab_in_l{l}", xs, row(small["mix_norm"], l), big["ab_w_in"])
            rec["cat"] = _pool_conv_fwd(f"pool_conv_fwd_l{l}", rec["h"], small["pool_w"], small["pool_b"],
                                        small["pool_scale"], small["conv_w"], small["conv_b"],
                                        small["conv_ln_g"], small["conv_ln_b"])
            xs = _matmul_residual(f"ab_out_l{l}", rec["cat"], big["ab_w_out"], xs)
        else:
            rec["xn_mix"], rec["zp"] = _rms_matmul(f"sgu_in_l{l}", xs, row(small["mix_norm"], l), big["sgu_w_in"])
            rec["q"] = _sgu_fwd(f"sgu_fwd_l{l}", rec["zp"], small["sgu_ln_g"], small["sgu_ln_b"],
                                small["sgu_w"], small["sgu_b"])
            xs = _matmul_residual(f"sgu_out_l{l}", rec["q"], big["sgu_w_out"], xs)
        rec["x_ffn2"] = xs
        xs, rec["xn_ffn2"], rec["gu_ffn2"] = run(
            _ffn_fwd, f"ffn2_fwd_l{l}", xs, row(small["ffn2_norm"], l), big["ffn2_w_in", l], big["ffn2_w_out", l])
        saved.append(rec)

    loss, dx, gs["final_norm"] = _loss_head("loss_head", xs, small["final_norm"], target)

    norm_rows = {"ffn1_norm": [None] * n_layers, "mix_norm": [None] * n_layers, "ffn2_norm": [None] * n_layers}

    def ffn_backward(tag, l, dx, rec):
        dx, dgam, dyh, hh, dgu = run(_ffn_bwd, f"{tag}_bwd_l{l}", dx, rec[f"x_{tag}"], row(small[f"{tag}_norm"], l),
                                     rec[f"gu_{tag}"], big[f"{tag}_w_in", l], big[f"{tag}_w_out", l])
        norm_rows[f"{tag}_norm"][l] = dgam
        fb = dgu.shape[-1]
        gb[f"{tag}_w_out", l] = run(_matmul_tn, f"{tag}_dwout_l{l}", hh, dyh).reshape(N_DEV, -1, d)
        gb[f"{tag}_w_in", l] = run(_matmul_tn, f"{tag}_dwin_l{l}", dgu.reshape(-1, t, fb),
                                   rec[f"xn_{tag}"]).reshape(N_DEV, -1, d)
        return dx

    for l in reversed(range(n_layers)):
        rec = saved[l]
        dx = ffn_backward("ffn2", l, dx, rec)
        if l % 2 == 0:
            dcat, dxb = _matmul_nt(f"ab_out_bwd_l{l}", dx, big["ab_w_out"])
            gb["ab_w_out", 0] = _matmul_tn(f"ab_dwout_l{l}", rec["cat"], dxb)
            dh, gs["pool_w"], gs["pool_b"], gs["pool_scale"], gs["conv_w"], gs["conv_b"], gs["conv_ln_g"], \
                gs["conv_ln_b"] = run(
                    _pool_conv_bwd, f"pool_conv_bwd_l{l}", rec["h"], dcat, small["pool_w"], small["pool_b"],
                    small["pool_scale"], small["conv_w"], small["conv_b"], small["conv_ln_g"], small["conv_ln_b"])
            gb["ab_w_in", 0] = _matmul_tn(f"ab_dwin_l{l}", rec["xn_mix"], dh)
            dx, dgam = _matmul_nt_rms_bwd(f"ab_in_bwd_l{l}", dh, big["ab_w_in"], dx, rec["x_mix"],
                                          row(small["mix_norm"], l))
        else:
            dq, dxb = _matmul_nt(f"sgu_out_bwd_l{l}", dx, big["sgu_w_out"])
            gb["sgu_w_out", 0] = _matmul_tn(f"sgu_dwout_l{l}", rec["q"], dxb)
            dzp, gs["sgu_ln_g"], gs["sgu_ln_b"], gs["sgu_w"], gs["sgu_b"] = _sgu_bwd(
                f"sgu_bwd_l{l}", rec["zp"], dq, small["sgu_ln_g"], small["sgu_ln_b"], small["sgu_w"], small["sgu_b"])
            gb["sgu_w_in", 0] = _matmul_tn(f"sgu_dwin_l{l}", rec["xn_mix"], dzp)
            dx, dgam = _matmul_nt_rms_bwd(f"sgu_in_bwd_l{l}", dzp, big["sgu_w_in"], dx, rec["x_mix"],
                                          row(small["mix_norm"], l))
        norm_rows["mix_norm"][l] = dgam
        dx = ffn_backward("ffn1", l, dx, rec)

    for k, rows in norm_rows.items():
        gs[k] = jnp.concatenate(rows, axis=0)
    return loss, dx, gb, gs


SHARDED_SMALL = ("conv_w", "sgu_ln_g", "sgu_ln_b")
REPLICATED = ("ffn1_norm", "mix_norm", "ffn2_norm", "pool_w", "pool_b", "pool_scale", "conv_b", "conv_ln_g",
              "conv_ln_b", "sgu_w", "sgu_b", "final_norm")
WEIGHTS = ("ffn1_norm", "ffn1_w_in", "ffn1_w_out", "mix_norm", "ffn2_norm", "ffn2_w_in", "ffn2_w_out", "ab_w_in",
           "pool_w", "pool_b", "pool_scale", "conv_w", "conv_b", "conv_ln_g", "conv_ln_b", "ab_w_out", "sgu_w_in",
           "sgu_ln_g", "sgu_ln_b", "sgu_w", "sgu_b", "sgu_w_out", "final_norm")
LANES = 128


def _interleave_cols(g):
    n, k, c = g.shape
    return jnp.transpose(g, (1, 0, 2)).reshape(k, n * c)


def _split_cols(a):
    k, nc = a.shape
    return jnp.transpose(a.reshape(k, N_DEV, nc // N_DEV), (1, 0, 2))


def _as3(a):
    if a.ndim == 1:
        return a.reshape(1, 1, -1)
    if a.ndim == 2:
        return a.reshape(a.shape[0], 1, a.shape[1])
    return a.reshape(a.shape[0], -1, a.shape[-1])


def _pack_rows(a):
    flat = a.reshape(-1)
    pad = (-flat.shape[0]) % (8 * LANES)
    if pad:
        flat = jnp.concatenate([flat, jnp.zeros((pad,), flat.dtype)])
    return flat.reshape(-1, LANES)


FIRST_GATHER = (("ffn1_w_in", 0), ("ffn1_w_out", 0), ("conv_w", 0), ("sgu_ln_g", 0), ("sgu_ln_b", 0))
GATHER_PLAN = {
    "ffn1_fwd_l0": (("ab_w_in", 0), ("ab_w_out", 0), ("ffn2_w_in", 0), ("ffn2_w_out", 0)),
    "ffn2_fwd_l0": (("ffn1_w_in", 1), ("ffn1_w_out", 1), ("sgu_w_in", 0), ("sgu_w_out", 0)),
    "ffn1_fwd_l1": (("ffn2_w_in", 1), ("ffn2_w_out", 1)),
}
SCATTER_PLAN = {
    "ffn1_bwd_l1": (("ffn2_w_in", 1), ("ffn2_w_out", 1)),
    "ffn2_bwd_l0": (("sgu_w_in", 0), ("sgu_w_out", 0), ("sgu_ln_g", 0), ("sgu_ln_b", 0),
                    ("ffn1_w_in", 1), ("ffn1_w_out", 1)),
    "pool_conv_bwd_l0": (("ffn2_w_in", 0), ("ffn2_w_out", 0)),
    "ffn1_dwout_l0": (("ab_w_in", 0), ("ab_w_out", 0), ("conv_w", 0)),
    "ffn1_dwin_l0": (("ffn1_w_out", 0),),
}
LAST_SCATTER = (("ffn1_w_in", 0),)


class _Schedule:
    def __init__(self, shards, big, small):
        self.shards, self.big, self.small = shards, big, small
        self.recv = {}
        self.pending = {}

    def gather_comm(self, keys):
        comm = _Comm()
        for key in keys:
            comm.gather(*self.shards[key])
        self.pending[id(comm)] = ("gather", keys)
        return comm

    def scatter_comm(self, keys, gb, gs):
        comm = _Comm()
        for name, l in keys:
            if name in ("ab_w_in", "sgu_w_in"):
                send = _split_cols(gb[name, l][0])
            elif name in ("ab_w_out", "sgu_w_out"):
                send = gb[name, l][0]
                send = send.reshape(N_DEV, -1, send.shape[-1])
            elif name == "conv_w":
                send = _split_cols(gs[name][:CONV_WIDTH])
            elif name in ("sgu_ln_g", "sgu_ln_b"):
                send = gs[name].reshape(N_DEV, 1, -1)
            else:
                send = gb[name, l]
            comm.scatter(send)
        self.pending[id(comm)] = ("scatter", keys)
        return comm

    def plan(self, name, gb, gs):
        if name in GATHER_PLAN:
            return self.gather_comm(GATHER_PLAN[name])
        if name in SCATTER_PLAN:
            return self.scatter_comm(SCATTER_PLAN[name], gb, gs)
        return None

    def deliver(self, comm, got):
        kind, keys = self.pending.pop(id(comm))
        for (name, l), arr in zip(keys, got):
            if kind == "scatter":
                self.recv[name, l] = arr
            elif name in ("ffn1_w_in", "ffn2_w_in"):
                self.big[name, l] = arr.reshape(2, HIDDEN_SPLIT, -1, arr.shape[-1])
            elif name in ("ffn1_w_out", "ffn2_w_out"):
                self.big[name, l] = arr.reshape(-1, arr.shape[-1])
            elif name in ("ab_w_in", "sgu_w_in"):
                self.big[name] = _interleave_cols(arr)
            elif name in ("ab_w_out", "sgu_w_out"):
                self.big[name] = arr.reshape(-1, arr.shape[-1])
            elif name == "conv_w":
                self.small[name] = jnp.pad(_interleave_cols(arr), ((0, 1), (0, 0)))
            else:
                self.small[name] = arr.reshape(1, -1)


def kernel(x, ffn1_norm, ffn1_w_in, ffn1_w_out, mix_norm, ffn2_norm, ffn2_w_in, ffn2_w_out, ab_w_in, pool_w, pool_b, pool_scale, conv_w, conv_b, conv_ln_g, conv_ln_b, ab_w_out, sgu_w_in, sgu_ln_g, sgu_ln_b, sgu_w, sgu_b, sgu_w_out, final_norm, loss_target, m_ffn1_norm, m_ffn1_w_in, m_ffn1_w_out, m_mix_norm, m_ffn2_norm, m_ffn2_w_in, m_ffn2_w_out, m_ab_w_in, m_pool_w, m_pool_b, m_pool_scale, m_conv_w, m_conv_b, m_conv_ln_g, m_conv_ln_b, m_ab_w_out, m_sgu_w_in, m_sgu_ln_g, m_sgu_ln_b, m_sgu_w, m_sgu_b, m_sgu_w_out, m_final_norm, v_ffn1_norm, v_ffn1_w_in, v_ffn1_w_out, v_mix_norm, v_ffn2_norm, v_ffn2_w_in, v_ffn2_w_out, v_ab_w_in, v_pool_w, v_pool_b, v_pool_scale, v_conv_w, v_conv_b, v_conv_ln_g, v_conv_ln_b, v_ab_w_out, v_sgu_w_in, v_sgu_ln_g, v_sgu_ln_b, v_sgu_w, v_sgu_b, v_sgu_w_out, v_final_norm):
    args = dict(locals())
    w = {n: args[n] for n in WEIGHTS}
    m = {n: args["m_" + n] for n in WEIGHTS}
    v = {n: args["v_" + n] for n in WEIGHTS}
    n_layers = ffn1_norm.shape[0]

    shards = {}
    for n in ("ffn1_w_in", "ffn2_w_in"):
        wt = jnp.swapaxes(w[n], 1, 2).astype(BF16)
        for l in range(n_layers):
            shards[n, l] = (wt, l)
    for n in ("ffn1_w_out", "ffn2_w_out"):
        wb = w[n].astype(BF16)
        for l in range(n_layers):
            shards[n, l] = (wb, l)
    for n in ("ab_w_in", "ab_w_out", "sgu_w_in", "sgu_w_out"):
        shards[n, 0] = (w[n][0].astype(BF16), None)
    shards["conv_w", 0] = (conv_w[0], None)
    shards["sgu_ln_g", 0] = (sgu_ln_g, None)
    shards["sgu_ln_b", 0] = (sgu_ln_b, None)

    big = {}
    small = {
        "ffn1_norm": ffn1_norm, "mix_norm": mix_norm, "ffn2_norm": ffn2_norm, "final_norm": final_norm.reshape(1, -1),
        "pool_w": pool_w[0], "pool_b": pool_b[0], "pool_scale": pool_scale,
        "conv_b": conv_b, "conv_ln_g": conv_ln_g, "conv_ln_b": conv_ln_b,
        "sgu_w": sgu_w[0], "sgu_b": sgu_b[0][:, :, None],
    }
    sched = _Schedule(shards, big, small)
    first = sched.gather_comm(FIRST_GATHER)
    sched.deliver(first, _exchange("gather_first", first))

    loss, grad_x, gb, gs = _local_step(x[0], loss_target[0], big, small, sched)

    rep_grads = {
        "ffn1_norm": gs["ffn1_norm"], "mix_norm": gs["mix_norm"], "ffn2_norm": gs["ffn2_norm"],
        "pool_w": gs["pool_w"], "pool_b": gs["pool_b"], "pool_scale": gs["pool_scale"],
        "conv_b": gs["conv_b"], "conv_ln_g": gs["conv_ln_g"], "conv_ln_b": gs["conv_ln_b"],
        "sgu_w": gs["sgu_w"], "sgu_b": gs["sgu_b"], "final_norm": gs["final_norm"],
    }
    packs = [_pack_rows(rep_grads[n]) for n in REPLICATED] + [_pack_rows(loss)]
    offsets = [0]
    for p in packs:
        offsets.append(offsets[-1] + p.shape[0])
    packed = jnp.concatenate(packs, axis=0)
    n_rows = packed.shape[0]

    last = sched.scatter_comm(LAST_SCATTER, gb, gs)
    packed_id = last.gather(packed)
    got = _exchange("reduce_last", last)
    packed_all = got[packed_id]
    sched.deliver(last, got[:packed_id])
    recv = sched.recv

    out = {}
    for n in ("ffn1_w_in", "ffn2_w_in"):
        res = _adamw(f"adamw_{n}", [recv[n, l] for l in range(n_layers)], jnp.swapaxes(w[n], 1, 2),
                     jnp.swapaxes(m[n], 1, 2), jnp.swapaxes(v[n], 1, 2), 176)
        out[n] = [jnp.swapaxes(r, 1, 2) for r in res]
    for n in ("ffn1_w_out", "ffn2_w_out"):
        out[n] = _adamw(f"adamw_{n}", [recv[n, l] for l in range(n_layers)], w[n], m[n], v[n], 176)
    for n in ("ab_w_in", "ab_w_out", "sgu_w_in", "sgu_w_out") + SHARDED_SMALL:
        w3 = _as3(w[n])
        parts = recv[n, 0].reshape((N_DEV,) + w3.shape[1:])
        res = _adamw(f"adamw_{n}", [parts], w3, _as3(m[n]), _as3(v[n]), 512)
        out[n] = [r.reshape(w[n].shape) for r in res]

    def pack_rep(src):
        tail = [jnp.zeros((offsets[-1] - offsets[-2], LANES), F32)]
        return jnp.concatenate([_pack_rows(src[n]) for n in REPLICATED] + tail, axis=0)[None]

    res = _adamw("adamw_replicated", [packed_all], pack_rep(w), pack_rep(m), pack_rep(v), n_rows)
    for i, n in enumerate(REPLICATED):
        size = w[n].size
        out[n] = [r[0, offsets[i]:offsets[i + 1]].reshape(-1)[:size].reshape(w[n].shape) for r in res]
    loss_sum = res[0][0, offsets[-2], 0]

    return (loss_sum, grad_x[None],
            *[out[n][0] for n in WEIGHTS], *[out[n][1] for n in WEIGHTS],
            *[out[n][2] for n in WEIGHTS], *[out[n][3] for n in WEIGHTS])
```

```python
import functools

import jax
import jax.numpy as jnp
from jax import lax
from jax.experimental import pallas as pl
from jax.experimental.pallas import tpu as pltpu

F32 = jnp.float32
BF16 = jnp.bfloat16
EPS = 1e-6
N_DEV = 8
POOL_WINDOWS = (2, 4, 8, 16)
CONV_WIDTH = 31
HALO = 32
GROUP = 128
TOKEN_TILE = 512
CONTRACT_TILE = 2048
HIDDEN_SPLIT = 2
FFN_TILE = 256
ADAM_LR, ADAM_B1, ADAM_B2, ADAM_EPS, ADAM_WD, ADAM_STEP = 0.001, 0.9, 0.999, 1e-08, 0.01, 10
VMEM_LIMIT = 56 * 1024 * 1024

NT = (((1,), (1,)), ((), ()))
TN = (((0,), (0,)), ((), ()))


def _pallas(body, side_effects, **kw):
    params = pltpu.CompilerParams(vmem_limit_bytes=VMEM_LIMIT, has_side_effects=side_effects)
    return pl.pallas_call(body, compiler_params=params, **kw)


def _call(body, comm=None, **kw):
    if comm is None:
        return _pallas(body, False, **kw)
    in_specs = list(kw.pop("in_specs"))
    out_specs = kw.pop("out_specs")
    out_shape = kw.pop("out_shape")
    scratch = list(kw.pop("scratch_shapes", []))
    single = not isinstance(out_shape, (list, tuple))
    if single:
        out_specs, out_shape = [out_specs], [out_shape]
    n_in, n_out, n_scr = len(in_specs), len(out_shape), len(scratch)
    n_ci, n_co = len(comm.inputs), len(comm.out_shapes)
    grid = tuple(kw.get("grid", ()))

    def wrapped(*refs):
        pos = 0
        parts = []
        for n in (n_in, n_ci, n_out, n_co, n_scr, 3):
            parts.append(refs[pos:pos + n])
            pos += n
        a_in, c_in, a_out, c_out, a_scr, sems = parts
        if grid:
            ids = [pl.program_id(ax) for ax in range(len(grid))]
            first = functools.reduce(lambda p, q: p & q, [i == 0 for i in ids])
            last = functools.reduce(lambda p, q: p & q, [i == g - 1 for i, g in zip(ids, grid)])
            pl.when(first)(lambda: comm.start(c_in, c_out, sems))
            body(*a_in, *a_out, *a_scr)
            pl.when(last)(lambda: comm.wait(c_in, c_out, sems))
        else:
            comm.start(c_in, c_out, sems)
            body(*a_in, *a_out, *a_scr)
            comm.wait(c_in, c_out, sems)

    hbm = pl.BlockSpec(memory_space=pl.ANY)
    fn = _pallas(wrapped, True, in_specs=in_specs + [hbm] * n_ci, out_specs=list(out_specs) + [hbm] * n_co,
                 out_shape=list(out_shape) + list(comm.out_shapes), scratch_shapes=scratch + comm.semaphores(), **kw)

    def run(*operands):
        outs = fn(*operands, *comm.inputs)
        res = outs[:n_out]
        return (res[0] if single else res), outs[n_out:]

    return run


class _Comm:
    def __init__(self):
        self.inputs, self.sel, self.kinds, self.out_shapes = [], [], [], []

    def gather(self, arr, sel=None):
        block = arr.shape if sel is None else arr.shape[1:]
        self.inputs.append(arr)
        self.sel.append(sel)
        self.kinds.append("gather")
        self.out_shapes.append(jax.ShapeDtypeStruct((N_DEV,) + tuple(block), arr.dtype))
        return len(self.inputs) - 1

    def scatter(self, arr):
        self.inputs.append(arr)
        self.sel.append(None)
        self.kinds.append("scatter")
        self.out_shapes.append(jax.ShapeDtypeStruct(arr.shape, arr.dtype))
        return len(self.inputs) - 1

    def semaphores(self):
        n = len(self.inputs)
        return [pltpu.SemaphoreType.DMA((n, N_DEV - 1)), pltpu.SemaphoreType.DMA((n, N_DEV - 1)),
                pltpu.SemaphoreType.DMA((n,))]

    def _copies(self, ins, outs, sems, with_passed=True):
        send_sems, recv_sems, local_sems = sems
        x, y, c = lax.axis_index("x"), lax.axis_index("y"), lax.axis_index("c")
        me = 4 * x + 2 * y + c
        sibling = (x, y, 1 - c)
        chips = [(1 - x, y), (x, 1 - y), (1 - x, 1 - y)]
        items = []
        for a, kind in enumerate(self.kinds):
            def remote(src, dst, k, to, a=a):
                return pltpu.make_async_remote_copy(
                    src_ref=src, dst_ref=dst, send_sem=send_sems.at[a, k], recv_sem=recv_sems.at[a, k],
                    device_id=to, device_id_type=pl.DeviceIdType.MESH)
            if kind == "gather":
                src = ins[a] if self.sel[a] is None else ins[a].at[self.sel[a]]
                mine = outs[a].at[me]
                local = pltpu.make_async_copy(src, mine, local_sems.at[a])
                first = [remote(src, mine, 0, sibling)]
                first += [remote(src, mine, 1 + j, (*chip, c)) for j, chip in enumerate(chips)]
                passed = []
                for j, chip in enumerate(chips if with_passed else []):
                    got = outs[a].at[4 * chip[0] + 2 * chip[1] + c]
                    passed.append(remote(got, got, 4 + j, sibling))
            else:
                local = pltpu.make_async_copy(ins[a].at[me], outs[a].at[me], local_sems.at[a])
                first, passed = [], []
                for k in (1, 4, 2, 6, 5, 3, 7):
                    peer = ((1 - x) if k & 4 else x, (1 - y) if k & 2 else y, (1 - c) if k & 1 else c)
                    pid = 4 * peer[0] + 2 * peer[1] + peer[2]
                    first.append(remote(ins[a].at[pid], outs[a].at[me], k - 1, peer))
            items.append((local, first, passed))
        return items

    def start(self, ins, outs, sems):
        for local, first, _ in self._copies(ins, outs, sems, with_passed=False):
            local.start()
            for cp in first:
                cp.start()

    def wait(self, ins, outs, sems):
        items = self._copies(ins, outs, sems)
        for _, first, passed in items:
            for j, cp in enumerate(passed):
                first[1 + j].wait_recv()
                cp.start()
        for local, first, passed in items:
            if passed:
                first[0].wait_recv()
                for cp in passed:
                    cp.wait_recv()
                for cp in first + passed:
                    cp.wait_send()
            else:
                for cp in first:
                    cp.wait()
            local.wait()


def _exchange(name, comm):
    _, outs = _call(lambda: None, comm=comm, name=name, in_specs=[], out_specs=[], out_shape=[])()
    return outs


def _sigmoid(x):
    return 1.0 / (1.0 + jnp.exp(-x))


def _tile(t):
    return min(TOKEN_TILE, t)


def _load_weights(pairs, sems):
    @pl.when(pl.program_id(0) == 0)
    def _():
        copies = [pltpu.make_async_copy(src, dst, sems.at[n]) for n, (src, dst) in enumerate(pairs)]
        for cp in copies:
            cp.start()
        for cp in copies:
            cp.wait()


def _ffn_fwd(name, x, gamma, win_t, wout, comm=None):
    t, d = x.shape
    f = wout.shape[0]
    fc = f // HIDDEN_SPLIT
    tm = min(FFN_TILE, t)

    def body(x_ref, gam_ref, win_hbm, wo_hbm, xo_ref, xn_ref, gu_ref, win_v, wo_v, h_s, sems):
        _load_weights([(win_hbm, win_v), (wo_hbm, wo_v)], sems)
        xv = x_ref[...]
        r = lax.rsqrt(jnp.mean(xv * xv, axis=-1, keepdims=True) + EPS)
        xn = (xv * r * gam_ref[...]).astype(BF16)
        xn_ref[...] = xn
        for c in range(HIDDEN_SPLIT):
            lo = c * fc
            g = lax.dot_general(xn, win_v[lo:lo + fc, :], NT, preferred_element_type=F32)
            u = lax.dot_general(xn, win_v[f + lo:f + lo + fc, :], NT, preferred_element_type=F32)
            gu_ref[:, lo:lo + fc] = g.astype(BF16)
            gu_ref[:, f + lo:f + lo + fc] = u.astype(BF16)
            h_s[:, lo:lo + fc] = (g * _sigmoid(g) * u).astype(BF16)
        xo_ref[...] = xv + 0.5 * jnp.dot(h_s[...], wo_v[...], preferred_element_type=F32)

    hbm = pl.BlockSpec(memory_space=pl.ANY)
    return _call(
        body, comm=comm, name=name, grid=(t // tm,),
        in_specs=[pl.BlockSpec((tm, d), lambda i: (i, 0)), pl.BlockSpec((1, d), lambda i: (0, 0)), hbm, hbm],
        out_specs=[pl.BlockSpec((tm, d), lambda i: (i, 0)), pl.BlockSpec((tm, d), lambda i: (i, 0)),
                   pl.BlockSpec((tm, 2 * f), lambda i: (i, 0))],
        out_shape=[jax.ShapeDtypeStruct((t, d), F32), jax.ShapeDtypeStruct((t, d), BF16),
                   jax.ShapeDtypeStruct((t, 2 * f), BF16)],
        scratch_shapes=[pltpu.VMEM((2 * f, d), BF16), pltpu.VMEM((f, d), BF16), pltpu.VMEM((tm, f), BF16),
                        pltpu.SemaphoreType.DMA((2,))],
    )(x, gamma, win_t, wout)


def _ffn_bwd(name, dy, x, gamma, gu, win_t, wout, comm=None):
    t, d = x.shape
    f = wout.shape[0]
    fc = f // HIDDEN_SPLIT
    tm = min(FFN_TILE, t)

    def body(dy_ref, x_ref, gam_ref, gu_ref, win_hbm, wo_hbm,
             dx_ref, dgam_ref, dyh_ref, h_ref, dgu_ref, win_v, wo_v, sems):
        _load_weights([(win_hbm, win_v), (wo_hbm, wo_v)], sems)

        @pl.when(pl.program_id(0) == 0)
        def _():
            dgam_ref[...] = jnp.zeros_like(dgam_ref)

        dyb = (0.5 * dy_ref[...]).astype(BF16)
        dyh_ref[...] = dyb
        for c in range(HIDDEN_SPLIT):
            lo = c * fc
            dh = lax.dot_general(dyb, wo_v[lo:lo + fc, :], NT, preferred_element_type=F32)
            g = gu_ref[:, lo:lo + fc].astype(F32)
            u = gu_ref[:, f + lo:f + lo + fc].astype(F32)
            sig = _sigmoid(g)
            silu = g * sig
            h_ref[:, lo:lo + fc] = (silu * u).astype(BF16)
            dgu_ref[:, lo:lo + fc] = (dh * u * (sig * (1.0 + g * (1.0 - sig)))).astype(BF16)
            dgu_ref[:, f + lo:f + lo + fc] = (dh * silu).astype(BF16)
        dxn = jnp.dot(dgu_ref[...], win_v[...], preferred_element_type=F32)
        xv = x_ref[...]
        r = lax.rsqrt(jnp.mean(xv * xv, axis=-1, keepdims=True) + EPS)
        yv = xv * r
        dgam_ref[...] += jnp.sum(dxn * yv, axis=0, keepdims=True)
        dyn = dxn * gam_ref[...]
        dx_ref[...] = dy_ref[...] + r * (dyn - yv * jnp.mean(dyn * yv, axis=-1, keepdims=True))

    hbm = pl.BlockSpec(memory_space=pl.ANY)
    row = lambda width: pl.BlockSpec((tm, width), lambda i: (i, 0))
    return _call(
        body, comm=comm, name=name, grid=(t // tm,),
        in_specs=[row(d), row(d), pl.BlockSpec((1, d), lambda i: (0, 0)), row(2 * f), hbm, hbm],
        out_specs=[row(d), pl.BlockSpec((1, d), lambda i: (0, 0)), row(d), row(f), row(2 * f)],
        out_shape=[
            jax.ShapeDtypeStruct((t, d), F32),
            jax.ShapeDtypeStruct((1, d), F32),
            jax.ShapeDtypeStruct((t, d), BF16),
            jax.ShapeDtypeStruct((t, f), BF16),
            jax.ShapeDtypeStruct((t, 2 * f), BF16),
        ],
        scratch_shapes=[pltpu.VMEM((2 * f, d), BF16), pltpu.VMEM((f, d), BF16), pltpu.SemaphoreType.DMA((2,))],
    )(dy, x, gamma, gu, win_t, wout)


def _matmul_tn(name, a, b, out_dtype=BF16, comm=None, a_split=None):
    a_b = a.ndim == 3
    b_b = b.ndim == 3
    nb = a_split if a_split else a.shape[0] if a_b else b.shape[0] if b_b else 1
    t, m = a.shape[-2:]
    if a_split:
        m = m // a_split
    n = b.shape[-1]
    tk = min(CONTRACT_TILE, t)
    nt = t // tk

    def body(a_ref, b_ref, o_ref, acc):
        s = pl.program_id(1)

        @pl.when(s == 0)
        def _():
            acc[...] = jnp.zeros_like(acc)

        acc[...] += lax.dot_general(a_ref[...], b_ref[...], TN, preferred_element_type=F32)

        @pl.when(s == nt - 1)
        def _():
            o_ref[...] = acc[...].astype(o_ref.dtype)

    a_spec = (pl.BlockSpec((None, tk, m), lambda j, s: (j, s, 0)) if a_b
              else pl.BlockSpec((tk, m), lambda j, s: (s, j)) if a_split
              else pl.BlockSpec((tk, m), lambda j, s: (s, 0)))
    b_spec = (pl.BlockSpec((None, tk, n), lambda j, s: (j, s, 0)) if b_b
              else pl.BlockSpec((tk, n), lambda j, s: (s, 0)))
    return _call(
        body, comm=comm, name=name, grid=(nb, nt),
        in_specs=[a_spec, b_spec],
        out_specs=pl.BlockSpec((None, m, n), lambda j, s: (j, 0, 0)),
        out_shape=jax.ShapeDtypeStruct((nb, m, n), out_dtype),
        scratch_shapes=[pltpu.VMEM((m, n), F32)],
    )(a, b)


def _rms_matmul(name, x, gamma, w):
    t, d = x.shape
    n = w.shape[1]
    tm = _tile(t)

    def body(x_ref, gam_ref, w_ref, xn_ref, h_ref):
        xv = x_ref[...]
        r = lax.rsqrt(jnp.mean(xv * xv, axis=-1, keepdims=True) + EPS)
        xn = (xv * r * gam_ref[...]).astype(BF16)
        xn_ref[...] = xn
        h_ref[...] = jnp.dot(xn, w_ref[...], preferred_element_type=F32)

    return _call(
        body, name=name, grid=(t // tm,),
        in_specs=[pl.BlockSpec((tm, d), lambda i: (i, 0)),
                  pl.BlockSpec((1, d), lambda i: (0, 0)),
                  pl.BlockSpec((d, n), lambda i: (0, 0))],
        out_specs=[pl.BlockSpec((tm, d), lambda i: (i, 0)),
                   pl.BlockSpec((tm, n), lambda i: (i, 0))],
        out_shape=[jax.ShapeDtypeStruct((t, d), BF16), jax.ShapeDtypeStruct((t, n), F32)],
    )(x, gamma, w)


def _matmul_residual(name, a, w, res):
    t, kdim = a.shape
    n = w.shape[1]
    tm = _tile(t)

    def body(a_ref, w_ref, r_ref, o_ref):
        o_ref[...] = r_ref[...] + jnp.dot(a_ref[...], w_ref[...], preferred_element_type=F32)

    return _call(
        body, name=name, grid=(t // tm,),
        in_specs=[pl.BlockSpec((tm, kdim), lambda i: (i, 0)),
                  pl.BlockSpec((kdim, n), lambda i: (0, 0)),
                  pl.BlockSpec((tm, n), lambda i: (i, 0))],
        out_specs=pl.BlockSpec((tm, n), lambda i: (i, 0)),
        out_shape=jax.ShapeDtypeStruct((t, n), F32),
    )(a, w, res)


def _matmul_nt(name, dy, w):
    t, n = dy.shape
    kdim = w.shape[0]
    tm = _tile(t)

    def body(dy_ref, w_ref, da_ref, dyb_ref):
        dyb = dy_ref[...].astype(BF16)
        dyb_ref[...] = dyb
        da_ref[...] = lax.dot_general(dyb, w_ref[...], NT, preferred_element_type=F32)

    return _call(
        body, name=name, grid=(t // tm,),
        in_specs=[pl.BlockSpec((tm, n), lambda i: (i, 0)),
                  pl.BlockSpec((kdim, n), lambda i: (0, 0))],
        out_specs=[pl.BlockSpec((tm, kdim), lambda i: (i, 0)),
                   pl.BlockSpec((tm, n), lambda i: (i, 0))],
        out_shape=[jax.ShapeDtypeStruct((t, kdim), F32), jax.ShapeDtypeStruct((t, n), BF16)],
    )(dy, w)


def _matmul_nt_rms_bwd(name, dz, w, dres, x, gamma):
    t, kdim = dz.shape
    d = w.shape[0]
    tm = _tile(t)

    def body(dz_ref, w_ref, dres_ref, x_ref, gam_ref, dx_ref, dgam_ref):
        i = pl.program_id(0)

        @pl.when(i == 0)
        def _():
            dgam_ref[...] = jnp.zeros_like(dgam_ref)

        dxn = lax.dot_general(dz_ref[...], w_ref[...], NT, preferred_element_type=F32)
        xv = x_ref[...]
        r = lax.rsqrt(jnp.mean(xv * xv, axis=-1, keepdims=True) + EPS)
        yv = xv * r
        dgam_ref[...] += jnp.sum(dxn * yv, axis=0, keepdims=True)
        dyn = dxn * gam_ref[...]
        dx_ref[...] = dres_ref[...] + r * (dyn - yv * jnp.mean(dyn * yv, axis=-1, keepdims=True))

    return _call(
        body, name=name, grid=(t // tm,),
        in_specs=[pl.BlockSpec((tm, kdim), lambda i: (i, 0)),
                  pl.BlockSpec((d, kdim), lambda i: (0, 0)),
                  pl.BlockSpec((tm, d), lambda i: (i, 0)),
                  pl.BlockSpec((tm, d), lambda i: (i, 0)),
                  pl.BlockSpec((1, d), lambda i: (0, 0))],
        out_specs=[pl.BlockSpec((tm, d), lambda i: (i, 0)),
                   pl.BlockSpec((1, d), lambda i: (0, 0))],
        out_shape=[jax.ShapeDtypeStruct((t, d), F32), jax.ShapeDtypeStruct((1, d), F32)],
    )(dz, w, dres, x, gamma)


def _pool_means(uext_ref, pos, tm, g, win):
    cols = slice(g * GROUP, (g + 1) * GROUP)
    acc = uext_ref[pl.ds(HALO, tm), cols]
    for j in range(1, win):
        acc = acc + uext_ref[pl.ds(HALO - j, tm), cols]
    cnt = jnp.minimum(pos + 1, win).astype(F32)
    return acc / cnt - uext_ref[pl.ds(HALO, tm), cols]


def _conv_taps(gext_ref, cw_ref, cb_ref, start, rows):
    y = cb_ref[...] + cw_ref[0:1, :] * gext_ref[pl.ds(start, rows), :]
    for k in range(1, CONV_WIDTH):
        y = y + cw_ref[k:k + 1, :] * gext_ref[pl.ds(start + k, rows), :]
    return y


def _pool_conv_fwd(name, h, pool_w, pool_b, pool_scale, conv_w, conv_b, ln_g, ln_b):
    t, hw = h.shape
    pc = len(POOL_WINDOWS) * GROUP
    cc = (hw - pc) // 2
    tm = _tile(t)
    per = tm // HALO

    def body(h_ref, hp_ref, pw_ref, pb_ref, ps_ref, cw_ref, cb_ref, lg_ref, lb_ref, cat_ref, uext, gext):
        i = pl.program_id(0)
        keep = (i > 0).astype(F32)
        hp = hp_ref[...] * keep
        uext[0:HALO, :] = hp[:, :pc]
        uext[HALO:, :] = h_ref[:, :pc]
        gext[0:HALO, :] = hp[:, pc:pc + cc] * _sigmoid(hp[:, pc + cc:])
        gext[HALO:, :] = h_ref[:, pc:pc + cc] * _sigmoid(h_ref[:, pc + cc:])
        pos = i * tm + lax.broadcasted_iota(jnp.int32, (tm, 1), 0)
        for g, win in enumerate(POOL_WINDOWS):
            cols = slice(g * GROUP, (g + 1) * GROUP)
            pooled = _pool_means(uext, pos, tm, g, win)
            mixed = jnp.dot(pooled.astype(BF16), pw_ref[g].astype(BF16),
                            preferred_element_type=F32) + pb_ref[g:g + 1, :]
            cat_ref[:, cols] = (mixed * ps_ref[:, cols]).astype(BF16)
        y = _conv_taps(gext, cw_ref, cb_ref, HALO - (CONV_WIDTH - 1), tm)
        mu = jnp.mean(y, axis=-1, keepdims=True)
        dv = y - mu
        rstd = lax.rsqrt(jnp.mean(dv * dv, axis=-1, keepdims=True) + EPS)
        ln = dv * rstd * lg_ref[...] + lb_ref[...]
        cat_ref[:, pc:] = (ln * _sigmoid(ln)).astype(BF16)

    small = lambda a: pl.BlockSpec(a.shape, lambda i: (0,) * a.ndim)
    return _call(
        body, name=name, grid=(t // tm,),
        in_specs=[pl.BlockSpec((tm, hw), lambda i: (i, 0)),
                  pl.BlockSpec((HALO, hw), lambda i: (jnp.maximum(i * per - 1, 0), 0)),
                  small(pool_w), small(pool_b), small(pool_scale), small(conv_w), small(conv_b),
                  small(ln_g), small(ln_b)],
        out_specs=pl.BlockSpec((tm, pc + cc), lambda i: (i, 0)),
        out_shape=jax.ShapeDtypeStruct((t, pc + cc), BF16),
        scratch_shapes=[pltpu.VMEM((HALO + tm, pc), F32), pltpu.VMEM((HALO + tm, cc), F32)],
    )(h, h, pool_w, pool_b, pool_scale, conv_w, conv_b, ln_g, ln_b)


def _pool_conv_bwd(name, h, dcat, pool_w, pool_b, pool_scale, conv_w, conv_b, ln_g, ln_b, comm=None):
    t, hw = h.shape
    pc = len(POOL_WINDOWS) * GROUP
    cc = (hw - pc) // 2
    ng = len(POOL_WINDOWS)
    tm = _tile(t)
    per = tm // HALO
    nt = t // tm
    r2 = tm + HALO
    taps = CONV_WIDTH - 1

    def body(h_ref, hp_ref, hn_ref, dc_ref, dcn_ref, pw_ref, pb_ref, ps_ref, cw_ref, cb_ref, lg_ref, lb_ref,
             dh_ref, dpw_ref, dpb_ref, dps_ref, dcw_ref, dcb_ref, dlg_ref, dlb_ref,
             uext, gext, dcext, dqext, dycext):
        i = pl.program_id(0)

        @pl.when(i == 0)
        def _():
            for ref in (dpw_ref, dpb_ref, dps_ref, dcw_ref, dcb_ref, dlg_ref, dlb_ref):
                ref[...] = jnp.zeros_like(ref)

        keep_p = (i > 0).astype(F32)
        keep_n = (i < nt - 1).astype(F32)
        hp = hp_ref[...] * keep_p
        hn = hn_ref[...] * keep_n
        uext[0:HALO, :] = hp[:, :pc]
        uext[HALO:, :] = h_ref[:, :pc]
        gext[0:HALO, :] = hp[:, pc:pc + cc] * _sigmoid(hp[:, pc + cc:])
        gext[pl.ds(HALO, tm), :] = h_ref[:, pc:pc + cc] * _sigmoid(h_ref[:, pc + cc:])
        gext[pl.ds(HALO + tm, HALO), :] = hn[:, pc:pc + cc] * _sigmoid(hn[:, pc + cc:])
        dcext[0:tm, :] = dc_ref[...]
        dcext[pl.ds(tm, HALO), :] = dcn_ref[...] * keep_n

        pos = i * tm + lax.broadcasted_iota(jnp.int32, (tm, 1), 0)
        pos2 = i * tm + lax.broadcasted_iota(jnp.int32, (r2, 1), 0)
        for g, win in enumerate(POOL_WINDOWS):
            cols = slice(g * GROUP, (g + 1) * GROUP)
            wg = pw_ref[g].astype(BF16)
            dya = dcext[:, cols]
            dmixed = dya * ps_ref[:, cols]
            dpooled = lax.dot_general(dmixed.astype(BF16), wg, NT, preferred_element_type=F32)
            cnt2 = jnp.minimum(pos2 + 1, win).astype(F32)
            dqext[:, cols] = dpooled / cnt2
            du = -dpooled[0:tm]
            for j in range(win):
                du = du + dqext[pl.ds(j, tm), cols]
            dh_ref[:, cols] = du.astype(BF16)
            pooled = _pool_means(uext, pos, tm, g, win)
            pooled_b = pooled.astype(BF16)
            mixed = jnp.dot(pooled_b, wg, preferred_element_type=F32) + pb_ref[g:g + 1, :]
            dps_ref[:, cols] += jnp.sum(dya[0:tm] * mixed, axis=0, keepdims=True)
            dpb_ref[g:g + 1, :] += jnp.sum(dmixed[0:tm], axis=0, keepdims=True)
            dpw_ref[g] += lax.dot_general(pooled_b, dmixed[0:tm].astype(BF16), TN, preferred_element_type=F32)

        y = _conv_taps(gext, cw_ref, cb_ref, HALO - taps, r2)
        mu = jnp.mean(y, axis=-1, keepdims=True)
        dv = y - mu
        rstd = lax.rsqrt(jnp.mean(dv * dv, axis=-1, keepdims=True) + EPS)
        norm = dv * rstd
        ln = norm * lg_ref[...] + lb_ref[...]
        sig = _sigmoid(ln)
        dln = dcext[:, pc:] * (sig * (1.0 + ln * (1.0 - sig)))
        dnorm = dln * lg_ref[...]
        dyc = rstd * (dnorm - jnp.mean(dnorm, axis=-1, keepdims=True)
                      - norm * jnp.mean(dnorm * norm, axis=-1, keepdims=True))
        dycext[...] = dyc
        dlg_ref[...] += jnp.sum((dln * norm)[0:tm], axis=0, keepdims=True)
        dlb_ref[...] += jnp.sum(dln[0:tm], axis=0, keepdims=True)
        dcb_ref[...] += jnp.sum(dyc[0:tm], axis=0, keepdims=True)
        dyc_t = dycext[0:tm, :]
        dg = jnp.zeros((tm, cc), F32)
        for k in range(CONV_WIDTH):
            dcw_ref[k:k + 1, :] += jnp.sum(dyc_t * gext[pl.ds(HALO - taps + k, tm), :], axis=0, keepdims=True)
            dg = dg + cw_ref[k:k + 1, :] * dycext[pl.ds(taps - k, tm), :]
        a = h_ref[:, pc:pc + cc]
        sg = _sigmoid(h_ref[:, pc + cc:])
        dh_ref[:, pc:pc + cc] = (dg * sg).astype(BF16)
        dh_ref[:, pc + cc:] = (dg * a * sg * (1.0 - sg)).astype(BF16)

    small = lambda a: pl.BlockSpec(a.shape, lambda i: (0,) * a.ndim)
    smalls = (pool_w, pool_b, pool_scale, conv_w, conv_b, ln_g, ln_b)
    return _call(
        body, comm=comm, name=name, grid=(nt,),
        in_specs=[pl.BlockSpec((tm, hw), lambda i: (i, 0)),
                  pl.BlockSpec((HALO, hw), lambda i: (jnp.maximum(i * per - 1, 0), 0)),
                  pl.BlockSpec((HALO, hw), lambda i: (jnp.minimum((i + 1) * per, t // HALO - 1), 0)),
                  pl.BlockSpec((tm, pc + cc), lambda i: (i, 0)),
                  pl.BlockSpec((HALO, pc + cc), lambda i: (jnp.minimum((i + 1) * per, t // HALO - 1), 0)),
                  ] + [small(a) for a in smalls],
        out_specs=[pl.BlockSpec((tm, hw), lambda i: (i, 0))] + [small(a) for a in smalls],
        out_shape=[jax.ShapeDtypeStruct((t, hw), BF16)] + [jax.ShapeDtypeStruct(a.shape, F32) for a in smalls],
        scratch_shapes=[pltpu.VMEM((HALO + tm, pc), F32), pltpu.VMEM((HALO + tm + HALO, cc), F32),
                        pltpu.VMEM((r2, pc + cc), F32), pltpu.VMEM((r2, pc), F32), pltpu.VMEM((r2, cc), F32)],
    )(h, h, h, dcat, dcat, *smalls)


SQRT_HALF = 0.7071067811865476
INV_SQRT_2PI = 0.3989422804014327


def _sgu_core(zp_ref, lg_ref, lb_ref, ws_ref, bs_ref, vo_s, tm, sc, heads):
    zp = zp_ref[...]
    z = 0.5 * zp * (1.0 + lax.erf(zp * SQRT_HALF))
    u = z[:, :sc]
    v = z[:, sc:]
    mu = jnp.mean(v, axis=-1, keepdims=True)
    dv = v - mu
    rstd = lax.rsqrt(jnp.mean(dv * dv, axis=-1, keepdims=True) + EPS)
    norm = dv * rstd
    vb = (norm * lg_ref[...] + lb_ref[...]).astype(BF16)
    row = lax.broadcasted_iota(jnp.int32, (GROUP, GROUP), 0)
    col = lax.broadcasted_iota(jnp.int32, (GROUP, GROUP), 1)
    mask = (col <= row).astype(F32)
    wm = [ws_ref[hd] * mask for hd in range(heads)]
    for hd in range(heads):
        cols = slice(hd * GROUP, (hd + 1) * GROUP)
        wb = wm[hd].astype(BF16)
        for n in range(tm // GROUP):
            rows = slice(n * GROUP, (n + 1) * GROUP)
            vo_s[rows, cols] = jnp.dot(wb, vb[rows, cols], preferred_element_type=F32) + bs_ref[hd]
    return zp, u, norm, rstd, vb, wm, mask


def _sgu_fwd(name, zp, ln_g, ln_b, w_s, b_s):
    t, two_sc = zp.shape
    sc = two_sc // 2
    heads = sc // GROUP
    tm = _tile(t)

    def body(zp_ref, lg_ref, lb_ref, ws_ref, bs_ref, q_ref, vo_s):
        _, u, _, _, _, _, _ = _sgu_core(zp_ref, lg_ref, lb_ref, ws_ref, bs_ref, vo_s, tm, sc, heads)
        q_ref[...] = (u * vo_s[...]).astype(BF16)

    small = lambda a: pl.BlockSpec(a.shape, lambda i: (0,) * a.ndim)
    return _call(
        body, name=name, grid=(t // tm,),
        in_specs=[pl.BlockSpec((tm, two_sc), lambda i: (i, 0)), small(ln_g), small(ln_b), small(w_s), small(b_s)],
        out_specs=pl.BlockSpec((tm, sc), lambda i: (i, 0)),
        out_shape=jax.ShapeDtypeStruct((t, sc), BF16),
        scratch_shapes=[pltpu.VMEM((tm, sc), F32)],
    )(zp, ln_g, ln_b, w_s, b_s)


def _sgu_bwd(name, zp, dq, ln_g, ln_b, w_s, b_s):
    t, two_sc = zp.shape
    sc = two_sc // 2
    heads = sc // GROUP
    tm = _tile(t)
    nt = t // tm

    def body(zp_ref, dq_ref, lg_ref, lb_ref, ws_ref, bs_ref,
             dzp_ref, dlg_ref, dlb_ref, dws_ref, dbs_ref, vo_s, dvl_s, dws_acc):
        i = pl.program_id(0)

        @pl.when(i == 0)
        def _():
            dlg_ref[...] = jnp.zeros_like(dlg_ref)
            dlb_ref[...] = jnp.zeros_like(dlb_ref)
            dbs_ref[...] = jnp.zeros_like(dbs_ref)
            dws_acc[...] = jnp.zeros_like(dws_acc)

        zp, u, norm, rstd, vb, wm, mask = _sgu_core(zp_ref, lg_ref, lb_ref, ws_ref, bs_ref, vo_s, tm, sc, heads)
        dq = dq_ref[...]
        du = dq * vo_s[...]
        dvo = dq * u
        dvob = dvo.astype(BF16)
        for hd in range(heads):
            cols = slice(hd * GROUP, (hd + 1) * GROUP)
            wtb = jnp.transpose(wm[hd]).astype(BF16)
            for n in range(tm // GROUP):
                rows = slice(n * GROUP, (n + 1) * GROUP)
                blk = dvob[rows, cols]
                dws_acc[hd] += lax.dot_general(blk, vb[rows, cols], NT, preferred_element_type=F32)
                dvl_s[rows, cols] = jnp.dot(wtb, blk, preferred_element_type=F32)
                dbs_ref[hd] += jnp.sum(dvo[rows, cols], axis=-1, keepdims=True)
        dvl = dvl_s[...]
        dlg_ref[...] += jnp.sum(dvl * norm, axis=0, keepdims=True)
        dlb_ref[...] += jnp.sum(dvl, axis=0, keepdims=True)
        dnorm = dvl * lg_ref[...]
        dv = rstd * (dnorm - jnp.mean(dnorm, axis=-1, keepdims=True)
                     - norm * jnp.mean(dnorm * norm, axis=-1, keepdims=True))
        dgelu = 0.5 * (1.0 + lax.erf(zp * SQRT_HALF)) + zp * (INV_SQRT_2PI * jnp.exp(-0.5 * zp * zp))
        dzp_ref[:, :sc] = (du * dgelu[:, :sc]).astype(BF16)
        dzp_ref[:, sc:] = (dv * dgelu[:, sc:]).astype(BF16)

        @pl.when(i == nt - 1)
        def _():
            for hd in range(heads):
                dws_ref[hd] = dws_acc[hd] * mask

    small = lambda a: pl.BlockSpec(a.shape, lambda i: (0,) * a.ndim)
    smalls = (ln_g, ln_b, w_s, b_s)
    return _call(
        body, name=name, grid=(nt,),
        in_specs=[pl.BlockSpec((tm, two_sc), lambda i: (i, 0)), pl.BlockSpec((tm, sc), lambda i: (i, 0))]
                 + [small(a) for a in smalls],
        out_specs=[pl.BlockSpec((tm, two_sc), lambda i: (i, 0))] + [small(a) for a in smalls],
        out_shape=[jax.ShapeDtypeStruct((t, two_sc), BF16)] + [jax.ShapeDtypeStruct(a.shape, F32) for a in smalls],
        scratch_shapes=[pltpu.VMEM((tm, sc), F32), pltpu.VMEM((tm, sc), F32), pltpu.VMEM(w_s.shape, F32)],
    )(zp, dq, ln_g, ln_b, w_s, b_s)


def _loss_head(name, x, gamma, target):
    t, d = x.shape
    tm = _tile(t)

    def body(x_ref, gam_ref, tg_ref, loss_ref, dx_ref, dgam_ref):
        i = pl.program_id(0)

        @pl.when(i == 0)
        def _():
            loss_ref[...] = jnp.zeros_like(loss_ref)
            dgam_ref[...] = jnp.zeros_like(dgam_ref)

        xv = x_ref[...]
        r = lax.rsqrt(jnp.mean(xv * xv, axis=-1, keepdims=True) + EPS)
        yv = xv * r
        err = yv * gam_ref[...] - tg_ref[...]
        row = jnp.sum(err * err, axis=-1, keepdims=True)
        loss_ref[...] += (0.5 / d) * jnp.sum(row, axis=0, keepdims=True)
        dout = err * (1.0 / d)
        dgam_ref[...] += jnp.sum(dout * yv, axis=0, keepdims=True)
        dyn = dout * gam_ref[...]
        dx_ref[...] = r * (dyn - yv * jnp.mean(dyn * yv, axis=-1, keepdims=True))

    return _call(
        body, name=name, grid=(t // tm,),
        in_specs=[pl.BlockSpec((tm, d), lambda i: (i, 0)),
                  pl.BlockSpec((1, d), lambda i: (0, 0)),
                  pl.BlockSpec((tm, d), lambda i: (i, 0))],
        out_specs=[pl.BlockSpec((1, 1), lambda i: (0, 0)),
                   pl.BlockSpec((tm, d), lambda i: (i, 0)),
                   pl.BlockSpec((1, d), lambda i: (0, 0))],
        out_shape=[jax.ShapeDtypeStruct((1, 1), F32), jax.ShapeDtypeStruct((t, d), F32),
                   jax.ShapeDtypeStruct((1, d), F32)],
    )(x, gamma, target)


def _adamw(name, parts, w, m, v, rows):
    l_n, r_n, c_n = w.shape
    s_n = parts[0].shape[0]
    tr = min(rows, r_n)
    nr = r_n // tr
    c1 = 1.0 - ADAM_B1 ** ADAM_STEP
    c2 = 1.0 - ADAM_B2 ** ADAM_STEP

    def body(*refs):
        p_refs = refs[:l_n]
        w_ref, m_ref, v_ref, g_ref, d_ref, mo_ref, vo_ref = refs[l_n:]
        layer = pl.program_id(0)

        def update(p_ref):
            g = p_ref[0].astype(F32)
            for s in range(1, s_n):
                g = g + p_ref[s].astype(F32)
            mn = ADAM_B1 * m_ref[...] + (1.0 - ADAM_B1) * g
            vn = ADAM_B2 * v_ref[...] + (1.0 - ADAM_B2) * (g * g)
            m_hat = mn / c1
            v_hat = vn / c2
            g_ref[...] = g
            d_ref[...] = -ADAM_LR * (m_hat / (jnp.sqrt(v_hat) + ADAM_EPS) + ADAM_WD * w_ref[...])
            mo_ref[...] = mn
            vo_ref[...] = vn

        for j in range(l_n):
            pl.when(layer == j)(functools.partial(update, p_refs[j]))

    def part_spec(j):
        return pl.BlockSpec((s_n, tr, c_n), lambda l, i: (0, jnp.where(l == j, i, jnp.where(l < j, 0, nr - 1)), 0))

    blk = pl.BlockSpec((None, tr, c_n), lambda l, i: (l, i, 0))
    return _call(
        body, name=name, grid=(l_n, nr),
        in_specs=[part_spec(j) for j in range(l_n)] + [blk, blk, blk],
        out_specs=[blk] * 4,
        out_shape=[jax.ShapeDtypeStruct((l_n, r_n, c_n), F32)] * 4,
    )(*parts, w, m, v)


def _local_step(x, target, big, small, sched=None):
    t, d = x.shape
    n_layers = small["ffn1_norm"].shape[0]
    gb = {}
    gs = {}

    def row(a, l):
        return a[l:l + 1]

    def run(fn, name, *operands, **kw):
        comm = sched.plan(name, gb, gs) if sched is not None else None
        if comm is None:
            return fn(name, *operands, **kw)
        res, got = fn(name, *operands, comm=comm, **kw)
        sched.deliver(comm, got)
        return res

    saved = []
    xs = x
    for l in range(n_layers):
        rec = {"x_ffn1": xs}
        xs, rec["xn_ffn1"], rec["gu_ffn1"] = run(
            _ffn_fwd, f"ffn1_fwd_l{l}", xs, row(small["ffn1_norm"], l), big["ffn1_w_in", l], big["ffn1_w_out", l])
        rec["x_mix"] = xs
        if l % 2 == 0:
            rec["xn_mix"], rec["h"] = _rms_matmul(f"ab_in_l{l}", xs, row(small["mix_norm"], l), big["ab_w_in"])
            rec["cat"] = _pool_conv_fwd(f"pool_conv_fwd_l{l}", rec["h"], small["pool_w"], small["pool_b"],
                                        small["pool_scale"], small["conv_w"], small["conv_b"],
                                        small["conv_ln_g"], small["conv_ln_b"])
            xs = _matmul_residual(f"ab_out_l{l}", rec["cat"], big["ab_w_out"], xs)
        else:
            rec["xn_mix"], rec["zp"] = _rms_matmul(f"sgu_in_l{l}", xs, row(small["mix_norm"], l), big["sgu_w_in"])
            rec["q"] = _sgu_fwd(f"sgu_fwd_l{l}", rec["zp"], small["sgu_ln_g"], small["sgu_ln_b"],
                                small["sgu_w"], small["sgu_b"])
            xs = _matmul_residual(f"sgu_out_l{l}", rec["q"], big["sgu_w_out"], xs)
        rec["x_ffn2"] = xs
        xs, rec["xn_ffn2"], rec["gu_ffn2"] = run(
            _ffn_fwd, f"ffn2_fwd_l{l}", xs, row(small["ffn2_norm"], l), big["ffn2_w_in", l], big["ffn2_w_out", l])
        saved.append(rec)

    loss, dx, gs["final_norm"] = _loss_head("loss_head", xs, small["final_norm"], target)

    norm_rows = {"ffn1_norm": [None] * n_layers, "mix_norm": [None] * n_layers, "ffn2_norm": [None] * n_layers}

    def ffn_backward(tag, l, dx, rec):
        dx, dgam, dyh, hh, dgu = run(_ffn_bwd, f"{tag}_bwd_l{l}", dx, rec[f"x_{tag}"], row(small[f"{tag}_norm"], l),
                                     rec[f"gu_{tag}"], big[f"{tag}_w_in", l], big[f"{tag}_w_out", l])
        norm_rows[f"{tag}_norm"][l] = dgam
        gb[f"{tag}_w_out", l] = run(_matmul_tn, f"{tag}_dwout_l{l}", hh, dyh,
                                    a_split=HIDDEN_SPLIT).reshape(N_DEV, -1, d)
        gb[f"{tag}_w_in", l] = run(_matmul_tn, f"{tag}_dwin_l{l}", dgu, rec[f"xn_{tag}"],
                                   a_split=2 * HIDDEN_SPLIT).reshape(N_DEV, -1, d)
        return dx

    for l in reversed(range(n_layers)):
        rec = saved[l]
        dx = ffn_backward("ffn2", l, dx, rec)
        if l % 2 == 0:
            dcat, dxb = _matmul_nt(f"ab_out_bwd_l{l}", dx, big["ab_w_out"])
            gb["ab_w_out", 0] = _matmul_tn(f"ab_dwout_l{l}", rec["cat"], dxb)
            dh, gs["pool_w"], gs["pool_b"], gs["pool_scale"], gs["conv_w"], gs["conv_b"], gs["conv_ln_g"], \
                gs["conv_ln_b"] = run(
                    _pool_conv_bwd, f"pool_conv_bwd_l{l}", rec["h"], dcat, small["pool_w"], small["pool_b"],
                    small["pool_scale"], small["conv_w"], small["conv_b"], small["conv_ln_g"], small["conv_ln_b"])
            gb["ab_w_in", 0] = _matmul_tn(f"ab_dwin_l{l}", rec["xn_mix"], dh)
            dx, dgam = _matmul_nt_rms_bwd(f"ab_in_bwd_l{l}", dh, big["ab_w_in"], dx, rec["x_mix"],
                                          row(small["mix_norm"], l))
        else:
            dq, dxb = _matmul_nt(f"sgu_out_bwd_l{l}", dx, big["sgu_w_out"])
            gb["sgu_w_out", 0] = _matmul_tn(f"sgu_dwout_l{l}", rec["q"], dxb)
            dzp, gs["sgu_ln_g"], gs["sgu_ln_b"], gs["sgu_w"], gs["sgu_b"] = _sgu_bwd(
                f"sgu_bwd_l{l}", rec["zp"], dq, small["sgu_ln_g"], small["sgu_ln_b"], small["sgu_w"], small["sgu_b"])
            gb["sgu_w_in", 0] = _matmul_tn(f"sgu_dwin_l{l}", rec["xn_mix"], dzp)
            dx, dgam = _matmul_nt_rms_bwd(f"sgu_in_bwd_l{l}", dzp, big["sgu_w_in"], dx, rec["x_mix"],
                                          row(small["mix_norm"], l))
        norm_rows["mix_norm"][l] = dgam
        dx = ffn_backward("ffn1", l, dx, rec)

    for k, rows in norm_rows.items():
        gs[k] = jnp.concatenate(rows, axis=0)
    return loss, dx, gb, gs


SHARDED_SMALL = ("conv_w", "sgu_ln_g", "sgu_ln_b")
REPLICATED = ("ffn1_norm", "mix_norm", "ffn2_norm", "pool_w", "pool_b", "pool_scale", "conv_b", "conv_ln_g",
              "conv_ln_b", "sgu_w", "sgu_b", "final_norm")
WEIGHTS = ("ffn1_norm", "ffn1_w_in", "ffn1_w_out", "mix_norm", "ffn2_norm", "ffn2_w_in", "ffn2_w_out", "ab_w_in",
           "pool_w", "pool_b", "pool_scale", "conv_w", "conv_b", "conv_ln_g", "conv_ln_b", "ab_w_out", "sgu_w_in",
           "sgu_ln_g", "sgu_ln_b", "sgu_w", "sgu_b", "sgu_w_out", "final_norm")
LANES = 128


def _interleave_cols(g):
    n, k, c = g.shape
    return jnp.transpose(g, (1, 0, 2)).reshape(k, n * c)


def _split_cols(a):
    k, nc = a.shape
    return jnp.transpose(a.reshape(k, N_DEV, nc // N_DEV), (1, 0, 2))


def _as3(a):
    if a.ndim == 1:
        return a.reshape(1, 1, -1)
    if a.ndim == 2:
        return a.reshape(a.shape[0], 1, a.shape[1])
    return a.reshape(a.shape[0], -1, a.shape[-1])


def _pack_rows(a):
    flat = a.reshape(-1)
    pad = (-flat.shape[0]) % (8 * LANES)
    if pad:
        flat = jnp.concatenate([flat, jnp.zeros((pad,), flat.dtype)])
    return flat.reshape(-1, LANES)


FIRST_GATHER = (("ffn1_w_in", 0), ("ffn1_w_out", 0), ("conv_w", 0), ("sgu_ln_g", 0), ("sgu_ln_b", 0))
GATHER_PLAN = {
    "ffn1_fwd_l0": (("ab_w_in", 0), ("ab_w_out", 0), ("ffn2_w_in", 0), ("ffn2_w_out", 0)),
    "ffn2_fwd_l0": (("ffn1_w_in", 1), ("ffn1_w_out", 1), ("sgu_w_in", 0), ("sgu_w_out", 0)),
    "ffn1_fwd_l1": (("ffn2_w_in", 1), ("ffn2_w_out", 1)),
}
SCATTER_PLAN = {
    "ffn2_dwin_l1": (("ffn2_w_out", 1),),
    "ffn1_bwd_l1": (("ffn2_w_in", 1), ("sgu_w_out", 0)),
    "ffn1_dwout_l1": (("sgu_w_in", 0), ("sgu_ln_g", 0), ("sgu_ln_b", 0)),
    "ffn1_dwin_l1": (("ffn1_w_out", 1),),
    "ffn2_bwd_l0": (("ffn1_w_in", 1),),
    "ffn2_dwin_l0": (("ffn2_w_out", 0),),
    "pool_conv_bwd_l0": (("ffn2_w_in", 0), ("ab_w_out", 0)),
    "ffn1_dwout_l0": (("ab_w_in", 0), ("conv_w", 0)),
    "ffn1_dwin_l0": (("ffn1_w_out", 0),),
}
LAST_SCATTER = (("ffn1_w_in", 0),)


class _Schedule:
    def __init__(self, shards, big, small):
        self.shards, self.big, self.small = shards, big, small
        self.recv = {}
        self.pending = {}

    def gather_comm(self, keys):
        comm = _Comm()
        for key in keys:
            comm.gather(*self.shards[key])
        self.pending[id(comm)] = ("gather", keys)
        return comm

    def scatter_comm(self, keys, gb, gs):
        comm = _Comm()
        for name, l in keys:
            if name in ("ab_w_in", "sgu_w_in"):
                send = _split_cols(gb[name, l][0])
            elif name in ("ab_w_out", "sgu_w_out"):
                send = gb[name, l][0]
                send = send.reshape(N_DEV, -1, send.shape[-1])
            elif name == "conv_w":
                send = _split_cols(gs[name][:CONV_WIDTH])
            elif name in ("sgu_ln_g", "sgu_ln_b"):
                send = gs[name].reshape(N_DEV, 1, -1)
            else:
                send = gb[name, l]
            comm.scatter(send)
        self.pending[id(comm)] = ("scatter", keys)
        return comm

    def plan(self, name, gb, gs):
        if name in GATHER_PLAN:
            return self.gather_comm(GATHER_PLAN[name])
        if name in SCATTER_PLAN:
            return self.scatter_comm(SCATTER_PLAN[name], gb, gs)
        return None

    def deliver(self, comm, got):
        kind, keys = self.pending.pop(id(comm))
        for (name, l), arr in zip(keys, got):
            if kind == "scatter":
                self.recv[name, l] = arr
            elif name in ("ffn1_w_in", "ffn2_w_in"):
                self.big[name, l] = arr.reshape(-1, arr.shape[-1])
            elif name in ("ffn1_w_out", "ffn2_w_out"):
                self.big[name, l] = arr.reshape(-1, arr.shape[-1])
            elif name in ("ab_w_in", "sgu_w_in"):
                self.big[name] = _interleave_cols(arr)
            elif name in ("ab_w_out", "sgu_w_out"):
                self.big[name] = arr.reshape(-1, arr.shape[-1])
            elif name == "conv_w":
                self.small[name] = jnp.pad(_interleave_cols(arr), ((0, 1), (0, 0)))
            else:
                self.small[name] = arr.reshape(1, -1)


def kernel(x, ffn1_norm, ffn1_w_in, ffn1_w_out, mix_norm, ffn2_norm, ffn2_w_in, ffn2_w_out, ab_w_in, pool_w, pool_b, pool_scale, conv_w, conv_b, conv_ln_g, conv_ln_b, ab_w_out, sgu_w_in, sgu_ln_g, sgu_ln_b, sgu_w, sgu_b, sgu_w_out, final_norm, loss_target, m_ffn1_norm, m_ffn1_w_in, m_ffn1_w_out, m_mix_norm, m_ffn2_norm, m_ffn2_w_in, m_ffn2_w_out, m_ab_w_in, m_pool_w, m_pool_b, m_pool_scale, m_conv_w, m_conv_b, m_conv_ln_g, m_conv_ln_b, m_ab_w_out, m_sgu_w_in, m_sgu_ln_g, m_sgu_ln_b, m_sgu_w, m_sgu_b, m_sgu_w_out, m_final_norm, v_ffn1_norm, v_ffn1_w_in, v_ffn1_w_out, v_mix_norm, v_ffn2_norm, v_ffn2_w_in, v_ffn2_w_out, v_ab_w_in, v_pool_w, v_pool_b, v_pool_scale, v_conv_w, v_conv_b, v_conv_ln_g, v_conv_ln_b, v_ab_w_out, v_sgu_w_in, v_sgu_ln_g, v_sgu_ln_b, v_sgu_w, v_sgu_b, v_sgu_w_out, v_final_norm):
    args = dict(locals())
    w = {n: args[n] for n in WEIGHTS}
    m = {n: args["m_" + n] for n in WEIGHTS}
    v = {n: args["v_" + n] for n in WEIGHTS}
    n_layers = ffn1_norm.shape[0]

    shards = {}
    for n in ("ffn1_w_in", "ffn2_w_in"):
        wt = jnp.swapaxes(w[n], 1, 2).astype(BF16)
        for l in range(n_layers):
            shards[n, l] = (wt, l)
    for n in ("ffn1_w_out", "ffn2_w_out"):
        wb = w[n].astype(BF16)
        for l in range(n_layers):
            shards[n, l] = (wb, l)
    for n in ("ab_w_in", "ab_w_out", "sgu_w_in", "sgu_w_out"):
        shards[n, 0] = (w[n][0].astype(BF16), None)
    shards["conv_w", 0] = (conv_w[0], None)
    shards["sgu_ln_g", 0] = (sgu_ln_g, None)
    shards["sgu_ln_b", 0] = (sgu_ln_b, None)

    big = {}
    small = {
        "ffn1_norm": ffn1_norm, "mix_norm": mix_norm, "ffn2_norm": ffn2_norm, "final_norm": final_norm.reshape(1, -1),
        "pool_w": pool_w[0], "pool_b": pool_b[0], "pool_scale": pool_scale,
        "conv_b": conv_b, "conv_ln_g": conv_ln_g, "conv_ln_b": conv_ln_b,
        "sgu_w": sgu_w[0], "sgu_b": sgu_b[0][:, :, None],
    }
    sched = _Schedule(shards, big, small)
    first = sched.gather_comm(FIRST_GATHER)
    sched.deliver(first, _exchange("gather_first", first))

    loss, grad_x, gb, gs = _local_step(x[0], loss_target[0], big, small, sched)

    rep_grads = {
        "ffn1_norm": gs["ffn1_norm"], "mix_norm": gs["mix_norm"], "ffn2_norm": gs["ffn2_norm"],
        "pool_w": gs["pool_w"], "pool_b": gs["pool_b"], "pool_scale": gs["pool_scale"],
        "conv_b": gs["conv_b"], "conv_ln_g": gs["conv_ln_g"], "conv_ln_b": gs["conv_ln_b"],
        "sgu_w": gs["sgu_w"], "sgu_b": gs["sgu_b"], "final_norm": gs["final_norm"],
    }
    packs = [_pack_rows(rep_grads[n]) for n in REPLICATED] + [_pack_rows(loss)]
    offsets = [0]
    for p in packs:
        offsets.append(offsets[-1] + p.shape[0])
    packed = jnp.concatenate(packs, axis=0)
    n_rows = packed.shape[0]

    last = sched.scatter_comm(LAST_SCATTER, gb, gs)
    packed_id = last.gather(packed)
    got = _exchange("reduce_last", last)
    packed_all = got[packed_id]
    sched.deliver(last, got[:packed_id])
    recv = sched.recv

    out = {}
    for n in ("ffn1_w_in", "ffn2_w_in"):
        res = _adamw(f"adamw_{n}", [recv[n, l] for l in range(n_layers)], jnp.swapaxes(w[n], 1, 2),
                     jnp.swapaxes(m[n], 1, 2), jnp.swapaxes(v[n], 1, 2), 176)
        out[n] = [jnp.swapaxes(r, 1, 2) for r in res]
    for n in ("ffn1_w_out", "ffn2_w_out"):
        out[n] = _adamw(f"adamw_{n}", [recv[n, l] for l in range(n_layers)], w[n], m[n], v[n], 176)
    for n in ("ab_w_in", "ab_w_out", "sgu_w_in", "sgu_w_out") + SHARDED_SMALL:
        w3 = _as3(w[n])
        parts = recv[n, 0].reshape((N_DEV,) + w3.shape[1:])
        res = _adamw(f"adamw_{n}", [parts], w3, _as3(m[n]), _as3(v[n]), 512)
        out[n] = [r.reshape(w[n].shape) for r in res]

    def pack_rep(src):
        tail = [jnp.zeros((offsets[-1] - offsets[-2], LANES), F32)]
        return jnp.concatenate([_pack_rows(src[n]) for n in REPLICATED] + tail, axis=0)[None]

    res = _adamw("adamw_replicated", [packed_all], pack_rep(w), pack_rep(m), pack_rep(v), n_rows)
    for i, n in enumerate(REPLICATED):
        size = w[n].size
        out[n] = [r[0, offsets[i]:offsets[i + 1]].reshape(-1)[:size].reshape(w[n].shape) for r in res]
    loss_sum = res[0][0, offsets[-2], 0]

    return (loss_sum, grad_x[None],
            *[out[n][0] for n in WEIGHTS], *[out[n][1] for n in WEIGHTS],
            *[out[n][2] for n in WEIGHTS], *[out[n][3] for n in WEIGHTS])
```

```python
import functools

import jax
import jax.numpy as jnp
from jax import lax
from jax.experimental import pallas as pl
from jax.experimental.pallas import tpu as pltpu

F32 = jnp.float32
BF16 = jnp.bfloat16
EPS = 1e-6
N_DEV = 8
POOL_WINDOWS = (2, 4, 8, 16)
CONV_WIDTH = 31
HALO = 32
GROUP = 128
SUBLANES = 8
ROW_CHUNK = 32
SUBLANES = 8
ROW_CHUNK = 32
TOKEN_TILE = 512
CONTRACT_TILE = 2048
HIDDEN_SPLIT = 2
FFN_TILE = 256
ADAM_LR, ADAM_B1, ADAM_B2, ADAM_EPS, ADAM_WD, ADAM_STEP = 0.001, 0.9, 0.999, 1e-08, 0.01, 10
VMEM_LIMIT = 56 * 1024 * 1024

NT = (((1,), (1,)), ((), ()))
TN = (((0,), (0,)), ((), ()))


def _pallas(body, side_effects, **kw):
    params = pltpu.CompilerParams(vmem_limit_bytes=VMEM_LIMIT, has_side_effects=side_effects)
    return pl.pallas_call(body, compiler_params=params, **kw)


def _call(body, comm=None, **kw):
    if comm is None:
        return _pallas(body, False, **kw)
    in_specs = list(kw.pop("in_specs"))
    out_specs = kw.pop("out_specs")
    out_shape = kw.pop("out_shape")
    scratch = list(kw.pop("scratch_shapes", []))
    single = not isinstance(out_shape, (list, tuple))
    if single:
        out_specs, out_shape = [out_specs], [out_shape]
    n_in, n_out, n_scr = len(in_specs), len(out_shape), len(scratch)
    n_ci, n_co = len(comm.inputs), len(comm.out_shapes)
    grid = tuple(kw.get("grid", ()))

    def wrapped(*refs):
        pos = 0
        parts = []
        for n in (n_in, n_ci, n_out, n_co, n_scr, 3):
            parts.append(refs[pos:pos + n])
            pos += n
        a_in, c_in, a_out, c_out, a_scr, sems = parts
        if grid:
            step = 0
            for ax, g in enumerate(grid):
                step = step * g + pl.program_id(ax)
            total = functools.reduce(lambda p, q: p * q, grid)
            pl.when(step == 0)(lambda: comm.start(c_in, c_out, sems))
            body(*a_in, *a_out, *a_scr)
            pl.when(step == (3 * total) // 4)(lambda: comm.forward(c_in, c_out, sems))
            pl.when(step == total - 1)(lambda: comm.finish(c_in, c_out, sems))
        else:
            comm.start(c_in, c_out, sems)
            body(*a_in, *a_out, *a_scr)
            comm.forward(c_in, c_out, sems)
            comm.finish(c_in, c_out, sems)

    hbm = pl.BlockSpec(memory_space=pl.ANY)
    fn = _pallas(wrapped, True, in_specs=in_specs + [hbm] * n_ci, out_specs=list(out_specs) + [hbm] * n_co,
                 out_shape=list(out_shape) + list(comm.out_shapes), scratch_shapes=scratch + comm.semaphores(), **kw)

    def run(*operands):
        outs = fn(*operands, *comm.inputs)
        res = outs[:n_out]
        return (res[0] if single else res), outs[n_out:]

    return run


class _Comm:
    def __init__(self):
        self.inputs, self.sel, self.kinds, self.out_shapes = [], [], [], []

    def gather(self, arr, sel=None):
        block = arr.shape if sel is None else arr.shape[1:]
        self.inputs.append(arr)
        self.sel.append(sel)
        self.kinds.append("gather")
        self.out_shapes.append(jax.ShapeDtypeStruct((N_DEV,) + tuple(block), arr.dtype))
        return len(self.inputs) - 1

    def scatter(self, arr):
        self.inputs.append(arr)
        self.sel.append(None)
        self.kinds.append("scatter")
        self.out_shapes.append(jax.ShapeDtypeStruct(arr.shape, arr.dtype))
        return len(self.inputs) - 1

    def semaphores(self):
        n = len(self.inputs)
        return [pltpu.SemaphoreType.DMA((n, N_DEV - 1)), pltpu.SemaphoreType.DMA((n, N_DEV - 1)),
                pltpu.SemaphoreType.DMA((n,))]

    def _copies(self, ins, outs, sems, with_passed=True):
        send_sems, recv_sems, local_sems = sems
        x, y, c = lax.axis_index("x"), lax.axis_index("y"), lax.axis_index("c")
        me = 4 * x + 2 * y + c
        sibling = (x, y, 1 - c)
        chips = [(1 - x, y), (x, 1 - y), (1 - x, 1 - y)]
        items = []
        for a, kind in enumerate(self.kinds):
            def remote(src, dst, k, to, a=a):
                return pltpu.make_async_remote_copy(
                    src_ref=src, dst_ref=dst, send_sem=send_sems.at[a, k], recv_sem=recv_sems.at[a, k],
                    device_id=to, device_id_type=pl.DeviceIdType.MESH)
            if kind == "gather":
                src = ins[a] if self.sel[a] is None else ins[a].at[self.sel[a]]
                mine = outs[a].at[me]
                local = pltpu.make_async_copy(src, mine, local_sems.at[a])
                first = [remote(src, mine, 0, sibling)]
                first += [remote(src, mine, 1 + j, (*chip, c)) for j, chip in enumerate(chips)]
                passed = []
                for j, chip in enumerate(chips if with_passed else []):
                    got = outs[a].at[4 * chip[0] + 2 * chip[1] + c]
                    passed.append(remote(got, got, 4 + j, sibling))
            else:
                local = pltpu.make_async_copy(ins[a].at[me], outs[a].at[me], local_sems.at[a])
                first, passed = [], []
                for k in (1, 4, 2, 6, 5, 3, 7):
                    peer = ((1 - x) if k & 4 else x, (1 - y) if k & 2 else y, (1 - c) if k & 1 else c)
                    pid = 4 * peer[0] + 2 * peer[1] + peer[2]
                    first.append(remote(ins[a].at[pid], outs[a].at[me], k - 1, peer))
            items.append((local, first, passed))
        return items

    def start(self, ins, outs, sems):
        for local, first, _ in self._copies(ins, outs, sems, with_passed=False):
            local.start()
            for cp in first:
                cp.start()

    def forward(self, ins, outs, sems):
        for _, first, passed in self._copies(ins, outs, sems):
            for j, cp in enumerate(passed):
                first[1 + j].wait_recv()
                cp.start()

    def finish(self, ins, outs, sems):
        for local, first, passed in self._copies(ins, outs, sems):
            if passed:
                first[0].wait_recv()
                for cp in passed:
                    cp.wait_recv()
                for cp in first + passed:
                    cp.wait_send()
            else:
                for cp in first:
                    cp.wait()
            local.wait()


def _exchange(name, comm):
    _, outs = _call(lambda: None, comm=comm, name=name, in_specs=[], out_specs=[], out_shape=[])()
    return outs


def _sigmoid(x):
    return 1.0 / (1.0 + jnp.exp(-x))


def _tile(t):
    return min(TOKEN_TILE, t)


def _load_weights(pairs, sems):
    @pl.when(pl.program_id(0) == 0)
    def _():
        copies = [pltpu.make_async_copy(src, dst, sems.at[n]) for n, (src, dst) in enumerate(pairs)]
        for cp in copies:
            cp.start()
        for cp in copies:
            cp.wait()


def _ffn_fwd(name, x, gamma, win_t, wout, comm=None):
    t, d = x.shape
    f = wout.shape[0]
    fc = f // HIDDEN_SPLIT
    tm = min(FFN_TILE, t)

    def body(x_ref, gam_ref, win_hbm, wo_hbm, xo_ref, xn_ref, gu_ref, win_v, wo_v, h_s, sems):
        _load_weights([(win_hbm, win_v), (wo_hbm, wo_v)], sems)
        xv = x_ref[...]
        r = lax.rsqrt(jnp.mean(xv * xv, axis=-1, keepdims=True) + EPS)
        xn = (xv * r * gam_ref[...]).astype(BF16)
        xn_ref[...] = xn
        for c in range(HIDDEN_SPLIT):
            lo = c * fc
            g = lax.dot_general(xn, win_v[lo:lo + fc, :], NT, preferred_element_type=F32)
            u = lax.dot_general(xn, win_v[f + lo:f + lo + fc, :], NT, preferred_element_type=F32)
            gu_ref[:, lo:lo + fc] = g.astype(BF16)
            gu_ref[:, f + lo:f + lo + fc] = u.astype(BF16)
            h_s[:, lo:lo + fc] = (g * _sigmoid(g) * u).astype(BF16)
        xo_ref[...] = xv + 0.5 * jnp.dot(h_s[...], wo_v[...], preferred_element_type=F32)

    hbm = pl.BlockSpec(memory_space=pl.ANY)
    return _call(
        body, comm=comm, name=name, grid=(t // tm,),
        in_specs=[pl.BlockSpec((tm, d), lambda i: (i, 0)), pl.BlockSpec((1, d), lambda i: (0, 0)), hbm, hbm],
        out_specs=[pl.BlockSpec((tm, d), lambda i: (i, 0)), pl.BlockSpec((tm, d), lambda i: (i, 0)),
                   pl.BlockSpec((tm, 2 * f), lambda i: (i, 0))],
        out_shape=[jax.ShapeDtypeStruct((t, d), F32), jax.ShapeDtypeStruct((t, d), BF16),
                   jax.ShapeDtypeStruct((t, 2 * f), BF16)],
        scratch_shapes=[pltpu.VMEM((2 * f, d), BF16), pltpu.VMEM((f, d), BF16), pltpu.VMEM((tm, f), BF16),
                        pltpu.SemaphoreType.DMA((2,))],
    )(x, gamma, win_t, wout)


def _ffn_bwd(name, dy, x, gamma, gu, win_t, wout, comm=None):
    t, d = x.shape
    f = wout.shape[0]
    fc = f // HIDDEN_SPLIT
    tm = min(FFN_TILE, t)

    def body(dy_ref, x_ref, gam_ref, gu_ref, win_hbm, wo_hbm,
             dx_ref, dgam_ref, dyh_ref, h_ref, dgu_ref, win_v, wo_v, sems):
        _load_weights([(win_hbm, win_v), (wo_hbm, wo_v)], sems)

        @pl.when(pl.program_id(0) == 0)
        def _():
            dgam_ref[...] = jnp.zeros_like(dgam_ref)

        dyb = (0.5 * dy_ref[...]).astype(BF16)
        dyh_ref[...] = dyb
        for c in range(HIDDEN_SPLIT):
            lo = c * fc
            dh = lax.dot_general(dyb, wo_v[lo:lo + fc, :], NT, preferred_element_type=F32)
            g = gu_ref[:, lo:lo + fc].astype(F32)
            u = gu_ref[:, f + lo:f + lo + fc].astype(F32)
            sig = _sigmoid(g)
            silu = g * sig
            h_ref[:, lo:lo + fc] = (silu * u).astype(BF16)
            dgu_ref[:, lo:lo + fc] = (dh * u * (sig * (1.0 + g * (1.0 - sig)))).astype(BF16)
            dgu_ref[:, f + lo:f + lo + fc] = (dh * silu).astype(BF16)
        dxn = jnp.dot(dgu_ref[...], win_v[...], preferred_element_type=F32)
        xv = x_ref[...]
        r = lax.rsqrt(jnp.mean(xv * xv, axis=-1, keepdims=True) + EPS)
        yv = xv * r
        dgam_ref[...] += jnp.sum(dxn * yv, axis=0, keepdims=True)
        dyn = dxn * gam_ref[...]
        dx_ref[...] = dy_ref[...] + r * (dyn - yv * jnp.mean(dyn * yv, axis=-1, keepdims=True))

    hbm = pl.BlockSpec(memory_space=pl.ANY)
    row = lambda width: pl.BlockSpec((tm, width), lambda i: (i, 0))
    return _call(
        body, comm=comm, name=name, grid=(t // tm,),
        in_specs=[row(d), row(d), pl.BlockSpec((1, d), lambda i: (0, 0)), row(2 * f), hbm, hbm],
        out_specs=[row(d), pl.BlockSpec((1, d), lambda i: (0, 0)), row(d), row(f), row(2 * f)],
        out_shape=[
            jax.ShapeDtypeStruct((t, d), F32),
            jax.ShapeDtypeStruct((1, d), F32),
            jax.ShapeDtypeStruct((t, d), BF16),
            jax.ShapeDtypeStruct((t, f), BF16),
            jax.ShapeDtypeStruct((t, 2 * f), BF16),
        ],
        scratch_shapes=[pltpu.VMEM((2 * f, d), BF16), pltpu.VMEM((f, d), BF16), pltpu.SemaphoreType.DMA((2,))],
    )(dy, x, gamma, gu, win_t, wout)


def _matmul_tn(name, a, b, out_dtype=BF16, comm=None, a_split=None):
    a_b = a.ndim == 3
    b_b = b.ndim == 3
    nb = a_split if a_split else a.shape[0] if a_b else b.shape[0] if b_b else 1
    t, m = a.shape[-2:]
    if a_split:
        m = m // a_split
    n = b.shape[-1]
    tk = min(CONTRACT_TILE, t)
    nt = t // tk

    def body(a_ref, b_ref, o_ref, acc):
        s = pl.program_id(1)

        @pl.when(s == 0)
        def _():
            acc[...] = jnp.zeros_like(acc)

        acc[...] += lax.dot_general(a_ref[...], b_ref[...], TN, preferred_element_type=F32)

        @pl.when(s == nt - 1)
        def _():
            o_ref[...] = acc[...].astype(o_ref.dtype)

    a_spec = (pl.BlockSpec((None, tk, m), lambda j, s: (j, s, 0)) if a_b
              else pl.BlockSpec((tk, m), lambda j, s: (s, j)) if a_split
              else pl.BlockSpec((tk, m), lambda j, s: (s, 0)))
    b_spec = (pl.BlockSpec((None, tk, n), lambda j, s: (j, s, 0)) if b_b
              else pl.BlockSpec((tk, n), lambda j, s: (s, 0)))
    return _call(
        body, comm=comm, name=name, grid=(nb, nt),
        in_specs=[a_spec, b_spec],
        out_specs=pl.BlockSpec((None, m, n), lambda j, s: (j, 0, 0)),
        out_shape=jax.ShapeDtypeStruct((nb, m, n), out_dtype),
        scratch_shapes=[pltpu.VMEM((m, n), F32)],
    )(a, b)


def _rms_matmul(name, x, gamma, w):
    t, d = x.shape
    n = w.shape[1]
    tm = _tile(t)

    def body(x_ref, gam_ref, w_ref, xn_ref, h_ref):
        xv = x_ref[...]
        r = lax.rsqrt(jnp.mean(xv * xv, axis=-1, keepdims=True) + EPS)
        xn = (xv * r * gam_ref[...]).astype(BF16)
        xn_ref[...] = xn
        h_ref[...] = jnp.dot(xn, w_ref[...], preferred_element_type=F32)

    return _call(
        body, name=name, grid=(t // tm,),
        in_specs=[pl.BlockSpec((tm, d), lambda i: (i, 0)),
                  pl.BlockSpec((1, d), lambda i: (0, 0)),
                  pl.BlockSpec((d, n), lambda i: (0, 0))],
        out_specs=[pl.BlockSpec((tm, d), lambda i: (i, 0)),
                   pl.BlockSpec((tm, n), lambda i: (i, 0))],
        out_shape=[jax.ShapeDtypeStruct((t, d), BF16), jax.ShapeDtypeStruct((t, n), F32)],
    )(x, gamma, w)


def _matmul_residual(name, a, w, res):
    t, kdim = a.shape
    n = w.shape[1]
    tm = _tile(t)

    def body(a_ref, w_ref, r_ref, o_ref):
        o_ref[...] = r_ref[...] + jnp.dot(a_ref[...], w_ref[...], preferred_element_type=F32)

    return _call(
        body, name=name, grid=(t // tm,),
        in_specs=[pl.BlockSpec((tm, kdim), lambda i: (i, 0)),
                  pl.BlockSpec((kdim, n), lambda i: (0, 0)),
                  pl.BlockSpec((tm, n), lambda i: (i, 0))],
        out_specs=pl.BlockSpec((tm, n), lambda i: (i, 0)),
        out_shape=jax.ShapeDtypeStruct((t, n), F32),
    )(a, w, res)


def _matmul_nt(name, dy, w):
    t, n = dy.shape
    kdim = w.shape[0]
    tm = _tile(t)

    def body(dy_ref, w_ref, da_ref, dyb_ref):
        dyb = dy_ref[...].astype(BF16)
        dyb_ref[...] = dyb
        da_ref[...] = lax.dot_general(dyb, w_ref[...], NT, preferred_element_type=F32)

    return _call(
        body, name=name, grid=(t // tm,),
        in_specs=[pl.BlockSpec((tm, n), lambda i: (i, 0)),
                  pl.BlockSpec((kdim, n), lambda i: (0, 0))],
        out_specs=[pl.BlockSpec((tm, kdim), lambda i: (i, 0)),
                   pl.BlockSpec((tm, n), lambda i: (i, 0))],
        out_shape=[jax.ShapeDtypeStruct((t, kdim), F32), jax.ShapeDtypeStruct((t, n), BF16)],
    )(dy, w)


def _matmul_nt_rms_bwd(name, dz, w, dres, x, gamma):
    t, kdim = dz.shape
    d = w.shape[0]
    tm = _tile(t)

    def body(dz_ref, w_ref, dres_ref, x_ref, gam_ref, dx_ref, dgam_ref):
        i = pl.program_id(0)

        @pl.when(i == 0)
        def _():
            dgam_ref[...] = jnp.zeros_like(dgam_ref)

        dxn = lax.dot_general(dz_ref[...], w_ref[...], NT, preferred_element_type=F32)
        xv = x_ref[...]
        r = lax.rsqrt(jnp.mean(xv * xv, axis=-1, keepdims=True) + EPS)
        yv = xv * r
        dgam_ref[...] += jnp.sum(dxn * yv, axis=0, keepdims=True)
        dyn = dxn * gam_ref[...]
        dx_ref[...] = dres_ref[...] + r * (dyn - yv * jnp.mean(dyn * yv, axis=-1, keepdims=True))

    return _call(
        body, name=name, grid=(t // tm,),
        in_specs=[pl.BlockSpec((tm, kdim), lambda i: (i, 0)),
                  pl.BlockSpec((d, kdim), lambda i: (0, 0)),
                  pl.BlockSpec((tm, d), lambda i: (i, 0)),
                  pl.BlockSpec((tm, d), lambda i: (i, 0)),
                  pl.BlockSpec((1, d), lambda i: (0, 0))],
        out_specs=[pl.BlockSpec((tm, d), lambda i: (i, 0)),
                   pl.BlockSpec((1, d), lambda i: (0, 0))],
        out_shape=[jax.ShapeDtypeStruct((t, d), F32), jax.ShapeDtypeStruct((1, d), F32)],
    )(dz, w, dres, x, gamma)


def _pool_means(uext_ref, pos, tm, g, win):
    cols = slice(g * GROUP, (g + 1) * GROUP)
    acc = uext_ref[pl.ds(HALO, tm), cols]
    for j in range(1, win):
        acc = acc + uext_ref[pl.ds(HALO - j, tm), cols]
    cnt = jnp.minimum(pos + 1, win).astype(F32)
    return acc / cnt - uext_ref[pl.ds(HALO, tm), cols]


def _shifted_copies(src_ref, dst_ref, rows):
    for b in range(SUBLANES):
        dst_ref[b, pl.ds(0, rows), :] = src_ref[pl.ds(b, rows), :]


def _tap_sum(sh_ref, cw_ref, offsets, out_ref, n_rows, bias_ref=None):
    width = out_ref.shape[-1]

    def chunk(c, carry):
        r0 = pl.multiple_of(c * ROW_CHUNK, ROW_CHUNK)
        acc = (jnp.zeros((ROW_CHUNK, width), F32) if bias_ref is None
               else jnp.broadcast_to(bias_ref[...], (ROW_CHUNK, width)))
        for k, off in enumerate(offsets):
            a, b = divmod(off, SUBLANES)
            acc = acc + cw_ref[k:k + 1, :] * sh_ref[b, pl.ds(r0 + SUBLANES * a, ROW_CHUNK), :]
        out_ref[pl.ds(r0, ROW_CHUNK), :] = acc
        return carry

    lax.fori_loop(0, n_rows // ROW_CHUNK, chunk, 0)


def _pool_conv_fwd(name, h, pool_w, pool_b, pool_scale, conv_w, conv_b, ln_g, ln_b):
    t, hw = h.shape
    pc = len(POOL_WINDOWS) * GROUP
    cc = (hw - pc) // 2
    tm = _tile(t)
    per = tm // HALO

    def body(h_ref, hp_ref, pw_ref, pb_ref, ps_ref, cw_ref, cb_ref, lg_ref, lb_ref, cat_ref, uext, gext, gsh, y_s):
        i = pl.program_id(0)
        keep = (i > 0).astype(F32)
        hp = hp_ref[...] * keep
        uext[0:HALO, :] = hp[:, :pc]
        uext[HALO:, :] = h_ref[:, :pc]
        gext[0:HALO, :] = hp[:, pc:pc + cc] * _sigmoid(hp[:, pc + cc:])
        gext[pl.ds(HALO, tm), :] = h_ref[:, pc:pc + cc] * _sigmoid(h_ref[:, pc + cc:])
        gext[pl.ds(HALO + tm, SUBLANES), :] = jnp.zeros((SUBLANES, cc), F32)
        pos = i * tm + lax.broadcasted_iota(jnp.int32, (tm, 1), 0)
        for g, win in enumerate(POOL_WINDOWS):
            cols = slice(g * GROUP, (g + 1) * GROUP)
            pooled = _pool_means(uext, pos, tm, g, win)
            mixed = jnp.dot(pooled.astype(BF16), pw_ref[g].astype(BF16),
                            preferred_element_type=F32) + pb_ref[g:g + 1, :]
            cat_ref[:, cols] = (mixed * ps_ref[:, cols]).astype(BF16)
        _shifted_copies(gext, gsh, HALO + tm)
        _tap_sum(gsh, cw_ref, [HALO - (CONV_WIDTH - 1) + k for k in range(CONV_WIDTH)], y_s, tm, cb_ref)
        y = y_s[...]
        mu = jnp.mean(y, axis=-1, keepdims=True)
        dv = y - mu
        rstd = lax.rsqrt(jnp.mean(dv * dv, axis=-1, keepdims=True) + EPS)
        ln = dv * rstd * lg_ref[...] + lb_ref[...]
        cat_ref[:, pc:] = (ln * _sigmoid(ln)).astype(BF16)

    small = lambda a: pl.BlockSpec(a.shape, lambda i: (0,) * a.ndim)
    return _call(
        body, name=name, grid=(t // tm,),
        in_specs=[pl.BlockSpec((tm, hw), lambda i: (i, 0)),
                  pl.BlockSpec((HALO, hw), lambda i: (jnp.maximum(i * per - 1, 0), 0)),
                  small(pool_w), small(pool_b), small(pool_scale), small(conv_w), small(conv_b),
                  small(ln_g), small(ln_b)],
        out_specs=pl.BlockSpec((tm, pc + cc), lambda i: (i, 0)),
        out_shape=jax.ShapeDtypeStruct((t, pc + cc), BF16),
        scratch_shapes=[pltpu.VMEM((HALO + tm, pc), F32), pltpu.VMEM((HALO + tm + SUBLANES, cc), F32),
                        pltpu.VMEM((SUBLANES, HALO + tm, cc), F32), pltpu.VMEM((tm, cc), F32)],
    )(h, h, pool_w, pool_b, pool_scale, conv_w, conv_b, ln_g, ln_b)


def _pool_conv_bwd(name, h, dcat, pool_w, pool_b, pool_scale, conv_w, conv_b, ln_g, ln_b, comm=None):
    t, hw = h.shape
    pc = len(POOL_WINDOWS) * GROUP
    cc = (hw - pc) // 2
    ng = len(POOL_WINDOWS)
    tm = _tile(t)
    per = tm // HALO
    nt = t // tm
    r2 = tm + HALO
    taps = CONV_WIDTH - 1

    def body(h_ref, hp_ref, hn_ref, dc_ref, dcn_ref, pw_ref, pb_ref, ps_ref, cw_ref, cb_ref, lg_ref, lb_ref,
             dh_ref, dpw_ref, dpb_ref, dps_ref, dcw_ref, dcb_ref, dlg_ref, dlb_ref,
             uext, gext, dcext, dqext, dycext, shifted, y_s, dg_s, dcw_acc):
        i = pl.program_id(0)

        @pl.when(i == 0)
        def _():
            for ref in (dpw_ref, dpb_ref, dps_ref, dcw_ref, dcb_ref, dlg_ref, dlb_ref, dcw_acc):
                ref[...] = jnp.zeros_like(ref)

        keep_p = (i > 0).astype(F32)
        keep_n = (i < nt - 1).astype(F32)
        hp = hp_ref[...] * keep_p
        hn = hn_ref[...] * keep_n
        uext[0:HALO, :] = hp[:, :pc]
        uext[HALO:, :] = h_ref[:, :pc]
        gext[0:HALO, :] = hp[:, pc:pc + cc] * _sigmoid(hp[:, pc + cc:])
        gext[pl.ds(HALO, tm), :] = h_ref[:, pc:pc + cc] * _sigmoid(h_ref[:, pc + cc:])
        gext[pl.ds(HALO + tm, HALO), :] = hn[:, pc:pc + cc] * _sigmoid(hn[:, pc + cc:])
        gext[pl.ds(HALO + tm + HALO, SUBLANES), :] = jnp.zeros((SUBLANES, cc), F32)
        dcext[0:tm, :] = dc_ref[...]
        dcext[pl.ds(tm, HALO), :] = dcn_ref[...] * keep_n

        pos = i * tm + lax.broadcasted_iota(jnp.int32, (tm, 1), 0)
        pos2 = i * tm + lax.broadcasted_iota(jnp.int32, (r2, 1), 0)
        for g, win in enumerate(POOL_WINDOWS):
            cols = slice(g * GROUP, (g + 1) * GROUP)
            wg = pw_ref[g].astype(BF16)
            dya = dcext[:, cols]
            dmixed = dya * ps_ref[:, cols]
            dpooled = lax.dot_general(dmixed.astype(BF16), wg, NT, preferred_element_type=F32)
            cnt2 = jnp.minimum(pos2 + 1, win).astype(F32)
            dqext[:, cols] = dpooled / cnt2
            du = -dpooled[0:tm]
            for j in range(win):
                du = du + dqext[pl.ds(j, tm), cols]
            dh_ref[:, cols] = du.astype(BF16)
            pooled = _pool_means(uext, pos, tm, g, win)
            pooled_b = pooled.astype(BF16)
            mixed = jnp.dot(pooled_b, wg, preferred_element_type=F32) + pb_ref[g:g + 1, :]
            dps_ref[:, cols] += jnp.sum(dya[0:tm] * mixed, axis=0, keepdims=True)
            dpb_ref[g:g + 1, :] += jnp.sum(dmixed[0:tm], axis=0, keepdims=True)
            dpw_ref[g] += lax.dot_general(pooled_b, dmixed[0:tm].astype(BF16), TN, preferred_element_type=F32)

        _shifted_copies(gext, shifted, HALO + tm + HALO)
        _tap_sum(shifted, cw_ref, [HALO - taps + k for k in range(CONV_WIDTH)], y_s, r2, cb_ref)
        y = y_s[...]
        mu = jnp.mean(y, axis=-1, keepdims=True)
        dv = y - mu
        rstd = lax.rsqrt(jnp.mean(dv * dv, axis=-1, keepdims=True) + EPS)
        norm = dv * rstd
        ln = norm * lg_ref[...] + lb_ref[...]
        sig = _sigmoid(ln)
        dln = dcext[:, pc:] * (sig * (1.0 + ln * (1.0 - sig)))
        dnorm = dln * lg_ref[...]
        dyc = rstd * (dnorm - jnp.mean(dnorm, axis=-1, keepdims=True)
                      - norm * jnp.mean(dnorm * norm, axis=-1, keepdims=True))
        dycext[pl.ds(0, r2), :] = dyc
        dycext[pl.ds(r2, SUBLANES), :] = jnp.zeros((SUBLANES, cc), F32)
        dlg_ref[...] += jnp.sum((dln * norm)[0:tm], axis=0, keepdims=True)
        dlb_ref[...] += jnp.sum(dln[0:tm], axis=0, keepdims=True)
        dcb_ref[...] += jnp.sum(dyc[0:tm], axis=0, keepdims=True)

        def fold(c, carry):
            r0 = pl.multiple_of(c * ROW_CHUNK, ROW_CHUNK)
            dchunk = dycext[pl.ds(r0, ROW_CHUNK), :]
            for k in range(CONV_WIDTH):
                a8, b8 = divmod(HALO - taps + k, SUBLANES)
                prod = dchunk * shifted[b8, pl.ds(r0 + SUBLANES * a8, ROW_CHUNK), :]
                part = prod[0:SUBLANES]
                for q in range(1, ROW_CHUNK // SUBLANES):
                    part = part + prod[q * SUBLANES:(q + 1) * SUBLANES]
                dcw_acc[k] += part
            return carry

        lax.fori_loop(0, tm // ROW_CHUNK, fold, 0)

        @pl.when(i == nt - 1)
        def _():
            for k in range(CONV_WIDTH):
                dcw_ref[k:k + 1, :] = jnp.sum(dcw_acc[k], axis=0, keepdims=True)

        _shifted_copies(dycext, shifted, r2)
        _tap_sum(shifted, cw_ref, [taps - k for k in range(CONV_WIDTH)], dg_s, tm)
        dg = dg_s[...]
        a = h_ref[:, pc:pc + cc]
        sg = _sigmoid(h_ref[:, pc + cc:])
        dh_ref[:, pc:pc + cc] = (dg * sg).astype(BF16)
        dh_ref[:, pc + cc:] = (dg * a * sg * (1.0 - sg)).astype(BF16)

    small = lambda a: pl.BlockSpec(a.shape, lambda i: (0,) * a.ndim)
    smalls = (pool_w, pool_b, pool_scale, conv_w, conv_b, ln_g, ln_b)
    return _call(
        body, comm=comm, name=name, grid=(nt,),
        in_specs=[pl.BlockSpec((tm, hw), lambda i: (i, 0)),
                  pl.BlockSpec((HALO, hw), lambda i: (jnp.maximum(i * per - 1, 0), 0)),
                  pl.BlockSpec((HALO, hw), lambda i: (jnp.minimum((i + 1) * per, t // HALO - 1), 0)),
                  pl.BlockSpec((tm, pc + cc), lambda i: (i, 0)),
                  pl.BlockSpec((HALO, pc + cc), lambda i: (jnp.minimum((i + 1) * per, t // HALO - 1), 0)),
                  ] + [small(a) for a in smalls],
        out_specs=[pl.BlockSpec((tm, hw), lambda i: (i, 0))] + [small(a) for a in smalls],
        out_shape=[jax.ShapeDtypeStruct((t, hw), BF16)] + [jax.ShapeDtypeStruct(a.shape, F32) for a in smalls],
        scratch_shapes=[pltpu.VMEM((HALO + tm, pc), F32), pltpu.VMEM((HALO + tm + HALO + SUBLANES, cc), F32),
                        pltpu.VMEM((r2, pc + cc), F32), pltpu.VMEM((r2, pc), F32),
                        pltpu.VMEM((r2 + SUBLANES, cc), F32), pltpu.VMEM((SUBLANES, HALO + tm + HALO, cc), F32),
                        pltpu.VMEM((r2, cc), F32), pltpu.VMEM((tm, cc), F32),
                        pltpu.VMEM((CONV_WIDTH + 1, SUBLANES, cc), F32)],
    )(h, h, h, dcat, dcat, *smalls)


SQRT_HALF = 0.7071067811865476
INV_SQRT_2PI = 0.3989422804014327


def _sgu_core(zp_ref, lg_ref, lb_ref, ws_ref, bs_ref, vo_s, tm, sc, heads):
    zp = zp_ref[...]
    z = 0.5 * zp * (1.0 + lax.erf(zp * SQRT_HALF))
    u = z[:, :sc]
    v = z[:, sc:]
    mu = jnp.mean(v, axis=-1, keepdims=True)
    dv = v - mu
    rstd = lax.rsqrt(jnp.mean(dv * dv, axis=-1, keepdims=True) + EPS)
    norm = dv * rstd
    vb = (norm * lg_ref[...] + lb_ref[...]).astype(BF16)
    row = lax.broadcasted_iota(jnp.int32, (GROUP, GROUP), 0)
    col = lax.broadcasted_iota(jnp.int32, (GROUP, GROUP), 1)
    mask = (col <= row).astype(F32)
    wm = [ws_ref[hd] * mask for hd in range(heads)]
    for hd in range(heads):
        cols = slice(hd * GROUP, (hd + 1) * GROUP)
        wb = wm[hd].astype(BF16)
        for n in range(tm // GROUP):
            rows = slice(n * GROUP, (n + 1) * GROUP)
            vo_s[rows, cols] = jnp.dot(wb, vb[rows, cols], preferred_element_type=F32) + bs_ref[hd]
    return zp, u, norm, rstd, vb, wm, mask


def _sgu_fwd(name, zp, ln_g, ln_b, w_s, b_s):
    t, two_sc = zp.shape
    sc = two_sc // 2
    heads = sc // GROUP
    tm = _tile(t)

    def body(zp_ref, lg_ref, lb_ref, ws_ref, bs_ref, q_ref, vo_s):
        _, u, _, _, _, _, _ = _sgu_core(zp_ref, lg_ref, lb_ref, ws_ref, bs_ref, vo_s, tm, sc, heads)
        q_ref[...] = (u * vo_s[...]).astype(BF16)

    small = lambda a: pl.BlockSpec(a.shape, lambda i: (0,) * a.ndim)
    return _call(
        body, name=name, grid=(t // tm,),
        in_specs=[pl.BlockSpec((tm, two_sc), lambda i: (i, 0)), small(ln_g), small(ln_b), small(w_s), small(b_s)],
        out_specs=pl.BlockSpec((tm, sc), lambda i: (i, 0)),
        out_shape=jax.ShapeDtypeStruct((t, sc), BF16),
        scratch_shapes=[pltpu.VMEM((tm, sc), F32)],
    )(zp, ln_g, ln_b, w_s, b_s)


def _sgu_bwd(name, zp, dq, ln_g, ln_b, w_s, b_s):
    t, two_sc = zp.shape
    sc = two_sc // 2
    heads = sc // GROUP
    tm = _tile(t)
    nt = t // tm

    def body(zp_ref, dq_ref, lg_ref, lb_ref, ws_ref, bs_ref,
             dzp_ref, dlg_ref, dlb_ref, dws_ref, dbs_ref, vo_s, dvl_s, dws_acc):
        i = pl.program_id(0)

        @pl.when(i == 0)
        def _():
            dlg_ref[...] = jnp.zeros_like(dlg_ref)
            dlb_ref[...] = jnp.zeros_like(dlb_ref)
            dbs_ref[...] = jnp.zeros_like(dbs_ref)
            dws_acc[...] = jnp.zeros_like(dws_acc)

        zp, u, norm, rstd, vb, wm, mask = _sgu_core(zp_ref, lg_ref, lb_ref, ws_ref, bs_ref, vo_s, tm, sc, heads)
        dq = dq_ref[...]
        du = dq * vo_s[...]
        dvo = dq * u
        dvob = dvo.astype(BF16)
        for hd in range(heads):
            cols = slice(hd * GROUP, (hd + 1) * GROUP)
            wtb = jnp.transpose(wm[hd]).astype(BF16)
            for n in range(tm // GROUP):
                rows = slice(n * GROUP, (n + 1) * GROUP)
                blk = dvob[rows, cols]
                dws_acc[hd] += lax.dot_general(blk, vb[rows, cols], NT, preferred_element_type=F32)
                dvl_s[rows, cols] = jnp.dot(wtb, blk, preferred_element_type=F32)
                dbs_ref[hd] += jnp.sum(dvo[rows, cols], axis=-1, keepdims=True)
        dvl = dvl_s[...]
        dlg_ref[...] += jnp.sum(dvl * norm, axis=0, keepdims=True)
        dlb_ref[...] += jnp.sum(dvl, axis=0, keepdims=True)
        dnorm = dvl * lg_ref[...]
        dv = rstd * (dnorm - jnp.mean(dnorm, axis=-1, keepdims=True)
                     - norm * jnp.mean(dnorm * norm, axis=-1, keepdims=True))
        dgelu = 0.5 * (1.0 + lax.erf(zp * SQRT_HALF)) + zp * (INV_SQRT_2PI * jnp.exp(-0.5 * zp * zp))
        dzp_ref[:, :sc] = (du * dgelu[:, :sc]).astype(BF16)
        dzp_ref[:, sc:] = (dv * dgelu[:, sc:]).astype(BF16)

        @pl.when(i == nt - 1)
        def _():
            for hd in range(heads):
                dws_ref[hd] = dws_acc[hd] * mask

    small = lambda a: pl.BlockSpec(a.shape, lambda i: (0,) * a.ndim)
    smalls = (ln_g, ln_b, w_s, b_s)
    return _call(
        body, name=name, grid=(nt,),
        in_specs=[pl.BlockSpec((tm, two_sc), lambda i: (i, 0)), pl.BlockSpec((tm, sc), lambda i: (i, 0))]
                 + [small(a) for a in smalls],
        out_specs=[pl.BlockSpec((tm, two_sc), lambda i: (i, 0))] + [small(a) for a in smalls],
        out_shape=[jax.ShapeDtypeStruct((t, two_sc), BF16)] + [jax.ShapeDtypeStruct(a.shape, F32) for a in smalls],
        scratch_shapes=[pltpu.VMEM((tm, sc), F32), pltpu.VMEM((tm, sc), F32), pltpu.VMEM(w_s.shape, F32)],
    )(zp, dq, ln_g, ln_b, w_s, b_s)


def _loss_head(name, x, gamma, target):
    t, d = x.shape
    tm = _tile(t)

    def body(x_ref, gam_ref, tg_ref, loss_ref, dx_ref, dgam_ref):
        i = pl.program_id(0)

        @pl.when(i == 0)
        def _():
            loss_ref[...] = jnp.zeros_like(loss_ref)
            dgam_ref[...] = jnp.zeros_like(dgam_ref)

        xv = x_ref[...]
        r = lax.rsqrt(jnp.mean(xv * xv, axis=-1, keepdims=True) + EPS)
        yv = xv * r
        err = yv * gam_ref[...] - tg_ref[...]
        row = jnp.sum(err * err, axis=-1, keepdims=True)
        loss_ref[...] += (0.5 / d) * jnp.sum(row, axis=0, keepdims=True)
        dout = err * (1.0 / d)
        dgam_ref[...] += jnp.sum(dout * yv, axis=0, keepdims=True)
        dyn = dout * gam_ref[...]
        dx_ref[...] = r * (dyn - yv * jnp.mean(dyn * yv, axis=-1, keepdims=True))

    return _call(
        body, name=name, grid=(t // tm,),
        in_specs=[pl.BlockSpec((tm, d), lambda i: (i, 0)),
                  pl.BlockSpec((1, d), lambda i: (0, 0)),
                  pl.BlockSpec((tm, d), lambda i: (i, 0))],
        out_specs=[pl.BlockSpec((1, 1), lambda i: (0, 0)),
                   pl.BlockSpec((tm, d), lambda i: (i, 0)),
                   pl.BlockSpec((1, d), lambda i: (0, 0))],
        out_shape=[jax.ShapeDtypeStruct((1, 1), F32), jax.ShapeDtypeStruct((t, d), F32),
                   jax.ShapeDtypeStruct((1, d), F32)],
    )(x, gamma, target)


def _adamw(name, parts, w, m, v, rows):
    l_n, r_n, c_n = w.shape
    s_n = parts[0].shape[0]
    tr = min(rows, r_n)
    nr = r_n // tr
    c1 = 1.0 - ADAM_B1 ** ADAM_STEP
    c2 = 1.0 - ADAM_B2 ** ADAM_STEP

    def body(*refs):
        p_refs = refs[:l_n]
        w_ref, m_ref, v_ref, g_ref, d_ref, mo_ref, vo_ref = refs[l_n:]
        layer = pl.program_id(0)

        def update(p_ref):
            g = p_ref[0].astype(F32)
            for s in range(1, s_n):
                g = g + p_ref[s].astype(F32)
            mn = ADAM_B1 * m_ref[...] + (1.0 - ADAM_B1) * g
            vn = ADAM_B2 * v_ref[...] + (1.0 - ADAM_B2) * (g * g)
            m_hat = mn / c1
            v_hat = vn / c2
            g_ref[...] = g
            d_ref[...] = -ADAM_LR * (m_hat / (jnp.sqrt(v_hat) + ADAM_EPS) + ADAM_WD * w_ref[...])
            mo_ref[...] = mn
            vo_ref[...] = vn

        for j in range(l_n):
            pl.when(layer == j)(functools.partial(update, p_refs[j]))

    def part_spec(j):
        return pl.BlockSpec((s_n, tr, c_n), lambda l, i: (0, jnp.where(l == j, i, jnp.where(l < j, 0, nr - 1)), 0))

    blk = pl.BlockSpec((None, tr, c_n), lambda l, i: (l, i, 0))
    return _call(
        body, name=name, grid=(l_n, nr),
        in_specs=[part_spec(j) for j in range(l_n)] + [blk, blk, blk],
        out_specs=[blk] * 4,
        out_shape=[jax.ShapeDtypeStruct((l_n, r_n, c_n), F32)] * 4,
    )(*parts, w, m, v)


def _local_step(x, target, big, small, sched=None):
    t, d = x.shape
    n_layers = small["ffn1_norm"].shape[0]
    gb = {}
    gs = {}

    def row(a, l):
        return a[l:l + 1]

    def run(fn, name, *operands, **kw):
        comm = sched.plan(name, gb, gs) if sched is not None else None
        if comm is None:
            return fn(name, *operands, **kw)
        res, got = fn(name, *operands, comm=comm, **kw)
        sched.deliver(comm, got)
        return res

    saved = []
    xs = x
    for l in range(n_layers):
        rec = {"x_ffn1": xs}
        xs, rec["xn_ffn1"], rec["gu_ffn1"] = run(
            _ffn_fwd, f"ffn1_fwd_l{l}", xs, row(small["ffn1_norm"], l), big["ffn1_w_in", l], big["ffn1_w_out", l])
        rec["x_mix"] = xs
        if l % 2 == 0:
            rec["xn_mix"], rec["h"] = _rms_matmul(f"ab_in_l{l}", xs, row(small["mix_norm"], l), big["ab_w_in"])
            rec["cat"] = _pool_conv_fwd(f"pool_conv_fwd_l{l}", rec["h"], small["pool_w"], small["pool_b"],
                                        small["pool_scale"], small["conv_w"], small["conv_b"],
                                        small["conv_ln_g"], small["conv_ln_b"])
            xs = _matmul_residual(f"ab_out_l{l}", rec["cat"], big["ab_w_out"], xs)
        else:
            rec["xn_mix"], rec["zp"] = _rms_matmul(f"sgu_in_l{l}", xs, row(small["mix_norm"], l), big["sgu_w_in"])
            rec["q"] = _sgu_fwd(f"sgu_fwd_l{l}", rec["zp"], small["sgu_ln_g"], small["sgu_ln_b"],
                                small["sgu_w"], small["sgu_b"])
            xs = _matmul_residual(f"sgu_out_l{l}", rec["q"], big["sgu_w_out"], xs)
        rec["x_ffn2"] = xs
        xs, rec["xn_ffn2"], rec["gu_ffn2"] = run(
            _ffn_fwd, f"ffn2_fwd_l{l}", xs, row(small["ffn2_norm"], l), big["ffn2_w_in", l], big["ffn2_w_out", l])
        saved.append(rec)

    loss, dx, gs["final_norm"] = _loss_head("loss_head", xs, small["final_norm"], target)

    norm_rows = {"ffn1_norm": [None] * n_layers, "mix_norm": [None] * n_layers, "ffn2_norm": [None] * n_layers}

    def ffn_backward(tag, l, dx, rec):
        dx, dgam, dyh, hh, dgu = run(_ffn_bwd, f"{tag}_bwd_l{l}", dx, rec[f"x_{tag}"], row(small[f"{tag}_norm"], l),
                                     rec[f"gu_{tag}"], big[f"{tag}_w_in", l], big[f"{tag}_w_out", l])
        norm_rows[f"{tag}_norm"][l] = dgam
        gb[f"{tag}_w_out", l] = run(_matmul_tn, f"{tag}_dwout_l{l}", hh, dyh,
                                    a_split=HIDDEN_SPLIT).reshape(N_DEV, -1, d)
        gb[f"{tag}_w_in", l] = run(_matmul_tn, f"{tag}_dwin_l{l}", dgu, rec[f"xn_{tag}"],
                                   a_split=2 * HIDDEN_SPLIT).reshape(N_DEV, -1, d)
        return dx

    for l in reversed(range(n_layers)):
        rec = saved[l]
        dx = ffn_backward("ffn2", l, dx, rec)
        if l % 2 == 0:
            dcat, dxb = _matmul_nt(f"ab_out_bwd_l{l}", dx, big["ab_w_out"])
            gb["ab_w_out", 0] = _matmul_tn(f"ab_dwout_l{l}", rec["cat"], dxb)
            dh, gs["pool_w"], gs["pool_b"], gs["pool_scale"], gs["conv_w"], gs["conv_b"], gs["conv_ln_g"], \
                gs["conv_ln_b"] = run(
                    _pool_conv_bwd, f"pool_conv_bwd_l{l}", rec["h"], dcat, small["pool_w"], small["pool_b"],
                    small["pool_scale"], small["conv_w"], small["conv_b"], small["conv_ln_g"], small["conv_ln_b"])
            gb["ab_w_in", 0] = _matmul_tn(f"ab_dwin_l{l}", rec["xn_mix"], dh)
            dx, dgam = _matmul_nt_rms_bwd(f"ab_in_bwd_l{l}", dh, big["ab_w_in"], dx, rec["x_mix"],
                                          row(small["mix_norm"], l))
        else:
            dq, dxb = _matmul_nt(f"sgu_out_bwd_l{l}", dx, big["sgu_w_out"])
            gb["sgu_w_out", 0] = _matmul_tn(f"sgu_dwout_l{l}", rec["q"], dxb)
            dzp, gs["sgu_ln_g"], gs["sgu_ln_b"], gs["sgu_w"], gs["sgu_b"] = _sgu_bwd(
                f"sgu_bwd_l{l}", rec["zp"], dq, small["sgu_ln_g"], small["sgu_ln_b"], small["sgu_w"], small["sgu_b"])
            gb["sgu_w_in", 0] = _matmul_tn(f"sgu_dwin_l{l}", rec["xn_mix"], dzp)
            dx, dgam = _matmul_nt_rms_bwd(f"sgu_in_bwd_l{l}", dzp, big["sgu_w_in"], dx, rec["x_mix"],
                                          row(small["mix_norm"], l))
        norm_rows["mix_norm"][l] = dgam
        dx = ffn_backward("ffn1", l, dx, rec)

    for k, rows in norm_rows.items():
        gs[k] = jnp.concatenate(rows, axis=0)
    return loss, dx, gb, gs


SHARDED_SMALL = ("conv_w", "sgu_ln_g", "sgu_ln_b")
REPLICATED = ("ffn1_norm", "mix_norm", "ffn2_norm", "pool_w", "pool_b", "pool_scale", "conv_b", "conv_ln_g",
              "conv_ln_b", "sgu_w", "sgu_b", "final_norm")
WEIGHTS = ("ffn1_norm", "ffn1_w_in", "ffn1_w_out", "mix_norm", "ffn2_norm", "ffn2_w_in", "ffn2_w_out", "ab_w_in",
           "pool_w", "pool_b", "pool_scale", "conv_w", "conv_b", "conv_ln_g", "conv_ln_b", "ab_w_out", "sgu_w_in",
           "sgu_ln_g", "sgu_ln_b", "sgu_w", "sgu_b", "sgu_w_out", "final_norm")
LANES = 128


def _interleave_cols(g):
    n, k, c = g.shape
    return jnp.transpose(g, (1, 0, 2)).reshape(k, n * c)


def _split_cols(a):
    k, nc = a.shape
    return jnp.transpose(a.reshape(k, N_DEV, nc // N_DEV), (1, 0, 2))


def _as3(a):
    if a.ndim == 1:
        return a.reshape(1, 1, -1)
    if a.ndim == 2:
        return a.reshape(a.shape[0], 1, a.shape[1])
    return a.reshape(a.shape[0], -1, a.shape[-1])


def _pack_rows(a):
    flat = a.reshape(-1)
    pad = (-flat.shape[0]) % (8 * LANES)
    if pad:
        flat = jnp.concatenate([flat, jnp.zeros((pad,), flat.dtype)])
    return flat.reshape(-1, LANES)


FIRST_GATHER = (("ffn1_w_in", 0), ("ffn1_w_out", 0), ("conv_w", 0), ("sgu_ln_g", 0), ("sgu_ln_b", 0))
GATHER_PLAN = {
    "ffn1_fwd_l0": (("ab_w_in", 0), ("ab_w_out", 0), ("ffn2_w_in", 0), ("ffn2_w_out", 0)),
    "ffn2_fwd_l0": (("ffn1_w_in", 1), ("ffn1_w_out", 1), ("sgu_w_in", 0), ("sgu_w_out", 0)),
    "ffn1_fwd_l1": (("ffn2_w_in", 1), ("ffn2_w_out", 1)),
}
SCATTER_PLAN = {
    "ffn2_dwin_l1": (("ffn2_w_out", 1),),
    "ffn1_bwd_l1": (("ffn2_w_in", 1), ("sgu_w_out", 0)),
    "ffn1_dwout_l1": (("sgu_w_in", 0), ("sgu_ln_g", 0), ("sgu_ln_b", 0)),
    "ffn1_dwin_l1": (("ffn1_w_out", 1),),
    "ffn2_bwd_l0": (("ffn1_w_in", 1),),
    "ffn2_dwin_l0": (("ffn2_w_out", 0),),
    "pool_conv_bwd_l0": (("ffn2_w_in", 0), ("ab_w_out", 0)),
    "ffn1_dwout_l0": (("ab_w_in", 0), ("conv_w", 0)),
    "ffn1_dwin_l0": (("ffn1_w_out", 0),),
}
LAST_SCATTER = (("ffn1_w_in", 0),)


class _Schedule:
    def __init__(self, shards, big, small):
        self.shards, self.big, self.small = shards, big, small
        self.recv = {}
        self.pending = {}

    def gather_comm(self, keys):
        comm = _Comm()
        for key in keys:
            comm.gather(*self.shards[key])
        self.pending[id(comm)] = ("gather", keys)
        return comm

    def scatter_comm(self, keys, gb, gs):
        comm = _Comm()
        for name, l in keys:
            if name in ("ab_w_in", "sgu_w_in"):
                send = _split_cols(gb[name, l][0])
            elif name in ("ab_w_out", "sgu_w_out"):
                send = gb[name, l][0]
                send = send.reshape(N_DEV, -1, send.shape[-1])
            elif name == "conv_w":
                send = _split_cols(gs[name][:CONV_WIDTH])
            elif name in ("sgu_ln_g", "sgu_ln_b"):
                send = gs[name].reshape(N_DEV, 1, -1)
            else:
                send = gb[name, l]
            comm.scatter(send)
        self.pending[id(comm)] = ("scatter", keys)
        return comm

    def plan(self, name, gb, gs):
        if name in GATHER_PLAN:
            return self.gather_comm(GATHER_PLAN[name])
        if name in SCATTER_PLAN:
            return self.scatter_comm(SCATTER_PLAN[name], gb, gs)
        return None

    def deliver(self, comm, got):
        kind, keys = self.pending.pop(id(comm))
        for (name, l), arr in zip(keys, got):
            if kind == "scatter":
                self.recv[name, l] = arr
            elif name in ("ffn1_w_in", "ffn2_w_in"):
                self.big[name, l] = arr.reshape(-1, arr.shape[-1])
            elif name in ("ffn1_w_out", "ffn2_w_out"):
                self.big[name, l] = arr.reshape(-1, arr.shape[-1])
            elif name in ("ab_w_in", "sgu_w_in"):
                self.big[name] = _interleave_cols(arr)
            elif name in ("ab_w_out", "sgu_w_out"):
                self.big[name] = arr.reshape(-1, arr.shape[-1])
            elif name == "conv_w":
                self.small[name] = jnp.pad(_interleave_cols(arr), ((0, 1), (0, 0)))
            else:
                self.small[name] = arr.reshape(1, -1)


def kernel(x, ffn1_norm, ffn1_w_in, ffn1_w_out, mix_norm, ffn2_norm, ffn2_w_in, ffn2_w_out, ab_w_in, pool_w, pool_b, pool_scale, conv_w, conv_b, conv_ln_g, conv_ln_b, ab_w_out, sgu_w_in, sgu_ln_g, sgu_ln_b, sgu_w, sgu_b, sgu_w_out, final_norm, loss_target, m_ffn1_norm, m_ffn1_w_in, m_ffn1_w_out, m_mix_norm, m_ffn2_norm, m_ffn2_w_in, m_ffn2_w_out, m_ab_w_in, m_pool_w, m_pool_b, m_pool_scale, m_conv_w, m_conv_b, m_conv_ln_g, m_conv_ln_b, m_ab_w_out, m_sgu_w_in, m_sgu_ln_g, m_sgu_ln_b, m_sgu_w, m_sgu_b, m_sgu_w_out, m_final_norm, v_ffn1_norm, v_ffn1_w_in, v_ffn1_w_out, v_mix_norm, v_ffn2_norm, v_ffn2_w_in, v_ffn2_w_out, v_ab_w_in, v_pool_w, v_pool_b, v_pool_scale, v_conv_w, v_conv_b, v_conv_ln_g, v_conv_ln_b, v_ab_w_out, v_sgu_w_in, v_sgu_ln_g, v_sgu_ln_b, v_sgu_w, v_sgu_b, v_sgu_w_out, v_final_norm):
    args = dict(locals())
    w = {n: args[n] for n in WEIGHTS}
    m = {n: args["m_" + n] for n in WEIGHTS}
    v = {n: args["v_" + n] for n in WEIGHTS}
    n_layers = ffn1_norm.shape[0]

    shards = {}
    for n in ("ffn1_w_in", "ffn2_w_in"):
        wt = jnp.swapaxes(w[n], 1, 2).astype(BF16)
        for l in range(n_layers):
            shards[n, l] = (wt, l)
    for n in ("ffn1_w_out", "ffn2_w_out"):
        wb = w[n].astype(BF16)
        for l in range(n_layers):
            shards[n, l] = (wb, l)
    for n in ("ab_w_in", "ab_w_out", "sgu_w_in", "sgu_w_out"):
        shards[n, 0] = (w[n][0].astype(BF16), None)
    shards["conv_w", 0] = (conv_w[0], None)
    shards["sgu_ln_g", 0] = (sgu_ln_g, None)
    shards["sgu_ln_b", 0] = (sgu_ln_b, None)

    big = {}
    small = {
        "ffn1_norm": ffn1_norm, "mix_norm": mix_norm, "ffn2_norm": ffn2_norm, "final_norm": final_norm.reshape(1, -1),
        "pool_w": pool_w[0], "pool_b": pool_b[0], "pool_scale": pool_scale,
        "conv_b": conv_b, "conv_ln_g": conv_ln_g, "conv_ln_b": conv_ln_b,
        "sgu_w": sgu_w[0], "sgu_b": sgu_b[0][:, :, None],
    }
    sched = _Schedule(shards, big, small)
    first = sched.gather_comm(FIRST_GATHER)
    sched.deliver(first, _exchange("gather_first", first))

    loss, grad_x, gb, gs = _local_step(x[0], loss_target[0], big, small, sched)

    rep_grads = {
        "ffn1_norm": gs["ffn1_norm"], "mix_norm": gs["mix_norm"], "ffn2_norm": gs["ffn2_norm"],
        "pool_w": gs["pool_w"], "pool_b": gs["pool_b"], "pool_scale": gs["pool_scale"],
        "conv_b": gs["conv_b"], "conv_ln_g": gs["conv_ln_g"], "conv_ln_b": gs["conv_ln_b"],
        "sgu_w": gs["sgu_w"], "sgu_b": gs["sgu_b"], "final_norm": gs["final_norm"],
    }
    packs = [_pack_rows(rep_grads[n]) for n in REPLICATED] + [_pack_rows(loss)]
    offsets = [0]
    for p in packs:
        offsets.append(offsets[-1] + p.shape[0])
    packed = jnp.concatenate(packs, axis=0)
    n_rows = packed.shape[0]

    last = sched.scatter_comm(LAST_SCATTER, gb, gs)
    packed_id = last.gather(packed)
    got = _exchange("reduce_last", last)
    packed_all = got[packed_id]
    sched.deliver(last, got[:packed_id])
    recv = sched.recv

    out = {}
    for n in ("ffn1_w_in", "ffn2_w_in"):
        res = _adamw(f"adamw_{n}", [recv[n, l] for l in range(n_layers)], jnp.swapaxes(w[n], 1, 2),
                     jnp.swapaxes(m[n], 1, 2), jnp.swapaxes(v[n], 1, 2), 176)
        out[n] = [jnp.swapaxes(r, 1, 2) for r in res]
    for n in ("ffn1_w_out", "ffn2_w_out"):
        out[n] = _adamw(f"adamw_{n}", [recv[n, l] for l in range(n_layers)], w[n], m[n], v[n], 176)
    for n in ("ab_w_in", "ab_w_out", "sgu_w_in", "sgu_w_out") + SHARDED_SMALL:
        w3 = _as3(w[n])
        parts = recv[n, 0].reshape((N_DEV,) + w3.shape[1:])
        res = _adamw(f"adamw_{n}", [parts], w3, _as3(m[n]), _as3(v[n]), 512)
        out[n] = [r.reshape(w[n].shape) for r in res]

    def pack_rep(src):
        tail = [jnp.zeros((offsets[-1] - offsets[-2], LANES), F32)]
        return jnp.concatenate([_pack_rows(src[n]) for n in REPLICATED] + tail, axis=0)[None]

    res = _adamw("adamw_replicated", [packed_all], pack_rep(w), pack_rep(m), pack_rep(v), n_rows)
    for i, n in enumerate(REPLICATED):
        size = w[n].size
        out[n] = [r[0, offsets[i]:offsets[i + 1]].reshape(-1)[:size].reshape(w[n].shape) for r in res]
    loss_sum = res[0][0, offsets[-2], 0]

    return (loss_sum, grad_x[None],
            *[out[n][0] for n in WEIGHTS], *[out[n][1] for n in WEIGHTS],
            *[out[n][2] for n in WEIGHTS], *[out[n][3] for n in WEIGHTS])
```

```python
import functools

import jax
import jax.numpy as jnp
from jax import lax
from jax.experimental import pallas as pl
from jax.experimental.pallas import tpu as pltpu

F32 = jnp.float32
BF16 = jnp.bfloat16
EPS = 1e-6
N_DEV = 8
POOL_WINDOWS = (2, 4, 8, 16)
CONV_WIDTH = 31
HALO = 32
GROUP = 128
SUBLANES = 8
ROW_CHUNK = 32
SUBLANES = 8
ROW_CHUNK = 32
TOKEN_TILE = 512
CONTRACT_TILE = 2048
HIDDEN_SPLIT = 2
FFN_TILE = 256
ADAM_LR, ADAM_B1, ADAM_B2, ADAM_EPS, ADAM_WD, ADAM_STEP = 0.001, 0.9, 0.999, 1e-08, 0.01, 10
VMEM_LIMIT = 56 * 1024 * 1024

NT = (((1,), (1,)), ((), ()))
TN = (((0,), (0,)), ((), ()))


def _pallas(body, side_effects, **kw):
    params = pltpu.CompilerParams(vmem_limit_bytes=VMEM_LIMIT, has_side_effects=side_effects)
    return pl.pallas_call(body, compiler_params=params, **kw)


def _call(body, comm=None, **kw):
    if comm is None:
        return _pallas(body, False, **kw)
    in_specs = list(kw.pop("in_specs"))
    out_specs = kw.pop("out_specs")
    out_shape = kw.pop("out_shape")
    scratch = list(kw.pop("scratch_shapes", []))
    single = not isinstance(out_shape, (list, tuple))
    if single:
        out_specs, out_shape = [out_specs], [out_shape]
    n_in, n_out, n_scr = len(in_specs), len(out_shape), len(scratch)
    n_ci, n_co = len(comm.inputs), len(comm.out_shapes)
    grid = tuple(kw.get("grid", ()))

    def wrapped(*refs):
        pos = 0
        parts = []
        for n in (n_in, n_ci, n_out, n_co, n_scr, 3):
            parts.append(refs[pos:pos + n])
            pos += n
        a_in, c_in, a_out, c_out, a_scr, sems = parts
        if grid:
            step = 0
            for ax, g in enumerate(grid):
                step = step * g + pl.program_id(ax)
            total = functools.reduce(lambda p, q: p * q, grid)
            pl.when(step == 0)(lambda: comm.start(c_in, c_out, sems))
            body(*a_in, *a_out, *a_scr)
            pl.when(step == (3 * total) // 4)(lambda: comm.forward(c_in, c_out, sems))
            pl.when(step == total - 1)(lambda: comm.finish(c_in, c_out, sems))
        else:
            comm.start(c_in, c_out, sems)
            body(*a_in, *a_out, *a_scr)
            comm.forward(c_in, c_out, sems)
            comm.finish(c_in, c_out, sems)

    hbm = pl.BlockSpec(memory_space=pl.ANY)
    fn = _pallas(wrapped, True, in_specs=in_specs + [hbm] * n_ci, out_specs=list(out_specs) + [hbm] * n_co,
                 out_shape=list(out_shape) + list(comm.out_shapes), scratch_shapes=scratch + comm.semaphores(), **kw)

    def run(*operands):
        outs = fn(*operands, *comm.inputs)
        res = outs[:n_out]
        return (res[0] if single else res), outs[n_out:]

    return run


class _Comm:
    def __init__(self):
        self.inputs, self.sel, self.kinds, self.out_shapes = [], [], [], []

    def gather(self, arr, sel=None):
        block = arr.shape if sel is None else arr.shape[1:]
        self.inputs.append(arr)
        self.sel.append(sel)
        self.kinds.append("gather")
        self.out_shapes.append(jax.ShapeDtypeStruct((N_DEV,) + tuple(block), arr.dtype))
        return len(self.inputs) - 1

    def scatter(self, arr):
        self.inputs.append(arr)
        self.sel.append(None)
        self.kinds.append("scatter")
        self.out_shapes.append(jax.ShapeDtypeStruct(arr.shape, arr.dtype))
        return len(self.inputs) - 1

    def semaphores(self):
        n = len(self.inputs)
        return [pltpu.SemaphoreType.DMA((n, N_DEV - 1)), pltpu.SemaphoreType.DMA((n, N_DEV - 1)),
                pltpu.SemaphoreType.DMA((n,))]

    def _copies(self, ins, outs, sems, with_passed=True):
        send_sems, recv_sems, local_sems = sems
        x, y, c = lax.axis_index("x"), lax.axis_index("y"), lax.axis_index("c")
        me = 4 * x + 2 * y + c
        sibling = (x, y, 1 - c)
        chips = [(1 - x, y), (x, 1 - y), (1 - x, 1 - y)]
        items = []
        for a, kind in enumerate(self.kinds):
            def remote(src, dst, k, to, a=a):
                return pltpu.make_async_remote_copy(
                    src_ref=src, dst_ref=dst, send_sem=send_sems.at[a, k], recv_sem=recv_sems.at[a, k],
                    device_id=to, device_id_type=pl.DeviceIdType.MESH)
            if kind == "gather":
                src = ins[a] if self.sel[a] is None else ins[a].at[self.sel[a]]
                mine = outs[a].at[me]
                local = pltpu.make_async_copy(src, mine, local_sems.at[a])
                first = [remote(src, mine, 0, sibling)]
                first += [remote(src, mine, 1 + j, (*chip, c)) for j, chip in enumerate(chips)]
                passed = []
                for j, chip in enumerate(chips if with_passed else []):
                    got = outs[a].at[4 * chip[0] + 2 * chip[1] + c]
                    passed.append(remote(got, got, 4 + j, sibling))
            else:
                local = pltpu.make_async_copy(ins[a].at[me], outs[a].at[me], local_sems.at[a])
                first, passed = [], []
                for k in (1, 4, 2, 6, 5, 3, 7):
                    peer = ((1 - x) if k & 4 else x, (1 - y) if k & 2 else y, (1 - c) if k & 1 else c)
                    pid = 4 * peer[0] + 2 * peer[1] + peer[2]
                    first.append(remote(ins[a].at[pid], outs[a].at[me], k - 1, peer))
            items.append((local, first, passed))
        return items

    def start(self, ins, outs, sems):
        for local, first, _ in self._copies(ins, outs, sems, with_passed=False):
            local.start()
            for cp in first:
                cp.start()

    def forward(self, ins, outs, sems):
        for _, first, passed in self._copies(ins, outs, sems):
            for j, cp in enumerate(passed):
                first[1 + j].wait_recv()
                cp.start()

    def finish(self, ins, outs, sems):
        for local, first, passed in self._copies(ins, outs, sems):
            if passed:
                first[0].wait_recv()
                for cp in passed:
                    cp.wait_recv()
                for cp in first + passed:
                    cp.wait_send()
            else:
                for cp in first:
                    cp.wait()
            local.wait()


def _exchange(name, comm):
    _, outs = _call(lambda: None, comm=comm, name=name, in_specs=[], out_specs=[], out_shape=[])()
    return outs


def _sigmoid(x):
    return 0.5 * jnp.tanh(0.5 * x) + 0.5


def _tile(t):
    return min(TOKEN_TILE, t)


def _load_weights(pairs, sems):
    @pl.when(pl.program_id(0) == 0)
    def _():
        copies = [pltpu.make_async_copy(src, dst, sems.at[n]) for n, (src, dst) in enumerate(pairs)]
        for cp in copies:
            cp.start()
        for cp in copies:
            cp.wait()


def _ffn_fwd(name, x, gamma, win_t, wout, comm=None):
    t, d = x.shape
    f = wout.shape[0]
    fc = f // HIDDEN_SPLIT
    tm = min(FFN_TILE, t)

    def body(x_ref, gam_ref, win_hbm, wo_hbm, xo_ref, xn_ref, gu_ref, win_v, wo_v, h_s, sems):
        _load_weights([(win_hbm, win_v), (wo_hbm, wo_v)], sems)
        xv = x_ref[...]
        r = lax.rsqrt(jnp.mean(xv * xv, axis=-1, keepdims=True) + EPS)
        xn = (xv * r * gam_ref[...]).astype(BF16)
        xn_ref[...] = xn
        for c in range(HIDDEN_SPLIT):
            lo = c * fc
            g = lax.dot_general(xn, win_v[lo:lo + fc, :], NT, preferred_element_type=F32)
            u = lax.dot_general(xn, win_v[f + lo:f + lo + fc, :], NT, preferred_element_type=F32)
            gu_ref[:, lo:lo + fc] = g.astype(BF16)
            gu_ref[:, f + lo:f + lo + fc] = u.astype(BF16)
            h_s[:, lo:lo + fc] = (g * _sigmoid(g) * u).astype(BF16)
        xo_ref[...] = xv + 0.5 * jnp.dot(h_s[...], wo_v[...], preferred_element_type=F32)

    hbm = pl.BlockSpec(memory_space=pl.ANY)
    return _call(
        body, comm=comm, name=name, grid=(t // tm,),
        in_specs=[pl.BlockSpec((tm, d), lambda i: (i, 0)), pl.BlockSpec((1, d), lambda i: (0, 0)), hbm, hbm],
        out_specs=[pl.BlockSpec((tm, d), lambda i: (i, 0)), pl.BlockSpec((tm, d), lambda i: (i, 0)),
                   pl.BlockSpec((tm, 2 * f), lambda i: (i, 0))],
        out_shape=[jax.ShapeDtypeStruct((t, d), F32), jax.ShapeDtypeStruct((t, d), BF16),
                   jax.ShapeDtypeStruct((t, 2 * f), BF16)],
        scratch_shapes=[pltpu.VMEM((2 * f, d), BF16), pltpu.VMEM((f, d), BF16), pltpu.VMEM((tm, f), BF16),
                        pltpu.SemaphoreType.DMA((2,))],
    )(x, gamma, win_t, wout)


def _ffn_bwd(name, dy, x, gamma, gu, win_t, wout, comm=None):
    t, d = x.shape
    f = wout.shape[0]
    fc = f // HIDDEN_SPLIT
    tm = min(FFN_TILE, t)

    def body(dy_ref, x_ref, gam_ref, gu_ref, win_hbm, wo_hbm,
             dx_ref, dgam_ref, dyh_ref, h_ref, dgu_ref, win_v, wo_v, sems):
        _load_weights([(win_hbm, win_v), (wo_hbm, wo_v)], sems)

        @pl.when(pl.program_id(0) == 0)
        def _():
            dgam_ref[...] = jnp.zeros_like(dgam_ref)

        dyb = (0.5 * dy_ref[...]).astype(BF16)
        dyh_ref[...] = dyb
        for c in range(HIDDEN_SPLIT):
            lo = c * fc
            dh = lax.dot_general(dyb, wo_v[lo:lo + fc, :], NT, preferred_element_type=F32)
            g = gu_ref[:, lo:lo + fc].astype(F32)
            u = gu_ref[:, f + lo:f + lo + fc].astype(F32)
            sig = _sigmoid(g)
            silu = g * sig
            h_ref[:, lo:lo + fc] = (silu * u).astype(BF16)
            dgu_ref[:, lo:lo + fc] = (dh * u * (sig * (1.0 + g * (1.0 - sig)))).astype(BF16)
            dgu_ref[:, f + lo:f + lo + fc] = (dh * silu).astype(BF16)
        dxn = jnp.dot(dgu_ref[...], win_v[...], preferred_element_type=F32)
        xv = x_ref[...]
        r = lax.rsqrt(jnp.mean(xv * xv, axis=-1, keepdims=True) + EPS)
        yv = xv * r
        dgam_ref[...] += jnp.sum(dxn * yv, axis=0, keepdims=True)
        dyn = dxn * gam_ref[...]
        dx_ref[...] = dy_ref[...] + r * (dyn - yv * jnp.mean(dyn * yv, axis=-1, keepdims=True))

    hbm = pl.BlockSpec(memory_space=pl.ANY)
    row = lambda width: pl.BlockSpec((tm, width), lambda i: (i, 0))
    return _call(
        body, comm=comm, name=name, grid=(t // tm,),
        in_specs=[row(d), row(d), pl.BlockSpec((1, d), lambda i: (0, 0)), row(2 * f), hbm, hbm],
        out_specs=[row(d), pl.BlockSpec((1, d), lambda i: (0, 0)), row(d), row(f), row(2 * f)],
        out_shape=[
            jax.ShapeDtypeStruct((t, d), F32),
            jax.ShapeDtypeStruct((1, d), F32),
            jax.ShapeDtypeStruct((t, d), BF16),
            jax.ShapeDtypeStruct((t, f), BF16),
            jax.ShapeDtypeStruct((t, 2 * f), BF16),
        ],
        scratch_shapes=[pltpu.VMEM((2 * f, d), BF16), pltpu.VMEM((f, d), BF16), pltpu.SemaphoreType.DMA((2,))],
    )(dy, x, gamma, gu, win_t, wout)


def _ffn_bwd_hidden(name, dy, gu, wout):
    t, d = dy.shape
    f = wout.shape[0]
    fc = f // HIDDEN_SPLIT
    tm = min(FFN_TILE, t)

    def body(dy_ref, gu_ref, wo_hbm, dyh_ref, h_ref, dgu_ref, wo_v, sems):
        _load_weights([(wo_hbm, wo_v)], sems)
        dyb = (0.5 * dy_ref[...]).astype(BF16)
        dyh_ref[...] = dyb
        for c in range(HIDDEN_SPLIT):
            lo = c * fc
            dh = lax.dot_general(dyb, wo_v[lo:lo + fc, :], NT, preferred_element_type=F32)
            g = gu_ref[:, lo:lo + fc].astype(F32)
            u = gu_ref[:, f + lo:f + lo + fc].astype(F32)
            sig = _sigmoid(g)
            silu = g * sig
            h_ref[:, lo:lo + fc] = (silu * u).astype(BF16)
            dgu_ref[:, lo:lo + fc] = (dh * u * (sig * (1.0 + g * (1.0 - sig)))).astype(BF16)
            dgu_ref[:, f + lo:f + lo + fc] = (dh * silu).astype(BF16)

    row = lambda width: pl.BlockSpec((tm, width), lambda i: (i, 0))
    return _call(
        body, name=name, grid=(t // tm,),
        in_specs=[row(d), row(2 * f), pl.BlockSpec(memory_space=pl.ANY)],
        out_specs=[row(d), row(f), row(2 * f)],
        out_shape=[jax.ShapeDtypeStruct((t, d), BF16), jax.ShapeDtypeStruct((t, f), BF16),
                   jax.ShapeDtypeStruct((t, 2 * f), BF16)],
        scratch_shapes=[pltpu.VMEM((f, d), BF16), pltpu.SemaphoreType.DMA((1,))],
    )(dy, gu, wout)


def _matmul_tn(name, a, b, out_dtype=BF16, comm=None, a_split=None):
    a_b = a.ndim == 3
    b_b = b.ndim == 3
    nb = a_split if a_split else a.shape[0] if a_b else b.shape[0] if b_b else 1
    t, m = a.shape[-2:]
    if a_split:
        m = m // a_split
    n = b.shape[-1]
    tk = min(CONTRACT_TILE, t)
    nt = t // tk

    def body(a_ref, b_ref, o_ref, acc):
        s = pl.program_id(1)

        @pl.when(s == 0)
        def _():
            acc[...] = jnp.zeros_like(acc)

        acc[...] += lax.dot_general(a_ref[...], b_ref[...], TN, preferred_element_type=F32)

        @pl.when(s == nt - 1)
        def _():
            o_ref[...] = acc[...].astype(o_ref.dtype)

    a_spec = (pl.BlockSpec((None, tk, m), lambda j, s: (j, s, 0)) if a_b
              else pl.BlockSpec((tk, m), lambda j, s: (s, j)) if a_split
              else pl.BlockSpec((tk, m), lambda j, s: (s, 0)))
    b_spec = (pl.BlockSpec((None, tk, n), lambda j, s: (j, s, 0)) if b_b
              else pl.BlockSpec((tk, n), lambda j, s: (s, 0)))
    return _call(
        body, comm=comm, name=name, grid=(nb, nt),
        in_specs=[a_spec, b_spec],
        out_specs=pl.BlockSpec((None, m, n), lambda j, s: (j, 0, 0)),
        out_shape=jax.ShapeDtypeStruct((nb, m, n), out_dtype),
        scratch_shapes=[pltpu.VMEM((m, n), F32)],
    )(a, b)


def _rms_matmul(name, x, gamma, w):
    t, d = x.shape
    n = w.shape[1]
    tm = _tile(t)

    def body(x_ref, gam_ref, w_ref, xn_ref, h_ref):
        xv = x_ref[...]
        r = lax.rsqrt(jnp.mean(xv * xv, axis=-1, keepdims=True) + EPS)
        xn = (xv * r * gam_ref[...]).astype(BF16)
        xn_ref[...] = xn
        h_ref[...] = jnp.dot(xn, w_ref[...], preferred_element_type=F32)

    return _call(
        body, name=name, grid=(t // tm,),
        in_specs=[pl.BlockSpec((tm, d), lambda i: (i, 0)),
                  pl.BlockSpec((1, d), lambda i: (0, 0)),
                  pl.BlockSpec((d, n), lambda i: (0, 0))],
        out_specs=[pl.BlockSpec((tm, d), lambda i: (i, 0)),
                   pl.BlockSpec((tm, n), lambda i: (i, 0))],
        out_shape=[jax.ShapeDtypeStruct((t, d), BF16), jax.ShapeDtypeStruct((t, n), F32)],
    )(x, gamma, w)


def _matmul_residual(name, a, w, res):
    t, kdim = a.shape
    n = w.shape[1]
    tm = _tile(t)

    def body(a_ref, w_ref, r_ref, o_ref):
        o_ref[...] = r_ref[...] + jnp.dot(a_ref[...], w_ref[...], preferred_element_type=F32)

    return _call(
        body, name=name, grid=(t // tm,),
        in_specs=[pl.BlockSpec((tm, kdim), lambda i: (i, 0)),
                  pl.BlockSpec((kdim, n), lambda i: (0, 0)),
                  pl.BlockSpec((tm, n), lambda i: (i, 0))],
        out_specs=pl.BlockSpec((tm, n), lambda i: (i, 0)),
        out_shape=jax.ShapeDtypeStruct((t, n), F32),
    )(a, w, res)


def _matmul_nt(name, dy, w):
    t, n = dy.shape
    kdim = w.shape[0]
    tm = _tile(t)

    def body(dy_ref, w_ref, da_ref, dyb_ref):
        dyb = dy_ref[...].astype(BF16)
        dyb_ref[...] = dyb
        da_ref[...] = lax.dot_general(dyb, w_ref[...], NT, preferred_element_type=F32)

    return _call(
        body, name=name, grid=(t // tm,),
        in_specs=[pl.BlockSpec((tm, n), lambda i: (i, 0)),
                  pl.BlockSpec((kdim, n), lambda i: (0, 0))],
        out_specs=[pl.BlockSpec((tm, kdim), lambda i: (i, 0)),
                   pl.BlockSpec((tm, n), lambda i: (i, 0))],
        out_shape=[jax.ShapeDtypeStruct((t, kdim), F32), jax.ShapeDtypeStruct((t, n), BF16)],
    )(dy, w)


def _matmul_nt_rms_bwd(name, dz, w, dres, x, gamma, w_rows_are_k=False, tile=TOKEN_TILE, comm=None):
    t, kdim = dz.shape
    d = x.shape[1]
    tm = min(tile, t)

    def body(dz_ref, w_ref, dres_ref, x_ref, gam_ref, dx_ref, dgam_ref):
        i = pl.program_id(0)

        @pl.when(i == 0)
        def _():
            dgam_ref[...] = jnp.zeros_like(dgam_ref)

        if w_rows_are_k:
            dxn = jnp.dot(dz_ref[...], w_ref[...], preferred_element_type=F32)
        else:
            dxn = lax.dot_general(dz_ref[...], w_ref[...], NT, preferred_element_type=F32)
        xv = x_ref[...]
        r = lax.rsqrt(jnp.mean(xv * xv, axis=-1, keepdims=True) + EPS)
        yv = xv * r
        dgam_ref[...] += jnp.sum(dxn * yv, axis=0, keepdims=True)
        dyn = dxn * gam_ref[...]
        dx_ref[...] = dres_ref[...] + r * (dyn - yv * jnp.mean(dyn * yv, axis=-1, keepdims=True))

    return _call(
        body, comm=comm, name=name, grid=(t // tm,),
        in_specs=[pl.BlockSpec((tm, kdim), lambda i: (i, 0)),
                  pl.BlockSpec(w.shape, lambda i: (0, 0)),
                  pl.BlockSpec((tm, d), lambda i: (i, 0)),
                  pl.BlockSpec((tm, d), lambda i: (i, 0)),
                  pl.BlockSpec((1, d), lambda i: (0, 0))],
        out_specs=[pl.BlockSpec((tm, d), lambda i: (i, 0)),
                   pl.BlockSpec((1, d), lambda i: (0, 0))],
        out_shape=[jax.ShapeDtypeStruct((t, d), F32), jax.ShapeDtypeStruct((1, d), F32)],
    )(dz, w, dres, x, gamma)


def _pool_means(uext_ref, pos, tm, g, win):
    cols = slice(g * GROUP, (g + 1) * GROUP)
    acc = uext_ref[pl.ds(HALO, tm), cols]
    for j in range(1, win):
        acc = acc + uext_ref[pl.ds(HALO - j, tm), cols]
    cnt = jnp.minimum(pos + 1, win).astype(F32)
    return acc / cnt - uext_ref[pl.ds(HALO, tm), cols]


def _shifted_copies(src_ref, dst_ref, rows):
    for b in range(SUBLANES):
        dst_ref[b, pl.ds(0, rows), :] = src_ref[pl.ds(b, rows), :]


def _tap_sum(sh_ref, cw_ref, offsets, out_ref, n_rows, bias_ref=None):
    width = out_ref.shape[-1]

    def chunk(c, carry):
        r0 = pl.multiple_of(c * ROW_CHUNK, ROW_CHUNK)
        acc = (jnp.zeros((ROW_CHUNK, width), F32) if bias_ref is None
               else jnp.broadcast_to(bias_ref[...], (ROW_CHUNK, width)))
        for k, off in enumerate(offsets):
            a, b = divmod(off, SUBLANES)
            acc = acc + cw_ref[k:k + 1, :] * sh_ref[b, pl.ds(r0 + SUBLANES * a, ROW_CHUNK), :]
        out_ref[pl.ds(r0, ROW_CHUNK), :] = acc
        return carry

    lax.fori_loop(0, n_rows // ROW_CHUNK, chunk, 0)


def _pool_conv_fwd(name, h, pool_w, pool_b, pool_scale, conv_w, conv_b, ln_g, ln_b):
    t, hw = h.shape
    pc = len(POOL_WINDOWS) * GROUP
    cc = (hw - pc) // 2
    tm = _tile(t)
    per = tm // HALO

    def body(h_ref, hp_ref, pw_ref, pb_ref, ps_ref, cw_ref, cb_ref, lg_ref, lb_ref, cat_ref, uext, gext, gsh, y_s):
        i = pl.program_id(0)
        keep = (i > 0).astype(F32)
        hp = hp_ref[...] * keep
        uext[0:HALO, :] = hp[:, :pc]
        uext[HALO:, :] = h_ref[:, :pc]
        gext[0:HALO, :] = hp[:, pc:pc + cc] * _sigmoid(hp[:, pc + cc:])
        gext[pl.ds(HALO, tm), :] = h_ref[:, pc:pc + cc] * _sigmoid(h_ref[:, pc + cc:])
        gext[pl.ds(HALO + tm, SUBLANES), :] = jnp.zeros((SUBLANES, cc), F32)
        pos = i * tm + lax.broadcasted_iota(jnp.int32, (tm, 1), 0)
        for g, win in enumerate(POOL_WINDOWS):
            cols = slice(g * GROUP, (g + 1) * GROUP)
            pooled = _pool_means(uext, pos, tm, g, win)
            mixed = jnp.dot(pooled.astype(BF16), pw_ref[g].astype(BF16),
                            preferred_element_type=F32) + pb_ref[g:g + 1, :]
            cat_ref[:, cols] = (mixed * ps_ref[:, cols]).astype(BF16)
        _shifted_copies(gext, gsh, HALO + tm)
        _tap_sum(gsh, cw_ref, [HALO - (CONV_WIDTH - 1) + k for k in range(CONV_WIDTH)], y_s, tm, cb_ref)
        y = y_s[...]
        mu = jnp.mean(y, axis=-1, keepdims=True)
        dv = y - mu
        rstd = lax.rsqrt(jnp.mean(dv * dv, axis=-1, keepdims=True) + EPS)
        ln = dv * rstd * lg_ref[...] + lb_ref[...]
        cat_ref[:, pc:] = (ln * _sigmoid(ln)).astype(BF16)

    small = lambda a: pl.BlockSpec(a.shape, lambda i: (0,) * a.ndim)
    return _call(
        body, name=name, grid=(t // tm,),
        in_specs=[pl.BlockSpec((tm, hw), lambda i: (i, 0)),
                  pl.BlockSpec((HALO, hw), lambda i: (jnp.maximum(i * per - 1, 0), 0)),
                  small(pool_w), small(pool_b), small(pool_scale), small(conv_w), small(conv_b),
                  small(ln_g), small(ln_b)],
        out_specs=pl.BlockSpec((tm, pc + cc), lambda i: (i, 0)),
        out_shape=jax.ShapeDtypeStruct((t, pc + cc), BF16),
        scratch_shapes=[pltpu.VMEM((HALO + tm, pc), F32), pltpu.VMEM((HALO + tm + SUBLANES, cc), F32),
                        pltpu.VMEM((SUBLANES, HALO + tm, cc), F32), pltpu.VMEM((tm, cc), F32)],
    )(h, h, pool_w, pool_b, pool_scale, conv_w, conv_b, ln_g, ln_b)


def _pool_conv_bwd(name, h, dcat, pool_w, pool_b, pool_scale, conv_w, conv_b, ln_g, ln_b, comm=None):
    t, hw = h.shape
    pc = len(POOL_WINDOWS) * GROUP
    cc = (hw - pc) // 2
    ng = len(POOL_WINDOWS)
    tm = _tile(t)
    per = tm // HALO
    nt = t // tm
    r2 = tm + HALO
    taps = CONV_WIDTH - 1

    def body(h_ref, hp_ref, hn_ref, dc_ref, dcn_ref, pw_ref, pb_ref, ps_ref, cw_ref, cb_ref, lg_ref, lb_ref,
             dh_ref, dpw_ref, dpb_ref, dps_ref, dcw_ref, dcb_ref, dlg_ref, dlb_ref,
             uext, gext, dcext, dqext, dycext, shifted, y_s, dg_s, dcw_acc):
        i = pl.program_id(0)

        @pl.when(i == 0)
        def _():
            for ref in (dpw_ref, dpb_ref, dps_ref, dcw_ref, dcb_ref, dlg_ref, dlb_ref, dcw_acc):
                ref[...] = jnp.zeros_like(ref)

        keep_p = (i > 0).astype(F32)
        keep_n = (i < nt - 1).astype(F32)
        hp = hp_ref[...] * keep_p
        hn = hn_ref[...] * keep_n
        uext[0:HALO, :] = hp[:, :pc]
        uext[HALO:, :] = h_ref[:, :pc]
        gext[0:HALO, :] = hp[:, pc:pc + cc] * _sigmoid(hp[:, pc + cc:])
        gext[pl.ds(HALO, tm), :] = h_ref[:, pc:pc + cc] * _sigmoid(h_ref[:, pc + cc:])
        gext[pl.ds(HALO + tm, HALO), :] = hn[:, pc:pc + cc] * _sigmoid(hn[:, pc + cc:])
        gext[pl.ds(HALO + tm + HALO, SUBLANES), :] = jnp.zeros((SUBLANES, cc), F32)
        dcext[0:tm, :] = dc_ref[...]
        dcext[pl.ds(tm, HALO), :] = dcn_ref[...] * keep_n

        pos = i * tm + lax.broadcasted_iota(jnp.int32, (tm, 1), 0)
        pos2 = i * tm + lax.broadcasted_iota(jnp.int32, (r2, 1), 0)
        for g, win in enumerate(POOL_WINDOWS):
            cols = slice(g * GROUP, (g + 1) * GROUP)
            wg = pw_ref[g].astype(BF16)
            dya = dcext[:, cols]
            dmixed = dya * ps_ref[:, cols]
            dpooled = lax.dot_general(dmixed.astype(BF16), wg, NT, preferred_element_type=F32)
            cnt2 = jnp.minimum(pos2 + 1, win).astype(F32)
            dqext[:, cols] = dpooled / cnt2
            du = -dpooled[0:tm]
            for j in range(win):
                du = du + dqext[pl.ds(j, tm), cols]
            dh_ref[:, cols] = du.astype(BF16)
            pooled = _pool_means(uext, pos, tm, g, win)
            pooled_b = pooled.astype(BF16)
            mixed = jnp.dot(pooled_b, wg, preferred_element_type=F32) + pb_ref[g:g + 1, :]
            dps_ref[:, cols] += jnp.sum(dya[0:tm] * mixed, axis=0, keepdims=True)
            dpb_ref[g:g + 1, :] += jnp.sum(dmixed[0:tm], axis=0, keepdims=True)
            dpw_ref[g] += lax.dot_general(pooled_b, dmixed[0:tm].astype(BF16), TN, preferred_element_type=F32)

        _shifted_copies(gext, shifted, HALO + tm + HALO)
        _tap_sum(shifted, cw_ref, [HALO - taps + k for k in range(CONV_WIDTH)], y_s, r2, cb_ref)
        y = y_s[...]
        mu = jnp.mean(y, axis=-1, keepdims=True)
        dv = y - mu
        rstd = lax.rsqrt(jnp.mean(dv * dv, axis=-1, keepdims=True) + EPS)
        norm = dv * rstd
        ln = norm * lg_ref[...] + lb_ref[...]
        sig = _sigmoid(ln)
        dln = dcext[:, pc:] * (sig * (1.0 + ln * (1.0 - sig)))
        dnorm = dln * lg_ref[...]
        dyc = rstd * (dnorm - jnp.mean(dnorm, axis=-1, keepdims=True)
                      - norm * jnp.mean(dnorm * norm, axis=-1, keepdims=True))
        dycext[pl.ds(0, r2), :] = dyc
        dycext[pl.ds(r2, SUBLANES), :] = jnp.zeros((SUBLANES, cc), F32)
        dlg_ref[...] += jnp.sum((dln * norm)[0:tm], axis=0, keepdims=True)
        dlb_ref[...] += jnp.sum(dln[0:tm], axis=0, keepdims=True)
        dcb_ref[...] += jnp.sum(dyc[0:tm], axis=0, keepdims=True)

        def fold(c, carry):
            r0 = pl.multiple_of(c * ROW_CHUNK, ROW_CHUNK)
            dchunk = dycext[pl.ds(r0, ROW_CHUNK), :]
            for k in range(CONV_WIDTH):
                a8, b8 = divmod(HALO - taps + k, SUBLANES)
                prod = dchunk * shifted[b8, pl.ds(r0 + SUBLANES * a8, ROW_CHUNK), :]
                part = prod[0:SUBLANES]
                for q in range(1, ROW_CHUNK // SUBLANES):
                    part = part + prod[q * SUBLANES:(q + 1) * SUBLANES]
                dcw_acc[k] += part
            return carry

        lax.fori_loop(0, tm // ROW_CHUNK, fold, 0)

        @pl.when(i == nt - 1)
        def _():
            for k in range(CONV_WIDTH):
                dcw_ref[k:k + 1, :] = jnp.sum(dcw_acc[k], axis=0, keepdims=True)

        _shifted_copies(dycext, shifted, r2)
        _tap_sum(shifted, cw_ref, [taps - k for k in range(CONV_WIDTH)], dg_s, tm)
        dg = dg_s[...]
        a = h_ref[:, pc:pc + cc]
        sg = _sigmoid(h_ref[:, pc + cc:])
        dh_ref[:, pc:pc + cc] = (dg * sg).astype(BF16)
        dh_ref[:, pc + cc:] = (dg * a * sg * (1.0 - sg)).astype(BF16)

    small = lambda a: pl.BlockSpec(a.shape, lambda i: (0,) * a.ndim)
    smalls = (pool_w, pool_b, pool_scale, conv_w, conv_b, ln_g, ln_b)
    return _call(
        body, comm=comm, name=name, grid=(nt,),
        in_specs=[pl.BlockSpec((tm, hw), lambda i: (i, 0)),
                  pl.BlockSpec((HALO, hw), lambda i: (jnp.maximum(i * per - 1, 0), 0)),
                  pl.BlockSpec((HALO, hw), lambda i: (jnp.minimum((i + 1) * per, t // HALO - 1), 0)),
                  pl.BlockSpec((tm, pc + cc), lambda i: (i, 0)),
                  pl.BlockSpec((HALO, pc + cc), lambda i: (jnp.minimum((i + 1) * per, t // HALO - 1), 0)),
                  ] + [small(a) for a in smalls],
        out_specs=[pl.BlockSpec((tm, hw), lambda i: (i, 0))] + [small(a) for a in smalls],
        out_shape=[jax.ShapeDtypeStruct((t, hw), BF16)] + [jax.ShapeDtypeStruct(a.shape, F32) for a in smalls],
        scratch_shapes=[pltpu.VMEM((HALO + tm, pc), F32), pltpu.VMEM((HALO + tm + HALO + SUBLANES, cc), F32),
                        pltpu.VMEM((r2, pc + cc), F32), pltpu.VMEM((r2, pc), F32),
                        pltpu.VMEM((r2 + SUBLANES, cc), F32), pltpu.VMEM((SUBLANES, HALO + tm + HALO, cc), F32),
                        pltpu.VMEM((r2, cc), F32), pltpu.VMEM((tm, cc), F32),
                        pltpu.VMEM((CONV_WIDTH + 1, SUBLANES, cc), F32)],
    )(h, h, h, dcat, dcat, *smalls)


SQRT_HALF = 0.7071067811865476
INV_SQRT_2PI = 0.3989422804014327


def _sgu_core(zp_ref, lg_ref, lb_ref, ws_ref, bs_ref, vo_s, tm, sc, heads):
    zp = zp_ref[...]
    z = 0.5 * zp * (1.0 + lax.erf(zp * SQRT_HALF))
    u = z[:, :sc]
    v = z[:, sc:]
    mu = jnp.mean(v, axis=-1, keepdims=True)
    dv = v - mu
    rstd = lax.rsqrt(jnp.mean(dv * dv, axis=-1, keepdims=True) + EPS)
    norm = dv * rstd
    vb = (norm * lg_ref[...] + lb_ref[...]).astype(BF16)
    row = lax.broadcasted_iota(jnp.int32, (GROUP, GROUP), 0)
    col = lax.broadcasted_iota(jnp.int32, (GROUP, GROUP), 1)
    mask = (col <= row).astype(F32)
    wm = [ws_ref[hd] * mask for hd in range(heads)]
    for hd in range(heads):
        cols = slice(hd * GROUP, (hd + 1) * GROUP)
        wb = wm[hd].astype(BF16)
        for n in range(tm // GROUP):
            rows = slice(n * GROUP, (n + 1) * GROUP)
            vo_s[rows, cols] = jnp.dot(wb, vb[rows, cols], preferred_element_type=F32) + bs_ref[hd]
    return zp, u, norm, rstd, vb, wm, mask


def _sgu_fwd(name, zp, ln_g, ln_b, w_s, b_s):
    t, two_sc = zp.shape
    sc = two_sc // 2
    heads = sc // GROUP
    tm = _tile(t)

    def body(zp_ref, lg_ref, lb_ref, ws_ref, bs_ref, q_ref, vo_s):
        _, u, _, _, _, _, _ = _sgu_core(zp_ref, lg_ref, lb_ref, ws_ref, bs_ref, vo_s, tm, sc, heads)
        q_ref[...] = (u * vo_s[...]).astype(BF16)

    small = lambda a: pl.BlockSpec(a.shape, lambda i: (0,) * a.ndim)
    return _call(
        body, name=name, grid=(t // tm,),
        in_specs=[pl.BlockSpec((tm, two_sc), lambda i: (i, 0)), small(ln_g), small(ln_b), small(w_s), small(b_s)],
        out_specs=pl.BlockSpec((tm, sc), lambda i: (i, 0)),
        out_shape=jax.ShapeDtypeStruct((t, sc), BF16),
        scratch_shapes=[pltpu.VMEM((tm, sc), F32)],
    )(zp, ln_g, ln_b, w_s, b_s)


def _sgu_bwd(name, zp, dq, ln_g, ln_b, w_s, b_s):
    t, two_sc = zp.shape
    sc = two_sc // 2
    heads = sc // GROUP
    tm = _tile(t)
    nt = t // tm

    def body(zp_ref, dq_ref, lg_ref, lb_ref, ws_ref, bs_ref,
             dzp_ref, dlg_ref, dlb_ref, dws_ref, dbs_ref, vo_s, dvl_s, dws_acc):
        i = pl.program_id(0)

        @pl.when(i == 0)
        def _():
            dlg_ref[...] = jnp.zeros_like(dlg_ref)
            dlb_ref[...] = jnp.zeros_like(dlb_ref)
            dbs_ref[...] = jnp.zeros_like(dbs_ref)
            dws_acc[...] = jnp.zeros_like(dws_acc)

        zp, u, norm, rstd, vb, wm, mask = _sgu_core(zp_ref, lg_ref, lb_ref, ws_ref, bs_ref, vo_s, tm, sc, heads)
        dq = dq_ref[...]
        du = dq * vo_s[...]
        dvo = dq * u
        dvob = dvo.astype(BF16)
        for hd in range(heads):
            cols = slice(hd * GROUP, (hd + 1) * GROUP)
            wtb = jnp.transpose(wm[hd]).astype(BF16)
            for n in range(tm // GROUP):
                rows = slice(n * GROUP, (n + 1) * GROUP)
                blk = dvob[rows, cols]
                dws_acc[hd] += lax.dot_general(blk, vb[rows, cols], NT, preferred_element_type=F32)
                dvl_s[rows, cols] = jnp.dot(wtb, blk, preferred_element_type=F32)
                dbs_ref[hd] += jnp.sum(dvo[rows, cols], axis=-1, keepdims=True)
        dvl = dvl_s[...]
        dlg_ref[...] += jnp.sum(dvl * norm, axis=0, keepdims=True)
        dlb_ref[...] += jnp.sum(dvl, axis=0, keepdims=True)
        dnorm = dvl * lg_ref[...]
        dv = rstd * (dnorm - jnp.mean(dnorm, axis=-1, keepdims=True)
                     - norm * jnp.mean(dnorm * norm, axis=-1, keepdims=True))
        dgelu = 0.5 * (1.0 + lax.erf(zp * SQRT_HALF)) + zp * (INV_SQRT_2PI * jnp.exp(-0.5 * zp * zp))
        dzp_ref[:, :sc] = (du * dgelu[:, :sc]).astype(BF16)
        dzp_ref[:, sc:] = (dv * dgelu[:, sc:]).astype(BF16)

        @pl.when(i == nt - 1)
        def _():
            for hd in range(heads):
                dws_ref[hd] = dws_acc[hd] * mask

    small = lambda a: pl.BlockSpec(a.shape, lambda i: (0,) * a.ndim)
    smalls = (ln_g, ln_b, w_s, b_s)
    return _call(
        body, name=name, grid=(nt,),
        in_specs=[pl.BlockSpec((tm, two_sc), lambda i: (i, 0)), pl.BlockSpec((tm, sc), lambda i: (i, 0))]
                 + [small(a) for a in smalls],
        out_specs=[pl.BlockSpec((tm, two_sc), lambda i: (i, 0))] + [small(a) for a in smalls],
        out_shape=[jax.ShapeDtypeStruct((t, two_sc), BF16)] + [jax.ShapeDtypeStruct(a.shape, F32) for a in smalls],
        scratch_shapes=[pltpu.VMEM((tm, sc), F32), pltpu.VMEM((tm, sc), F32), pltpu.VMEM(w_s.shape, F32)],
    )(zp, dq, ln_g, ln_b, w_s, b_s)


def _loss_head(name, x, gamma, target):
    t, d = x.shape
    tm = _tile(t)

    def body(x_ref, gam_ref, tg_ref, loss_ref, dx_ref, dgam_ref):
        i = pl.program_id(0)

        @pl.when(i == 0)
        def _():
            loss_ref[...] = jnp.zeros_like(loss_ref)
            dgam_ref[...] = jnp.zeros_like(dgam_ref)

        xv = x_ref[...]
        r = lax.rsqrt(jnp.mean(xv * xv, axis=-1, keepdims=True) + EPS)
        yv = xv * r
        err = yv * gam_ref[...] - tg_ref[...]
        row = jnp.sum(err * err, axis=-1, keepdims=True)
        loss_ref[...] += (0.5 / d) * jnp.sum(row, axis=0, keepdims=True)
        dout = err * (1.0 / d)
        dgam_ref[...] += jnp.sum(dout * yv, axis=0, keepdims=True)
        dyn = dout * gam_ref[...]
        dx_ref[...] = r * (dyn - yv * jnp.mean(dyn * yv, axis=-1, keepdims=True))

    return _call(
        body, name=name, grid=(t // tm,),
        in_specs=[pl.BlockSpec((tm, d), lambda i: (i, 0)),
                  pl.BlockSpec((1, d), lambda i: (0, 0)),
                  pl.BlockSpec((tm, d), lambda i: (i, 0))],
        out_specs=[pl.BlockSpec((1, 1), lambda i: (0, 0)),
                   pl.BlockSpec((tm, d), lambda i: (i, 0)),
                   pl.BlockSpec((1, d), lambda i: (0, 0))],
        out_shape=[jax.ShapeDtypeStruct((1, 1), F32), jax.ShapeDtypeStruct((t, d), F32),
                   jax.ShapeDtypeStruct((1, d), F32)],
    )(x, gamma, target)


def _adamw(name, parts, w, m, v, rows):
    l_n, r_n, c_n = w.shape
    s_n = parts[0].shape[0]
    tr = min(rows, r_n)
    nr = r_n // tr
    c1 = 1.0 - ADAM_B1 ** ADAM_STEP
    c2 = 1.0 - ADAM_B2 ** ADAM_STEP

    def body(*refs):
        p_refs = refs[:l_n]
        w_ref, m_ref, v_ref, g_ref, d_ref, mo_ref, vo_ref = refs[l_n:]
        layer = pl.program_id(0)

        def update(p_ref):
            g = p_ref[0].astype(F32)
            for s in range(1, s_n):
                g = g + p_ref[s].astype(F32)
            mn = ADAM_B1 * m_ref[...] + (1.0 - ADAM_B1) * g
            vn = ADAM_B2 * v_ref[...] + (1.0 - ADAM_B2) * (g * g)
            m_hat = mn / c1
            v_hat = vn / c2
            g_ref[...] = g
            d_ref[...] = -ADAM_LR * (m_hat / (jnp.sqrt(v_hat) + ADAM_EPS) + ADAM_WD * w_ref[...])
            mo_ref[...] = mn
            vo_ref[...] = vn

        for j in range(l_n):
            pl.when(layer == j)(functools.partial(update, p_refs[j]))

    def part_spec(j):
        return pl.BlockSpec((s_n, tr, c_n), lambda l, i: (0, jnp.where(l == j, i, jnp.where(l < j, 0, nr - 1)), 0))

    blk = pl.BlockSpec((None, tr, c_n), lambda l, i: (l, i, 0))
    return _call(
        body, name=name, grid=(l_n, nr),
        in_specs=[part_spec(j) for j in range(l_n)] + [blk, blk, blk],
        out_specs=[blk] * 4,
        out_shape=[jax.ShapeDtypeStruct((l_n, r_n, c_n), F32)] * 4,
    )(*parts, w, m, v)


def _local_step(x, target, big, small, sched=None):
    t, d = x.shape
    n_layers = small["ffn1_norm"].shape[0]
    gb = {}
    gs = {}

    def row(a, l):
        return a[l:l + 1]

    def run(fn, name, *operands, **kw):
        comm = sched.plan(name, gb, gs) if sched is not None else None
        if comm is None:
            return fn(name, *operands, **kw)
        res, got = fn(name, *operands, comm=comm, **kw)
        sched.deliver(comm, got)
        return res

    saved = []
    xs = x
    for l in range(n_layers):
        rec = {"x_ffn1": xs}
        xs, rec["xn_ffn1"], rec["gu_ffn1"] = run(
            _ffn_fwd, f"ffn1_fwd_l{l}", xs, row(small["ffn1_norm"], l), big["ffn1_w_in", l], big["ffn1_w_out", l])
        rec["x_mix"] = xs
        if l % 2 == 0:
            rec["xn_mix"], rec["h"] = _rms_matmul(f"ab_in_l{l}", xs, row(small["mix_norm"], l), big["ab_w_in"])
            rec["cat"] = _pool_conv_fwd(f"pool_conv_fwd_l{l}", rec["h"], small["pool_w"], small["pool_b"],
                                        small["pool_scale"], small["conv_w"], small["conv_b"],
                                        small["conv_ln_g"], small["conv_ln_b"])
            xs = _matmul_residual(f"ab_out_l{l}", rec["cat"], big["ab_w_out"], xs)
        else:
            rec["xn_mix"], rec["zp"] = _rms_matmul(f"sgu_in_l{l}", xs, row(small["mix_norm"], l), big["sgu_w_in"])
            rec["q"] = _sgu_fwd(f"sgu_fwd_l{l}", rec["zp"], small["sgu_ln_g"], small["sgu_ln_b"],
                                small["sgu_w"], small["sgu_b"])
            xs = _matmul_residual(f"sgu_out_l{l}", rec["q"], big["sgu_w_out"], xs)
        rec["x_ffn2"] = xs
        xs, rec["xn_ffn2"], rec["gu_ffn2"] = run(
            _ffn_fwd, f"ffn2_fwd_l{l}", xs, row(small["ffn2_norm"], l), big["ffn2_w_in", l], big["ffn2_w_out", l])
        saved.append(rec)

    loss, dx, gs["final_norm"] = _loss_head("loss_head", xs, small["final_norm"], target)

    norm_rows = {"ffn1_norm": [None] * n_layers, "mix_norm": [None] * n_layers, "ffn2_norm": [None] * n_layers}

    def ffn_backward(tag, l, dy, rec):
        gamma = row(small[f"{tag}_norm"], l)
        weights_first = (tag, l) == ("ffn1", 0)
        if weights_first:
            dyh, hh, dgu = _ffn_bwd_hidden(f"{tag}_bwd_hidden_l{l}", dy, rec[f"gu_{tag}"], big[f"{tag}_w_out", l])
        else:
            dx, dgam, dyh, hh, dgu = run(_ffn_bwd, f"{tag}_bwd_l{l}", dy, rec[f"x_{tag}"], gamma,
                                         rec[f"gu_{tag}"], big[f"{tag}_w_in", l], big[f"{tag}_w_out", l])
        gb[f"{tag}_w_out", l] = run(_matmul_tn, f"{tag}_dwout_l{l}", hh, dyh,
                                    a_split=HIDDEN_SPLIT).reshape(N_DEV, -1, d)
        gb[f"{tag}_w_in", l] = run(_matmul_tn, f"{tag}_dwin_l{l}", dgu, rec[f"xn_{tag}"],
                                   a_split=2 * HIDDEN_SPLIT).reshape(N_DEV, -1, d)
        if weights_first:
            dx, dgam = run(_matmul_nt_rms_bwd, f"{tag}_dx_l{l}", dgu, big[f"{tag}_w_in", l], dy, rec[f"x_{tag}"],
                           gamma, w_rows_are_k=True, tile=FFN_TILE)
        norm_rows[f"{tag}_norm"][l] = dgam
        return dx

    for l in reversed(range(n_layers)):
        rec = saved[l]
        dx = ffn_backward("ffn2", l, dx, rec)
        if l % 2 == 0:
            dcat, dxb = _matmul_nt(f"ab_out_bwd_l{l}", dx, big["ab_w_out"])
            gb["ab_w_out", 0] = _matmul_tn(f"ab_dwout_l{l}", rec["cat"], dxb)
            dh, gs["pool_w"], gs["pool_b"], gs["pool_scale"], gs["conv_w"], gs["conv_b"], gs["conv_ln_g"], \
                gs["conv_ln_b"] = run(
                    _pool_conv_bwd, f"pool_conv_bwd_l{l}", rec["h"], dcat, small["pool_w"], small["pool_b"],
                    small["pool_scale"], small["conv_w"], small["conv_b"], small["conv_ln_g"], small["conv_ln_b"])
            gb["ab_w_in", 0] = _matmul_tn(f"ab_dwin_l{l}", rec["xn_mix"], dh)
            dx, dgam = _matmul_nt_rms_bwd(f"ab_in_bwd_l{l}", dh, big["ab_w_in"], dx, rec["x_mix"],
                                          row(small["mix_norm"], l))
        else:
            dq, dxb = _matmul_nt(f"sgu_out_bwd_l{l}", dx, big["sgu_w_out"])
            gb["sgu_w_out", 0] = _matmul_tn(f"sgu_dwout_l{l}", rec["q"], dxb)
            dzp, gs["sgu_ln_g"], gs["sgu_ln_b"], gs["sgu_w"], gs["sgu_b"] = _sgu_bwd(
                f"sgu_bwd_l{l}", rec["zp"], dq, small["sgu_ln_g"], small["sgu_ln_b"], small["sgu_w"], small["sgu_b"])
            gb["sgu_w_in", 0] = _matmul_tn(f"sgu_dwin_l{l}", rec["xn_mix"], dzp)
            dx, dgam = _matmul_nt_rms_bwd(f"sgu_in_bwd_l{l}", dzp, big["sgu_w_in"], dx, rec["x_mix"],
                                          row(small["mix_norm"], l))
        norm_rows["mix_norm"][l] = dgam
        dx = ffn_backward("ffn1", l, dx, rec)

    for k, rows in norm_rows.items():
        gs[k] = jnp.concatenate(rows, axis=0)
    return loss, dx, gb, gs


SHARDED_SMALL = ("conv_w", "sgu_ln_g", "sgu_ln_b")
REPLICATED = ("ffn1_norm", "mix_norm", "ffn2_norm", "pool_w", "pool_b", "pool_scale", "conv_b", "conv_ln_g",
              "conv_ln_b", "sgu_w", "sgu_b", "final_norm")
WEIGHTS = ("ffn1_norm", "ffn1_w_in", "ffn1_w_out", "mix_norm", "ffn2_norm", "ffn2_w_in", "ffn2_w_out", "ab_w_in",
           "pool_w", "pool_b", "pool_scale", "conv_w", "conv_b", "conv_ln_g", "conv_ln_b", "ab_w_out", "sgu_w_in",
           "sgu_ln_g", "sgu_ln_b", "sgu_w", "sgu_b", "sgu_w_out", "final_norm")
LANES = 128


def _interleave_cols(g):
    n, k, c = g.shape
    return jnp.transpose(g, (1, 0, 2)).reshape(k, n * c)


def _split_cols(a):
    k, nc = a.shape
    return jnp.transpose(a.reshape(k, N_DEV, nc // N_DEV), (1, 0, 2))


def _as3(a):
    if a.ndim == 1:
        return a.reshape(1, 1, -1)
    if a.ndim == 2:
        return a.reshape(a.shape[0], 1, a.shape[1])
    return a.reshape(a.shape[0], -1, a.shape[-1])


def _pack_rows(a):
    flat = a.reshape(-1)
    pad = (-flat.shape[0]) % (8 * LANES)
    if pad:
        flat = jnp.concatenate([flat, jnp.zeros((pad,), flat.dtype)])
    return flat.reshape(-1, LANES)


FIRST_GATHER = (("ffn1_w_in", 0), ("ffn1_w_out", 0), ("conv_w", 0), ("sgu_ln_g", 0), ("sgu_ln_b", 0))
GATHER_PLAN = {
    "ffn1_fwd_l0": (("ab_w_in", 0), ("ab_w_out", 0), ("ffn2_w_in", 0), ("ffn2_w_out", 0)),
    "ffn2_fwd_l0": (("ffn1_w_in", 1), ("ffn1_w_out", 1), ("sgu_w_in", 0), ("sgu_w_out", 0)),
    "ffn1_fwd_l1": (("ffn2_w_in", 1), ("ffn2_w_out", 1)),
}
SCATTER_PLAN = {
    "ffn2_dwin_l1": (("ffn2_w_out", 1),),
    "ffn1_bwd_l1": (("ffn2_w_in", 1), ("sgu_w_out", 0)),
    "ffn1_dwout_l1": (("sgu_w_in", 0), ("sgu_ln_g", 0), ("sgu_ln_b", 0)),
    "ffn1_dwin_l1": (("ffn1_w_out", 1),),
    "ffn2_bwd_l0": (("ffn1_w_in", 1),),
    "ffn2_dwin_l0": (("ffn2_w_out", 0),),
    "pool_conv_bwd_l0": (("ffn2_w_in", 0), ("ab_w_out", 0)),
    "ffn1_dwout_l0": (("ab_w_in", 0), ("conv_w", 0)),
    "ffn1_dwin_l0": (("ffn1_w_out", 0),),
    "ffn1_dx_l0": (("ffn1_w_in", 0),),
}
LAST_SCATTER = ()


class _Schedule:
    def __init__(self, shards, big, small):
        self.shards, self.big, self.small = shards, big, small
        self.recv = {}
        self.pending = {}

    def gather_comm(self, keys):
        comm = _Comm()
        for key in keys:
            comm.gather(*self.shards[key])
        self.pending[id(comm)] = ("gather", keys)
        return comm

    def scatter_comm(self, keys, gb, gs):
        comm = _Comm()
        for name, l in keys:
            if name in ("ab_w_in", "sgu_w_in"):
                send = _split_cols(gb[name, l][0])
            elif name in ("ab_w_out", "sgu_w_out"):
                send = gb[name, l][0]
                send = send.reshape(N_DEV, -1, send.shape[-1])
            elif name == "conv_w":
                send = _split_cols(gs[name][:CONV_WIDTH])
            elif name in ("sgu_ln_g", "sgu_ln_b"):
                send = gs[name].reshape(N_DEV, 1, -1)
            else:
                send = gb[name, l]
            comm.scatter(send)
        self.pending[id(comm)] = ("scatter", keys)
        return comm

    def plan(self, name, gb, gs):
        if name in GATHER_PLAN:
            return self.gather_comm(GATHER_PLAN[name])
        if name in SCATTER_PLAN:
            return self.scatter_comm(SCATTER_PLAN[name], gb, gs)
        return None

    def deliver(self, comm, got):
        kind, keys = self.pending.pop(id(comm))
        for (name, l), arr in zip(keys, got):
            if kind == "scatter":
                self.recv[name, l] = arr
            elif name in ("ffn1_w_in", "ffn2_w_in"):
                self.big[name, l] = arr.reshape(-1, arr.shape[-1])
            elif name in ("ffn1_w_out", "ffn2_w_out"):
                self.big[name, l] = arr.reshape(-1, arr.shape[-1])
            elif name in ("ab_w_in", "sgu_w_in"):
                self.big[name] = _interleave_cols(arr)
            elif name in ("ab_w_out", "sgu_w_out"):
                self.big[name] = arr.reshape(-1, arr.shape[-1])
            elif name == "conv_w":
                self.small[name] = jnp.pad(_interleave_cols(arr), ((0, 1), (0, 0)))
            else:
                self.small[name] = arr.reshape(1, -1)


def kernel(x, ffn1_norm, ffn1_w_in, ffn1_w_out, mix_norm, ffn2_norm, ffn2_w_in, ffn2_w_out, ab_w_in, pool_w, pool_b, pool_scale, conv_w, conv_b, conv_ln_g, conv_ln_b, ab_w_out, sgu_w_in, sgu_ln_g, sgu_ln_b, sgu_w, sgu_b, sgu_w_out, final_norm, loss_target, m_ffn1_norm, m_ffn1_w_in, m_ffn1_w_out, m_mix_norm, m_ffn2_norm, m_ffn2_w_in, m_ffn2_w_out, m_ab_w_in, m_pool_w, m_pool_b, m_pool_scale, m_conv_w, m_conv_b, m_conv_ln_g, m_conv_ln_b, m_ab_w_out, m_sgu_w_in, m_sgu_ln_g, m_sgu_ln_b, m_sgu_w, m_sgu_b, m_sgu_w_out, m_final_norm, v_ffn1_norm, v_ffn1_w_in, v_ffn1_w_out, v_mix_norm, v_ffn2_norm, v_ffn2_w_in, v_ffn2_w_out, v_ab_w_in, v_pool_w, v_pool_b, v_pool_scale, v_conv_w, v_conv_b, v_conv_ln_g, v_conv_ln_b, v_ab_w_out, v_sgu_w_in, v_sgu_ln_g, v_sgu_ln_b, v_sgu_w, v_sgu_b, v_sgu_w_out, v_final_norm):
    args = dict(locals())
    w = {n: args[n] for n in WEIGHTS}
    m = {n: args["m_" + n] for n in WEIGHTS}
    v = {n: args["v_" + n] for n in WEIGHTS}
    n_layers = ffn1_norm.shape[0]

    shards = {}
    for n in ("ffn1_w_in", "ffn2_w_in"):
        wt = jnp.swapaxes(w[n], 1, 2).astype(BF16)
        for l in range(n_layers):
            shards[n, l] = (wt, l)
    for n in ("ffn1_w_out", "ffn2_w_out"):
        wb = w[n].astype(BF16)
        for l in range(n_layers):
            shards[n, l] = (wb, l)
    for n in ("ab_w_in", "ab_w_out", "sgu_w_in", "sgu_w_out"):
        shards[n, 0] = (w[n][0].astype(BF16), None)
    shards["conv_w", 0] = (conv_w[0], None)
    shards["sgu_ln_g", 0] = (sgu_ln_g, None)
    shards["sgu_ln_b", 0] = (sgu_ln_b, None)

    big = {}
    small = {
        "ffn1_norm": ffn1_norm, "mix_norm": mix_norm, "ffn2_norm": ffn2_norm, "final_norm": final_norm.reshape(1, -1),
        "pool_w": pool_w[0], "pool_b": pool_b[0], "pool_scale": pool_scale,
        "conv_b": conv_b, "conv_ln_g": conv_ln_g, "conv_ln_b": conv_ln_b,
        "sgu_w": sgu_w[0], "sgu_b": sgu_b[0][:, :, None],
    }
    sched = _Schedule(shards, big, small)
    first = sched.gather_comm(FIRST_GATHER)
    sched.deliver(first, _exchange("gather_first", first))

    loss, grad_x, gb, gs = _local_step(x[0], loss_target[0], big, small, sched)

    rep_grads = {
        "ffn1_norm": gs["ffn1_norm"], "mix_norm": gs["mix_norm"], "ffn2_norm": gs["ffn2_norm"],
        "pool_w": gs["pool_w"], "pool_b": gs["pool_b"], "pool_scale": gs["pool_scale"],
        "conv_b": gs["conv_b"], "conv_ln_g": gs["conv_ln_g"], "conv_ln_b": gs["conv_ln_b"],
        "sgu_w": gs["sgu_w"], "sgu_b": gs["sgu_b"], "final_norm": gs["final_norm"],
    }
    packs = [_pack_rows(rep_grads[n]) for n in REPLICATED] + [_pack_rows(loss)]
    offsets = [0]
    for p in packs:
        offsets.append(offsets[-1] + p.shape[0])
    packed = jnp.concatenate(packs, axis=0)
    n_rows = packed.shape[0]

    last = sched.scatter_comm(LAST_SCATTER, gb, gs)
    packed_id = last.gather(packed)
    got = _exchange("reduce_last", last)
    packed_all = got[packed_id]
    sched.deliver(last, got[:packed_id])
    recv = sched.recv

    out = {}
    for n in ("ffn1_w_in", "ffn2_w_in"):
        res = _adamw(f"adamw_{n}", [recv[n, l] for l in range(n_layers)], jnp.swapaxes(w[n], 1, 2),
                     jnp.swapaxes(m[n], 1, 2), jnp.swapaxes(v[n], 1, 2), 176)
        out[n] = [jnp.swapaxes(r, 1, 2) for r in res]
    for n in ("ffn1_w_out", "ffn2_w_out"):
        out[n] = _adamw(f"adamw_{n}", [recv[n, l] for l in range(n_layers)], w[n], m[n], v[n], 176)
    for n in ("ab_w_in", "ab_w_out", "sgu_w_in", "sgu_w_out") + SHARDED_SMALL:
        w3 = _as3(w[n])
        parts = recv[n, 0].reshape((N_DEV,) + w3.shape[1:])
        res = _adamw(f"adamw_{n}", [parts], w3, _as3(m[n]), _as3(v[n]), 512)
        out[n] = [r.reshape(w[n].shape) for r in res]

    def pack_rep(src):
        tail = [jnp.zeros((offsets[-1] - offsets[-2], LANES), F32)]
        return jnp.concatenate([_pack_rows(src[n]) for n in REPLICATED] + tail, axis=0)[None]

    res = _adamw("adamw_replicated", [packed_all], pack_rep(w), pack_rep(m), pack_rep(v), n_rows)
    for i, n in enumerate(REPLICATED):
        size = w[n].size
        out[n] = [r[0, offsets[i]:offsets[i + 1]].reshape(-1)[:size].reshape(w[n].shape) for r in res]
    loss_sum = res[0][0, offsets[-2], 0]

    return (loss_sum, grad_x[None],
            *[out[n][0] for n in WEIGHTS], *[out[n][1] for n in WEIGHTS],
            *[out[n][2] for n in WEIGHTS], *[out[n][3] for n in WEIGHTS])
```

```python
import functools

import jax
import jax.numpy as jnp
from jax import lax
from jax.experimental import pallas as pl
from jax.experimental.pallas import tpu as pltpu

F32 = jnp.float32
BF16 = jnp.bfloat16
EPS = 1e-6
N_DEV = 8
POOL_WINDOWS = (2, 4, 8, 16)
CONV_WIDTH = 31
HALO = 32
GROUP = 128
SUBLANES = 8
ROW_CHUNK = 32
SUBLANES = 8
ROW_CHUNK = 32
TOKEN_TILE = 512
CONTRACT_TILE = 2048
HIDDEN_SPLIT = 2
FFN_TILE = 256
ADAM_LR, ADAM_B1, ADAM_B2, ADAM_EPS, ADAM_WD, ADAM_STEP = 0.001, 0.9, 0.999, 1e-08, 0.01, 10
VMEM_LIMIT = 56 * 1024 * 1024

NT = (((1,), (1,)), ((), ()))
TN = (((0,), (0,)), ((), ()))


def _pallas(body, side_effects, **kw):
    params = pltpu.CompilerParams(vmem_limit_bytes=VMEM_LIMIT, has_side_effects=side_effects)
    return pl.pallas_call(body, compiler_params=params, **kw)


def _call(body, comm=None, **kw):
    if comm is None:
        return _pallas(body, False, **kw)
    in_specs = list(kw.pop("in_specs"))
    out_specs = kw.pop("out_specs")
    out_shape = kw.pop("out_shape")
    scratch = list(kw.pop("scratch_shapes", []))
    single = not isinstance(out_shape, (list, tuple))
    if single:
        out_specs, out_shape = [out_specs], [out_shape]
    n_in, n_out, n_scr = len(in_specs), len(out_shape), len(scratch)
    n_ci, n_co = len(comm.inputs), len(comm.out_shapes)
    grid = tuple(kw.get("grid", ()))

    def wrapped(*refs):
        pos = 0
        parts = []
        for n in (n_in, n_ci, n_out, n_co, n_scr, 3):
            parts.append(refs[pos:pos + n])
            pos += n
        a_in, c_in, a_out, c_out, a_scr, sems = parts
        if grid:
            step = 0
            for ax, g in enumerate(grid):
                step = step * g + pl.program_id(ax)
            total = functools.reduce(lambda p, q: p * q, grid)
            pl.when(step == 0)(lambda: comm.start(c_in, c_out, sems))
            body(*a_in, *a_out, *a_scr)
            pl.when(step == (3 * total) // 4)(lambda: comm.forward(c_in, c_out, sems))
            pl.when(step == total - 1)(lambda: comm.finish(c_in, c_out, sems))
        else:
            comm.start(c_in, c_out, sems)
            body(*a_in, *a_out, *a_scr)
            comm.forward(c_in, c_out, sems)
            comm.finish(c_in, c_out, sems)

    hbm = pl.BlockSpec(memory_space=pl.ANY)
    fn = _pallas(wrapped, True, in_specs=in_specs + [hbm] * n_ci, out_specs=list(out_specs) + [hbm] * n_co,
                 out_shape=list(out_shape) + list(comm.out_shapes), scratch_shapes=scratch + comm.semaphores(), **kw)

    def run(*operands):
        outs = fn(*operands, *comm.inputs)
        res = outs[:n_out]
        return (res[0] if single else res), outs[n_out:]

    return run


class _Comm:
    def __init__(self):
        self.inputs, self.sel, self.kinds, self.out_shapes = [], [], [], []

    def gather(self, arr, sel=None):
        block = arr.shape if sel is None else arr.shape[1:]
        self.inputs.append(arr)
        self.sel.append(sel)
        self.kinds.append("gather")
        self.out_shapes.append(jax.ShapeDtypeStruct((N_DEV,) + tuple(block), arr.dtype))
        return len(self.inputs) - 1

    def scatter(self, arr):
        self.inputs.append(arr)
        self.sel.append(None)
        self.kinds.append("scatter")
        self.out_shapes.append(jax.ShapeDtypeStruct(arr.shape, arr.dtype))
        return len(self.inputs) - 1

    def semaphores(self):
        n = len(self.inputs)
        return [pltpu.SemaphoreType.DMA((n, N_DEV - 1)), pltpu.SemaphoreType.DMA((n, N_DEV - 1)),
                pltpu.SemaphoreType.DMA((n,))]

    def _copies(self, ins, outs, sems, with_passed=True):
        send_sems, recv_sems, local_sems = sems
        x, y, c = lax.axis_index("x"), lax.axis_index("y"), lax.axis_index("c")
        me = 4 * x + 2 * y + c
        sibling = (x, y, 1 - c)
        chips = [(1 - x, y), (x, 1 - y), (1 - x, 1 - y)]
        items = []
        for a, kind in enumerate(self.kinds):
            def remote(src, dst, k, to, a=a):
                return pltpu.make_async_remote_copy(
                    src_ref=src, dst_ref=dst, send_sem=send_sems.at[a, k], recv_sem=recv_sems.at[a, k],
                    device_id=to, device_id_type=pl.DeviceIdType.MESH)
            if kind == "gather":
                src = ins[a] if self.sel[a] is None else ins[a].at[self.sel[a]]
                mine = outs[a].at[me]
                local = pltpu.make_async_copy(src, mine, local_sems.at[a])
                first = [remote(src, mine, 0, sibling)]
                first += [remote(src, mine, 1 + j, (*chip, c)) for j, chip in enumerate(chips)]
                passed = []
                for j, chip in enumerate(chips if with_passed else []):
                    got = outs[a].at[4 * chip[0] + 2 * chip[1] + c]
                    passed.append(remote(got, got, 4 + j, sibling))
            else:
                local = pltpu.make_async_copy(ins[a].at[me], outs[a].at[me], local_sems.at[a])
                first, passed = [], []
                for k in (1, 4, 2, 6, 5, 3, 7):
                    peer = ((1 - x) if k & 4 else x, (1 - y) if k & 2 else y, (1 - c) if k & 1 else c)
                    pid = 4 * peer[0] + 2 * peer[1] + peer[2]
                    first.append(remote(ins[a].at[pid], outs[a].at[me], k - 1, peer))
            items.append((local, first, passed))
        return items

    def start(self, ins, outs, sems):
        for local, first, _ in self._copies(ins, outs, sems, with_passed=False):
            local.start()
            for cp in first:
                cp.start()

    def forward(self, ins, outs, sems):
        for _, first, passed in self._copies(ins, outs, sems):
            for j, cp in enumerate(passed):
                first[1 + j].wait_recv()
                cp.start()

    def finish(self, ins, outs, sems):
        for local, first, passed in self._copies(ins, outs, sems):
            if passed:
                first[0].wait_recv()
                for cp in passed:
                    cp.wait_recv()
                for cp in first + passed:
                    cp.wait_send()
            else:
                for cp in first:
                    cp.wait()
            local.wait()


def _exchange(name, comm):
    _, outs = _call(lambda: None, comm=comm, name=name, in_specs=[], out_specs=[], out_shape=[])()
    return outs


def _sigmoid(x):
    return 0.5 * jnp.tanh(0.5 * x) + 0.5


def _tile(t):
    return min(TOKEN_TILE, t)


def _load_weights(pairs, sems):
    @pl.when(pl.program_id(0) == 0)
    def _():
        copies = [pltpu.make_async_copy(src, dst, sems.at[n]) for n, (src, dst) in enumerate(pairs)]
        for cp in copies:
            cp.start()
        for cp in copies:
            cp.wait()


def _ffn_fwd(name, x, gamma, win_t, wout, comm=None):
    t, d = x.shape
    f = wout.shape[0]
    fc = f // HIDDEN_SPLIT
    tm = min(FFN_TILE, t)

    def body(x_ref, gam_ref, win_hbm, wo_hbm, xo_ref, xn_ref, gu_ref, win_v, wo_v, h_s, sems):
        _load_weights([(win_hbm, win_v), (wo_hbm, wo_v)], sems)
        xv = x_ref[...]
        r = lax.rsqrt(jnp.mean(xv * xv, axis=-1, keepdims=True) + EPS)
        xn = (xv * r * gam_ref[...]).astype(BF16)
        xn_ref[...] = xn
        for c in range(HIDDEN_SPLIT):
            lo = c * fc
            g = lax.dot_general(xn, win_v[lo:lo + fc, :], NT, preferred_element_type=F32)
            u = lax.dot_general(xn, win_v[f + lo:f + lo + fc, :], NT, preferred_element_type=F32)
            gu_ref[:, lo:lo + fc] = g.astype(BF16)
            gu_ref[:, f + lo:f + lo + fc] = u.astype(BF16)
            h_s[:, lo:lo + fc] = (g * _sigmoid(g) * u).astype(BF16)
        xo_ref[...] = xv + 0.5 * jnp.dot(h_s[...], wo_v[...], preferred_element_type=F32)

    hbm = pl.BlockSpec(memory_space=pl.ANY)
    return _call(
        body, comm=comm, name=name, grid=(t // tm,),
        in_specs=[pl.BlockSpec((tm, d), lambda i: (i, 0)), pl.BlockSpec((1, d), lambda i: (0, 0)), hbm, hbm],
        out_specs=[pl.BlockSpec((tm, d), lambda i: (i, 0)), pl.BlockSpec((tm, d), lambda i: (i, 0)),
                   pl.BlockSpec((tm, 2 * f), lambda i: (i, 0))],
        out_shape=[jax.ShapeDtypeStruct((t, d), F32), jax.ShapeDtypeStruct((t, d), BF16),
                   jax.ShapeDtypeStruct((t, 2 * f), BF16)],
        scratch_shapes=[pltpu.VMEM((2 * f, d), BF16), pltpu.VMEM((f, d), BF16), pltpu.VMEM((tm, f), BF16),
                        pltpu.SemaphoreType.DMA((2,))],
    )(x, gamma, win_t, wout)


def _ffn_bwd(name, dy, x, gamma, gu, win_t, wout, comm=None):
    t, d = x.shape
    f = wout.shape[0]
    fc = f // HIDDEN_SPLIT
    tm = min(FFN_TILE, t)

    def body(dy_ref, x_ref, gam_ref, gu_ref, win_hbm, wo_hbm,
             dx_ref, dgam_ref, dyh_ref, h_ref, dgu_ref, win_v, wo_v, sems):
        _load_weights([(win_hbm, win_v), (wo_hbm, wo_v)], sems)

        @pl.when(pl.program_id(0) == 0)
        def _():
            dgam_ref[...] = jnp.zeros_like(dgam_ref)

        dyb = (0.5 * dy_ref[...]).astype(BF16)
        dyh_ref[...] = dyb
        for c in range(HIDDEN_SPLIT):
            lo = c * fc
            dh = lax.dot_general(dyb, wo_v[lo:lo + fc, :], NT, preferred_element_type=F32)
            g = gu_ref[:, lo:lo + fc].astype(F32)
            u = gu_ref[:, f + lo:f + lo + fc].astype(F32)
            sig = _sigmoid(g)
            silu = g * sig
            h_ref[:, lo:lo + fc] = (silu * u).astype(BF16)
            dgu_ref[:, lo:lo + fc] = (dh * u * (sig * (1.0 + g * (1.0 - sig)))).astype(BF16)
            dgu_ref[:, f + lo:f + lo + fc] = (dh * silu).astype(BF16)
        dxn = jnp.dot(dgu_ref[...], win_v[...], preferred_element_type=F32)
        xv = x_ref[...]
        r = lax.rsqrt(jnp.mean(xv * xv, axis=-1, keepdims=True) + EPS)
        yv = xv * r
        dgam_ref[...] += jnp.sum(dxn * yv, axis=0, keepdims=True)
        dyn = dxn * gam_ref[...]
        dx_ref[...] = dy_ref[...] + r * (dyn - yv * jnp.mean(dyn * yv, axis=-1, keepdims=True))

    hbm = pl.BlockSpec(memory_space=pl.ANY)
    row = lambda width: pl.BlockSpec((tm, width), lambda i: (i, 0))
    return _call(
        body, comm=comm, name=name, grid=(t // tm,),
        in_specs=[row(d), row(d), pl.BlockSpec((1, d), lambda i: (0, 0)), row(2 * f), hbm, hbm],
        out_specs=[row(d), pl.BlockSpec((1, d), lambda i: (0, 0)), row(d), row(f), row(2 * f)],
        out_shape=[
            jax.ShapeDtypeStruct((t, d), F32),
            jax.ShapeDtypeStruct((1, d), F32),
            jax.ShapeDtypeStruct((t, d), BF16),
            jax.ShapeDtypeStruct((t, f), BF16),
            jax.ShapeDtypeStruct((t, 2 * f), BF16),
        ],
        scratch_shapes=[pltpu.VMEM((2 * f, d), BF16), pltpu.VMEM((f, d), BF16), pltpu.SemaphoreType.DMA((2,))],
    )(dy, x, gamma, gu, win_t, wout)


def _ffn_bwd_hidden(name, dy, gu, wout):
    t, d = dy.shape
    f = wout.shape[0]
    fc = f // HIDDEN_SPLIT
    tm = min(FFN_TILE, t)

    def body(dy_ref, gu_ref, wo_hbm, dyh_ref, h_ref, dgu_ref, wo_v, sems):
        _load_weights([(wo_hbm, wo_v)], sems)
        dyb = (0.5 * dy_ref[...]).astype(BF16)
        dyh_ref[...] = dyb
        for c in range(HIDDEN_SPLIT):
            lo = c * fc
            dh = lax.dot_general(dyb, wo_v[lo:lo + fc, :], NT, preferred_element_type=F32)
            g = gu_ref[:, lo:lo + fc].astype(F32)
            u = gu_ref[:, f + lo:f + lo + fc].astype(F32)
            sig = _sigmoid(g)
            silu = g * sig
            h_ref[:, lo:lo + fc] = (silu * u).astype(BF16)
            dgu_ref[:, lo:lo + fc] = (dh * u * (sig * (1.0 + g * (1.0 - sig)))).astype(BF16)
            dgu_ref[:, f + lo:f + lo + fc] = (dh * silu).astype(BF16)

    row = lambda width: pl.BlockSpec((tm, width), lambda i: (i, 0))
    return _call(
        body, name=name, grid=(t // tm,),
        in_specs=[row(d), row(2 * f), pl.BlockSpec(memory_space=pl.ANY)],
        out_specs=[row(d), row(f), row(2 * f)],
        out_shape=[jax.ShapeDtypeStruct((t, d), BF16), jax.ShapeDtypeStruct((t, f), BF16),
                   jax.ShapeDtypeStruct((t, 2 * f), BF16)],
        scratch_shapes=[pltpu.VMEM((f, d), BF16), pltpu.SemaphoreType.DMA((1,))],
    )(dy, gu, wout)


def _matmul_tn(name, a, b, out_dtype=BF16, comm=None, a_split=None):
    a_b = a.ndim == 3
    b_b = b.ndim == 3
    nb = a_split if a_split else a.shape[0] if a_b else b.shape[0] if b_b else 1
    t, m = a.shape[-2:]
    if a_split:
        m = m // a_split
    n = b.shape[-1]
    tk = min(CONTRACT_TILE, t)
    nt = t // tk

    def body(a_ref, b_ref, o_ref, acc):
        s = pl.program_id(1)

        @pl.when(s == 0)
        def _():
            acc[...] = jnp.zeros_like(acc)

        acc[...] += lax.dot_general(a_ref[...], b_ref[...], TN, preferred_element_type=F32)

        @pl.when(s == nt - 1)
        def _():
            o_ref[...] = acc[...].astype(o_ref.dtype)

    a_spec = (pl.BlockSpec((None, tk, m), lambda j, s: (j, s, 0)) if a_b
              else pl.BlockSpec((tk, m), lambda j, s: (s, j)) if a_split
              else pl.BlockSpec((tk, m), lambda j, s: (s, 0)))
    b_spec = (pl.BlockSpec((None, tk, n), lambda j, s: (j, s, 0)) if b_b
              else pl.BlockSpec((tk, n), lambda j, s: (s, 0)))
    return _call(
        body, comm=comm, name=name, grid=(nb, nt),
        in_specs=[a_spec, b_spec],
        out_specs=pl.BlockSpec((None, m, n), lambda j, s: (j, 0, 0)),
        out_shape=jax.ShapeDtypeStruct((nb, m, n), out_dtype),
        scratch_shapes=[pltpu.VMEM((m, n), F32)],
    )(a, b)


def _matmul_residual(name, a, w, res):
    t, kdim = a.shape
    n = w.shape[1]
    tm = _tile(t)

    def body(a_ref, w_ref, r_ref, o_ref):
        o_ref[...] = r_ref[...] + jnp.dot(a_ref[...], w_ref[...], preferred_element_type=F32)

    return _call(
        body, name=name, grid=(t // tm,),
        in_specs=[pl.BlockSpec((tm, kdim), lambda i: (i, 0)),
                  pl.BlockSpec((kdim, n), lambda i: (0, 0)),
                  pl.BlockSpec((tm, n), lambda i: (i, 0))],
        out_specs=pl.BlockSpec((tm, n), lambda i: (i, 0)),
        out_shape=jax.ShapeDtypeStruct((t, n), F32),
    )(a, w, res)


def _matmul_nt(name, dy, w):
    t, n = dy.shape
    kdim = w.shape[0]
    tm = _tile(t)

    def body(dy_ref, w_ref, da_ref, dyb_ref):
        dyb = dy_ref[...].astype(BF16)
        dyb_ref[...] = dyb
        da_ref[...] = lax.dot_general(dyb, w_ref[...], NT, preferred_element_type=F32)

    return _call(
        body, name=name, grid=(t // tm,),
        in_specs=[pl.BlockSpec((tm, n), lambda i: (i, 0)),
                  pl.BlockSpec((kdim, n), lambda i: (0, 0))],
        out_specs=[pl.BlockSpec((tm, kdim), lambda i: (i, 0)),
                   pl.BlockSpec((tm, n), lambda i: (i, 0))],
        out_shape=[jax.ShapeDtypeStruct((t, kdim), F32), jax.ShapeDtypeStruct((t, n), BF16)],
    )(dy, w)


def _matmul_nt_rms_bwd(name, dz, w, dres, x, gamma, w_rows_are_k=False, tile=TOKEN_TILE, comm=None):
    t, kdim = dz.shape
    d = x.shape[1]
    tm = min(tile, t)

    def body(dz_ref, w_ref, dres_ref, x_ref, gam_ref, dx_ref, dgam_ref):
        i = pl.program_id(0)

        @pl.when(i == 0)
        def _():
            dgam_ref[...] = jnp.zeros_like(dgam_ref)

        if w_rows_are_k:
            dxn = jnp.dot(dz_ref[...], w_ref[...], preferred_element_type=F32)
        else:
            dxn = lax.dot_general(dz_ref[...], w_ref[...], NT, preferred_element_type=F32)
        xv = x_ref[...]
        r = lax.rsqrt(jnp.mean(xv * xv, axis=-1, keepdims=True) + EPS)
        yv = xv * r
        dgam_ref[...] += jnp.sum(dxn * yv, axis=0, keepdims=True)
        dyn = dxn * gam_ref[...]
        dx_ref[...] = dres_ref[...] + r * (dyn - yv * jnp.mean(dyn * yv, axis=-1, keepdims=True))

    return _call(
        body, comm=comm, name=name, grid=(t // tm,),
        in_specs=[pl.BlockSpec((tm, kdim), lambda i: (i, 0)),
                  pl.BlockSpec(w.shape, lambda i: (0, 0)),
                  pl.BlockSpec((tm, d), lambda i: (i, 0)),
                  pl.BlockSpec((tm, d), lambda i: (i, 0)),
                  pl.BlockSpec((1, d), lambda i: (0, 0))],
        out_specs=[pl.BlockSpec((tm, d), lambda i: (i, 0)),
                   pl.BlockSpec((1, d), lambda i: (0, 0))],
        out_shape=[jax.ShapeDtypeStruct((t, d), F32), jax.ShapeDtypeStruct((1, d), F32)],
    )(dz, w, dres, x, gamma)


def _pool_means(uext_ref, pos, tm, g, win):
    cols = slice(g * GROUP, (g + 1) * GROUP)
    acc = uext_ref[pl.ds(HALO, tm), cols]
    for j in range(1, win):
        acc = acc + uext_ref[pl.ds(HALO - j, tm), cols]
    cnt = jnp.minimum(pos + 1, win).astype(F32)
    return acc / cnt - uext_ref[pl.ds(HALO, tm), cols]


def _shifted_copies(src_ref, dst_ref, rows):
    for b in range(SUBLANES):
        dst_ref[b, pl.ds(0, rows), :] = src_ref[pl.ds(b, rows), :]


def _tap_sum(sh_ref, cw_ref, offsets, out_ref, n_rows, bias_ref=None):
    width = out_ref.shape[-1]

    def chunk(c, carry):
        r0 = pl.multiple_of(c * ROW_CHUNK, ROW_CHUNK)
        acc = (jnp.zeros((ROW_CHUNK, width), F32) if bias_ref is None
               else jnp.broadcast_to(bias_ref[...], (ROW_CHUNK, width)))
        for k, off in enumerate(offsets):
            a, b = divmod(off, SUBLANES)
            acc = acc + cw_ref[k:k + 1, :] * sh_ref[b, pl.ds(r0 + SUBLANES * a, ROW_CHUNK), :]
        out_ref[pl.ds(r0, ROW_CHUNK), :] = acc
        return carry

    lax.fori_loop(0, n_rows // ROW_CHUNK, chunk, 0)


def _pool_conv_fwd(name, x, gamma, w, pool_w, pool_b, pool_scale, conv_w, conv_b, ln_g, ln_b):
    t, d = x.shape
    hw = w.shape[1]
    pc = len(POOL_WINDOWS) * GROUP
    cc = (hw - pc) // 2
    tm = _tile(t)

    def body(x_ref, gam_ref, w_ref, pw_ref, pb_ref, ps_ref, cw_ref, cb_ref, lg_ref, lb_ref,
             xn_ref, h_ref, cat_ref, uext, gext, gsh, y_s, hp_s):
        i = pl.program_id(0)

        @pl.when(i == 0)
        def _():
            hp_s[...] = jnp.zeros_like(hp_s)

        xv = x_ref[...]
        r = lax.rsqrt(jnp.mean(xv * xv, axis=-1, keepdims=True) + EPS)
        xn = (xv * r * gam_ref[...]).astype(BF16)
        xn_ref[...] = xn
        h_ref[...] = jnp.dot(xn, w_ref[...], preferred_element_type=F32)
        hp = hp_s[...]
        uext[0:HALO, :] = hp[:, :pc]
        uext[HALO:, :] = h_ref[:, :pc]
        gext[0:HALO, :] = hp[:, pc:pc + cc] * _sigmoid(hp[:, pc + cc:])
        gext[pl.ds(HALO, tm), :] = h_ref[:, pc:pc + cc] * _sigmoid(h_ref[:, pc + cc:])
        gext[pl.ds(HALO + tm, SUBLANES), :] = jnp.zeros((SUBLANES, cc), F32)
        pos = i * tm + lax.broadcasted_iota(jnp.int32, (tm, 1), 0)
        for g, win in enumerate(POOL_WINDOWS):
            cols = slice(g * GROUP, (g + 1) * GROUP)
            pooled = _pool_means(uext, pos, tm, g, win)
            mixed = jnp.dot(pooled.astype(BF16), pw_ref[g].astype(BF16),
                            preferred_element_type=F32) + pb_ref[g:g + 1, :]
            cat_ref[:, cols] = (mixed * ps_ref[:, cols]).astype(BF16)
        _shifted_copies(gext, gsh, HALO + tm)
        _tap_sum(gsh, cw_ref, [HALO - (CONV_WIDTH - 1) + k for k in range(CONV_WIDTH)], y_s, tm, cb_ref)
        y = y_s[...]
        mu = jnp.mean(y, axis=-1, keepdims=True)
        dv = y - mu
        rstd = lax.rsqrt(jnp.mean(dv * dv, axis=-1, keepdims=True) + EPS)
        ln = dv * rstd * lg_ref[...] + lb_ref[...]
        cat_ref[:, pc:] = (ln * _sigmoid(ln)).astype(BF16)
        hp_s[...] = h_ref[pl.ds(tm - HALO, HALO), :]

    small = lambda a: pl.BlockSpec(a.shape, lambda i: (0,) * a.ndim)
    return _call(
        body, name=name, grid=(t // tm,),
        in_specs=[pl.BlockSpec((tm, d), lambda i: (i, 0)), small(gamma), small(w),
                  small(pool_w), small(pool_b), small(pool_scale), small(conv_w), small(conv_b),
                  small(ln_g), small(ln_b)],
        out_specs=[pl.BlockSpec((tm, d), lambda i: (i, 0)), pl.BlockSpec((tm, hw), lambda i: (i, 0)),
                   pl.BlockSpec((tm, pc + cc), lambda i: (i, 0))],
        out_shape=[jax.ShapeDtypeStruct((t, d), BF16), jax.ShapeDtypeStruct((t, hw), F32),
                   jax.ShapeDtypeStruct((t, pc + cc), BF16)],
        scratch_shapes=[pltpu.VMEM((HALO + tm, pc), F32), pltpu.VMEM((HALO + tm + SUBLANES, cc), F32),
                        pltpu.VMEM((SUBLANES, HALO + tm, cc), F32), pltpu.VMEM((tm, cc), F32),
                        pltpu.VMEM((HALO, hw), F32)],
    )(x, gamma, w, pool_w, pool_b, pool_scale, conv_w, conv_b, ln_g, ln_b)


def _pool_conv_bwd(name, h, dcat, pool_w, pool_b, pool_scale, conv_w, conv_b, ln_g, ln_b, comm=None):
    t, hw = h.shape
    pc = len(POOL_WINDOWS) * GROUP
    cc = (hw - pc) // 2
    ng = len(POOL_WINDOWS)
    tm = _tile(t)
    per = tm // HALO
    nt = t // tm
    r2 = tm + HALO
    taps = CONV_WIDTH - 1

    def body(h_ref, hp_ref, hn_ref, dc_ref, dcn_ref, pw_ref, pb_ref, ps_ref, cw_ref, cb_ref, lg_ref, lb_ref,
             dh_ref, dpw_ref, dpb_ref, dps_ref, dcw_ref, dcb_ref, dlg_ref, dlb_ref,
             uext, gext, dcext, dqext, dycext, shifted, y_s, dg_s, dcw_acc):
        i = pl.program_id(0)

        @pl.when(i == 0)
        def _():
            for ref in (dpw_ref, dpb_ref, dps_ref, dcw_ref, dcb_ref, dlg_ref, dlb_ref, dcw_acc):
                ref[...] = jnp.zeros_like(ref)

        keep_p = (i > 0).astype(F32)
        keep_n = (i < nt - 1).astype(F32)
        hp = hp_ref[...] * keep_p
        hn = hn_ref[...] * keep_n
        uext[0:HALO, :] = hp[:, :pc]
        uext[HALO:, :] = h_ref[:, :pc]
        gext[0:HALO, :] = hp[:, pc:pc + cc] * _sigmoid(hp[:, pc + cc:])
        gext[pl.ds(HALO, tm), :] = h_ref[:, pc:pc + cc] * _sigmoid(h_ref[:, pc + cc:])
        gext[pl.ds(HALO + tm, HALO), :] = hn[:, pc:pc + cc] * _sigmoid(hn[:, pc + cc:])
        gext[pl.ds(HALO + tm + HALO, SUBLANES), :] = jnp.zeros((SUBLANES, cc), F32)
        dcext[0:tm, :] = dc_ref[...]
        dcext[pl.ds(tm, HALO), :] = dcn_ref[...] * keep_n

        pos = i * tm + lax.broadcasted_iota(jnp.int32, (tm, 1), 0)
        pos2 = i * tm + lax.broadcasted_iota(jnp.int32, (r2, 1), 0)
        for g, win in enumerate(POOL_WINDOWS):
            cols = slice(g * GROUP, (g + 1) * GROUP)
            wg = pw_ref[g].astype(BF16)
            dya = dcext[:, cols]
            dmixed = dya * ps_ref[:, cols]
            dpooled = lax.dot_general(dmixed.astype(BF16), wg, NT, preferred_element_type=F32)
            cnt2 = jnp.minimum(pos2 + 1, win).astype(F32)
            dqext[:, cols] = dpooled / cnt2
            du = -dpooled[0:tm]
            for j in range(win):
                du = du + dqext[pl.ds(j, tm), cols]
            dh_ref[:, cols] = du.astype(BF16)
            pooled = _pool_means(uext, pos, tm, g, win)
            pooled_b = pooled.astype(BF16)
            mixed = jnp.dot(pooled_b, wg, preferred_element_type=F32) + pb_ref[g:g + 1, :]
            dps_ref[:, cols] += jnp.sum(dya[0:tm] * mixed, axis=0, keepdims=True)
            dpb_ref[g:g + 1, :] += jnp.sum(dmixed[0:tm], axis=0, keepdims=True)
            dpw_ref[g] += lax.dot_general(pooled_b, dmixed[0:tm].astype(BF16), TN, preferred_element_type=F32)

        _shifted_copies(gext, shifted, HALO + tm + HALO)
        _tap_sum(shifted, cw_ref, [HALO - taps + k for k in range(CONV_WIDTH)], y_s, r2, cb_ref)
        y = y_s[...]
        mu = jnp.mean(y, axis=-1, keepdims=True)
        dv = y - mu
        rstd = lax.rsqrt(jnp.mean(dv * dv, axis=-1, keepdims=True) + EPS)
        norm = dv * rstd
        ln = norm * lg_ref[...] + lb_ref[...]
        sig = _sigmoid(ln)
        dln = dcext[:, pc:] * (sig * (1.0 + ln * (1.0 - sig)))
        dnorm = dln * lg_ref[...]
        dyc = rstd * (dnorm - jnp.mean(dnorm, axis=-1, keepdims=True)
                      - norm * jnp.mean(dnorm * norm, axis=-1, keepdims=True))
        dycext[pl.ds(0, r2), :] = dyc
        dycext[pl.ds(r2, SUBLANES), :] = jnp.zeros((SUBLANES, cc), F32)
        dlg_ref[...] += jnp.sum((dln * norm)[0:tm], axis=0, keepdims=True)
        dlb_ref[...] += jnp.sum(dln[0:tm], axis=0, keepdims=True)
        dcb_ref[...] += jnp.sum(dyc[0:tm], axis=0, keepdims=True)

        def fold(c, carry):
            r0 = pl.multiple_of(c * ROW_CHUNK, ROW_CHUNK)
            dchunk = dycext[pl.ds(r0, ROW_CHUNK), :]
            for k in range(CONV_WIDTH):
                a8, b8 = divmod(HALO - taps + k, SUBLANES)
                prod = dchunk * shifted[b8, pl.ds(r0 + SUBLANES * a8, ROW_CHUNK), :]
                part = prod[0:SUBLANES]
                for q in range(1, ROW_CHUNK // SUBLANES):
                    part = part + prod[q * SUBLANES:(q + 1) * SUBLANES]
                dcw_acc[k] += part
            return carry

        lax.fori_loop(0, tm // ROW_CHUNK, fold, 0)

        @pl.when(i == nt - 1)
        def _():
            for k in range(CONV_WIDTH):
                dcw_ref[k:k + 1, :] = jnp.sum(dcw_acc[k], axis=0, keepdims=True)

        _shifted_copies(dycext, shifted, r2)
        _tap_sum(shifted, cw_ref, [taps - k for k in range(CONV_WIDTH)], dg_s, tm)
        dg = dg_s[...]
        a = h_ref[:, pc:pc + cc]
        sg = _sigmoid(h_ref[:, pc + cc:])
        dh_ref[:, pc:pc + cc] = (dg * sg).astype(BF16)
        dh_ref[:, pc + cc:] = (dg * a * sg * (1.0 - sg)).astype(BF16)

    small = lambda a: pl.BlockSpec(a.shape, lambda i: (0,) * a.ndim)
    smalls = (pool_w, pool_b, pool_scale, conv_w, conv_b, ln_g, ln_b)
    return _call(
        body, comm=comm, name=name, grid=(nt,),
        in_specs=[pl.BlockSpec((tm, hw), lambda i: (i, 0)),
                  pl.BlockSpec((HALO, hw), lambda i: (jnp.maximum(i * per - 1, 0), 0)),
                  pl.BlockSpec((HALO, hw), lambda i: (jnp.minimum((i + 1) * per, t // HALO - 1), 0)),
                  pl.BlockSpec((tm, pc + cc), lambda i: (i, 0)),
                  pl.BlockSpec((HALO, pc + cc), lambda i: (jnp.minimum((i + 1) * per, t // HALO - 1), 0)),
                  ] + [small(a) for a in smalls],
        out_specs=[pl.BlockSpec((tm, hw), lambda i: (i, 0))] + [small(a) for a in smalls],
        out_shape=[jax.ShapeDtypeStruct((t, hw), BF16)] + [jax.ShapeDtypeStruct(a.shape, F32) for a in smalls],
        scratch_shapes=[pltpu.VMEM((HALO + tm, pc), F32), pltpu.VMEM((HALO + tm + HALO + SUBLANES, cc), F32),
                        pltpu.VMEM((r2, pc + cc), F32), pltpu.VMEM((r2, pc), F32),
                        pltpu.VMEM((r2 + SUBLANES, cc), F32), pltpu.VMEM((SUBLANES, HALO + tm + HALO, cc), F32),
                        pltpu.VMEM((r2, cc), F32), pltpu.VMEM((tm, cc), F32),
                        pltpu.VMEM((CONV_WIDTH + 1, SUBLANES, cc), F32)],
    )(h, h, h, dcat, dcat, *smalls)


SQRT_HALF = 0.7071067811865476
INV_SQRT_2PI = 0.3989422804014327


def _sgu_core(zp_ref, lg_ref, lb_ref, ws_ref, bs_ref, vo_s, tm, sc, heads):
    zp = zp_ref[...]
    cdf = 0.5 * (1.0 + lax.erf(zp * SQRT_HALF))
    z = zp * cdf
    u = z[:, :sc]
    v = z[:, sc:]
    mu = jnp.mean(v, axis=-1, keepdims=True)
    dv = v - mu
    rstd = lax.rsqrt(jnp.mean(dv * dv, axis=-1, keepdims=True) + EPS)
    norm = dv * rstd
    vb = (norm * lg_ref[...] + lb_ref[...]).astype(BF16)
    row = lax.broadcasted_iota(jnp.int32, (GROUP, GROUP), 0)
    col = lax.broadcasted_iota(jnp.int32, (GROUP, GROUP), 1)
    mask = (col <= row).astype(F32)
    wm = [ws_ref[hd] * mask for hd in range(heads)]
    for hd in range(heads):
        cols = slice(hd * GROUP, (hd + 1) * GROUP)
        wb = wm[hd].astype(BF16)
        for n in range(tm // GROUP):
            rows = slice(n * GROUP, (n + 1) * GROUP)
            vo_s[rows, cols] = jnp.dot(wb, vb[rows, cols], preferred_element_type=F32) + bs_ref[hd]
    return (zp, cdf), u, norm, rstd, vb, wm, mask


def _sgu_fwd(name, x, gamma, w, ln_g, ln_b, w_s, b_s):
    t, d = x.shape
    two_sc = w.shape[1]
    sc = two_sc // 2
    heads = sc // GROUP
    tm = _tile(t)

    def body(x_ref, gam_ref, w_ref, lg_ref, lb_ref, ws_ref, bs_ref, xn_ref, zp_ref, q_ref, vo_s):
        xv = x_ref[...]
        r = lax.rsqrt(jnp.mean(xv * xv, axis=-1, keepdims=True) + EPS)
        xn = (xv * r * gam_ref[...]).astype(BF16)
        xn_ref[...] = xn
        zp_ref[...] = jnp.dot(xn, w_ref[...], preferred_element_type=F32)
        _, u, _, _, _, _, _ = _sgu_core(zp_ref, lg_ref, lb_ref, ws_ref, bs_ref, vo_s, tm, sc, heads)
        q_ref[...] = (u * vo_s[...]).astype(BF16)

    small = lambda a: pl.BlockSpec(a.shape, lambda i: (0,) * a.ndim)
    return _call(
        body, name=name, grid=(t // tm,),
        in_specs=[pl.BlockSpec((tm, d), lambda i: (i, 0)), small(gamma), small(w),
                  small(ln_g), small(ln_b), small(w_s), small(b_s)],
        out_specs=[pl.BlockSpec((tm, d), lambda i: (i, 0)), pl.BlockSpec((tm, two_sc), lambda i: (i, 0)),
                   pl.BlockSpec((tm, sc), lambda i: (i, 0))],
        out_shape=[jax.ShapeDtypeStruct((t, d), BF16), jax.ShapeDtypeStruct((t, two_sc), F32),
                   jax.ShapeDtypeStruct((t, sc), BF16)],
        scratch_shapes=[pltpu.VMEM((tm, sc), F32)],
    )(x, gamma, w, ln_g, ln_b, w_s, b_s)


def _sgu_bwd(name, zp, dy, w_out, ln_g, ln_b, w_s, b_s):
    t, two_sc = zp.shape
    d = dy.shape[1]
    sc = two_sc // 2
    heads = sc // GROUP
    tm = _tile(t)
    nt = t // tm

    def body(zp_ref, dy_ref, wo_ref, lg_ref, lb_ref, ws_ref, bs_ref,
             dzp_ref, dyb_ref, dlg_ref, dlb_ref, dws_ref, dbs_ref, vo_s, dvl_s, dws_acc):
        i = pl.program_id(0)

        @pl.when(i == 0)
        def _():
            dlg_ref[...] = jnp.zeros_like(dlg_ref)
            dlb_ref[...] = jnp.zeros_like(dlb_ref)
            dbs_ref[...] = jnp.zeros_like(dbs_ref)
            dws_acc[...] = jnp.zeros_like(dws_acc)

        dyb = dy_ref[...].astype(BF16)
        dyb_ref[...] = dyb
        dq = lax.dot_general(dyb, wo_ref[...], NT, preferred_element_type=F32)
        (zp, cdf), u, norm, rstd, vb, wm, mask = _sgu_core(zp_ref, lg_ref, lb_ref, ws_ref, bs_ref, vo_s, tm, sc, heads)
        du = dq * vo_s[...]
        dvo = dq * u
        dvob = dvo.astype(BF16)
        for hd in range(heads):
            cols = slice(hd * GROUP, (hd + 1) * GROUP)
            wtb = jnp.transpose(wm[hd]).astype(BF16)
            for n in range(tm // GROUP):
                rows = slice(n * GROUP, (n + 1) * GROUP)
                blk = dvob[rows, cols]
                dws_acc[hd] += lax.dot_general(blk, vb[rows, cols], NT, preferred_element_type=F32)
                dvl_s[rows, cols] = jnp.dot(wtb, blk, preferred_element_type=F32)
                dbs_ref[hd] += jnp.sum(dvo[rows, cols], axis=-1, keepdims=True)
        dvl = dvl_s[...]
        dlg_ref[...] += jnp.sum(dvl * norm, axis=0, keepdims=True)
        dlb_ref[...] += jnp.sum(dvl, axis=0, keepdims=True)
        dnorm = dvl * lg_ref[...]
        dv = rstd * (dnorm - jnp.mean(dnorm, axis=-1, keepdims=True)
                     - norm * jnp.mean(dnorm * norm, axis=-1, keepdims=True))
        dgelu = cdf + zp * (INV_SQRT_2PI * jnp.exp(-0.5 * zp * zp))
        dzp_ref[:, :sc] = (du * dgelu[:, :sc]).astype(BF16)
        dzp_ref[:, sc:] = (dv * dgelu[:, sc:]).astype(BF16)

        @pl.when(i == nt - 1)
        def _():
            for hd in range(heads):
                dws_ref[hd] = dws_acc[hd] * mask

    small = lambda a: pl.BlockSpec(a.shape, lambda i: (0,) * a.ndim)
    smalls = (ln_g, ln_b, w_s, b_s)
    return _call(
        body, name=name, grid=(nt,),
        in_specs=[pl.BlockSpec((tm, two_sc), lambda i: (i, 0)), pl.BlockSpec((tm, d), lambda i: (i, 0)),
                  small(w_out)] + [small(a) for a in smalls],
        out_specs=[pl.BlockSpec((tm, two_sc), lambda i: (i, 0)), pl.BlockSpec((tm, d), lambda i: (i, 0))]
                  + [small(a) for a in smalls],
        out_shape=[jax.ShapeDtypeStruct((t, two_sc), BF16), jax.ShapeDtypeStruct((t, d), BF16)]
                  + [jax.ShapeDtypeStruct(a.shape, F32) for a in smalls],
        scratch_shapes=[pltpu.VMEM((tm, sc), F32), pltpu.VMEM((tm, sc), F32), pltpu.VMEM(w_s.shape, F32)],
    )(zp, dy, w_out, ln_g, ln_b, w_s, b_s)


def _loss_head(name, x, gamma, target):
    t, d = x.shape
    tm = _tile(t)

    def body(x_ref, gam_ref, tg_ref, loss_ref, dx_ref, dgam_ref):
        i = pl.program_id(0)

        @pl.when(i == 0)
        def _():
            loss_ref[...] = jnp.zeros_like(loss_ref)
            dgam_ref[...] = jnp.zeros_like(dgam_ref)

        xv = x_ref[...]
        r = lax.rsqrt(jnp.mean(xv * xv, axis=-1, keepdims=True) + EPS)
        yv = xv * r
        err = yv * gam_ref[...] - tg_ref[...]
        row = jnp.sum(err * err, axis=-1, keepdims=True)
        loss_ref[...] += (0.5 / d) * jnp.sum(row, axis=0, keepdims=True)
        dout = err * (1.0 / d)
        dgam_ref[...] += jnp.sum(dout * yv, axis=0, keepdims=True)
        dyn = dout * gam_ref[...]
        dx_ref[...] = r * (dyn - yv * jnp.mean(dyn * yv, axis=-1, keepdims=True))

    return _call(
        body, name=name, grid=(t // tm,),
        in_specs=[pl.BlockSpec((tm, d), lambda i: (i, 0)),
                  pl.BlockSpec((1, d), lambda i: (0, 0)),
                  pl.BlockSpec((tm, d), lambda i: (i, 0))],
        out_specs=[pl.BlockSpec((1, 1), lambda i: (0, 0)),
                   pl.BlockSpec((tm, d), lambda i: (i, 0)),
                   pl.BlockSpec((1, d), lambda i: (0, 0))],
        out_shape=[jax.ShapeDtypeStruct((1, 1), F32), jax.ShapeDtypeStruct((t, d), F32),
                   jax.ShapeDtypeStruct((1, d), F32)],
    )(x, gamma, target)


def _adamw(name, parts, w, m, v, rows):
    l_n, r_n, c_n = w.shape
    s_n = parts[0].shape[0]
    tr = min(rows, r_n)
    nr = r_n // tr
    c1 = 1.0 - ADAM_B1 ** ADAM_STEP
    c2 = 1.0 - ADAM_B2 ** ADAM_STEP

    def body(*refs):
        p_refs = refs[:l_n]
        w_ref, m_ref, v_ref, g_ref, d_ref, mo_ref, vo_ref = refs[l_n:]
        layer = pl.program_id(0)

        def update(p_ref):
            g = p_ref[0].astype(F32)
            for s in range(1, s_n):
                g = g + p_ref[s].astype(F32)
            mn = ADAM_B1 * m_ref[...] + (1.0 - ADAM_B1) * g
            vn = ADAM_B2 * v_ref[...] + (1.0 - ADAM_B2) * (g * g)
            m_hat = mn / c1
            v_hat = vn / c2
            g_ref[...] = g
            d_ref[...] = -ADAM_LR * (m_hat / (jnp.sqrt(v_hat) + ADAM_EPS) + ADAM_WD * w_ref[...])
            mo_ref[...] = mn
            vo_ref[...] = vn

        for j in range(l_n):
            pl.when(layer == j)(functools.partial(update, p_refs[j]))

    def part_spec(j):
        return pl.BlockSpec((s_n, tr, c_n), lambda l, i: (0, jnp.where(l == j, i, jnp.where(l < j, 0, nr - 1)), 0))

    blk = pl.BlockSpec((None, tr, c_n), lambda l, i: (l, i, 0))
    return _call(
        body, name=name, grid=(l_n, nr),
        in_specs=[part_spec(j) for j in range(l_n)] + [blk, blk, blk],
        out_specs=[blk] * 4,
        out_shape=[jax.ShapeDtypeStruct((l_n, r_n, c_n), F32)] * 4,
    )(*parts, w, m, v)


def _local_step(x, target, big, small, sched=None):
    t, d = x.shape
    n_layers = small["ffn1_norm"].shape[0]
    gb = {}
    gs = {}

    def row(a, l):
        return a[l:l + 1]

    def run(fn, name, *operands, **kw):
        comm = sched.plan(name, gb, gs) if sched is not None else None
        if comm is None:
            return fn(name, *operands, **kw)
        res, got = fn(name, *operands, comm=comm, **kw)
        sched.deliver(comm, got)
        return res

    saved = []
    xs = x
    for l in range(n_layers):
        rec = {"x_ffn1": xs}
        xs, rec["xn_ffn1"], rec["gu_ffn1"] = run(
            _ffn_fwd, f"ffn1_fwd_l{l}", xs, row(small["ffn1_norm"], l), big["ffn1_w_in", l], big["ffn1_w_out", l])
        rec["x_mix"] = xs
        if l % 2 == 0:
            rec["xn_mix"], rec["h"], rec["cat"] = _pool_conv_fwd(
                f"pool_conv_fwd_l{l}", xs, row(small["mix_norm"], l), big["ab_w_in"], small["pool_w"],
                small["pool_b"], small["pool_scale"], small["conv_w"], small["conv_b"],
                small["conv_ln_g"], small["conv_ln_b"])
            xs = _matmul_residual(f"ab_out_l{l}", rec["cat"], big["ab_w_out"], xs)
        else:
            rec["xn_mix"], rec["zp"], rec["q"] = _sgu_fwd(
                f"sgu_fwd_l{l}", xs, row(small["mix_norm"], l), big["sgu_w_in"], small["sgu_ln_g"],
                small["sgu_ln_b"], small["sgu_w"], small["sgu_b"])
            xs = _matmul_residual(f"sgu_out_l{l}", rec["q"], big["sgu_w_out"], xs)
        rec["x_ffn2"] = xs
        xs, rec["xn_ffn2"], rec["gu_ffn2"] = run(
            _ffn_fwd, f"ffn2_fwd_l{l}", xs, row(small["ffn2_norm"], l), big["ffn2_w_in", l], big["ffn2_w_out", l])
        saved.append(rec)

    loss, dx, gs["final_norm"] = _loss_head("loss_head", xs, small["final_norm"], target)

    norm_rows = {"ffn1_norm": [None] * n_layers, "mix_norm": [None] * n_layers, "ffn2_norm": [None] * n_layers}

    def ffn_backward(tag, l, dy, rec):
        gamma = row(small[f"{tag}_norm"], l)
        weights_first = (tag, l) == ("ffn1", 0)
        if weights_first:
            dyh, hh, dgu = _ffn_bwd_hidden(f"{tag}_bwd_hidden_l{l}", dy, rec[f"gu_{tag}"], big[f"{tag}_w_out", l])
        else:
            dx, dgam, dyh, hh, dgu = run(_ffn_bwd, f"{tag}_bwd_l{l}", dy, rec[f"x_{tag}"], gamma,
                                         rec[f"gu_{tag}"], big[f"{tag}_w_in", l], big[f"{tag}_w_out", l])
        gb[f"{tag}_w_out", l] = run(_matmul_tn, f"{tag}_dwout_l{l}", hh, dyh,
                                    a_split=HIDDEN_SPLIT).reshape(N_DEV, -1, d)
        gb[f"{tag}_w_in", l] = run(_matmul_tn, f"{tag}_dwin_l{l}", dgu, rec[f"xn_{tag}"],
                                   a_split=2 * HIDDEN_SPLIT).reshape(N_DEV, -1, d)
        if weights_first:
            dx, dgam = run(_matmul_nt_rms_bwd, f"{tag}_dx_l{l}", dgu, big[f"{tag}_w_in", l], dy, rec[f"x_{tag}"],
                           gamma, w_rows_are_k=True, tile=FFN_TILE)
        norm_rows[f"{tag}_norm"][l] = dgam
        return dx

    for l in reversed(range(n_layers)):
        rec = saved[l]
        dx = ffn_backward("ffn2", l, dx, rec)
        if l % 2 == 0:
            dcat, dxb = _matmul_nt(f"ab_out_bwd_l{l}", dx, big["ab_w_out"])
            gb["ab_w_out", 0] = _matmul_tn(f"ab_dwout_l{l}", rec["cat"], dxb)
            dh, gs["pool_w"], gs["pool_b"], gs["pool_scale"], gs["conv_w"], gs["conv_b"], gs["conv_ln_g"], \
                gs["conv_ln_b"] = run(
                    _pool_conv_bwd, f"pool_conv_bwd_l{l}", rec["h"], dcat, small["pool_w"], small["pool_b"],
                    small["pool_scale"], small["conv_w"], small["conv_b"], small["conv_ln_g"], small["conv_ln_b"])
            gb["ab_w_in", 0] = _matmul_tn(f"ab_dwin_l{l}", rec["xn_mix"], dh)
            dx, dgam = _matmul_nt_rms_bwd(f"ab_in_bwd_l{l}", dh, big["ab_w_in"], dx, rec["x_mix"],
                                          row(small["mix_norm"], l))
        else:
            dzp, dxb, gs["sgu_ln_g"], gs["sgu_ln_b"], gs["sgu_w"], gs["sgu_b"] = _sgu_bwd(
                f"sgu_bwd_l{l}", rec["zp"], dx, big["sgu_w_out"], small["sgu_ln_g"], small["sgu_ln_b"],
                small["sgu_w"], small["sgu_b"])
            gb["sgu_w_out", 0] = _matmul_tn(f"sgu_dwout_l{l}", rec["q"], dxb)
            gb["sgu_w_in", 0] = _matmul_tn(f"sgu_dwin_l{l}", rec["xn_mix"], dzp)
            dx, dgam = _matmul_nt_rms_bwd(f"sgu_in_bwd_l{l}", dzp, big["sgu_w_in"], dx, rec["x_mix"],
                                          row(small["mix_norm"], l))
        norm_rows["mix_norm"][l] = dgam
        dx = ffn_backward("ffn1", l, dx, rec)

    for k, rows in norm_rows.items():
        gs[k] = jnp.concatenate(rows, axis=0)
    return loss, dx, gb, gs


SHARDED_SMALL = ("conv_w", "sgu_ln_g", "sgu_ln_b")
WEIGHTS = ("ffn1_norm", "ffn1_w_in", "ffn1_w_out", "mix_norm", "ffn2_norm", "ffn2_w_in", "ffn2_w_out", "ab_w_in",
           "pool_w", "pool_b", "pool_scale", "conv_w", "conv_b", "conv_ln_g", "conv_ln_b", "ab_w_out", "sgu_w_in",
           "sgu_ln_g", "sgu_ln_b", "sgu_w", "sgu_b", "sgu_w_out", "final_norm")
LANES = 128


def _interleave_cols(g):
    n, k, c = g.shape
    return jnp.transpose(g, (1, 0, 2)).reshape(k, n * c)


def _split_cols(a):
    k, nc = a.shape
    return jnp.transpose(a.reshape(k, N_DEV, nc // N_DEV), (1, 0, 2))


def _as3(a):
    if a.ndim == 1:
        return a.reshape(1, 1, -1)
    if a.ndim == 2:
        return a.reshape(a.shape[0], 1, a.shape[1])
    return a.reshape(a.shape[0], -1, a.shape[-1])


def _pack_rows(a):
    flat = a.reshape(-1)
    pad = (-flat.shape[0]) % (8 * LANES)
    if pad:
        flat = jnp.concatenate([flat, jnp.zeros((pad,), flat.dtype)])
    return flat.reshape(-1, LANES)


FIRST_GATHER = (("ffn1_w_in", 0), ("ffn1_w_out", 0), ("conv_w", 0), ("sgu_ln_g", 0), ("sgu_ln_b", 0))
GATHER_PLAN = {
    "ffn1_fwd_l0": (("ab_w_in", 0), ("ab_w_out", 0), ("ffn2_w_in", 0), ("ffn2_w_out", 0)),
    "ffn2_fwd_l0": (("ffn1_w_in", 1), ("ffn1_w_out", 1), ("sgu_w_in", 0), ("sgu_w_out", 0)),
    "ffn1_fwd_l1": (("ffn2_w_in", 1), ("ffn2_w_out", 1)),
}
SCATTER_PLAN = {
    "ffn2_dwin_l1": (("ffn2_w_out", 1),),
    "ffn1_bwd_l1": (("ffn2_w_in", 1), ("sgu_w_out", 0)),
    "ffn1_dwout_l1": (("sgu_w_in", 0), ("sgu_ln_g", 0), ("sgu_ln_b", 0)),
    "ffn1_dwin_l1": (("ffn1_w_out", 1),),
    "ffn2_bwd_l0": (("ffn1_w_in", 1), ("pack", 0)),
    "ffn2_dwin_l0": (("ffn2_w_out", 0),),
    "pool_conv_bwd_l0": (("ffn2_w_in", 0), ("ab_w_out", 0)),
    "ffn1_dwout_l0": (("ab_w_in", 0), ("conv_w", 0)),
    "ffn1_dwin_l0": (("ffn1_w_out", 0), ("pack", 1)),
    "ffn1_dx_l0": (("ffn1_w_in", 0),),
}
PACK_GROUPS = (("sgu_w", "sgu_b"),
               ("pool_w", "pool_b", "pool_scale", "conv_b", "conv_ln_g", "conv_ln_b"),
               ("ffn1_norm", "mix_norm", "ffn2_norm", "final_norm"))
REPLICATED = tuple(n for group in PACK_GROUPS for n in group)


class _Schedule:
    def __init__(self, shards, big, small):
        self.shards, self.big, self.small = shards, big, small
        self.recv = {}
        self.packs = {}
        self.pending = {}

    def gather_comm(self, keys):
        comm = _Comm()
        for key in keys:
            comm.gather(*self.shards[key])
        self.pending[id(comm)] = ("gather", keys)
        return comm

    def scatter_comm(self, keys, gb, gs):
        comm = _Comm()
        for name, l in keys:
            if name == "pack":
                comm.gather(jnp.concatenate([_pack_rows(gs[n]) for n in PACK_GROUPS[l]], axis=0))
                continue
            if name in ("ab_w_in", "sgu_w_in"):
                send = _split_cols(gb[name, l][0])
            elif name in ("ab_w_out", "sgu_w_out"):
                send = gb[name, l][0]
                send = send.reshape(N_DEV, -1, send.shape[-1])
            elif name == "conv_w":
                send = _split_cols(gs[name][:CONV_WIDTH])
            elif name in ("sgu_ln_g", "sgu_ln_b"):
                send = gs[name].reshape(N_DEV, 1, -1)
            else:
                send = gb[name, l]
            comm.scatter(send)
        self.pending[id(comm)] = ("scatter", keys)
        return comm

    def plan(self, name, gb, gs):
        if name in GATHER_PLAN:
            return self.gather_comm(GATHER_PLAN[name])
        if name in SCATTER_PLAN:
            return self.scatter_comm(SCATTER_PLAN[name], gb, gs)
        return None

    def deliver(self, comm, got):
        kind, keys = self.pending.pop(id(comm))
        for (name, l), arr in zip(keys, got):
            if name == "pack":
                self.packs[l] = arr
            elif kind == "scatter":
                self.recv[name, l] = arr
            elif name in ("ffn1_w_in", "ffn2_w_in"):
                self.big[name, l] = arr.reshape(-1, arr.shape[-1])
            elif name in ("ffn1_w_out", "ffn2_w_out"):
                self.big[name, l] = arr.reshape(-1, arr.shape[-1])
            elif name in ("ab_w_in", "sgu_w_in"):
                self.big[name] = _interleave_cols(arr)
            elif name in ("ab_w_out", "sgu_w_out"):
                self.big[name] = arr.reshape(-1, arr.shape[-1])
            elif name == "conv_w":
                self.small[name] = jnp.pad(_interleave_cols(arr), ((0, 1), (0, 0)))
            else:
                self.small[name] = arr.reshape(1, -1)


def kernel(x, ffn1_norm, ffn1_w_in, ffn1_w_out, mix_norm, ffn2_norm, ffn2_w_in, ffn2_w_out, ab_w_in, pool_w, pool_b, pool_scale, conv_w, conv_b, conv_ln_g, conv_ln_b, ab_w_out, sgu_w_in, sgu_ln_g, sgu_ln_b, sgu_w, sgu_b, sgu_w_out, final_norm, loss_target, m_ffn1_norm, m_ffn1_w_in, m_ffn1_w_out, m_mix_norm, m_ffn2_norm, m_ffn2_w_in, m_ffn2_w_out, m_ab_w_in, m_pool_w, m_pool_b, m_pool_scale, m_conv_w, m_conv_b, m_conv_ln_g, m_conv_ln_b, m_ab_w_out, m_sgu_w_in, m_sgu_ln_g, m_sgu_ln_b, m_sgu_w, m_sgu_b, m_sgu_w_out, m_final_norm, v_ffn1_norm, v_ffn1_w_in, v_ffn1_w_out, v_mix_norm, v_ffn2_norm, v_ffn2_w_in, v_ffn2_w_out, v_ab_w_in, v_pool_w, v_pool_b, v_pool_scale, v_conv_w, v_conv_b, v_conv_ln_g, v_conv_ln_b, v_ab_w_out, v_sgu_w_in, v_sgu_ln_g, v_sgu_ln_b, v_sgu_w, v_sgu_b, v_sgu_w_out, v_final_norm):
    args = dict(locals())
    w = {n: args[n] for n in WEIGHTS}
    m = {n: args["m_" + n] for n in WEIGHTS}
    v = {n: args["v_" + n] for n in WEIGHTS}
    n_layers = ffn1_norm.shape[0]

    shards = {}
    for n in ("ffn1_w_in", "ffn2_w_in"):
        wt = jnp.swapaxes(w[n], 1, 2).astype(BF16)
        for l in range(n_layers):
            shards[n, l] = (wt, l)
    for n in ("ffn1_w_out", "ffn2_w_out"):
        wb = w[n].astype(BF16)
        for l in range(n_layers):
            shards[n, l] = (wb, l)
    for n in ("ab_w_in", "ab_w_out", "sgu_w_in", "sgu_w_out"):
        shards[n, 0] = (w[n][0].astype(BF16), None)
    shards["conv_w", 0] = (conv_w[0], None)
    shards["sgu_ln_g", 0] = (sgu_ln_g, None)
    shards["sgu_ln_b", 0] = (sgu_ln_b, None)

    big = {}
    small = {
        "ffn1_norm": ffn1_norm, "mix_norm": mix_norm, "ffn2_norm": ffn2_norm, "final_norm": final_norm.reshape(1, -1),
        "pool_w": pool_w[0], "pool_b": pool_b[0], "pool_scale": pool_scale,
        "conv_b": conv_b, "conv_ln_g": conv_ln_g, "conv_ln_b": conv_ln_b,
        "sgu_w": sgu_w[0], "sgu_b": sgu_b[0][:, :, None],
    }
    sched = _Schedule(shards, big, small)
    first = sched.gather_comm(FIRST_GATHER)
    sched.deliver(first, _exchange("gather_first", first))

    loss, grad_x, gb, gs = _local_step(x[0], loss_target[0], big, small, sched)

    last_pack = jnp.concatenate([_pack_rows(gs[n]) for n in PACK_GROUPS[-1]] + [_pack_rows(loss)], axis=0)
    last = _Comm()
    last.gather(last_pack)
    packed_all = jnp.concatenate([sched.packs[0], sched.packs[1], _exchange("reduce_last", last)[0]], axis=1)
    offsets = [0]
    for n in REPLICATED:
        offsets.append(offsets[-1] + SUBLANES * (-(-w[n].size // (SUBLANES * LANES))))
    offsets.append(offsets[-1] + SUBLANES)
    n_rows = offsets[-1]
    recv = sched.recv

    out = {}
    for n in ("ffn1_w_in", "ffn2_w_in"):
        res = _adamw(f"adamw_{n}", [recv[n, l] for l in range(n_layers)], jnp.swapaxes(w[n], 1, 2),
                     jnp.swapaxes(m[n], 1, 2), jnp.swapaxes(v[n], 1, 2), 176)
        out[n] = [jnp.swapaxes(r, 1, 2) for r in res]
    for n in ("ffn1_w_out", "ffn2_w_out"):
        out[n] = _adamw(f"adamw_{n}", [recv[n, l] for l in range(n_layers)], w[n], m[n], v[n], 176)
    for n in ("ab_w_in", "ab_w_out", "sgu_w_in", "sgu_w_out") + SHARDED_SMALL:
        w3 = _as3(w[n])
        parts = recv[n, 0].reshape((N_DEV,) + w3.shape[1:])
        res = _adamw(f"adamw_{n}", [parts], w3, _as3(m[n]), _as3(v[n]), 512)
        out[n] = [r.reshape(w[n].shape) for r in res]

    def pack_rep(src):
        tail = [jnp.zeros((offsets[-1] - offsets[-2], LANES), F32)]
        return jnp.concatenate([_pack_rows(src[n]) for n in REPLICATED] + tail, axis=0)[None]

    res = _adamw("adamw_replicated", [packed_all], pack_rep(w), pack_rep(m), pack_rep(v), n_rows)
    for i, n in enumerate(REPLICATED):
        size = w[n].size
        out[n] = [r[0, offsets[i]:offsets[i + 1]].reshape(-1)[:size].reshape(w[n].shape) for r in res]
    loss_sum = res[0][0, offsets[-2], 0]

    return (loss_sum, grad_x[None],
            *[out[n][0] for n in WEIGHTS], *[out[n][1] for n in WEIGHTS],
            *[out[n][2] for n in WEIGHTS], *[out[n][3] for n in WEIGHTS])
```

```python
import functools

import jax
import jax.numpy as jnp
from jax import lax
from jax.experimental import pallas as pl
from jax.experimental.pallas import tpu as pltpu

F32 = jnp.float32
BF16 = jnp.bfloat16
EPS = 1e-6
N_DEV = 8
POOL_WINDOWS = (2, 4, 8, 16)
CONV_WIDTH = 31
HALO = 32
GROUP = 128
SUBLANES = 8
ROW_CHUNK = 32
SUBLANES = 8
ROW_CHUNK = 32
TOKEN_TILE = 512
CONTRACT_TILE = 2048
HIDDEN_SPLIT = 2
FFN_TILE = 256
ADAM_LR, ADAM_B1, ADAM_B2, ADAM_EPS, ADAM_WD, ADAM_STEP = 0.001, 0.9, 0.999, 1e-08, 0.01, 10
VMEM_LIMIT = 56 * 1024 * 1024

NT = (((1,), (1,)), ((), ()))
TN = (((0,), (0,)), ((), ()))


def _pallas(body, side_effects, **kw):
    params = pltpu.CompilerParams(vmem_limit_bytes=VMEM_LIMIT, has_side_effects=side_effects)
    return pl.pallas_call(body, compiler_params=params, **kw)


def _call(body, comm=None, **kw):
    if comm is None:
        return _pallas(body, False, **kw)
    in_specs = list(kw.pop("in_specs"))
    out_specs = kw.pop("out_specs")
    out_shape = kw.pop("out_shape")
    scratch = list(kw.pop("scratch_shapes", []))
    single = not isinstance(out_shape, (list, tuple))
    if single:
        out_specs, out_shape = [out_specs], [out_shape]
    n_in, n_out, n_scr = len(in_specs), len(out_shape), len(scratch)
    n_ci, n_co = len(comm.inputs), len(comm.out_shapes)
    grid = tuple(kw.get("grid", ()))

    def wrapped(*refs):
        pos = 0
        parts = []
        for n in (n_in, n_ci, n_out, n_co, n_scr, 3):
            parts.append(refs[pos:pos + n])
            pos += n
        a_in, c_in, a_out, c_out, a_scr, sems = parts
        if grid:
            step = 0
            for ax, g in enumerate(grid):
                step = step * g + pl.program_id(ax)
            total = functools.reduce(lambda p, q: p * q, grid)
            pl.when(step == 0)(lambda: comm.start(c_in, c_out, sems))
            body(*a_in, *a_out, *a_scr)
            pl.when(step == (3 * total) // 4)(lambda: comm.forward(c_in, c_out, sems))
            pl.when(step == total - 1)(lambda: comm.finish(c_in, c_out, sems))
        else:
            comm.start(c_in, c_out, sems)
            body(*a_in, *a_out, *a_scr)
            comm.forward(c_in, c_out, sems)
            comm.finish(c_in, c_out, sems)

    hbm = pl.BlockSpec(memory_space=pl.ANY)
    fn = _pallas(wrapped, True, in_specs=in_specs + [hbm] * n_ci, out_specs=list(out_specs) + [hbm] * n_co,
                 out_shape=list(out_shape) + list(comm.out_shapes), scratch_shapes=scratch + comm.semaphores(), **kw)

    def run(*operands):
        outs = fn(*operands, *comm.inputs)
        res = outs[:n_out]
        return (res[0] if single else res), outs[n_out:]

    return run


class _Comm:
    def __init__(self):
        self.inputs, self.sel, self.kinds, self.out_shapes = [], [], [], []

    def gather(self, arr, sel=None):
        block = arr.shape if sel is None else arr.shape[1:]
        self.inputs.append(arr)
        self.sel.append(sel)
        self.kinds.append("gather")
        self.out_shapes.append(jax.ShapeDtypeStruct((N_DEV,) + tuple(block), arr.dtype))
        return len(self.inputs) - 1

    def scatter(self, arr):
        self.inputs.append(arr)
        self.sel.append(None)
        self.kinds.append("scatter")
        self.out_shapes.append(jax.ShapeDtypeStruct(arr.shape, arr.dtype))
        return len(self.inputs) - 1

    def semaphores(self):
        n = len(self.inputs)
        return [pltpu.SemaphoreType.DMA((n, N_DEV - 1)), pltpu.SemaphoreType.DMA((n, N_DEV - 1)),
                pltpu.SemaphoreType.DMA((n,))]

    def _copies(self, ins, outs, sems, with_passed=True):
        send_sems, recv_sems, local_sems = sems
        x, y, c = lax.axis_index("x"), lax.axis_index("y"), lax.axis_index("c")
        me = 4 * x + 2 * y + c
        sibling = (x, y, 1 - c)
        chips = [(1 - x, y), (x, 1 - y), (1 - x, 1 - y)]
        items = []
        for a, kind in enumerate(self.kinds):
            def remote(src, dst, k, to, a=a):
                return pltpu.make_async_remote_copy(
                    src_ref=src, dst_ref=dst, send_sem=send_sems.at[a, k], recv_sem=recv_sems.at[a, k],
                    device_id=to, device_id_type=pl.DeviceIdType.MESH)
            if kind == "gather":
                src = ins[a] if self.sel[a] is None else ins[a].at[self.sel[a]]
                mine = outs[a].at[me]
                local = pltpu.make_async_copy(src, mine, local_sems.at[a])
                first = [remote(src, mine, 0, sibling)]
                first += [remote(src, mine, 1 + j, (*chip, c)) for j, chip in enumerate(chips)]
                passed = []
                for j, chip in enumerate(chips if with_passed else []):
                    got = outs[a].at[4 * chip[0] + 2 * chip[1] + c]
                    passed.append(remote(got, got, 4 + j, sibling))
            else:
                local = pltpu.make_async_copy(ins[a].at[me], outs[a].at[me], local_sems.at[a])
                first, passed = [], []
                for k in (1, 4, 2, 6, 5, 3, 7):
                    peer = ((1 - x) if k & 4 else x, (1 - y) if k & 2 else y, (1 - c) if k & 1 else c)
                    pid = 4 * peer[0] + 2 * peer[1] + peer[2]
                    first.append(remote(ins[a].at[pid], outs[a].at[me], k - 1, peer))
            items.append((local, first, passed))
        return items

    def start(self, ins, outs, sems):
        for local, first, _ in self._copies(ins, outs, sems, with_passed=False):
            local.start()
            for cp in first:
                cp.start()

    def forward(self, ins, outs, sems):
        for _, first, passed in self._copies(ins, outs, sems):
            for j, cp in enumerate(passed):
                first[1 + j].wait_recv()
                cp.start()

    def finish(self, ins, outs, sems):
        for local, first, passed in self._copies(ins, outs, sems):
            if passed:
                first[0].wait_recv()
                for cp in passed:
                    cp.wait_recv()
                for cp in first + passed:
                    cp.wait_send()
            else:
                for cp in first:
                    cp.wait()
            local.wait()


def _exchange(name, comm):
    _, outs = _call(lambda: None, comm=comm, name=name, in_specs=[], out_specs=[], out_shape=[])()
    return outs


def _sigmoid(x):
    return 0.5 * jnp.tanh(0.5 * x) + 0.5


def _tile(t):
    return min(TOKEN_TILE, t)


def _load_weights(pairs, sems):
    @pl.when(pl.program_id(0) == 0)
    def _():
        copies = [pltpu.make_async_copy(src, dst, sems.at[n]) for n, (src, dst) in enumerate(pairs)]
        for cp in copies:
            cp.start()
        for cp in copies:
            cp.wait()


def _ffn_fwd(name, x, gamma, win_t, wout, pre=None, head=None, comm=None):
    t, d = x.shape
    f = wout.shape[0]
    fc = f // HIDDEN_SPLIT
    tm = min(FFN_TILE, t)
    extra = (() if pre is None else tuple(pre)) + (() if head is None else tuple(head))

    def body(*refs):
        x_ref, gam_ref, win_hbm, wo_hbm = refs[:4]
        rest = list(refs[4:])
        a_ref, wp_ref = (rest.pop(0), rest.pop(0)) if pre is not None else (None, None)
        gf_ref, tg_ref = (rest.pop(0), rest.pop(0)) if head is not None else (None, None)
        xin_ref = rest.pop(0) if pre is not None else None
        if head is None:
            xo_ref = rest.pop(0)
        else:
            loss_ref, dx_ref, dgf_ref = rest.pop(0), rest.pop(0), rest.pop(0)
        xn_ref, gu_ref, win_v, wo_v, h_s, sems = rest
        _load_weights([(win_hbm, win_v), (wo_hbm, wo_v)], sems)
        xv = x_ref[...]
        if pre is not None:
            xv = xv + jnp.dot(a_ref[...], wp_ref[...], preferred_element_type=F32)
            xin_ref[...] = xv
        r = lax.rsqrt(jnp.mean(xv * xv, axis=-1, keepdims=True) + EPS)
        xn = (xv * r * gam_ref[...]).astype(BF16)
        xn_ref[...] = xn
        for c in range(HIDDEN_SPLIT):
            lo = c * fc
            g = lax.dot_general(xn, win_v[lo:lo + fc, :], NT, preferred_element_type=F32)
            u = lax.dot_general(xn, win_v[f + lo:f + lo + fc, :], NT, preferred_element_type=F32)
            gu_ref[:, lo:lo + fc] = g.astype(BF16)
            gu_ref[:, f + lo:f + lo + fc] = u.astype(BF16)
            h_s[:, lo:lo + fc] = (g * _sigmoid(g) * u).astype(BF16)
        xo = xv + 0.5 * jnp.dot(h_s[...], wo_v[...], preferred_element_type=F32)
        if head is None:
            xo_ref[...] = xo
        else:
            @pl.when(pl.program_id(0) == 0)
            def _():
                loss_ref[...] = jnp.zeros_like(loss_ref)
                dgf_ref[...] = jnp.zeros_like(dgf_ref)

            ro = lax.rsqrt(jnp.mean(xo * xo, axis=-1, keepdims=True) + EPS)
            yv = xo * ro
            err = yv * gf_ref[...] - tg_ref[...]
            loss_ref[...] += (0.5 / d) * jnp.sum(jnp.sum(err * err, axis=-1, keepdims=True), axis=0, keepdims=True)
            dout = err * (1.0 / d)
            dgf_ref[...] += jnp.sum(dout * yv, axis=0, keepdims=True)
            dyn = dout * gf_ref[...]
            dx_ref[...] = ro * (dyn - yv * jnp.mean(dyn * yv, axis=-1, keepdims=True))

    hbm = pl.BlockSpec(memory_space=pl.ANY)
    row = pl.BlockSpec((tm, d), lambda i: (i, 0))
    one = lambda shape: pl.BlockSpec(shape, lambda i: (0, 0))
    in_specs = [row, one((1, d)), hbm, hbm]
    out_specs, out_shape = [], []
    if pre is not None:
        in_specs += [pl.BlockSpec((tm, pre[0].shape[1]), lambda i: (i, 0)), one(pre[1].shape)]
        out_specs.append(row)
        out_shape.append(jax.ShapeDtypeStruct((t, d), F32))
    if head is None:
        out_specs.append(row)
        out_shape.append(jax.ShapeDtypeStruct((t, d), F32))
    else:
        in_specs += [one((1, d)), row]
        out_specs += [one((1, 1)), row, one((1, d))]
        out_shape += [jax.ShapeDtypeStruct((1, 1), F32), jax.ShapeDtypeStruct((t, d), F32),
                      jax.ShapeDtypeStruct((1, d), F32)]
    out_specs += [row, pl.BlockSpec((tm, 2 * f), lambda i: (i, 0))]
    out_shape += [jax.ShapeDtypeStruct((t, d), BF16), jax.ShapeDtypeStruct((t, 2 * f), BF16)]
    return _call(
        body, comm=comm, name=name, grid=(t // tm,),
        in_specs=in_specs, out_specs=out_specs, out_shape=out_shape,
        scratch_shapes=[pltpu.VMEM((2 * f, d), BF16), pltpu.VMEM((f, d), BF16), pltpu.VMEM((tm, f), BF16),
                        pltpu.SemaphoreType.DMA((2,))],
    )(x, gamma, win_t, wout, *extra)


def _ffn_bwd(name, dy, x, gamma, gu, win_t, wout, comm=None):
    t, d = x.shape
    f = wout.shape[0]
    fc = f // HIDDEN_SPLIT
    tm = min(FFN_TILE, t)

    def body(dy_ref, x_ref, gam_ref, gu_ref, win_hbm, wo_hbm,
             dx_ref, dgam_ref, dyh_ref, h_ref, dgu_ref, win_v, wo_v, sems):
        _load_weights([(win_hbm, win_v), (wo_hbm, wo_v)], sems)

        @pl.when(pl.program_id(0) == 0)
        def _():
            dgam_ref[...] = jnp.zeros_like(dgam_ref)

        dyb = (0.5 * dy_ref[...]).astype(BF16)
        dyh_ref[...] = dyb
        for c in range(HIDDEN_SPLIT):
            lo = c * fc
            dh = lax.dot_general(dyb, wo_v[lo:lo + fc, :], NT, preferred_element_type=F32)
            g = gu_ref[:, lo:lo + fc].astype(F32)
            u = gu_ref[:, f + lo:f + lo + fc].astype(F32)
            sig = _sigmoid(g)
            silu = g * sig
            h_ref[:, lo:lo + fc] = (silu * u).astype(BF16)
            dgu_ref[:, lo:lo + fc] = (dh * u * (sig * (1.0 + g * (1.0 - sig)))).astype(BF16)
            dgu_ref[:, f + lo:f + lo + fc] = (dh * silu).astype(BF16)
        dxn = jnp.dot(dgu_ref[...], win_v[...], preferred_element_type=F32)
        xv = x_ref[...]
        r = lax.rsqrt(jnp.mean(xv * xv, axis=-1, keepdims=True) + EPS)
        yv = xv * r
        dgam_ref[...] += jnp.sum(dxn * yv, axis=0, keepdims=True)
        dyn = dxn * gam_ref[...]
        dx_ref[...] = dy_ref[...] + r * (dyn - yv * jnp.mean(dyn * yv, axis=-1, keepdims=True))

    hbm = pl.BlockSpec(memory_space=pl.ANY)
    row = lambda width: pl.BlockSpec((tm, width), lambda i: (i, 0))
    return _call(
        body, comm=comm, name=name, grid=(t // tm,),
        in_specs=[row(d), row(d), pl.BlockSpec((1, d), lambda i: (0, 0)), row(2 * f), hbm, hbm],
        out_specs=[row(d), pl.BlockSpec((1, d), lambda i: (0, 0)), row(d), row(f), row(2 * f)],
        out_shape=[
            jax.ShapeDtypeStruct((t, d), F32),
            jax.ShapeDtypeStruct((1, d), F32),
            jax.ShapeDtypeStruct((t, d), BF16),
            jax.ShapeDtypeStruct((t, f), BF16),
            jax.ShapeDtypeStruct((t, 2 * f), BF16),
        ],
        scratch_shapes=[pltpu.VMEM((2 * f, d), BF16), pltpu.VMEM((f, d), BF16), pltpu.SemaphoreType.DMA((2,))],
    )(dy, x, gamma, gu, win_t, wout)


def _ffn_bwd_hidden(name, dy, gu, wout):
    t, d = dy.shape
    f = wout.shape[0]
    fc = f // HIDDEN_SPLIT
    tm = min(FFN_TILE, t)

    def body(dy_ref, gu_ref, wo_hbm, dyh_ref, h_ref, dgu_ref, wo_v, sems):
        _load_weights([(wo_hbm, wo_v)], sems)
        dyb = (0.5 * dy_ref[...]).astype(BF16)
        dyh_ref[...] = dyb
        for c in range(HIDDEN_SPLIT):
            lo = c * fc
            dh = lax.dot_general(dyb, wo_v[lo:lo + fc, :], NT, preferred_element_type=F32)
            g = gu_ref[:, lo:lo + fc].astype(F32)
            u = gu_ref[:, f + lo:f + lo + fc].astype(F32)
            sig = _sigmoid(g)
            silu = g * sig
            h_ref[:, lo:lo + fc] = (silu * u).astype(BF16)
            dgu_ref[:, lo:lo + fc] = (dh * u * (sig * (1.0 + g * (1.0 - sig)))).astype(BF16)
            dgu_ref[:, f + lo:f + lo + fc] = (dh * silu).astype(BF16)

    row = lambda width: pl.BlockSpec((tm, width), lambda i: (i, 0))
    return _call(
        body, name=name, grid=(t // tm,),
        in_specs=[row(d), row(2 * f), pl.BlockSpec(memory_space=pl.ANY)],
        out_specs=[row(d), row(f), row(2 * f)],
        out_shape=[jax.ShapeDtypeStruct((t, d), BF16), jax.ShapeDtypeStruct((t, f), BF16),
                   jax.ShapeDtypeStruct((t, 2 * f), BF16)],
        scratch_shapes=[pltpu.VMEM((f, d), BF16), pltpu.SemaphoreType.DMA((1,))],
    )(dy, gu, wout)


def _matmul_tn(name, a, b, out_dtype=BF16, comm=None, a_split=None):
    a_b = a.ndim == 3
    b_b = b.ndim == 3
    nb = a_split if a_split else a.shape[0] if a_b else b.shape[0] if b_b else 1
    t, m = a.shape[-2:]
    if a_split:
        m = m // a_split
    n = b.shape[-1]
    tk = min(CONTRACT_TILE, t)
    nt = t // tk

    def body(a_ref, b_ref, o_ref, acc):
        s = pl.program_id(1)

        @pl.when(s == 0)
        def _():
            acc[...] = jnp.zeros_like(acc)

        acc[...] += lax.dot_general(a_ref[...], b_ref[...], TN, preferred_element_type=F32)

        @pl.when(s == nt - 1)
        def _():
            o_ref[...] = acc[...].astype(o_ref.dtype)

    a_spec = (pl.BlockSpec((None, tk, m), lambda j, s: (j, s, 0)) if a_b
              else pl.BlockSpec((tk, m), lambda j, s: (s, j)) if a_split
              else pl.BlockSpec((tk, m), lambda j, s: (s, 0)))
    b_spec = (pl.BlockSpec((None, tk, n), lambda j, s: (j, s, 0)) if b_b
              else pl.BlockSpec((tk, n), lambda j, s: (s, 0)))
    return _call(
        body, comm=comm, name=name, grid=(nb, nt),
        in_specs=[a_spec, b_spec],
        out_specs=pl.BlockSpec((None, m, n), lambda j, s: (j, 0, 0)),
        out_shape=jax.ShapeDtypeStruct((nb, m, n), out_dtype),
        scratch_shapes=[pltpu.VMEM((m, n), F32)],
    )(a, b)


def _matmul_nt(name, dy, w):
    t, n = dy.shape
    kdim = w.shape[0]
    tm = _tile(t)

    def body(dy_ref, w_ref, da_ref, dyb_ref):
        dyb = dy_ref[...].astype(BF16)
        dyb_ref[...] = dyb
        da_ref[...] = lax.dot_general(dyb, w_ref[...], NT, preferred_element_type=F32)

    return _call(
        body, name=name, grid=(t // tm,),
        in_specs=[pl.BlockSpec((tm, n), lambda i: (i, 0)),
                  pl.BlockSpec((kdim, n), lambda i: (0, 0))],
        out_specs=[pl.BlockSpec((tm, kdim), lambda i: (i, 0)),
                   pl.BlockSpec((tm, n), lambda i: (i, 0))],
        out_shape=[jax.ShapeDtypeStruct((t, kdim), F32), jax.ShapeDtypeStruct((t, n), BF16)],
    )(dy, w)


def _matmul_nt_rms_bwd(name, dz, w, dres, x, gamma, w_rows_are_k=False, tile=TOKEN_TILE, comm=None):
    t, kdim = dz.shape
    d = x.shape[1]
    tm = min(tile, t)

    def body(dz_ref, w_ref, dres_ref, x_ref, gam_ref, dx_ref, dgam_ref):
        i = pl.program_id(0)

        @pl.when(i == 0)
        def _():
            dgam_ref[...] = jnp.zeros_like(dgam_ref)

        if w_rows_are_k:
            dxn = jnp.dot(dz_ref[...], w_ref[...], preferred_element_type=F32)
        else:
            dxn = lax.dot_general(dz_ref[...], w_ref[...], NT, preferred_element_type=F32)
        xv = x_ref[...]
        r = lax.rsqrt(jnp.mean(xv * xv, axis=-1, keepdims=True) + EPS)
        yv = xv * r
        dgam_ref[...] += jnp.sum(dxn * yv, axis=0, keepdims=True)
        dyn = dxn * gam_ref[...]
        dx_ref[...] = dres_ref[...] + r * (dyn - yv * jnp.mean(dyn * yv, axis=-1, keepdims=True))

    return _call(
        body, comm=comm, name=name, grid=(t // tm,),
        in_specs=[pl.BlockSpec((tm, kdim), lambda i: (i, 0)),
                  pl.BlockSpec(w.shape, lambda i: (0, 0)),
                  pl.BlockSpec((tm, d), lambda i: (i, 0)),
                  pl.BlockSpec((tm, d), lambda i: (i, 0)),
                  pl.BlockSpec((1, d), lambda i: (0, 0))],
        out_specs=[pl.BlockSpec((tm, d), lambda i: (i, 0)),
                   pl.BlockSpec((1, d), lambda i: (0, 0))],
        out_shape=[jax.ShapeDtypeStruct((t, d), F32), jax.ShapeDtypeStruct((1, d), F32)],
    )(dz, w, dres, x, gamma)


def _pool_means(uext_ref, pos, tm, g, win):
    cols = slice(g * GROUP, (g + 1) * GROUP)
    acc = uext_ref[pl.ds(HALO, tm), cols]
    for j in range(1, win):
        acc = acc + uext_ref[pl.ds(HALO - j, tm), cols]
    cnt = jnp.minimum(pos + 1, win).astype(F32)
    return acc / cnt - uext_ref[pl.ds(HALO, tm), cols]


def _shifted_copies(src_ref, dst_ref, rows):
    for b in range(SUBLANES):
        dst_ref[b, pl.ds(0, rows), :] = src_ref[pl.ds(b, rows), :]


def _tap_sum(sh_ref, cw_ref, offsets, out_ref, n_rows, bias_ref=None):
    width = out_ref.shape[-1]

    def chunk(c, carry):
        r0 = pl.multiple_of(c * ROW_CHUNK, ROW_CHUNK)
        acc = (jnp.zeros((ROW_CHUNK, width), F32) if bias_ref is None
               else jnp.broadcast_to(bias_ref[...], (ROW_CHUNK, width)))
        for k, off in enumerate(offsets):
            a, b = divmod(off, SUBLANES)
            acc = acc + cw_ref[k:k + 1, :] * sh_ref[b, pl.ds(r0 + SUBLANES * a, ROW_CHUNK), :]
        out_ref[pl.ds(r0, ROW_CHUNK), :] = acc
        return carry

    lax.fori_loop(0, n_rows // ROW_CHUNK, chunk, 0)


def _pool_conv_fwd(name, x, gamma, w, pool_w, pool_b, pool_scale, conv_w, conv_b, ln_g, ln_b):
    t, d = x.shape
    hw = w.shape[1]
    pc = len(POOL_WINDOWS) * GROUP
    cc = (hw - pc) // 2
    tm = _tile(t)

    def body(x_ref, gam_ref, w_ref, pw_ref, pb_ref, ps_ref, cw_ref, cb_ref, lg_ref, lb_ref,
             xn_ref, h_ref, cat_ref, uext, gext, gsh, y_s, hp_s):
        i = pl.program_id(0)

        @pl.when(i == 0)
        def _():
            hp_s[...] = jnp.zeros_like(hp_s)

        xv = x_ref[...]
        r = lax.rsqrt(jnp.mean(xv * xv, axis=-1, keepdims=True) + EPS)
        xn = (xv * r * gam_ref[...]).astype(BF16)
        xn_ref[...] = xn
        h_ref[...] = jnp.dot(xn, w_ref[...], preferred_element_type=F32)
        hp = hp_s[...]
        uext[0:HALO, :] = hp[:, :pc]
        uext[HALO:, :] = h_ref[:, :pc]
        gext[0:HALO, :] = hp[:, pc:pc + cc] * _sigmoid(hp[:, pc + cc:])
        gext[pl.ds(HALO, tm), :] = h_ref[:, pc:pc + cc] * _sigmoid(h_ref[:, pc + cc:])
        gext[pl.ds(HALO + tm, SUBLANES), :] = jnp.zeros((SUBLANES, cc), F32)
        pos = i * tm + lax.broadcasted_iota(jnp.int32, (tm, 1), 0)
        for g, win in enumerate(POOL_WINDOWS):
            cols = slice(g * GROUP, (g + 1) * GROUP)
            pooled = _pool_means(uext, pos, tm, g, win)
            mixed = jnp.dot(pooled.astype(BF16), pw_ref[g].astype(BF16),
                            preferred_element_type=F32) + pb_ref[g:g + 1, :]
            cat_ref[:, cols] = (mixed * ps_ref[:, cols]).astype(BF16)
        _shifted_copies(gext, gsh, HALO + tm)
        _tap_sum(gsh, cw_ref, [HALO - (CONV_WIDTH - 1) + k for k in range(CONV_WIDTH)], y_s, tm, cb_ref)
        y = y_s[...]
        mu = jnp.mean(y, axis=-1, keepdims=True)
        dv = y - mu
        rstd = lax.rsqrt(jnp.mean(dv * dv, axis=-1, keepdims=True) + EPS)
        ln = dv * rstd * lg_ref[...] + lb_ref[...]
        cat_ref[:, pc:] = (ln * _sigmoid(ln)).astype(BF16)
        hp_s[...] = h_ref[pl.ds(tm - HALO, HALO), :]

    small = lambda a: pl.BlockSpec(a.shape, lambda i: (0,) * a.ndim)
    return _call(
        body, name=name, grid=(t // tm,),
        in_specs=[pl.BlockSpec((tm, d), lambda i: (i, 0)), small(gamma), small(w),
                  small(pool_w), small(pool_b), small(pool_scale), small(conv_w), small(conv_b),
                  small(ln_g), small(ln_b)],
        out_specs=[pl.BlockSpec((tm, d), lambda i: (i, 0)), pl.BlockSpec((tm, hw), lambda i: (i, 0)),
                   pl.BlockSpec((tm, pc + cc), lambda i: (i, 0))],
        out_shape=[jax.ShapeDtypeStruct((t, d), BF16), jax.ShapeDtypeStruct((t, hw), F32),
                   jax.ShapeDtypeStruct((t, pc + cc), BF16)],
        scratch_shapes=[pltpu.VMEM((HALO + tm, pc), F32), pltpu.VMEM((HALO + tm + SUBLANES, cc), F32),
                        pltpu.VMEM((SUBLANES, HALO + tm, cc), F32), pltpu.VMEM((tm, cc), F32),
                        pltpu.VMEM((HALO, hw), F32)],
    )(x, gamma, w, pool_w, pool_b, pool_scale, conv_w, conv_b, ln_g, ln_b)


def _pool_conv_bwd(name, h, dcat, pool_w, pool_b, pool_scale, conv_w, conv_b, ln_g, ln_b, comm=None):
    t, hw = h.shape
    pc = len(POOL_WINDOWS) * GROUP
    cc = (hw - pc) // 2
    ng = len(POOL_WINDOWS)
    tm = _tile(t)
    per = tm // HALO
    nt = t // tm
    r2 = tm + HALO
    taps = CONV_WIDTH - 1

    def body(h_ref, hp_ref, hn_ref, dc_ref, dcn_ref, pw_ref, pb_ref, ps_ref, cw_ref, cb_ref, lg_ref, lb_ref,
             dh_ref, dpw_ref, dpb_ref, dps_ref, dcw_ref, dcb_ref, dlg_ref, dlb_ref,
             uext, gext, dcext, dqext, dycext, shifted, y_s, dg_s, dcw_acc):
        i = pl.program_id(0)

        @pl.when(i == 0)
        def _():
            for ref in (dpw_ref, dpb_ref, dps_ref, dcw_ref, dcb_ref, dlg_ref, dlb_ref, dcw_acc):
                ref[...] = jnp.zeros_like(ref)

        keep_p = (i > 0).astype(F32)
        keep_n = (i < nt - 1).astype(F32)
        hp = hp_ref[...] * keep_p
        hn = hn_ref[...] * keep_n
        uext[0:HALO, :] = hp[:, :pc]
        uext[HALO:, :] = h_ref[:, :pc]
        gext[0:HALO, :] = hp[:, pc:pc + cc] * _sigmoid(hp[:, pc + cc:])
        gext[pl.ds(HALO, tm), :] = h_ref[:, pc:pc + cc] * _sigmoid(h_ref[:, pc + cc:])
        gext[pl.ds(HALO + tm, HALO), :] = hn[:, pc:pc + cc] * _sigmoid(hn[:, pc + cc:])
        gext[pl.ds(HALO + tm + HALO, SUBLANES), :] = jnp.zeros((SUBLANES, cc), F32)
        dcext[0:tm, :] = dc_ref[...]
        dcext[pl.ds(tm, HALO), :] = dcn_ref[...] * keep_n

        pos = i * tm + lax.broadcasted_iota(jnp.int32, (tm, 1), 0)
        pos2 = i * tm + lax.broadcasted_iota(jnp.int32, (r2, 1), 0)
        for g, win in enumerate(POOL_WINDOWS):
            cols = slice(g * GROUP, (g + 1) * GROUP)
            wg = pw_ref[g].astype(BF16)
            dya = dcext[:, cols]
            dmixed = dya * ps_ref[:, cols]
            dpooled = lax.dot_general(dmixed.astype(BF16), wg, NT, preferred_element_type=F32)
            cnt2 = jnp.minimum(pos2 + 1, win).astype(F32)
            dqext[:, cols] = dpooled / cnt2
            du = -dpooled[0:tm]
            for j in range(win):
                du = du + dqext[pl.ds(j, tm), cols]
            dh_ref[:, cols] = du.astype(BF16)
            pooled = _pool_means(uext, pos, tm, g, win)
            pooled_b = pooled.astype(BF16)
            mixed = jnp.dot(pooled_b, wg, preferred_element_type=F32) + pb_ref[g:g + 1, :]
            dps_ref[:, cols] += jnp.sum(dya[0:tm] * mixed, axis=0, keepdims=True)
            dpb_ref[g:g + 1, :] += jnp.sum(dmixed[0:tm], axis=0, keepdims=True)
            dpw_ref[g] += lax.dot_general(pooled_b, dmixed[0:tm].astype(BF16), TN, preferred_element_type=F32)

        _shifted_copies(gext, shifted, HALO + tm + HALO)
        _tap_sum(shifted, cw_ref, [HALO - taps + k for k in range(CONV_WIDTH)], y_s, r2, cb_ref)
        y = y_s[...]
        mu = jnp.mean(y, axis=-1, keepdims=True)
        dv = y - mu
        rstd = lax.rsqrt(jnp.mean(dv * dv, axis=-1, keepdims=True) + EPS)
        norm = dv * rstd
        ln = norm * lg_ref[...] + lb_ref[...]
        sig = _sigmoid(ln)
        dln = dcext[:, pc:] * (sig * (1.0 + ln * (1.0 - sig)))
        dnorm = dln * lg_ref[...]
        dyc = rstd * (dnorm - jnp.mean(dnorm, axis=-1, keepdims=True)
                      - norm * jnp.mean(dnorm * norm, axis=-1, keepdims=True))
        dycext[pl.ds(0, r2), :] = dyc
        dycext[pl.ds(r2, SUBLANES), :] = jnp.zeros((SUBLANES, cc), F32)
        dlg_ref[...] += jnp.sum((dln * norm)[0:tm], axis=0, keepdims=True)
        dlb_ref[...] += jnp.sum(dln[0:tm], axis=0, keepdims=True)
        dcb_ref[...] += jnp.sum(dyc[0:tm], axis=0, keepdims=True)

        def fold(c, carry):
            r0 = pl.multiple_of(c * ROW_CHUNK, ROW_CHUNK)
            dchunk = dycext[pl.ds(r0, ROW_CHUNK), :]
            for k in range(CONV_WIDTH):
                a8, b8 = divmod(HALO - taps + k, SUBLANES)
                prod = dchunk * shifted[b8, pl.ds(r0 + SUBLANES * a8, ROW_CHUNK), :]
                part = prod[0:SUBLANES]
                for q in range(1, ROW_CHUNK // SUBLANES):
                    part = part + prod[q * SUBLANES:(q + 1) * SUBLANES]
                dcw_acc[k] += part
            return carry

        lax.fori_loop(0, tm // ROW_CHUNK, fold, 0)

        @pl.when(i == nt - 1)
        def _():
            for k in range(CONV_WIDTH):
                dcw_ref[k:k + 1, :] = jnp.sum(dcw_acc[k], axis=0, keepdims=True)

        _shifted_copies(dycext, shifted, r2)
        _tap_sum(shifted, cw_ref, [taps - k for k in range(CONV_WIDTH)], dg_s, tm)
        dg = dg_s[...]
        a = h_ref[:, pc:pc + cc]
        sg = _sigmoid(h_ref[:, pc + cc:])
        dh_ref[:, pc:pc + cc] = (dg * sg).astype(BF16)
        dh_ref[:, pc + cc:] = (dg * a * sg * (1.0 - sg)).astype(BF16)

    small = lambda a: pl.BlockSpec(a.shape, lambda i: (0,) * a.ndim)
    smalls = (pool_w, pool_b, pool_scale, conv_w, conv_b, ln_g, ln_b)
    return _call(
        body, comm=comm, name=name, grid=(nt,),
        in_specs=[pl.BlockSpec((tm, hw), lambda i: (i, 0)),
                  pl.BlockSpec((HALO, hw), lambda i: (jnp.maximum(i * per - 1, 0), 0)),
                  pl.BlockSpec((HALO, hw), lambda i: (jnp.minimum((i + 1) * per, t // HALO - 1), 0)),
                  pl.BlockSpec((tm, pc + cc), lambda i: (i, 0)),
                  pl.BlockSpec((HALO, pc + cc), lambda i: (jnp.minimum((i + 1) * per, t // HALO - 1), 0)),
                  ] + [small(a) for a in smalls],
        out_specs=[pl.BlockSpec((tm, hw), lambda i: (i, 0))] + [small(a) for a in smalls],
        out_shape=[jax.ShapeDtypeStruct((t, hw), BF16)] + [jax.ShapeDtypeStruct(a.shape, F32) for a in smalls],
        scratch_shapes=[pltpu.VMEM((HALO + tm, pc), F32), pltpu.VMEM((HALO + tm + HALO + SUBLANES, cc), F32),
                        pltpu.VMEM((r2, pc + cc), F32), pltpu.VMEM((r2, pc), F32),
                        pltpu.VMEM((r2 + SUBLANES, cc), F32), pltpu.VMEM((SUBLANES, HALO + tm + HALO, cc), F32),
                        pltpu.VMEM((r2, cc), F32), pltpu.VMEM((tm, cc), F32),
                        pltpu.VMEM((CONV_WIDTH + 1, SUBLANES, cc), F32)],
    )(h, h, h, dcat, dcat, *smalls)


SQRT_HALF = 0.7071067811865476
INV_SQRT_2PI = 0.3989422804014327


def _sgu_core(zp_ref, lg_ref, lb_ref, ws_ref, bs_ref, vo_s, tm, sc, heads):
    zp = zp_ref[...]
    cdf = 0.5 * (1.0 + lax.erf(zp * SQRT_HALF))
    z = zp * cdf
    u = z[:, :sc]
    v = z[:, sc:]
    mu = jnp.mean(v, axis=-1, keepdims=True)
    dv = v - mu
    rstd = lax.rsqrt(jnp.mean(dv * dv, axis=-1, keepdims=True) + EPS)
    norm = dv * rstd
    vb = (norm * lg_ref[...] + lb_ref[...]).astype(BF16)
    row = lax.broadcasted_iota(jnp.int32, (GROUP, GROUP), 0)
    col = lax.broadcasted_iota(jnp.int32, (GROUP, GROUP), 1)
    mask = (col <= row).astype(F32)
    wm = [ws_ref[hd] * mask for hd in range(heads)]
    for hd in range(heads):
        cols = slice(hd * GROUP, (hd + 1) * GROUP)
        wb = wm[hd].astype(BF16)
        for n in range(tm // GROUP):
            rows = slice(n * GROUP, (n + 1) * GROUP)
            vo_s[rows, cols] = jnp.dot(wb, vb[rows, cols], preferred_element_type=F32) + bs_ref[hd]
    return (zp, cdf), u, norm, rstd, vb, wm, mask


def _sgu_fwd(name, x, gamma, w, ln_g, ln_b, w_s, b_s):
    t, d = x.shape
    two_sc = w.shape[1]
    sc = two_sc // 2
    heads = sc // GROUP
    tm = _tile(t)

    def body(x_ref, gam_ref, w_ref, lg_ref, lb_ref, ws_ref, bs_ref, xn_ref, zp_ref, q_ref, vo_s):
        xv = x_ref[...]
        r = lax.rsqrt(jnp.mean(xv * xv, axis=-1, keepdims=True) + EPS)
        xn = (xv * r * gam_ref[...]).astype(BF16)
        xn_ref[...] = xn
        zp_ref[...] = jnp.dot(xn, w_ref[...], preferred_element_type=F32)
        _, u, _, _, _, _, _ = _sgu_core(zp_ref, lg_ref, lb_ref, ws_ref, bs_ref, vo_s, tm, sc, heads)
        q_ref[...] = (u * vo_s[...]).astype(BF16)

    small = lambda a: pl.BlockSpec(a.shape, lambda i: (0,) * a.ndim)
    return _call(
        body, name=name, grid=(t // tm,),
        in_specs=[pl.BlockSpec((tm, d), lambda i: (i, 0)), small(gamma), small(w),
                  small(ln_g), small(ln_b), small(w_s), small(b_s)],
        out_specs=[pl.BlockSpec((tm, d), lambda i: (i, 0)), pl.BlockSpec((tm, two_sc), lambda i: (i, 0)),
                   pl.BlockSpec((tm, sc), lambda i: (i, 0))],
        out_shape=[jax.ShapeDtypeStruct((t, d), BF16), jax.ShapeDtypeStruct((t, two_sc), F32),
                   jax.ShapeDtypeStruct((t, sc), BF16)],
        scratch_shapes=[pltpu.VMEM((tm, sc), F32)],
    )(x, gamma, w, ln_g, ln_b, w_s, b_s)


def _sgu_bwd(name, zp, dy, w_out, ln_g, ln_b, w_s, b_s):
    t, two_sc = zp.shape
    d = dy.shape[1]
    sc = two_sc // 2
    heads = sc // GROUP
    tm = _tile(t)
    nt = t // tm

    def body(zp_ref, dy_ref, wo_ref, lg_ref, lb_ref, ws_ref, bs_ref,
             dzp_ref, dyb_ref, dlg_ref, dlb_ref, dws_ref, dbs_ref, vo_s, dvl_s, dws_acc):
        i = pl.program_id(0)

        @pl.when(i == 0)
        def _():
            dlg_ref[...] = jnp.zeros_like(dlg_ref)
            dlb_ref[...] = jnp.zeros_like(dlb_ref)
            dbs_ref[...] = jnp.zeros_like(dbs_ref)
            dws_acc[...] = jnp.zeros_like(dws_acc)

        dyb = dy_ref[...].astype(BF16)
        dyb_ref[...] = dyb
        dq = lax.dot_general(dyb, wo_ref[...], NT, preferred_element_type=F32)
        (zp, cdf), u, norm, rstd, vb, wm, mask = _sgu_core(zp_ref, lg_ref, lb_ref, ws_ref, bs_ref, vo_s, tm, sc, heads)
        du = dq * vo_s[...]
        dvo = dq * u
        dvob = dvo.astype(BF16)
        for hd in range(heads):
            cols = slice(hd * GROUP, (hd + 1) * GROUP)
            wtb = jnp.transpose(wm[hd]).astype(BF16)
            for n in range(tm // GROUP):
                rows = slice(n * GROUP, (n + 1) * GROUP)
                blk = dvob[rows, cols]
                dws_acc[hd] += lax.dot_general(blk, vb[rows, cols], NT, preferred_element_type=F32)
                dvl_s[rows, cols] = jnp.dot(wtb, blk, preferred_element_type=F32)
                dbs_ref[hd] += jnp.sum(dvo[rows, cols], axis=-1, keepdims=True)
        dvl = dvl_s[...]
        dlg_ref[...] += jnp.sum(dvl * norm, axis=0, keepdims=True)
        dlb_ref[...] += jnp.sum(dvl, axis=0, keepdims=True)
        dnorm = dvl * lg_ref[...]
        dv = rstd * (dnorm - jnp.mean(dnorm, axis=-1, keepdims=True)
                     - norm * jnp.mean(dnorm * norm, axis=-1, keepdims=True))
        dgelu = cdf + zp * (INV_SQRT_2PI * jnp.exp(-0.5 * zp * zp))
        dzp_ref[:, :sc] = (du * dgelu[:, :sc]).astype(BF16)
        dzp_ref[:, sc:] = (dv * dgelu[:, sc:]).astype(BF16)

        @pl.when(i == nt - 1)
        def _():
            for hd in range(heads):
                dws_ref[hd] = dws_acc[hd] * mask

    small = lambda a: pl.BlockSpec(a.shape, lambda i: (0,) * a.ndim)
    smalls = (ln_g, ln_b, w_s, b_s)
    return _call(
        body, name=name, grid=(nt,),
        in_specs=[pl.BlockSpec((tm, two_sc), lambda i: (i, 0)), pl.BlockSpec((tm, d), lambda i: (i, 0)),
                  small(w_out)] + [small(a) for a in smalls],
        out_specs=[pl.BlockSpec((tm, two_sc), lambda i: (i, 0)), pl.BlockSpec((tm, d), lambda i: (i, 0))]
                  + [small(a) for a in smalls],
        out_shape=[jax.ShapeDtypeStruct((t, two_sc), BF16), jax.ShapeDtypeStruct((t, d), BF16)]
                  + [jax.ShapeDtypeStruct(a.shape, F32) for a in smalls],
        scratch_shapes=[pltpu.VMEM((tm, sc), F32), pltpu.VMEM((tm, sc), F32), pltpu.VMEM(w_s.shape, F32)],
    )(zp, dy, w_out, ln_g, ln_b, w_s, b_s)


def _adamw(name, parts, w, m, v, rows):
    l_n, r_n, c_n = w.shape
    s_n = parts[0].shape[0]
    tr = min(rows, r_n)
    nr = r_n // tr
    c1 = 1.0 - ADAM_B1 ** ADAM_STEP
    c2 = 1.0 - ADAM_B2 ** ADAM_STEP

    def body(*refs):
        p_refs = refs[:l_n]
        w_ref, m_ref, v_ref, g_ref, d_ref, mo_ref, vo_ref = refs[l_n:]
        layer = pl.program_id(0)

        def update(p_ref):
            g = p_ref[0].astype(F32)
            for s in range(1, s_n):
                g = g + p_ref[s].astype(F32)
            mn = ADAM_B1 * m_ref[...] + (1.0 - ADAM_B1) * g
            vn = ADAM_B2 * v_ref[...] + (1.0 - ADAM_B2) * (g * g)
            m_hat = mn / c1
            v_hat = vn / c2
            g_ref[...] = g
            d_ref[...] = -ADAM_LR * (m_hat / (jnp.sqrt(v_hat) + ADAM_EPS) + ADAM_WD * w_ref[...])
            mo_ref[...] = mn
            vo_ref[...] = vn

        for j in range(l_n):
            pl.when(layer == j)(functools.partial(update, p_refs[j]))

    def part_spec(j):
        return pl.BlockSpec((s_n, tr, c_n), lambda l, i: (0, jnp.where(l == j, i, jnp.where(l < j, 0, nr - 1)), 0))

    blk = pl.BlockSpec((None, tr, c_n), lambda l, i: (l, i, 0))
    return _call(
        body, name=name, grid=(l_n, nr),
        in_specs=[part_spec(j) for j in range(l_n)] + [blk, blk, blk],
        out_specs=[blk] * 4,
        out_shape=[jax.ShapeDtypeStruct((l_n, r_n, c_n), F32)] * 4,
    )(*parts, w, m, v)


def _local_step(x, target, big, small, sched=None):
    t, d = x.shape
    n_layers = small["ffn1_norm"].shape[0]
    gb = {}
    gs = {}

    def row(a, l):
        return a[l:l + 1]

    def run(fn, name, *operands, **kw):
        comm = sched.plan(name, gb, gs) if sched is not None else None
        if comm is None:
            return fn(name, *operands, **kw)
        res, got = fn(name, *operands, comm=comm, **kw)
        sched.deliver(comm, got)
        return res

    saved = []
    xs = x
    for l in range(n_layers):
        rec = {"x_ffn1": xs}
        xs, rec["xn_ffn1"], rec["gu_ffn1"] = run(
            _ffn_fwd, f"ffn1_fwd_l{l}", xs, row(small["ffn1_norm"], l), big["ffn1_w_in", l], big["ffn1_w_out", l])
        rec["x_mix"] = xs
        if l % 2 == 0:
            rec["xn_mix"], rec["h"], rec["cat"] = _pool_conv_fwd(
                f"pool_conv_fwd_l{l}", xs, row(small["mix_norm"], l), big["ab_w_in"], small["pool_w"],
                small["pool_b"], small["pool_scale"], small["conv_w"], small["conv_b"],
                small["conv_ln_g"], small["conv_ln_b"])
            pre = (rec["cat"], big["ab_w_out"])
        else:
            rec["xn_mix"], rec["zp"], rec["q"] = _sgu_fwd(
                f"sgu_fwd_l{l}", xs, row(small["mix_norm"], l), big["sgu_w_in"], small["sgu_ln_g"],
                small["sgu_ln_b"], small["sgu_w"], small["sgu_b"])
            pre = (rec["q"], big["sgu_w_out"])
        last = l == n_layers - 1
        res = run(_ffn_fwd, f"ffn2_fwd_l{l}", xs, row(small["ffn2_norm"], l), big["ffn2_w_in", l],
                  big["ffn2_w_out", l], pre=pre, head=(small["final_norm"], target) if last else None)
        if last:
            rec["x_ffn2"], loss, dx, gs["final_norm"], rec["xn_ffn2"], rec["gu_ffn2"] = res
        else:
            rec["x_ffn2"], xs, rec["xn_ffn2"], rec["gu_ffn2"] = res
        saved.append(rec)

    norm_rows = {"ffn1_norm": [None] * n_layers, "mix_norm": [None] * n_layers, "ffn2_norm": [None] * n_layers}

    def ffn_backward(tag, l, dy, rec):
        gamma = row(small[f"{tag}_norm"], l)
        weights_first = (tag, l) == ("ffn1", 0)
        if weights_first:
            dyh, hh, dgu = _ffn_bwd_hidden(f"{tag}_bwd_hidden_l{l}", dy, rec[f"gu_{tag}"], big[f"{tag}_w_out", l])
        else:
            dx, dgam, dyh, hh, dgu = run(_ffn_bwd, f"{tag}_bwd_l{l}", dy, rec[f"x_{tag}"], gamma,
                                         rec[f"gu_{tag}"], big[f"{tag}_w_in", l], big[f"{tag}_w_out", l])
        gb[f"{tag}_w_out", l] = run(_matmul_tn, f"{tag}_dwout_l{l}", hh, dyh,
                                    a_split=HIDDEN_SPLIT).reshape(N_DEV, -1, d)
        gb[f"{tag}_w_in", l] = run(_matmul_tn, f"{tag}_dwin_l{l}", dgu, rec[f"xn_{tag}"],
                                   a_split=2 * HIDDEN_SPLIT).reshape(N_DEV, -1, d)
        if weights_first:
            dx, dgam = run(_matmul_nt_rms_bwd, f"{tag}_dx_l{l}", dgu, big[f"{tag}_w_in", l], dy, rec[f"x_{tag}"],
                           gamma, w_rows_are_k=True, tile=FFN_TILE)
        norm_rows[f"{tag}_norm"][l] = dgam
        return dx

    for l in reversed(range(n_layers)):
        rec = saved[l]
        dx = ffn_backward("ffn2", l, dx, rec)
        if l % 2 == 0:
            dcat, dxb = _matmul_nt(f"ab_out_bwd_l{l}", dx, big["ab_w_out"])
            gb["ab_w_out", 0] = _matmul_tn(f"ab_dwout_l{l}", rec["cat"], dxb)
            dh, gs["pool_w"], gs["pool_b"], gs["pool_scale"], gs["conv_w"], gs["conv_b"], gs["conv_ln_g"], \
                gs["conv_ln_b"] = run(
                    _pool_conv_bwd, f"pool_conv_bwd_l{l}", rec["h"], dcat, small["pool_w"], small["pool_b"],
                    small["pool_scale"], small["conv_w"], small["conv_b"], small["conv_ln_g"], small["conv_ln_b"])
            gb["ab_w_in", 0] = _matmul_tn(f"ab_dwin_l{l}", rec["xn_mix"], dh)
            dx, dgam = _matmul_nt_rms_bwd(f"ab_in_bwd_l{l}", dh, big["ab_w_in"], dx, rec["x_mix"],
                                          row(small["mix_norm"], l))
        else:
            dzp, dxb, gs["sgu_ln_g"], gs["sgu_ln_b"], gs["sgu_w"], gs["sgu_b"] = _sgu_bwd(
                f"sgu_bwd_l{l}", rec["zp"], dx, big["sgu_w_out"], small["sgu_ln_g"], small["sgu_ln_b"],
                small["sgu_w"], small["sgu_b"])
            gb["sgu_w_out", 0] = _matmul_tn(f"sgu_dwout_l{l}", rec["q"], dxb)
            gb["sgu_w_in", 0] = _matmul_tn(f"sgu_dwin_l{l}", rec["xn_mix"], dzp)
            dx, dgam = _matmul_nt_rms_bwd(f"sgu_in_bwd_l{l}", dzp, big["sgu_w_in"], dx, rec["x_mix"],
                                          row(small["mix_norm"], l))
        norm_rows["mix_norm"][l] = dgam
        dx = ffn_backward("ffn1", l, dx, rec)

    for k, rows in norm_rows.items():
        gs[k] = jnp.concatenate(rows, axis=0)
    return loss, dx, gb, gs


SHARDED_SMALL = ("conv_w", "sgu_ln_g", "sgu_ln_b")
WEIGHTS = ("ffn1_norm", "ffn1_w_in", "ffn1_w_out", "mix_norm", "ffn2_norm", "ffn2_w_in", "ffn2_w_out", "ab_w_in",
           "pool_w", "pool_b", "pool_scale", "conv_w", "conv_b", "conv_ln_g", "conv_ln_b", "ab_w_out", "sgu_w_in",
           "sgu_ln_g", "sgu_ln_b", "sgu_w", "sgu_b", "sgu_w_out", "final_norm")
LANES = 128


def _interleave_cols(g):
    n, k, c = g.shape
    return jnp.transpose(g, (1, 0, 2)).reshape(k, n * c)


def _split_cols(a):
    k, nc = a.shape
    return jnp.transpose(a.reshape(k, N_DEV, nc // N_DEV), (1, 0, 2))


def _as3(a):
    if a.ndim == 1:
        return a.reshape(1, 1, -1)
    if a.ndim == 2:
        return a.reshape(a.shape[0], 1, a.shape[1])
    return a.reshape(a.shape[0], -1, a.shape[-1])


def _pack_rows(a):
    flat = a.reshape(-1)
    pad = (-flat.shape[0]) % (8 * LANES)
    if pad:
        flat = jnp.concatenate([flat, jnp.zeros((pad,), flat.dtype)])
    return flat.reshape(-1, LANES)


FIRST_GATHER = (("ffn1_w_in", 0), ("ffn1_w_out", 0), ("conv_w", 0), ("sgu_ln_g", 0), ("sgu_ln_b", 0))
GATHER_PLAN = {
    "ffn1_fwd_l0": (("ab_w_in", 0), ("ab_w_out", 0), ("ffn2_w_in", 0), ("ffn2_w_out", 0)),
    "ffn2_fwd_l0": (("ffn1_w_in", 1), ("ffn1_w_out", 1), ("sgu_w_in", 0), ("sgu_w_out", 0)),
    "ffn1_fwd_l1": (("ffn2_w_in", 1), ("ffn2_w_out", 1)),
}
SCATTER_PLAN = {
    "ffn2_dwin_l1": (("ffn2_w_out", 1),),
    "ffn1_bwd_l1": (("ffn2_w_in", 1), ("sgu_w_out", 0)),
    "ffn1_dwout_l1": (("sgu_w_in", 0), ("sgu_ln_g", 0), ("sgu_ln_b", 0)),
    "ffn1_dwin_l1": (("ffn1_w_out", 1),),
    "ffn2_bwd_l0": (("ffn1_w_in", 1), ("pack", 0)),
    "ffn2_dwin_l0": (("ffn2_w_out", 0),),
    "pool_conv_bwd_l0": (("ffn2_w_in", 0), ("ab_w_out", 0)),
    "ffn1_dwout_l0": (("ab_w_in", 0), ("conv_w", 0)),
    "ffn1_dwin_l0": (("ffn1_w_out", 0), ("pack", 1)),
    "ffn1_dx_l0": (("ffn1_w_in", 0),),
}
PACK_GROUPS = (("sgu_w", "sgu_b"),
               ("pool_w", "pool_b", "pool_scale", "conv_b", "conv_ln_g", "conv_ln_b"),
               ("ffn1_norm", "mix_norm", "ffn2_norm", "final_norm"))
REPLICATED = tuple(n for group in PACK_GROUPS for n in group)


class _Schedule:
    def __init__(self, shards, big, small):
        self.shards, self.big, self.small = shards, big, small
        self.recv = {}
        self.packs = {}
        self.pending = {}

    def gather_comm(self, keys):
        comm = _Comm()
        for key in keys:
            comm.gather(*self.shards[key])
        self.pending[id(comm)] = ("gather", keys)
        return comm

    def scatter_comm(self, keys, gb, gs):
        comm = _Comm()
        for name, l in keys:
            if name == "pack":
                comm.gather(jnp.concatenate([_pack_rows(gs[n]) for n in PACK_GROUPS[l]], axis=0))
                continue
            if name in ("ab_w_in", "sgu_w_in"):
                send = _split_cols(gb[name, l][0])
            elif name in ("ab_w_out", "sgu_w_out"):
                send = gb[name, l][0]
                send = send.reshape(N_DEV, -1, send.shape[-1])
            elif name == "conv_w":
                send = _split_cols(gs[name][:CONV_WIDTH])
            elif name in ("sgu_ln_g", "sgu_ln_b"):
                send = gs[name].reshape(N_DEV, 1, -1)
            else:
                send = gb[name, l]
            comm.scatter(send)
        self.pending[id(comm)] = ("scatter", keys)
        return comm

    def plan(self, name, gb, gs):
        if name in GATHER_PLAN:
            return self.gather_comm(GATHER_PLAN[name])
        if name in SCATTER_PLAN:
            return self.scatter_comm(SCATTER_PLAN[name], gb, gs)
        return None

    def deliver(self, comm, got):
        kind, keys = self.pending.pop(id(comm))
        for (name, l), arr in zip(keys, got):
            if name == "pack":
                self.packs[l] = arr
            elif kind == "scatter":
                self.recv[name, l] = arr
            elif name in ("ffn1_w_in", "ffn2_w_in"):
                self.big[name, l] = arr.reshape(-1, arr.shape[-1])
            elif name in ("ffn1_w_out", "ffn2_w_out"):
                self.big[name, l] = arr.reshape(-1, arr.shape[-1])
            elif name in ("ab_w_in", "sgu_w_in"):
                self.big[name] = _interleave_cols(arr)
            elif name in ("ab_w_out", "sgu_w_out"):
                self.big[name] = arr.reshape(-1, arr.shape[-1])
            elif name == "conv_w":
                self.small[name] = jnp.pad(_interleave_cols(arr), ((0, 1), (0, 0)))
            else:
                self.small[name] = arr.reshape(1, -1)


def kernel(x, ffn1_norm, ffn1_w_in, ffn1_w_out, mix_norm, ffn2_norm, ffn2_w_in, ffn2_w_out, ab_w_in, pool_w, pool_b, pool_scale, conv_w, conv_b, conv_ln_g, conv_ln_b, ab_w_out, sgu_w_in, sgu_ln_g, sgu_ln_b, sgu_w, sgu_b, sgu_w_out, final_norm, loss_target, m_ffn1_norm, m_ffn1_w_in, m_ffn1_w_out, m_mix_norm, m_ffn2_norm, m_ffn2_w_in, m_ffn2_w_out, m_ab_w_in, m_pool_w, m_pool_b, m_pool_scale, m_conv_w, m_conv_b, m_conv_ln_g, m_conv_ln_b, m_ab_w_out, m_sgu_w_in, m_sgu_ln_g, m_sgu_ln_b, m_sgu_w, m_sgu_b, m_sgu_w_out, m_final_norm, v_ffn1_norm, v_ffn1_w_in, v_ffn1_w_out, v_mix_norm, v_ffn2_norm, v_ffn2_w_in, v_ffn2_w_out, v_ab_w_in, v_pool_w, v_pool_b, v_pool_scale, v_conv_w, v_conv_b, v_conv_ln_g, v_conv_ln_b, v_ab_w_out, v_sgu_w_in, v_sgu_ln_g, v_sgu_ln_b, v_sgu_w, v_sgu_b, v_sgu_w_out, v_final_norm):
    args = dict(locals())
    w = {n: args[n] for n in WEIGHTS}
    m = {n: args["m_" + n] for n in WEIGHTS}
    v = {n: args["v_" + n] for n in WEIGHTS}
    n_layers = ffn1_norm.shape[0]

    shards = {}
    for n in ("ffn1_w_in", "ffn2_w_in"):
        wt = jnp.swapaxes(w[n], 1, 2).astype(BF16)
        for l in range(n_layers):
            shards[n, l] = (wt, l)
    for n in ("ffn1_w_out", "ffn2_w_out"):
        wb = w[n].astype(BF16)
        for l in range(n_layers):
            shards[n, l] = (wb, l)
    for n in ("ab_w_in", "ab_w_out", "sgu_w_in", "sgu_w_out"):
        shards[n, 0] = (w[n][0].astype(BF16), None)
    shards["conv_w", 0] = (conv_w[0], None)
    shards["sgu_ln_g", 0] = (sgu_ln_g, None)
    shards["sgu_ln_b", 0] = (sgu_ln_b, None)

    big = {}
    small = {
        "ffn1_norm": ffn1_norm, "mix_norm": mix_norm, "ffn2_norm": ffn2_norm, "final_norm": final_norm.reshape(1, -1),
        "pool_w": pool_w[0], "pool_b": pool_b[0], "pool_scale": pool_scale,
        "conv_b": conv_b, "conv_ln_g": conv_ln_g, "conv_ln_b": conv_ln_b,
        "sgu_w": sgu_w[0], "sgu_b": sgu_b[0][:, :, None],
    }
    sched = _Schedule(shards, big, small)
    first = sched.gather_comm(FIRST_GATHER)
    sched.deliver(first, _exchange("gather_first", first))

    loss, grad_x, gb, gs = _local_step(x[0], loss_target[0], big, small, sched)

    last_pack = jnp.concatenate([_pack_rows(gs[n]) for n in PACK_GROUPS[-1]] + [_pack_rows(loss)], axis=0)
    last = _Comm()
    last.gather(last_pack)
    packed_all = jnp.concatenate([sched.packs[0], sched.packs[1], _exchange("reduce_last", last)[0]], axis=1)
    offsets = [0]
    for n in REPLICATED:
        offsets.append(offsets[-1] + SUBLANES * (-(-w[n].size // (SUBLANES * LANES))))
    offsets.append(offsets[-1] + SUBLANES)
    n_rows = offsets[-1]
    recv = sched.recv

    out = {}
    for n in ("ffn1_w_in", "ffn2_w_in"):
        res = _adamw(f"adamw_{n}", [recv[n, l] for l in range(n_layers)], jnp.swapaxes(w[n], 1, 2),
                     jnp.swapaxes(m[n], 1, 2), jnp.swapaxes(v[n], 1, 2), 176)
        out[n] = [jnp.swapaxes(r, 1, 2) for r in res]
    for n in ("ffn1_w_out", "ffn2_w_out"):
        out[n] = _adamw(f"adamw_{n}", [recv[n, l] for l in range(n_layers)], w[n], m[n], v[n], 176)
    for n in ("ab_w_in", "ab_w_out", "sgu_w_in", "sgu_w_out") + SHARDED_SMALL:
        w3 = _as3(w[n])
        parts = recv[n, 0].reshape((N_DEV,) + w3.shape[1:])
        res = _adamw(f"adamw_{n}", [parts], w3, _as3(m[n]), _as3(v[n]), 512)
        out[n] = [r.reshape(w[n].shape) for r in res]

    def pack_rep(src):
        tail = [jnp.zeros((offsets[-1] - offsets[-2], LANES), F32)]
        return jnp.concatenate([_pack_rows(src[n]) for n in REPLICATED] + tail, axis=0)[None]

    res = _adamw("adamw_replicated", [packed_all], pack_rep(w), pack_rep(m), pack_rep(v), n_rows)
    for i, n in enumerate(REPLICATED):
        size = w[n].size
        out[n] = [r[0, offsets[i]:offsets[i + 1]].reshape(-1)[:size].reshape(w[n].shape) for r in res]
    loss_sum = res[0][0, offsets[-2], 0]

    return (loss_sum, grad_x[None],
            *[out[n][0] for n in WEIGHTS], *[out[n][1] for n in WEIGHTS],
            *[out[n][2] for n in WEIGHTS], *[out[n][3] for n in WEIGHTS])
```

```python
import functools

import jax
import jax.numpy as jnp
from jax import lax
from jax.experimental import pallas as pl
from jax.experimental.pallas import tpu as pltpu

F32 = jnp.float32
BF16 = jnp.bfloat16
EPS = 1e-6
N_DEV = 8
POOL_WINDOWS = (2, 4, 8, 16)
CONV_WIDTH = 31
HALO = 32
GROUP = 128
SUBLANES = 8
ROW_CHUNK = 32
SUBLANES = 8
ROW_CHUNK = 32
TOKEN_TILE = 512
CONTRACT_TILE = 2048
HIDDEN_SPLIT = 2
FFN_TILE = 256
ADAM_LR, ADAM_B1, ADAM_B2, ADAM_EPS, ADAM_WD, ADAM_STEP = 0.001, 0.9, 0.999, 1e-08, 0.01, 10
VMEM_LIMIT = 56 * 1024 * 1024

NT = (((1,), (1,)), ((), ()))
TN = (((0,), (0,)), ((), ()))


def _pallas(body, side_effects, **kw):
    params = pltpu.CompilerParams(vmem_limit_bytes=VMEM_LIMIT, has_side_effects=side_effects)
    return pl.pallas_call(body, compiler_params=params, **kw)


def _call(body, comm=None, **kw):
    if comm is None:
        return _pallas(body, False, **kw)
    in_specs = list(kw.pop("in_specs"))
    out_specs = kw.pop("out_specs")
    out_shape = kw.pop("out_shape")
    scratch = list(kw.pop("scratch_shapes", []))
    single = not isinstance(out_shape, (list, tuple))
    if single:
        out_specs, out_shape = [out_specs], [out_shape]
    n_in, n_out, n_scr = len(in_specs), len(out_shape), len(scratch)
    n_ci, n_co = len(comm.inputs), len(comm.out_shapes)
    grid = tuple(kw.get("grid", ()))

    def wrapped(*refs):
        pos = 0
        parts = []
        for n in (n_in, n_ci, n_out, n_co, n_scr, 3):
            parts.append(refs[pos:pos + n])
            pos += n
        a_in, c_in, a_out, c_out, a_scr, sems = parts
        if grid:
            step = 0
            for ax, g in enumerate(grid):
                step = step * g + pl.program_id(ax)
            total = functools.reduce(lambda p, q: p * q, grid)
            pl.when(step == 0)(lambda: comm.start(c_in, c_out, sems))
            body(*a_in, *a_out, *a_scr)
            pl.when(step == (3 * total) // 4)(lambda: comm.forward(c_in, c_out, sems))
            pl.when(step == total - 1)(lambda: comm.finish(c_in, c_out, sems))
        else:
            comm.start(c_in, c_out, sems)
            body(*a_in, *a_out, *a_scr)
            comm.forward(c_in, c_out, sems)
            comm.finish(c_in, c_out, sems)

    hbm = pl.BlockSpec(memory_space=pl.ANY)
    fn = _pallas(wrapped, True, in_specs=in_specs + [hbm] * n_ci, out_specs=list(out_specs) + [hbm] * n_co,
                 out_shape=list(out_shape) + list(comm.out_shapes), scratch_shapes=scratch + comm.semaphores(), **kw)

    def run(*operands):
        outs = fn(*operands, *comm.inputs)
        res = outs[:n_out]
        return (res[0] if single else res), outs[n_out:]

    return run


class _Comm:
    def __init__(self):
        self.inputs, self.sel, self.kinds, self.out_shapes = [], [], [], []

    def gather(self, arr, sel=None):
        block = arr.shape if sel is None else arr.shape[1:]
        self.inputs.append(arr)
        self.sel.append(sel)
        self.kinds.append("gather")
        self.out_shapes.append(jax.ShapeDtypeStruct((N_DEV,) + tuple(block), arr.dtype))
        return len(self.inputs) - 1

    def scatter(self, arr):
        self.inputs.append(arr)
        self.sel.append(None)
        self.kinds.append("scatter")
        self.out_shapes.append(jax.ShapeDtypeStruct(arr.shape, arr.dtype))
        return len(self.inputs) - 1

    def semaphores(self):
        n = len(self.inputs)
        return [pltpu.SemaphoreType.DMA((n, N_DEV - 1)), pltpu.SemaphoreType.DMA((n, N_DEV - 1)),
                pltpu.SemaphoreType.DMA((n,))]

    def _copies(self, ins, outs, sems, with_passed=True):
        send_sems, recv_sems, local_sems = sems
        x, y, c = lax.axis_index("x"), lax.axis_index("y"), lax.axis_index("c")
        me = 4 * x + 2 * y + c
        sibling = (x, y, 1 - c)
        chips = [(1 - x, y), (x, 1 - y), (1 - x, 1 - y)]
        items = []
        for a, kind in enumerate(self.kinds):
            def remote(src, dst, k, to, a=a):
                return pltpu.make_async_remote_copy(
                    src_ref=src, dst_ref=dst, send_sem=send_sems.at[a, k], recv_sem=recv_sems.at[a, k],
                    device_id=to, device_id_type=pl.DeviceIdType.MESH)
            if kind == "gather":
                src = ins[a] if self.sel[a] is None else ins[a].at[self.sel[a]]
                mine = outs[a].at[me]
                local = pltpu.make_async_copy(src, mine, local_sems.at[a])
                first = [remote(src, mine, 0, sibling)]
                first += [remote(src, mine, 1 + j, (*chip, c)) for j, chip in enumerate(chips)]
                passed = []
                for j, chip in enumerate(chips if with_passed else []):
                    got = outs[a].at[4 * chip[0] + 2 * chip[1] + c]
                    passed.append(remote(got, got, 4 + j, sibling))
            else:
                local = pltpu.make_async_copy(ins[a].at[me], outs[a].at[me], local_sems.at[a])
                first, passed = [], []
                for k in (1, 4, 2, 6, 5, 3, 7):
                    peer = ((1 - x) if k & 4 else x, (1 - y) if k & 2 else y, (1 - c) if k & 1 else c)
                    pid = 4 * peer[0] + 2 * peer[1] + peer[2]
                    first.append(remote(ins[a].at[pid], outs[a].at[me], k - 1, peer))
            items.append((local, first, passed))
        return items

    def start(self, ins, outs, sems):
        for local, first, _ in self._copies(ins, outs, sems, with_passed=False):
            local.start()
            for cp in first:
                cp.start()

    def forward(self, ins, outs, sems):
        for _, first, passed in self._copies(ins, outs, sems):
            for j, cp in enumerate(passed):
                first[1 + j].wait_recv()
                cp.start()

    def finish(self, ins, outs, sems):
        for local, first, passed in self._copies(ins, outs, sems):
            if passed:
                first[0].wait_recv()
                for cp in passed:
                    cp.wait_recv()
                for cp in first + passed:
                    cp.wait_send()
            else:
                for cp in first:
                    cp.wait()
            local.wait()


def _exchange(name, comm):
    _, outs = _call(lambda: None, comm=comm, name=name, in_specs=[], out_specs=[], out_shape=[])()
    return outs


def _sigmoid(x):
    return 0.5 * jnp.tanh(0.5 * x) + 0.5


def _tile(t):
    return min(TOKEN_TILE, t)


def _load_weights(pairs, sems):
    @pl.when(pl.program_id(0) == 0)
    def _():
        copies = [pltpu.make_async_copy(src, dst, sems.at[n]) for n, (src, dst) in enumerate(pairs)]
        for cp in copies:
            cp.start()
        for cp in copies:
            cp.wait()


def _ffn_fwd(name, x, gamma, win_t, wout, pre=None, head=None, comm=None):
    t, d = x.shape
    f = wout.shape[0]
    fc = f // HIDDEN_SPLIT
    tm = min(FFN_TILE, t)
    extra = (() if pre is None else tuple(pre)) + (() if head is None else tuple(head))

    def body(*refs):
        x_ref, gam_ref, win_hbm, wo_hbm = refs[:4]
        rest = list(refs[4:])
        a_ref, wp_ref = (rest.pop(0), rest.pop(0)) if pre is not None else (None, None)
        gf_ref, tg_ref = (rest.pop(0), rest.pop(0)) if head is not None else (None, None)
        xin_ref = rest.pop(0) if pre is not None else None
        if head is None:
            xo_ref = rest.pop(0)
        else:
            loss_ref, dx_ref, dgf_ref = rest.pop(0), rest.pop(0), rest.pop(0)
        xn_ref, gu_ref, win_v, wo_v, h_s, sems = rest
        _load_weights([(win_hbm, win_v), (wo_hbm, wo_v)], sems)
        xv = x_ref[...]
        if pre is not None:
            xv = xv + jnp.dot(a_ref[...], wp_ref[...], preferred_element_type=F32)
            xin_ref[...] = xv
        r = lax.rsqrt(jnp.mean(xv * xv, axis=-1, keepdims=True) + EPS)
        xn = (xv * r * gam_ref[...]).astype(BF16)
        xn_ref[...] = xn
        for c in range(HIDDEN_SPLIT):
            lo = c * fc
            g = lax.dot_general(xn, win_v[lo:lo + fc, :], NT, preferred_element_type=F32)
            u = lax.dot_general(xn, win_v[f + lo:f + lo + fc, :], NT, preferred_element_type=F32)
            gu_ref[:, lo:lo + fc] = g.astype(BF16)
            gu_ref[:, f + lo:f + lo + fc] = u.astype(BF16)
            h_s[:, lo:lo + fc] = (g * _sigmoid(g) * u).astype(BF16)
        xo = xv + 0.5 * jnp.dot(h_s[...], wo_v[...], preferred_element_type=F32)
        if head is None:
            xo_ref[...] = xo
        else:
            @pl.when(pl.program_id(0) == 0)
            def _():
                loss_ref[...] = jnp.zeros_like(loss_ref)
                dgf_ref[...] = jnp.zeros_like(dgf_ref)

            ro = lax.rsqrt(jnp.mean(xo * xo, axis=-1, keepdims=True) + EPS)
            yv = xo * ro
            err = yv * gf_ref[...] - tg_ref[...]
            loss_ref[...] += (0.5 / d) * jnp.sum(jnp.sum(err * err, axis=-1, keepdims=True), axis=0, keepdims=True)
            dout = err * (1.0 / d)
            dgf_ref[...] += jnp.sum(dout * yv, axis=0, keepdims=True)
            dyn = dout * gf_ref[...]
            dx_ref[...] = ro * (dyn - yv * jnp.mean(dyn * yv, axis=-1, keepdims=True))

    hbm = pl.BlockSpec(memory_space=pl.ANY)
    row = pl.BlockSpec((tm, d), lambda i: (i, 0))
    one = lambda shape: pl.BlockSpec(shape, lambda i: (0, 0))
    in_specs = [row, one((1, d)), hbm, hbm]
    out_specs, out_shape = [], []
    if pre is not None:
        in_specs += [pl.BlockSpec((tm, pre[0].shape[1]), lambda i: (i, 0)), one(pre[1].shape)]
        out_specs.append(row)
        out_shape.append(jax.ShapeDtypeStruct((t, d), F32))
    if head is None:
        out_specs.append(row)
        out_shape.append(jax.ShapeDtypeStruct((t, d), F32))
    else:
        in_specs += [one((1, d)), row]
        out_specs += [one((1, 1)), row, one((1, d))]
        out_shape += [jax.ShapeDtypeStruct((1, 1), F32), jax.ShapeDtypeStruct((t, d), F32),
                      jax.ShapeDtypeStruct((1, d), F32)]
    out_specs += [row, pl.BlockSpec((tm, 2 * f), lambda i: (i, 0))]
    out_shape += [jax.ShapeDtypeStruct((t, d), BF16), jax.ShapeDtypeStruct((t, 2 * f), BF16)]
    return _call(
        body, comm=comm, name=name, grid=(t // tm,),
        in_specs=in_specs, out_specs=out_specs, out_shape=out_shape,
        scratch_shapes=[pltpu.VMEM((2 * f, d), BF16), pltpu.VMEM((f, d), BF16), pltpu.VMEM((tm, f), BF16),
                        pltpu.SemaphoreType.DMA((2,))],
    )(x, gamma, win_t, wout, *extra)


def _ffn_bwd(name, dy, x, gamma, gu, win_t, wout, comm=None):
    t, d = x.shape
    f = wout.shape[0]
    fc = f // HIDDEN_SPLIT
    tm = min(FFN_TILE, t)

    def body(dy_ref, x_ref, gam_ref, gu_ref, win_hbm, wo_hbm,
             dx_ref, dgam_ref, dyh_ref, h_ref, dgu_ref, win_v, wo_v, sems):
        _load_weights([(win_hbm, win_v), (wo_hbm, wo_v)], sems)

        @pl.when(pl.program_id(0) == 0)
        def _():
            dgam_ref[...] = jnp.zeros_like(dgam_ref)

        dyb = (0.5 * dy_ref[...]).astype(BF16)
        dyh_ref[...] = dyb
        for c in range(HIDDEN_SPLIT):
            lo = c * fc
            dh = lax.dot_general(dyb, wo_v[lo:lo + fc, :], NT, preferred_element_type=F32)
            g = gu_ref[:, lo:lo + fc].astype(F32)
            u = gu_ref[:, f + lo:f + lo + fc].astype(F32)
            sig = _sigmoid(g)
            silu = g * sig
            h_ref[:, lo:lo + fc] = (silu * u).astype(BF16)
            dgu_ref[:, lo:lo + fc] = (dh * u * (sig * (1.0 + g * (1.0 - sig)))).astype(BF16)
            dgu_ref[:, f + lo:f + lo + fc] = (dh * silu).astype(BF16)
        dxn = jnp.dot(dgu_ref[...], win_v[...], preferred_element_type=F32)
        xv = x_ref[...]
        r = lax.rsqrt(jnp.mean(xv * xv, axis=-1, keepdims=True) + EPS)
        yv = xv * r
        dgam_ref[...] += jnp.sum(dxn * yv, axis=0, keepdims=True)
        dyn = dxn * gam_ref[...]
        dx_ref[...] = dy_ref[...] + r * (dyn - yv * jnp.mean(dyn * yv, axis=-1, keepdims=True))

    hbm = pl.BlockSpec(memory_space=pl.ANY)
    row = lambda width: pl.BlockSpec((tm, width), lambda i: (i, 0))
    return _call(
        body, comm=comm, name=name, grid=(t // tm,),
        in_specs=[row(d), row(d), pl.BlockSpec((1, d), lambda i: (0, 0)), row(2 * f), hbm, hbm],
        out_specs=[row(d), pl.BlockSpec((1, d), lambda i: (0, 0)), row(d), row(f), row(2 * f)],
        out_shape=[
            jax.ShapeDtypeStruct((t, d), F32),
            jax.ShapeDtypeStruct((1, d), F32),
            jax.ShapeDtypeStruct((t, d), BF16),
            jax.ShapeDtypeStruct((t, f), BF16),
            jax.ShapeDtypeStruct((t, 2 * f), BF16),
        ],
        scratch_shapes=[pltpu.VMEM((2 * f, d), BF16), pltpu.VMEM((f, d), BF16), pltpu.SemaphoreType.DMA((2,))],
    )(dy, x, gamma, gu, win_t, wout)


def _ffn_bwd_hidden(name, dy, gu, wout):
    t, d = dy.shape
    f = wout.shape[0]
    fc = f // HIDDEN_SPLIT
    tm = min(FFN_TILE, t)

    def body(dy_ref, gu_ref, wo_hbm, dyh_ref, h_ref, dgu_ref, wo_v, sems):
        _load_weights([(wo_hbm, wo_v)], sems)
        dyb = (0.5 * dy_ref[...]).astype(BF16)
        dyh_ref[...] = dyb
        for c in range(HIDDEN_SPLIT):
            lo = c * fc
            dh = lax.dot_general(dyb, wo_v[lo:lo + fc, :], NT, preferred_element_type=F32)
            g = gu_ref[:, lo:lo + fc].astype(F32)
            u = gu_ref[:, f + lo:f + lo + fc].astype(F32)
            sig = _sigmoid(g)
            silu = g * sig
            h_ref[:, lo:lo + fc] = (silu * u).astype(BF16)
            dgu_ref[:, lo:lo + fc] = (dh * u * (sig * (1.0 + g * (1.0 - sig)))).astype(BF16)
            dgu_ref[:, f + lo:f + lo + fc] = (dh * silu).astype(BF16)

    row = lambda width: pl.BlockSpec((tm, width), lambda i: (i, 0))
    return _call(
        body, name=name, grid=(t // tm,),
        in_specs=[row(d), row(2 * f), pl.BlockSpec(memory_space=pl.ANY)],
        out_specs=[row(d), row(f), row(2 * f)],
        out_shape=[jax.ShapeDtypeStruct((t, d), BF16), jax.ShapeDtypeStruct((t, f), BF16),
                   jax.ShapeDtypeStruct((t, 2 * f), BF16)],
        scratch_shapes=[pltpu.VMEM((f, d), BF16), pltpu.SemaphoreType.DMA((1,))],
    )(dy, gu, wout)


def _matmul_tn(name, a, b, out_dtype=BF16, comm=None, a_split=None):
    a_b = a.ndim == 3
    b_b = b.ndim == 3
    nb = a_split if a_split else a.shape[0] if a_b else b.shape[0] if b_b else 1
    t, m = a.shape[-2:]
    if a_split:
        m = m // a_split
    n = b.shape[-1]
    tk = min(CONTRACT_TILE, t)
    nt = t // tk

    def body(a_ref, b_ref, o_ref, acc):
        s = pl.program_id(1)

        @pl.when(s == 0)
        def _():
            acc[...] = jnp.zeros_like(acc)

        acc[...] += lax.dot_general(a_ref[...], b_ref[...], TN, preferred_element_type=F32)

        @pl.when(s == nt - 1)
        def _():
            o_ref[...] = acc[...].astype(o_ref.dtype)

    a_spec = (pl.BlockSpec((None, tk, m), lambda j, s: (j, s, 0)) if a_b
              else pl.BlockSpec((tk, m), lambda j, s: (s, j)) if a_split
              else pl.BlockSpec((tk, m), lambda j, s: (s, 0)))
    b_spec = (pl.BlockSpec((None, tk, n), lambda j, s: (j, s, 0)) if b_b
              else pl.BlockSpec((tk, n), lambda j, s: (s, 0)))
    return _call(
        body, comm=comm, name=name, grid=(nb, nt),
        in_specs=[a_spec, b_spec],
        out_specs=pl.BlockSpec((None, m, n), lambda j, s: (j, 0, 0)),
        out_shape=jax.ShapeDtypeStruct((nb, m, n), out_dtype),
        scratch_shapes=[pltpu.VMEM((m, n), F32)],
    )(a, b)


def _matmul_nt(name, dy, w):
    t, n = dy.shape
    kdim = w.shape[0]
    tm = _tile(t)

    def body(dy_ref, w_ref, da_ref, dyb_ref):
        dyb = dy_ref[...].astype(BF16)
        dyb_ref[...] = dyb
        da_ref[...] = lax.dot_general(dyb, w_ref[...], NT, preferred_element_type=F32)

    return _call(
        body, name=name, grid=(t // tm,),
        in_specs=[pl.BlockSpec((tm, n), lambda i: (i, 0)),
                  pl.BlockSpec((kdim, n), lambda i: (0, 0))],
        out_specs=[pl.BlockSpec((tm, kdim), lambda i: (i, 0)),
                   pl.BlockSpec((tm, n), lambda i: (i, 0))],
        out_shape=[jax.ShapeDtypeStruct((t, kdim), F32), jax.ShapeDtypeStruct((t, n), BF16)],
    )(dy, w)


def _matmul_nt_rms_bwd(name, dz, w, dres, x, gamma, w_rows_are_k=False, tile=TOKEN_TILE, comm=None):
    t, kdim = dz.shape
    d = x.shape[1]
    tm = min(tile, t)

    def body(dz_ref, w_ref, dres_ref, x_ref, gam_ref, dx_ref, dgam_ref):
        i = pl.program_id(0)

        @pl.when(i == 0)
        def _():
            dgam_ref[...] = jnp.zeros_like(dgam_ref)

        if w_rows_are_k:
            dxn = jnp.dot(dz_ref[...], w_ref[...], preferred_element_type=F32)
        else:
            dxn = lax.dot_general(dz_ref[...], w_ref[...], NT, preferred_element_type=F32)
        xv = x_ref[...]
        r = lax.rsqrt(jnp.mean(xv * xv, axis=-1, keepdims=True) + EPS)
        yv = xv * r
        dgam_ref[...] += jnp.sum(dxn * yv, axis=0, keepdims=True)
        dyn = dxn * gam_ref[...]
        dx_ref[...] = dres_ref[...] + r * (dyn - yv * jnp.mean(dyn * yv, axis=-1, keepdims=True))

    return _call(
        body, comm=comm, name=name, grid=(t // tm,),
        in_specs=[pl.BlockSpec((tm, kdim), lambda i: (i, 0)),
                  pl.BlockSpec(w.shape, lambda i: (0, 0)),
                  pl.BlockSpec((tm, d), lambda i: (i, 0)),
                  pl.BlockSpec((tm, d), lambda i: (i, 0)),
                  pl.BlockSpec((1, d), lambda i: (0, 0))],
        out_specs=[pl.BlockSpec((tm, d), lambda i: (i, 0)),
                   pl.BlockSpec((1, d), lambda i: (0, 0))],
        out_shape=[jax.ShapeDtypeStruct((t, d), F32), jax.ShapeDtypeStruct((1, d), F32)],
    )(dz, w, dres, x, gamma)


def _pool_means(uext_ref, pos, tm, g, win):
    cols = slice(g * GROUP, (g + 1) * GROUP)
    acc = uext_ref[pl.ds(HALO, tm), cols]
    for j in range(1, win):
        acc = acc + uext_ref[pl.ds(HALO - j, tm), cols]
    cnt = jnp.minimum(pos + 1, win).astype(F32)
    return acc / cnt - uext_ref[pl.ds(HALO, tm), cols]


def _shifted_copies(src_ref, dst_ref, rows):
    for b in range(SUBLANES):
        dst_ref[b, pl.ds(0, rows), :] = src_ref[pl.ds(b, rows), :]


def _tap_sum(sh_ref, cw_ref, offsets, out_ref, n_rows, bias_ref=None):
    width = out_ref.shape[-1]

    def chunk(c, carry):
        r0 = pl.multiple_of(c * ROW_CHUNK, ROW_CHUNK)
        acc = (jnp.zeros((ROW_CHUNK, width), F32) if bias_ref is None
               else jnp.broadcast_to(bias_ref[...], (ROW_CHUNK, width)))
        for k, off in enumerate(offsets):
            a, b = divmod(off, SUBLANES)
            acc = acc + cw_ref[k:k + 1, :] * sh_ref[b, pl.ds(r0 + SUBLANES * a, ROW_CHUNK), :]
        out_ref[pl.ds(r0, ROW_CHUNK), :] = acc
        return carry

    lax.fori_loop(0, n_rows // ROW_CHUNK, chunk, 0)


def _pool_conv_fwd(name, x, gamma, w, pool_w, pool_b, pool_scale, conv_w, conv_b, ln_g, ln_b):
    t, d = x.shape
    hw = w.shape[1]
    pc = len(POOL_WINDOWS) * GROUP
    cc = (hw - pc) // 2
    tm = _tile(t)

    def body(x_ref, gam_ref, w_ref, pw_ref, pb_ref, ps_ref, cw_ref, cb_ref, lg_ref, lb_ref,
             xn_ref, h_ref, cat_ref, uext, gext, gsh, y_s, hp_s):
        i = pl.program_id(0)

        @pl.when(i == 0)
        def _():
            hp_s[...] = jnp.zeros_like(hp_s)

        xv = x_ref[...]
        r = lax.rsqrt(jnp.mean(xv * xv, axis=-1, keepdims=True) + EPS)
        xn = (xv * r * gam_ref[...]).astype(BF16)
        xn_ref[...] = xn
        h_ref[...] = jnp.dot(xn, w_ref[...], preferred_element_type=F32)
        hp = hp_s[...]
        uext[0:HALO, :] = hp[:, :pc]
        uext[HALO:, :] = h_ref[:, :pc]
        gext[0:HALO, :] = hp[:, pc:pc + cc] * _sigmoid(hp[:, pc + cc:])
        gext[pl.ds(HALO, tm), :] = h_ref[:, pc:pc + cc] * _sigmoid(h_ref[:, pc + cc:])
        gext[pl.ds(HALO + tm, SUBLANES), :] = jnp.zeros((SUBLANES, cc), F32)
        pos = i * tm + lax.broadcasted_iota(jnp.int32, (tm, 1), 0)
        for g, win in enumerate(POOL_WINDOWS):
            cols = slice(g * GROUP, (g + 1) * GROUP)
            pooled = _pool_means(uext, pos, tm, g, win)
            mixed = jnp.dot(pooled.astype(BF16), pw_ref[g].astype(BF16),
                            preferred_element_type=F32) + pb_ref[g:g + 1, :]
            cat_ref[:, cols] = (mixed * ps_ref[:, cols]).astype(BF16)
        _shifted_copies(gext, gsh, HALO + tm)
        _tap_sum(gsh, cw_ref, [HALO - (CONV_WIDTH - 1) + k for k in range(CONV_WIDTH)], y_s, tm, cb_ref)
        y = y_s[...]
        mu = jnp.mean(y, axis=-1, keepdims=True)
        dv = y - mu
        rstd = lax.rsqrt(jnp.mean(dv * dv, axis=-1, keepdims=True) + EPS)
        ln = dv * rstd * lg_ref[...] + lb_ref[...]
        cat_ref[:, pc:] = (ln * _sigmoid(ln)).astype(BF16)
        hp_s[...] = h_ref[pl.ds(tm - HALO, HALO), :]

    small = lambda a: pl.BlockSpec(a.shape, lambda i: (0,) * a.ndim)
    return _call(
        body, name=name, grid=(t // tm,),
        in_specs=[pl.BlockSpec((tm, d), lambda i: (i, 0)), small(gamma), small(w),
                  small(pool_w), small(pool_b), small(pool_scale), small(conv_w), small(conv_b),
                  small(ln_g), small(ln_b)],
        out_specs=[pl.BlockSpec((tm, d), lambda i: (i, 0)), pl.BlockSpec((tm, hw), lambda i: (i, 0)),
                   pl.BlockSpec((tm, pc + cc), lambda i: (i, 0))],
        out_shape=[jax.ShapeDtypeStruct((t, d), BF16), jax.ShapeDtypeStruct((t, hw), F32),
                   jax.ShapeDtypeStruct((t, pc + cc), BF16)],
        scratch_shapes=[pltpu.VMEM((HALO + tm, pc), F32), pltpu.VMEM((HALO + tm + SUBLANES, cc), F32),
                        pltpu.VMEM((SUBLANES, HALO + tm, cc), F32), pltpu.VMEM((tm, cc), F32),
                        pltpu.VMEM((HALO, hw), F32)],
    )(x, gamma, w, pool_w, pool_b, pool_scale, conv_w, conv_b, ln_g, ln_b)


def _pool_conv_bwd(name, h, dcat, pool_w, pool_b, pool_scale, conv_w, conv_b, ln_g, ln_b, comm=None):
    t, hw = h.shape
    pc = len(POOL_WINDOWS) * GROUP
    cc = (hw - pc) // 2
    ng = len(POOL_WINDOWS)
    tm = _tile(t)
    per = tm // HALO
    nt = t // tm
    r2 = tm + HALO
    taps = CONV_WIDTH - 1

    def body(h_ref, hp_ref, hn_ref, dc_ref, dcn_ref, pw_ref, pb_ref, ps_ref, cw_ref, cb_ref, lg_ref, lb_ref,
             dh_ref, dpw_ref, dpb_ref, dps_ref, dcw_ref, dcb_ref, dlg_ref, dlb_ref,
             uext, gext, dcext, dqext, dycext, shifted, y_s, dg_s, dcw_acc):
        i = pl.program_id(0)

        @pl.when(i == 0)
        def _():
            for ref in (dpw_ref, dpb_ref, dps_ref, dcw_ref, dcb_ref, dlg_ref, dlb_ref, dcw_acc):
                ref[...] = jnp.zeros_like(ref)

        keep_p = (i > 0).astype(F32)
        keep_n = (i < nt - 1).astype(F32)
        hp = hp_ref[...] * keep_p
        hn = hn_ref[...] * keep_n
        uext[0:HALO, :] = hp[:, :pc]
        uext[HALO:, :] = h_ref[:, :pc]
        gext[0:HALO, :] = hp[:, pc:pc + cc] * _sigmoid(hp[:, pc + cc:])
        gext[pl.ds(HALO, tm), :] = h_ref[:, pc:pc + cc] * _sigmoid(h_ref[:, pc + cc:])
        gext[pl.ds(HALO + tm, HALO), :] = hn[:, pc:pc + cc] * _sigmoid(hn[:, pc + cc:])
        gext[pl.ds(HALO + tm + HALO, SUBLANES), :] = jnp.zeros((SUBLANES, cc), F32)
        dcext[0:tm, :] = dc_ref[...]
        dcext[pl.ds(tm, HALO), :] = dcn_ref[...] * keep_n

        pos = i * tm + lax.broadcasted_iota(jnp.int32, (tm, 1), 0)
        pos2 = i * tm + lax.broadcasted_iota(jnp.int32, (r2, 1), 0)
        for g, win in enumerate(POOL_WINDOWS):
            cols = slice(g * GROUP, (g + 1) * GROUP)
            wg = pw_ref[g].astype(BF16)
            dya = dcext[:, cols]
            dmixed = dya * ps_ref[:, cols]
            dpooled = lax.dot_general(dmixed.astype(BF16), wg, NT, preferred_element_type=F32)
            cnt2 = jnp.minimum(pos2 + 1, win).astype(F32)
            dqext[:, cols] = dpooled / cnt2
            du = -dpooled[0:tm]
            for j in range(win):
                du = du + dqext[pl.ds(j, tm), cols]
            dh_ref[:, cols] = du.astype(BF16)
            pooled = _pool_means(uext, pos, tm, g, win)
            pooled_b = pooled.astype(BF16)
            mixed = jnp.dot(pooled_b, wg, preferred_element_type=F32) + pb_ref[g:g + 1, :]
            dps_ref[:, cols] += jnp.sum(dya[0:tm] * mixed, axis=0, keepdims=True)
            dpb_ref[g:g + 1, :] += jnp.sum(dmixed[0:tm], axis=0, keepdims=True)
            dpw_ref[g] += lax.dot_general(pooled_b, dmixed[0:tm].astype(BF16), TN, preferred_element_type=F32)

        _shifted_copies(gext, shifted, HALO + tm + HALO)
        _tap_sum(shifted, cw_ref, [HALO - taps + k for k in range(CONV_WIDTH)], y_s, r2, cb_ref)
        y = y_s[...]
        mu = jnp.mean(y, axis=-1, keepdims=True)
        dv = y - mu
        rstd = lax.rsqrt(jnp.mean(dv * dv, axis=-1, keepdims=True) + EPS)
        norm = dv * rstd
        ln = norm * lg_ref[...] + lb_ref[...]
        sig = _sigmoid(ln)
        dln = dcext[:, pc:] * (sig * (1.0 + ln * (1.0 - sig)))
        dnorm = dln * lg_ref[...]
        dyc = rstd * (dnorm - jnp.mean(dnorm, axis=-1, keepdims=True)
                      - norm * jnp.mean(dnorm * norm, axis=-1, keepdims=True))
        dycext[pl.ds(0, r2), :] = dyc
        dycext[pl.ds(r2, SUBLANES), :] = jnp.zeros((SUBLANES, cc), F32)
        dlg_ref[...] += jnp.sum((dln * norm)[0:tm], axis=0, keepdims=True)
        dlb_ref[...] += jnp.sum(dln[0:tm], axis=0, keepdims=True)
        dcb_ref[...] += jnp.sum(dyc[0:tm], axis=0, keepdims=True)

        def fold(c, carry):
            r0 = pl.multiple_of(c * ROW_CHUNK, ROW_CHUNK)
            dchunk = dycext[pl.ds(r0, ROW_CHUNK), :]
            for k in range(CONV_WIDTH):
                a8, b8 = divmod(HALO - taps + k, SUBLANES)
                prod = dchunk * shifted[b8, pl.ds(r0 + SUBLANES * a8, ROW_CHUNK), :]
                part = prod[0:SUBLANES]
                for q in range(1, ROW_CHUNK // SUBLANES):
                    part = part + prod[q * SUBLANES:(q + 1) * SUBLANES]
                dcw_acc[k] += part
            return carry

        lax.fori_loop(0, tm // ROW_CHUNK, fold, 0)

        @pl.when(i == nt - 1)
        def _():
            for k in range(CONV_WIDTH):
                dcw_ref[k:k + 1, :] = jnp.sum(dcw_acc[k], axis=0, keepdims=True)

        _shifted_copies(dycext, shifted, r2)
        _tap_sum(shifted, cw_ref, [taps - k for k in range(CONV_WIDTH)], dg_s, tm)
        dg = dg_s[...]
        a = h_ref[:, pc:pc + cc]
        sg = _sigmoid(h_ref[:, pc + cc:])
        dh_ref[:, pc:pc + cc] = (dg * sg).astype(BF16)
        dh_ref[:, pc + cc:] = (dg * a * sg * (1.0 - sg)).astype(BF16)

    small = lambda a: pl.BlockSpec(a.shape, lambda i: (0,) * a.ndim)
    smalls = (pool_w, pool_b, pool_scale, conv_w, conv_b, ln_g, ln_b)
    return _call(
        body, comm=comm, name=name, grid=(nt,),
        in_specs=[pl.BlockSpec((tm, hw), lambda i: (i, 0)),
                  pl.BlockSpec((HALO, hw), lambda i: (jnp.maximum(i * per - 1, 0), 0)),
                  pl.BlockSpec((HALO, hw), lambda i: (jnp.minimum((i + 1) * per, t // HALO - 1), 0)),
                  pl.BlockSpec((tm, pc + cc), lambda i: (i, 0)),
                  pl.BlockSpec((HALO, pc + cc), lambda i: (jnp.minimum((i + 1) * per, t // HALO - 1), 0)),
                  ] + [small(a) for a in smalls],
        out_specs=[pl.BlockSpec((tm, hw), lambda i: (i, 0))] + [small(a) for a in smalls],
        out_shape=[jax.ShapeDtypeStruct((t, hw), BF16)] + [jax.ShapeDtypeStruct(a.shape, F32) for a in smalls],
        scratch_shapes=[pltpu.VMEM((HALO + tm, pc), F32), pltpu.VMEM((HALO + tm + HALO + SUBLANES, cc), F32),
                        pltpu.VMEM((r2, pc + cc), F32), pltpu.VMEM((r2, pc), F32),
                        pltpu.VMEM((r2 + SUBLANES, cc), F32), pltpu.VMEM((SUBLANES, HALO + tm + HALO, cc), F32),
                        pltpu.VMEM((r2, cc), F32), pltpu.VMEM((tm, cc), F32),
                        pltpu.VMEM((CONV_WIDTH + 1, SUBLANES, cc), F32)],
    )(h, h, h, dcat, dcat, *smalls)


SQRT_HALF = 0.7071067811865476
INV_SQRT_2PI = 0.3989422804014327


def _sgu_core(zp_ref, lg_ref, lb_ref, ws_ref, bs_ref, vo_s, tm, sc, heads):
    zp = zp_ref[...].astype(F32)
    cdf =0.5 * (1.0 + lax.erf(zp * SQRT_HALF))
    z = zp * cdf
    u = z[:, :sc]
    v = z[:, sc:]
    mu = jnp.mean(v, axis=-1, keepdims=True)
    dv = v - mu
    rstd = lax.rsqrt(jnp.mean(dv * dv, axis=-1, keepdims=True) + EPS)
    norm = dv * rstd
    vb = (norm * lg_ref[...] + lb_ref[...]).astype(BF16)
    row = lax.broadcasted_iota(jnp.int32, (GROUP, GROUP), 0)
    col = lax.broadcasted_iota(jnp.int32, (GROUP, GROUP), 1)
    mask = (col <= row).astype(F32)
    wm = [ws_ref[hd] * mask for hd in range(heads)]
    for hd in range(heads):
        cols = slice(hd * GROUP, (hd + 1) * GROUP)
        wb = wm[hd].astype(BF16)
        for n in range(tm // GROUP):
            rows = slice(n * GROUP, (n + 1) * GROUP)
            vo_s[rows, cols] = jnp.dot(wb, vb[rows, cols], preferred_element_type=F32) + bs_ref[hd]
    return (zp, cdf), u, norm, rstd, vb, wm, mask


def _sgu_fwd(name, x, gamma, w, ln_g, ln_b, w_s, b_s):
    t, d = x.shape
    two_sc = w.shape[1]
    sc = two_sc // 2
    heads = sc // GROUP
    tm = _tile(t)

    def body(x_ref, gam_ref, w_ref, lg_ref, lb_ref, ws_ref, bs_ref, xn_ref, zp_ref, q_ref, vo_s):
        xv = x_ref[...]
        r = lax.rsqrt(jnp.mean(xv * xv, axis=-1, keepdims=True) + EPS)
        xn = (xv * r * gam_ref[...]).astype(BF16)
        xn_ref[...] = xn
        zp_ref[...] = jnp.dot(xn, w_ref[...], preferred_element_type=F32).astype(BF16)
        _, u, _, _, _, _, _ = _sgu_core(zp_ref, lg_ref, lb_ref, ws_ref, bs_ref, vo_s, tm, sc, heads)
        q_ref[...] = (u * vo_s[...]).astype(BF16)

    small = lambda a: pl.BlockSpec(a.shape, lambda i: (0,) * a.ndim)
    return _call(
        body, name=name, grid=(t // tm,),
        in_specs=[pl.BlockSpec((tm, d), lambda i: (i, 0)), small(gamma), small(w),
                  small(ln_g), small(ln_b), small(w_s), small(b_s)],
        out_specs=[pl.BlockSpec((tm, d), lambda i: (i, 0)), pl.BlockSpec((tm, two_sc), lambda i: (i, 0)),
                   pl.BlockSpec((tm, sc), lambda i: (i, 0))],
        out_shape=[jax.ShapeDtypeStruct((t, d), BF16), jax.ShapeDtypeStruct((t, two_sc), BF16),
                   jax.ShapeDtypeStruct((t, sc), BF16)],
        scratch_shapes=[pltpu.VMEM((tm, sc), F32)],
    )(x, gamma, w, ln_g, ln_b, w_s, b_s)


def _sgu_bwd(name, zp, dy, w_out, ln_g, ln_b, w_s, b_s):
    t, two_sc = zp.shape
    d = dy.shape[1]
    sc = two_sc // 2
    heads = sc // GROUP
    tm = _tile(t)
    nt = t // tm

    def body(zp_ref, dy_ref, wo_ref, lg_ref, lb_ref, ws_ref, bs_ref,
             dzp_ref, dyb_ref, dlg_ref, dlb_ref, dws_ref, dbs_ref, vo_s, dvl_s, dws_acc):
        i = pl.program_id(0)

        @pl.when(i == 0)
        def _():
            dlg_ref[...] = jnp.zeros_like(dlg_ref)
            dlb_ref[...] = jnp.zeros_like(dlb_ref)
            dbs_ref[...] = jnp.zeros_like(dbs_ref)
            dws_acc[...] = jnp.zeros_like(dws_acc)

        dyb = dy_ref[...].astype(BF16)
        dyb_ref[...] = dyb
        dq = lax.dot_general(dyb, wo_ref[...], NT, preferred_element_type=F32)
        (zp, cdf), u, norm, rstd, vb, wm, mask = _sgu_core(zp_ref, lg_ref, lb_ref, ws_ref, bs_ref, vo_s, tm, sc, heads)
        du = dq * vo_s[...]
        dvo = dq * u
        dvob = dvo.astype(BF16)
        for hd in range(heads):
            cols = slice(hd * GROUP, (hd + 1) * GROUP)
            wtb = jnp.transpose(wm[hd]).astype(BF16)
            for n in range(tm // GROUP):
                rows = slice(n * GROUP, (n + 1) * GROUP)
                blk = dvob[rows, cols]
                dws_acc[hd] += lax.dot_general(blk, vb[rows, cols], NT, preferred_element_type=F32)
                dvl_s[rows, cols] = jnp.dot(wtb, blk, preferred_element_type=F32)
                dbs_ref[hd] += jnp.sum(dvo[rows, cols], axis=-1, keepdims=True)
        dvl = dvl_s[...]
        dlg_ref[...] += jnp.sum(dvl * norm, axis=0, keepdims=True)
        dlb_ref[...] += jnp.sum(dvl, axis=0, keepdims=True)
        dnorm = dvl * lg_ref[...]
        dv = rstd * (dnorm - jnp.mean(dnorm, axis=-1, keepdims=True)
                     - norm * jnp.mean(dnorm * norm, axis=-1, keepdims=True))
        dgelu = cdf + zp * (INV_SQRT_2PI * jnp.exp(-0.5 * zp * zp))
        dzp_ref[:, :sc] = (du * dgelu[:, :sc]).astype(BF16)
        dzp_ref[:, sc:] = (dv * dgelu[:, sc:]).astype(BF16)

        @pl.when(i == nt - 1)
        def _():
            for hd in range(heads):
                dws_ref[hd] = dws_acc[hd] * mask

    small = lambda a: pl.BlockSpec(a.shape, lambda i: (0,) * a.ndim)
    smalls = (ln_g, ln_b, w_s, b_s)
    return _call(
        body, name=name, grid=(nt,),
        in_specs=[pl.BlockSpec((tm, two_sc), lambda i: (i, 0)), pl.BlockSpec((tm, d), lambda i: (i, 0)),
                  small(w_out)] + [small(a) for a in smalls],
        out_specs=[pl.BlockSpec((tm, two_sc), lambda i: (i, 0)), pl.BlockSpec((tm, d), lambda i: (i, 0))]
                  + [small(a) for a in smalls],
        out_shape=[jax.ShapeDtypeStruct((t, two_sc), BF16), jax.ShapeDtypeStruct((t, d), BF16)]
                  + [jax.ShapeDtypeStruct(a.shape, F32) for a in smalls],
        scratch_shapes=[pltpu.VMEM((tm, sc), F32), pltpu.VMEM((tm, sc), F32), pltpu.VMEM(w_s.shape, F32)],
    )(zp, dy, w_out, ln_g, ln_b, w_s, b_s)


def _adamw(name, parts, w, m, v, rows):
    l_n, r_n, c_n = w.shape
    s_n = parts[0].shape[0]
    tr = min(rows, r_n)
    nr = r_n // tr
    c1 = 1.0 - ADAM_B1 ** ADAM_STEP
    c2 = 1.0 - ADAM_B2 ** ADAM_STEP

    def body(*refs):
        p_refs = refs[:l_n]
        w_ref, m_ref, v_ref, g_ref, d_ref, mo_ref, vo_ref = refs[l_n:]
        layer = pl.program_id(0)

        def update(p_ref):
            g = p_ref[0].astype(F32)
            for s in range(1, s_n):
                g = g + p_ref[s].astype(F32)
            mn = ADAM_B1 * m_ref[...] + (1.0 - ADAM_B1) * g
            vn = ADAM_B2 * v_ref[...] + (1.0 - ADAM_B2) * (g * g)
            m_hat = mn / c1
            v_hat = vn / c2
            g_ref[...] = g
            d_ref[...] = -ADAM_LR * (m_hat / (jnp.sqrt(v_hat) + ADAM_EPS) + ADAM_WD * w_ref[...])
            mo_ref[...] = mn
            vo_ref[...] = vn

        for j in range(l_n):
            pl.when(layer == j)(functools.partial(update, p_refs[j]))

    def part_spec(j):
        return pl.BlockSpec((s_n, tr, c_n), lambda l, i: (0, jnp.where(l == j, i, jnp.where(l < j, 0, nr - 1)), 0))

    blk = pl.BlockSpec((None, tr, c_n), lambda l, i: (l, i, 0))
    return _call(
        body, name=name, grid=(l_n, nr),
        in_specs=[part_spec(j) for j in range(l_n)] + [blk, blk, blk],
        out_specs=[blk] * 4,
        out_shape=[jax.ShapeDtypeStruct((l_n, r_n, c_n), F32)] * 4,
    )(*parts, w, m, v)


def _local_step(x, target, big, small, sched=None):
    t, d = x.shape
    n_layers = small["ffn1_norm"].shape[0]
    gb = {}
    gs = {}

    def row(a, l):
        return a[l:l + 1]

    def run(fn, name, *operands, **kw):
        comm = sched.plan(name, gb, gs) if sched is not None else None
        if comm is None:
            return fn(name, *operands, **kw)
        res, got = fn(name, *operands, comm=comm, **kw)
        sched.deliver(comm, got)
        return res

    saved = []
    xs = x
    for l in range(n_layers):
        rec = {"x_ffn1": xs}
        xs, rec["xn_ffn1"], rec["gu_ffn1"] = run(
            _ffn_fwd, f"ffn1_fwd_l{l}", xs, row(small["ffn1_norm"], l), big["ffn1_w_in", l], big["ffn1_w_out", l])
        rec["x_mix"] = xs
        if l % 2 == 0:
            rec["xn_mix"], rec["h"], rec["cat"] = _pool_conv_fwd(
                f"pool_conv_fwd_l{l}", xs, row(small["mix_norm"], l), big["ab_w_in"], small["pool_w"],
                small["pool_b"], small["pool_scale"], small["conv_w"], small["conv_b"],
                small["conv_ln_g"], small["conv_ln_b"])
            pre = (rec["cat"], big["ab_w_out"])
        else:
            rec["xn_mix"], rec["zp"], rec["q"] = _sgu_fwd(
                f"sgu_fwd_l{l}", xs, row(small["mix_norm"], l), big["sgu_w_in"], small["sgu_ln_g"],
                small["sgu_ln_b"], small["sgu_w"], small["sgu_b"])
            pre = (rec["q"], big["sgu_w_out"])
        last = l == n_layers - 1
        res = run(_ffn_fwd, f"ffn2_fwd_l{l}", xs, row(small["ffn2_norm"], l), big["ffn2_w_in", l],
                  big["ffn2_w_out", l], pre=pre, head=(small["final_norm"], target) if last else None)
        if last:
            rec["x_ffn2"], loss, dx, gs["final_norm"], rec["xn_ffn2"], rec["gu_ffn2"] = res
        else:
            rec["x_ffn2"], xs, rec["xn_ffn2"], rec["gu_ffn2"] = res
        saved.append(rec)

    norm_rows = {"ffn1_norm": [None] * n_layers, "mix_norm": [None] * n_layers, "ffn2_norm": [None] * n_layers}

    def ffn_backward(tag, l, dy, rec):
        gamma = row(small[f"{tag}_norm"], l)
        weights_first = (tag, l) == ("ffn1", 0)
        if weights_first:
            dyh, hh, dgu = _ffn_bwd_hidden(f"{tag}_bwd_hidden_l{l}", dy, rec[f"gu_{tag}"], big[f"{tag}_w_out", l])
        else:
            dx, dgam, dyh, hh, dgu = run(_ffn_bwd, f"{tag}_bwd_l{l}", dy, rec[f"x_{tag}"], gamma,
                                         rec[f"gu_{tag}"], big[f"{tag}_w_in", l], big[f"{tag}_w_out", l])
        gb[f"{tag}_w_out", l] = run(_matmul_tn, f"{tag}_dwout_l{l}", hh, dyh,
                                    a_split=HIDDEN_SPLIT).reshape(N_DEV, -1, d)
        gb[f"{tag}_w_in", l] = run(_matmul_tn, f"{tag}_dwin_l{l}", dgu, rec[f"xn_{tag}"],
                                   a_split=2 * HIDDEN_SPLIT).reshape(N_DEV, -1, d)
        if weights_first:
            dx, dgam = run(_matmul_nt_rms_bwd, f"{tag}_dx_l{l}", dgu, big[f"{tag}_w_in", l], dy, rec[f"x_{tag}"],
                           gamma, w_rows_are_k=True, tile=FFN_TILE)
        norm_rows[f"{tag}_norm"][l] = dgam
        return dx

    for l in reversed(range(n_layers)):
        rec = saved[l]
        dx = ffn_backward("ffn2", l, dx, rec)
        if l % 2 == 0:
            dcat, dxb = _matmul_nt(f"ab_out_bwd_l{l}", dx, big["ab_w_out"])
            gb["ab_w_out", 0] = _matmul_tn(f"ab_dwout_l{l}", rec["cat"], dxb)
            dh, gs["pool_w"], gs["pool_b"], gs["pool_scale"], gs["conv_w"], gs["conv_b"], gs["conv_ln_g"], \
                gs["conv_ln_b"] = run(
                    _pool_conv_bwd, f"pool_conv_bwd_l{l}", rec["h"], dcat, small["pool_w"], small["pool_b"],
                    small["pool_scale"], small["conv_w"], small["conv_b"], small["conv_ln_g"], small["conv_ln_b"])
            gb["ab_w_in", 0] = _matmul_tn(f"ab_dwin_l{l}", rec["xn_mix"], dh)
            dx, dgam = _matmul_nt_rms_bwd(f"ab_in_bwd_l{l}", dh, big["ab_w_in"], dx, rec["x_mix"],
                                          row(small["mix_norm"], l))
        else:
            dzp, dxb, gs["sgu_ln_g"], gs["sgu_ln_b"], gs["sgu_w"], gs["sgu_b"] = _sgu_bwd(
                f"sgu_bwd_l{l}", rec["zp"], dx, big["sgu_w_out"], small["sgu_ln_g"], small["sgu_ln_b"],
                small["sgu_w"], small["sgu_b"])
            gb["sgu_w_out", 0] = _matmul_tn(f"sgu_dwout_l{l}", rec["q"], dxb)
            gb["sgu_w_in", 0] = _matmul_tn(f"sgu_dwin_l{l}", rec["xn_mix"], dzp)
            dx, dgam = _matmul_nt_rms_bwd(f"sgu_in_bwd_l{l}", dzp, big["sgu_w_in"], dx, rec["x_mix"],
                                          row(small["mix_norm"], l))
        norm_rows["mix_norm"][l] = dgam
        dx = ffn_backward("ffn1", l, dx, rec)

    for k, rows in norm_rows.items():
        gs[k] = jnp.concatenate(rows, axis=0)
    return loss, dx, gb, gs


SHARDED_SMALL = ("conv_w", "sgu_ln_g", "sgu_ln_b")
WEIGHTS = ("ffn1_norm", "ffn1_w_in", "ffn1_w_out", "mix_norm", "ffn2_norm", "ffn2_w_in", "ffn2_w_out", "ab_w_in",
           "pool_w", "pool_b", "pool_scale", "conv_w", "conv_b", "conv_ln_g", "conv_ln_b", "ab_w_out", "sgu_w_in",
           "sgu_ln_g", "sgu_ln_b", "sgu_w", "sgu_b", "sgu_w_out", "final_norm")
LANES = 128


def _interleave_cols(g):
    n, k, c = g.shape
    return jnp.transpose(g, (1, 0, 2)).reshape(k, n * c)


def _split_cols(a):
    k, nc = a.shape
    return jnp.transpose(a.reshape(k, N_DEV, nc // N_DEV), (1, 0, 2))


def _as3(a):
    if a.ndim == 1:
        return a.reshape(1, 1, -1)
    if a.ndim == 2:
        return a.reshape(a.shape[0], 1, a.shape[1])
    return a.reshape(a.shape[0], -1, a.shape[-1])


def _pack_rows(a):
    flat = a.reshape(-1)
    pad = (-flat.shape[0]) % (8 * LANES)
    if pad:
        flat = jnp.concatenate([flat, jnp.zeros((pad,), flat.dtype)])
    return flat.reshape(-1, LANES)


FIRST_GATHER = (("ffn1_w_in", 0), ("ffn1_w_out", 0), ("conv_w", 0), ("sgu_ln_g", 0), ("sgu_ln_b", 0))
GATHER_PLAN = {
    "ffn1_fwd_l0": (("ab_w_in", 0), ("ab_w_out", 0), ("ffn2_w_in", 0), ("ffn2_w_out", 0)),
    "ffn2_fwd_l0": (("ffn1_w_in", 1), ("ffn1_w_out", 1), ("sgu_w_in", 0), ("sgu_w_out", 0)),
    "ffn1_fwd_l1": (("ffn2_w_in", 1), ("ffn2_w_out", 1)),
}
SCATTER_PLAN = {
    "ffn2_dwin_l1": (("ffn2_w_out", 1),),
    "ffn1_bwd_l1": (("ffn2_w_in", 1), ("sgu_w_out", 0)),
    "ffn1_dwout_l1": (("sgu_w_in", 0), ("sgu_ln_g", 0), ("sgu_ln_b", 0)),
    "ffn1_dwin_l1": (("ffn1_w_out", 1),),
    "ffn2_bwd_l0": (("ffn1_w_in", 1), ("pack", 0)),
    "ffn2_dwin_l0": (("ffn2_w_out", 0),),
    "pool_conv_bwd_l0": (("ffn2_w_in", 0), ("ab_w_out", 0)),
    "ffn1_dwout_l0": (("ab_w_in", 0), ("conv_w", 0)),
    "ffn1_dwin_l0": (("ffn1_w_out", 0), ("pack", 1)),
    "ffn1_dx_l0": (("ffn1_w_in", 0),),
}
PACK_GROUPS = (("sgu_w", "sgu_b"),
               ("pool_w", "pool_b", "pool_scale", "conv_b", "conv_ln_g", "conv_ln_b"),
               ("ffn1_norm", "mix_norm", "ffn2_norm", "final_norm"))
REPLICATED = tuple(n for group in PACK_GROUPS for n in group)


class _Schedule:
    def __init__(self, shards, big, small):
        self.shards, self.big, self.small = shards, big, small
        self.recv = {}
        self.packs = {}
        self.pending = {}

    def gather_comm(self, keys):
        comm = _Comm()
        for key in keys:
            comm.gather(*self.shards[key])
        self.pending[id(comm)] = ("gather", keys)
        return comm

    def scatter_comm(self, keys, gb, gs):
        comm = _Comm()
        for name, l in keys:
            if name == "pack":
                comm.gather(jnp.concatenate([_pack_rows(gs[n]) for n in PACK_GROUPS[l]], axis=0))
                continue
            if name in ("ab_w_in", "sgu_w_in"):
                send = _split_cols(gb[name, l][0])
            elif name in ("ab_w_out", "sgu_w_out"):
                send = gb[name, l][0]
                send = send.reshape(N_DEV, -1, send.shape[-1])
            elif name == "conv_w":
                send = _split_cols(gs[name][:CONV_WIDTH])
            elif name in ("sgu_ln_g", "sgu_ln_b"):
                send = gs[name].reshape(N_DEV, 1, -1)
            else:
                send = gb[name, l]
            comm.scatter(send)
        self.pending[id(comm)] = ("scatter", keys)
        return comm

    def plan(self, name, gb, gs):
        if name in GATHER_PLAN:
            return self.gather_comm(GATHER_PLAN[name])
        if name in SCATTER_PLAN:
            return self.scatter_comm(SCATTER_PLAN[name], gb, gs)
        return None

    def deliver(self, comm, got):
        kind, keys = self.pending.pop(id(comm))
        for (name, l), arr in zip(keys, got):
            if name == "pack":
                self.packs[l] = arr
            elif kind == "scatter":
                self.recv[name, l] = arr
            elif name in ("ffn1_w_in", "ffn2_w_in"):
                self.big[name, l] = arr.reshape(-1, arr.shape[-1])
            elif name in ("ffn1_w_out", "ffn2_w_out"):
                self.big[name, l] = arr.reshape(-1, arr.shape[-1])
            elif name in ("ab_w_in", "sgu_w_in"):
                self.big[name] = _interleave_cols(arr)
            elif name in ("ab_w_out", "sgu_w_out"):
                self.big[name] = arr.reshape(-1, arr.shape[-1])
            elif name == "conv_w":
                self.small[name] = jnp.pad(_interleave_cols(arr), ((0, 1), (0, 0)))
            else:
                self.small[name] = arr.reshape(1, -1)


def kernel(x, ffn1_norm, ffn1_w_in, ffn1_w_out, mix_norm, ffn2_norm, ffn2_w_in, ffn2_w_out, ab_w_in, pool_w, pool_b, pool_scale, conv_w, conv_b, conv_ln_g, conv_ln_b, ab_w_out, sgu_w_in, sgu_ln_g, sgu_ln_b, sgu_w, sgu_b, sgu_w_out, final_norm, loss_target, m_ffn1_norm, m_ffn1_w_in, m_ffn1_w_out, m_mix_norm, m_ffn2_norm, m_ffn2_w_in, m_ffn2_w_out, m_ab_w_in, m_pool_w, m_pool_b, m_pool_scale, m_conv_w, m_conv_b, m_conv_ln_g, m_conv_ln_b, m_ab_w_out, m_sgu_w_in, m_sgu_ln_g, m_sgu_ln_b, m_sgu_w, m_sgu_b, m_sgu_w_out, m_final_norm, v_ffn1_norm, v_ffn1_w_in, v_ffn1_w_out, v_mix_norm, v_ffn2_norm, v_ffn2_w_in, v_ffn2_w_out, v_ab_w_in, v_pool_w, v_pool_b, v_pool_scale, v_conv_w, v_conv_b, v_conv_ln_g, v_conv_ln_b, v_ab_w_out, v_sgu_w_in, v_sgu_ln_g, v_sgu_ln_b, v_sgu_w, v_sgu_b, v_sgu_w_out, v_final_norm):
    args = dict(locals())
    w = {n: args[n] for n in WEIGHTS}
    m = {n: args["m_" + n] for n in WEIGHTS}
    v = {n: args["v_" + n] for n in WEIGHTS}
    n_layers = ffn1_norm.shape[0]

    shards = {}
    for n in ("ffn1_w_in", "ffn2_w_in"):
        wt = jnp.swapaxes(w[n], 1, 2).astype(BF16)
        for l in range(n_layers):
            shards[n, l] = (wt, l)
    for n in ("ffn1_w_out", "ffn2_w_out"):
        wb = w[n].astype(BF16)
        for l in range(n_layers):
            shards[n, l] = (wb, l)
    for n in ("ab_w_in", "ab_w_out", "sgu_w_in", "sgu_w_out"):
        shards[n, 0] = (w[n][0].astype(BF16), None)
    shards["conv_w", 0] = (conv_w[0], None)
    shards["sgu_ln_g", 0] = (sgu_ln_g, None)
    shards["sgu_ln_b", 0] = (sgu_ln_b, None)

    big = {}
    small = {
        "ffn1_norm": ffn1_norm, "mix_norm": mix_norm, "ffn2_norm": ffn2_norm, "final_norm": final_norm.reshape(1, -1),
        "pool_w": pool_w[0], "pool_b": pool_b[0], "pool_scale": pool_scale,
        "conv_b": conv_b, "conv_ln_g": conv_ln_g, "conv_ln_b": conv_ln_b,
        "sgu_w": sgu_w[0], "sgu_b": sgu_b[0][:, :, None],
    }
    sched = _Schedule(shards, big, small)
    first = sched.gather_comm(FIRST_GATHER)
    sched.deliver(first, _exchange("gather_first", first))

    loss, grad_x, gb, gs = _local_step(x[0], loss_target[0], big, small, sched)

    last_pack = jnp.concatenate([_pack_rows(gs[n]) for n in PACK_GROUPS[-1]] + [_pack_rows(loss)], axis=0)
    last = _Comm()
    last.gather(last_pack)
    packed_all = jnp.concatenate([sched.packs[0], sched.packs[1], _exchange("reduce_last", last)[0]], axis=1)
    offsets = [0]
    for n in REPLICATED:
        offsets.append(offsets[-1] + SUBLANES * (-(-w[n].size // (SUBLANES * LANES))))
    offsets.append(offsets[-1] + SUBLANES)
    n_rows = offsets[-1]
    recv = sched.recv

    out = {}
    for n in ("ffn1_w_in", "ffn2_w_in"):
        res = _adamw(f"adamw_{n}", [recv[n, l] for l in range(n_layers)], jnp.swapaxes(w[n], 1, 2),
                     jnp.swapaxes(m[n], 1, 2), jnp.swapaxes(v[n], 1, 2), 176)
        out[n] = [jnp.swapaxes(r, 1, 2) for r in res]
    for n in ("ffn1_w_out", "ffn2_w_out"):
        out[n] = _adamw(f"adamw_{n}", [recv[n, l] for l in range(n_layers)], w[n], m[n], v[n], 176)
    for n in ("ab_w_in", "ab_w_out", "sgu_w_in", "sgu_w_out") + SHARDED_SMALL:
        w3 = _as3(w[n])
        parts = recv[n, 0].reshape((N_DEV,) + w3.shape[1:])
        res = _adamw(f"adamw_{n}", [parts], w3, _as3(m[n]), _as3(v[n]), 512)
        out[n] = [r.reshape(w[n].shape) for r in res]

    def pack_rep(src):
        tail = [jnp.zeros((offsets[-1] - offsets[-2], LANES), F32)]
        return jnp.concatenate([_pack_rows(src[n]) for n in REPLICATED] + tail, axis=0)[None]

    res = _adamw("adamw_replicated", [packed_all], pack_rep(w), pack_rep(m), pack_rep(v), n_rows)
    for i, n in enumerate(REPLICATED):
        size = w[n].size
        out[n] = [r[0, offsets[i]:offsets[i + 1]].reshape(-1)[:size].reshape(w[n].shape) for r in res]
    loss_sum = res[0][0, offsets[-2], 0]

    return (loss_sum, grad_x[None],
            *[out[n][0] for n in WEIGHTS], *[out[n][1] for n in WEIGHTS],
            *[out[n][2] for n in WEIGHTS], *[out[n][3] for n in WEIGHTS])
```

```python
import functools

import jax
import jax.numpy as jnp
from jax import lax
from jax.experimental import pallas as pl
from jax.experimental.pallas import tpu as pltpu

F32 = jnp.float32
BF16 = jnp.bfloat16
EPS = 1e-6
N_DEV = 8
POOL_WINDOWS = (2, 4, 8, 16)
CONV_WIDTH = 31
HALO = 32
GROUP = 128
SUBLANES = 8
ROW_CHUNK = 32
SUBLANES = 8
ROW_CHUNK = 32
TOKEN_TILE = 512
CONTRACT_TILE = 2048
HIDDEN_SPLIT = 2
FFN_TILE = 256
ADAM_LR, ADAM_B1, ADAM_B2, ADAM_EPS, ADAM_WD, ADAM_STEP = 0.001, 0.9, 0.999, 1e-08, 0.01, 10
VMEM_LIMIT = 56 * 1024 * 1024

NT = (((1,), (1,)), ((), ()))
TN = (((0,), (0,)), ((), ()))


def _pallas(body, side_effects, **kw):
    params = pltpu.CompilerParams(vmem_limit_bytes=VMEM_LIMIT, has_side_effects=side_effects)
    return pl.pallas_call(body, compiler_params=params, **kw)


def _call(body, comm=None, **kw):
    if comm is None:
        return _pallas(body, False, **kw)
    in_specs = list(kw.pop("in_specs"))
    out_specs = kw.pop("out_specs")
    out_shape = kw.pop("out_shape")
    scratch = list(kw.pop("scratch_shapes", []))
    single = not isinstance(out_shape, (list, tuple))
    if single:
        out_specs, out_shape = [out_specs], [out_shape]
    n_in, n_out, n_scr = len(in_specs), len(out_shape), len(scratch)
    n_ci, n_co = len(comm.inputs), len(comm.out_shapes)
    grid = tuple(kw.get("grid", ()))

    def wrapped(*refs):
        pos = 0
        parts = []
        for n in (n_in, n_ci, n_out, n_co, n_scr, 3):
            parts.append(refs[pos:pos + n])
            pos += n
        a_in, c_in, a_out, c_out, a_scr, sems = parts
        if grid:
            step = 0
            for ax, g in enumerate(grid):
                step = step * g + pl.program_id(ax)
            total = functools.reduce(lambda p, q: p * q, grid)
            pl.when(step == 0)(lambda: comm.start(c_in, c_out, sems))
            body(*a_in, *a_out, *a_scr)
            pl.when(step == (3 * total) // 4)(lambda: comm.forward(c_in, c_out, sems))
            pl.when(step == total - 1)(lambda: comm.finish(c_in, c_out, sems))
        else:
            comm.start(c_in, c_out, sems)
            body(*a_in, *a_out, *a_scr)
            comm.forward(c_in, c_out, sems)
            comm.finish(c_in, c_out, sems)

    hbm = pl.BlockSpec(memory_space=pl.ANY)
    fn = _pallas(wrapped, True, in_specs=in_specs + [hbm] * n_ci, out_specs=list(out_specs) + [hbm] * n_co,
                 out_shape=list(out_shape) + list(comm.out_shapes), scratch_shapes=scratch + comm.semaphores(), **kw)

    def run(*operands):
        outs = fn(*operands, *comm.inputs)
        res = outs[:n_out]
        return (res[0] if single else res), outs[n_out:]

    return run


class _Comm:
    def __init__(self):
        self.inputs, self.sel, self.kinds, self.out_shapes = [], [], [], []

    def gather(self, arr, sel=None):
        block = arr.shape if sel is None else arr.shape[1:]
        self.inputs.append(arr)
        self.sel.append(sel)
        self.kinds.append("gather")
        self.out_shapes.append(jax.ShapeDtypeStruct((N_DEV,) + tuple(block), arr.dtype))
        return len(self.inputs) - 1

    def scatter(self, arr):
        self.inputs.append(arr)
        self.sel.append(None)
        self.kinds.append("scatter")
        self.out_shapes.append(jax.ShapeDtypeStruct(arr.shape, arr.dtype))
        return len(self.inputs) - 1

    def semaphores(self):
        n = len(self.inputs)
        return [pltpu.SemaphoreType.DMA((n, N_DEV - 1)), pltpu.SemaphoreType.DMA((n, N_DEV - 1)),
                pltpu.SemaphoreType.DMA((n,))]

    def _copies(self, ins, outs, sems, with_passed=True):
        send_sems, recv_sems, local_sems = sems
        x, y, c = lax.axis_index("x"), lax.axis_index("y"), lax.axis_index("c")
        me = 4 * x + 2 * y + c
        sibling = (x, y, 1 - c)
        chips = [(1 - x, y), (x, 1 - y), (1 - x, 1 - y)]
        items = []
        for a, kind in enumerate(self.kinds):
            def remote(src, dst, k, to, a=a):
                return pltpu.make_async_remote_copy(
                    src_ref=src, dst_ref=dst, send_sem=send_sems.at[a, k], recv_sem=recv_sems.at[a, k],
                    device_id=to, device_id_type=pl.DeviceIdType.MESH)
            if kind == "gather":
                src = ins[a] if self.sel[a] is None else ins[a].at[self.sel[a]]
                mine = outs[a].at[me]
                local = pltpu.make_async_copy(src, mine, local_sems.at[a])
                first = [remote(src, mine, 0, sibling)]
                first += [remote(src, mine, 1 + j, (*chip, c)) for j, chip in enumerate(chips)]
                passed = []
                for j, chip in enumerate(chips if with_passed else []):
                    got = outs[a].at[4 * chip[0] + 2 * chip[1] + c]
                    passed.append(remote(got, got, 4 + j, sibling))
            else:
                local = pltpu.make_async_copy(ins[a].at[me], outs[a].at[me], local_sems.at[a])
                first, passed = [], []
                for k in (1, 4, 2, 6, 5, 3, 7):
                    peer = ((1 - x) if k & 4 else x, (1 - y) if k & 2 else y, (1 - c) if k & 1 else c)
                    pid = 4 * peer[0] + 2 * peer[1] + peer[2]
                    first.append(remote(ins[a].at[pid], outs[a].at[me], k - 1, peer))
            items.append((local, first, passed))
        return items

    def start(self, ins, outs, sems):
        for local, first, _ in self._copies(ins, outs, sems, with_passed=False):
            local.start()
            for cp in first:
                cp.start()

    def forward(self, ins, outs, sems):
        for _, first, passed in self._copies(ins, outs, sems):
            for j, cp in enumerate(passed):
                first[1 + j].wait_recv()
                cp.start()

    def finish(self, ins, outs, sems):
        for local, first, passed in self._copies(ins, outs, sems):
            if passed:
                first[0].wait_recv()
                for cp in passed:
                    cp.wait_recv()
                for cp in first + passed:
                    cp.wait_send()
            else:
                for cp in first:
                    cp.wait()
            local.wait()


def _exchange(name, comm):
    _, outs = _call(lambda: None, comm=comm, name=name, in_specs=[], out_specs=[], out_shape=[])()
    return outs


def _sigmoid(x):
    return 0.5 * jnp.tanh(0.5 * x) + 0.5


def _tile(t):
    return min(TOKEN_TILE, t)


def _load_weights(pairs, sems):
    @pl.when(pl.program_id(0) == 0)
    def _():
        copies = [pltpu.make_async_copy(src, dst, sems.at[n]) for n, (src, dst) in enumerate(pairs)]
        for n, cp in enumerate(copies):
            cp.start(priority=n % 2)
        for cp in copies:
            cp.wait()


def _ffn_fwd(name, x, gamma, win_t, wout, pre=None, head=None, comm=None):
    t, d = x.shape
    f = wout.shape[0]
    fc = f // HIDDEN_SPLIT
    tm = min(FFN_TILE, t)
    extra = (() if pre is None else tuple(pre)) + (() if head is None else tuple(head))

    def body(*refs):
        x_ref, gam_ref, win_hbm, wo_hbm = refs[:4]
        rest = list(refs[4:])
        a_ref, wp_ref = (rest.pop(0), rest.pop(0)) if pre is not None else (None, None)
        gf_ref, tg_ref = (rest.pop(0), rest.pop(0)) if head is not None else (None, None)
        xin_ref = rest.pop(0) if pre is not None else None
        if head is None:
            xo_ref = rest.pop(0)
        else:
            loss_ref, dx_ref, dgf_ref = rest.pop(0), rest.pop(0), rest.pop(0)
        xn_ref, gu_ref, win_v, wo_v, h_s, sems = rest
        _load_weights([(win_hbm, win_v), (wo_hbm, wo_v)], sems)
        xv = x_ref[...]
        if pre is not None:
            xv = xv + jnp.dot(a_ref[...], wp_ref[...], preferred_element_type=F32)
            xin_ref[...] = xv
        r = lax.rsqrt(jnp.mean(xv * xv, axis=-1, keepdims=True) + EPS)
        xn = (xv * r * gam_ref[...]).astype(BF16)
        xn_ref[...] = xn
        for c in range(HIDDEN_SPLIT):
            lo = c * fc
            g = lax.dot_general(xn, win_v[lo:lo + fc, :], NT, preferred_element_type=F32)
            u = lax.dot_general(xn, win_v[f + lo:f + lo + fc, :], NT, preferred_element_type=F32)
            gu_ref[:, lo:lo + fc] = g.astype(BF16)
            gu_ref[:, f + lo:f + lo + fc] = u.astype(BF16)
            h_s[:, lo:lo + fc] = (g * _sigmoid(g) * u).astype(BF16)
        xo = xv + 0.5 * jnp.dot(h_s[...], wo_v[...], preferred_element_type=F32)
        if head is None:
            xo_ref[...] = xo
        else:
            @pl.when(pl.program_id(0) == 0)
            def _():
                loss_ref[...] = jnp.zeros_like(loss_ref)
                dgf_ref[...] = jnp.zeros_like(dgf_ref)

            ro = lax.rsqrt(jnp.mean(xo * xo, axis=-1, keepdims=True) + EPS)
            yv = xo * ro
            err = yv * gf_ref[...] - tg_ref[...]
            loss_ref[...] += (0.5 / d) * jnp.sum(jnp.sum(err * err, axis=-1, keepdims=True), axis=0, keepdims=True)
            dout = err * (1.0 / d)
            dgf_ref[...] += jnp.sum(dout * yv, axis=0, keepdims=True)
            dyn = dout * gf_ref[...]
            dx_ref[...] = ro * (dyn - yv * jnp.mean(dyn * yv, axis=-1, keepdims=True))

    hbm = pl.BlockSpec(memory_space=pl.ANY)
    row = pl.BlockSpec((tm, d), lambda i: (i, 0))
    one = lambda shape: pl.BlockSpec(shape, lambda i: (0, 0))
    in_specs = [row, one((1, d)), hbm, hbm]
    out_specs, out_shape = [], []
    if pre is not None:
        in_specs += [pl.BlockSpec((tm, pre[0].shape[1]), lambda i: (i, 0)), one(pre[1].shape)]
        out_specs.append(row)
        out_shape.append(jax.ShapeDtypeStruct((t, d), F32))
    if head is None:
        out_specs.append(row)
        out_shape.append(jax.ShapeDtypeStruct((t, d), F32))
    else:
        in_specs += [one((1, d)), row]
        out_specs += [one((1, 1)), row, one((1, d))]
        out_shape += [jax.ShapeDtypeStruct((1, 1), F32), jax.ShapeDtypeStruct((t, d), F32),
                      jax.ShapeDtypeStruct((1, d), F32)]
    out_specs += [row, pl.BlockSpec((tm, 2 * f), lambda i: (i, 0))]
    out_shape += [jax.ShapeDtypeStruct((t, d), BF16), jax.ShapeDtypeStruct((t, 2 * f), BF16)]
    return _call(
        body, comm=comm, name=name, grid=(t // tm,),
        in_specs=in_specs, out_specs=out_specs, out_shape=out_shape,
        scratch_shapes=[pltpu.VMEM((2 * f, d), BF16), pltpu.VMEM((f, d), BF16), pltpu.VMEM((tm, f), BF16),
                        pltpu.SemaphoreType.DMA((2,))],
    )(x, gamma, win_t, wout, *extra)


def _ffn_bwd(name, dy, x, gamma, gu, win_t, wout, comm=None):
    t, d = x.shape
    f = wout.shape[0]
    fc = f // HIDDEN_SPLIT
    tm = min(FFN_TILE, t)

    def body(dy_ref, x_ref, gam_ref, gu_ref, win_hbm, wo_hbm,
             dx_ref, dgam_ref, dyh_ref, h_ref, dgu_ref, win_v, wo_v, sems):
        _load_weights([(win_hbm, win_v), (wo_hbm, wo_v)], sems)

        @pl.when(pl.program_id(0) == 0)
        def _():
            dgam_ref[...] = jnp.zeros_like(dgam_ref)

        dyb = (0.5 * dy_ref[...]).astype(BF16)
        dyh_ref[...] = dyb
        for c in range(HIDDEN_SPLIT):
            lo = c * fc
            dh = lax.dot_general(dyb, wo_v[lo:lo + fc, :], NT, preferred_element_type=F32)
            g = gu_ref[:, lo:lo + fc].astype(F32)
            u = gu_ref[:, f + lo:f + lo + fc].astype(F32)
            sig = _sigmoid(g)
            silu = g * sig
            h_ref[:, lo:lo + fc] = (silu * u).astype(BF16)
            dgu_ref[:, lo:lo + fc] = (dh * u * (sig * (1.0 + g * (1.0 - sig)))).astype(BF16)
            dgu_ref[:, f + lo:f + lo + fc] = (dh * silu).astype(BF16)
        dxn = jnp.dot(dgu_ref[...], win_v[...], preferred_element_type=F32)
        xv = x_ref[...]
        r = lax.rsqrt(jnp.mean(xv * xv, axis=-1, keepdims=True) + EPS)
        yv = xv * r
        dgam_ref[...] += jnp.sum(dxn * yv, axis=0, keepdims=True)
        dyn = dxn * gam_ref[...]
        dx_ref[...] = dy_ref[...] + r * (dyn - yv * jnp.mean(dyn * yv, axis=-1, keepdims=True))

    hbm = pl.BlockSpec(memory_space=pl.ANY)
    row = lambda width: pl.BlockSpec((tm, width), lambda i: (i, 0))
    return _call(
        body, comm=comm, name=name, grid=(t // tm,),
        in_specs=[row(d), row(d), pl.BlockSpec((1, d), lambda i: (0, 0)), row(2 * f), hbm, hbm],
        out_specs=[row(d), pl.BlockSpec((1, d), lambda i: (0, 0)), row(d), row(f), row(2 * f)],
        out_shape=[
            jax.ShapeDtypeStruct((t, d), F32),
            jax.ShapeDtypeStruct((1, d), F32),
            jax.ShapeDtypeStruct((t, d), BF16),
            jax.ShapeDtypeStruct((t, f), BF16),
            jax.ShapeDtypeStruct((t, 2 * f), BF16),
        ],
        scratch_shapes=[pltpu.VMEM((2 * f, d), BF16), pltpu.VMEM((f, d), BF16), pltpu.SemaphoreType.DMA((2,))],
    )(dy, x, gamma, gu, win_t, wout)


def _ffn_bwd_hidden(name, dy, gu, wout):
    t, d = dy.shape
    f = wout.shape[0]
    fc = f // HIDDEN_SPLIT
    tm = min(FFN_TILE, t)

    def body(dy_ref, gu_ref, wo_hbm, dyh_ref, h_ref, dgu_ref, wo_v, sems):
        _load_weights([(wo_hbm, wo_v)], sems)
        dyb = (0.5 * dy_ref[...]).astype(BF16)
        dyh_ref[...] = dyb
        for c in range(HIDDEN_SPLIT):
            lo = c * fc
            dh = lax.dot_general(dyb, wo_v[lo:lo + fc, :], NT, preferred_element_type=F32)
            g = gu_ref[:, lo:lo + fc].astype(F32)
            u = gu_ref[:, f + lo:f + lo + fc].astype(F32)
            sig = _sigmoid(g)
            silu = g * sig
            h_ref[:, lo:lo + fc] = (silu * u).astype(BF16)
            dgu_ref[:, lo:lo + fc] = (dh * u * (sig * (1.0 + g * (1.0 - sig)))).astype(BF16)
            dgu_ref[:, f + lo:f + lo + fc] = (dh * silu).astype(BF16)

    row = lambda width: pl.BlockSpec((tm, width), lambda i: (i, 0))
    return _call(
        body, name=name, grid=(t // tm,),
        in_specs=[row(d), row(2 * f), pl.BlockSpec(memory_space=pl.ANY)],
        out_specs=[row(d), row(f), row(2 * f)],
        out_shape=[jax.ShapeDtypeStruct((t, d), BF16), jax.ShapeDtypeStruct((t, f), BF16),
                   jax.ShapeDtypeStruct((t, 2 * f), BF16)],
        scratch_shapes=[pltpu.VMEM((f, d), BF16), pltpu.SemaphoreType.DMA((1,))],
    )(dy, gu, wout)


def _matmul_tn(name, a, b, out_dtype=BF16, comm=None, a_split=None):
    a_b = a.ndim == 3
    b_b = b.ndim == 3
    nb = a_split if a_split else a.shape[0] if a_b else b.shape[0] if b_b else 1
    t, m = a.shape[-2:]
    if a_split:
        m = m // a_split
    n = b.shape[-1]
    tk = min(CONTRACT_TILE, t)
    nt = t // tk

    def body(a_ref, b_ref, o_ref, acc):
        s = pl.program_id(1)

        @pl.when(s == 0)
        def _():
            acc[...] = jnp.zeros_like(acc)

        acc[...] += lax.dot_general(a_ref[...], b_ref[...], TN, preferred_element_type=F32)

        @pl.when(s == nt - 1)
        def _():
            o_ref[...] = acc[...].astype(o_ref.dtype)

    a_spec = (pl.BlockSpec((None, tk, m), lambda j, s: (j, s, 0)) if a_b
              else pl.BlockSpec((tk, m), lambda j, s: (s, j)) if a_split
              else pl.BlockSpec((tk, m), lambda j, s: (s, 0)))
    b_spec = (pl.BlockSpec((None, tk, n), lambda j, s: (j, s, 0)) if b_b
              else pl.BlockSpec((tk, n), lambda j, s: (s, 0)))
    return _call(
        body, comm=comm, name=name, grid=(nb, nt),
        in_specs=[a_spec, b_spec],
        out_specs=pl.BlockSpec((None, m, n), lambda j, s: (j, 0, 0)),
        out_shape=jax.ShapeDtypeStruct((nb, m, n), out_dtype),
        scratch_shapes=[pltpu.VMEM((m, n), F32)],
    )(a, b)


def _matmul_nt(name, dy, w):
    t, n = dy.shape
    kdim = w.shape[0]
    tm = _tile(t)

    def body(dy_ref, w_ref, da_ref, dyb_ref):
        dyb = dy_ref[...].astype(BF16)
        dyb_ref[...] = dyb
        da_ref[...] = lax.dot_general(dyb, w_ref[...], NT, preferred_element_type=F32)

    return _call(
        body, name=name, grid=(t // tm,),
        in_specs=[pl.BlockSpec((tm, n), lambda i: (i, 0)),
                  pl.BlockSpec((kdim, n), lambda i: (0, 0))],
        out_specs=[pl.BlockSpec((tm, kdim), lambda i: (i, 0)),
                   pl.BlockSpec((tm, n), lambda i: (i, 0))],
        out_shape=[jax.ShapeDtypeStruct((t, kdim), F32), jax.ShapeDtypeStruct((t, n), BF16)],
    )(dy, w)


def _matmul_nt_rms_bwd(name, dz, w, dres, x, gamma, w_rows_are_k=False, tile=TOKEN_TILE, comm=None):
    t, kdim = dz.shape
    d = x.shape[1]
    tm = min(tile, t)

    def body(dz_ref, w_ref, dres_ref, x_ref, gam_ref, dx_ref, dgam_ref):
        i = pl.program_id(0)

        @pl.when(i == 0)
        def _():
            dgam_ref[...] = jnp.zeros_like(dgam_ref)

        if w_rows_are_k:
            dxn = jnp.dot(dz_ref[...], w_ref[...], preferred_element_type=F32)
        else:
            dxn = lax.dot_general(dz_ref[...], w_ref[...], NT, preferred_element_type=F32)
        xv = x_ref[...]
        r = lax.rsqrt(jnp.mean(xv * xv, axis=-1, keepdims=True) + EPS)
        yv = xv * r
        dgam_ref[...] += jnp.sum(dxn * yv, axis=0, keepdims=True)
        dyn = dxn * gam_ref[...]
        dx_ref[...] = dres_ref[...] + r * (dyn - yv * jnp.mean(dyn * yv, axis=-1, keepdims=True))

    return _call(
        body, comm=comm, name=name, grid=(t // tm,),
        in_specs=[pl.BlockSpec((tm, kdim), lambda i: (i, 0)),
                  pl.BlockSpec(w.shape, lambda i: (0, 0)),
                  pl.BlockSpec((tm, d), lambda i: (i, 0)),
                  pl.BlockSpec((tm, d), lambda i: (i, 0)),
                  pl.BlockSpec((1, d), lambda i: (0, 0))],
        out_specs=[pl.BlockSpec((tm, d), lambda i: (i, 0)),
                   pl.BlockSpec((1, d), lambda i: (0, 0))],
        out_shape=[jax.ShapeDtypeStruct((t, d), F32), jax.ShapeDtypeStruct((1, d), F32)],
    )(dz, w, dres, x, gamma)


def _pool_means(uext_ref, pos, tm, g, win):
    cols = slice(g * GROUP, (g + 1) * GROUP)
    acc = uext_ref[pl.ds(HALO, tm), cols]
    for j in range(1, win):
        acc = acc + uext_ref[pl.ds(HALO - j, tm), cols]
    cnt = jnp.minimum(pos + 1, win).astype(F32)
    return acc / cnt - uext_ref[pl.ds(HALO, tm), cols]


def _shifted_copies(src_ref, dst_ref, rows):
    for b in range(SUBLANES):
        dst_ref[b, pl.ds(0, rows), :] = src_ref[pl.ds(b, rows), :]


def _tap_sum(sh_ref, cw_ref, offsets, out_ref, n_rows, bias_ref=None):
    width = out_ref.shape[-1]

    def chunk(c, carry):
        r0 = pl.multiple_of(c * ROW_CHUNK, ROW_CHUNK)
        acc = (jnp.zeros((ROW_CHUNK, width), F32) if bias_ref is None
               else jnp.broadcast_to(bias_ref[...], (ROW_CHUNK, width)))
        for k, off in enumerate(offsets):
            a, b = divmod(off, SUBLANES)
            acc = acc + cw_ref[k:k + 1, :] * sh_ref[b, pl.ds(r0 + SUBLANES * a, ROW_CHUNK), :]
        out_ref[pl.ds(r0, ROW_CHUNK), :] = acc
        return carry

    lax.fori_loop(0, n_rows // ROW_CHUNK, chunk, 0)


def _pool_conv_fwd(name, x, gamma, w, pool_w, pool_b, pool_scale, conv_w, conv_b, ln_g, ln_b):
    t, d = x.shape
    hw = w.shape[1]
    pc = len(POOL_WINDOWS) * GROUP
    cc = (hw - pc) // 2
    tm = _tile(t)

    def body(x_ref, gam_ref, w_ref, pw_ref, pb_ref, ps_ref, cw_ref, cb_ref, lg_ref, lb_ref,
             xn_ref, h_ref, cat_ref, uext, gext, gsh, y_s, hp_s):
        i = pl.program_id(0)

        @pl.when(i == 0)
        def _():
            hp_s[...] = jnp.zeros_like(hp_s)

        xv = x_ref[...]
        r = lax.rsqrt(jnp.mean(xv * xv, axis=-1, keepdims=True) + EPS)
        xn = (xv * r * gam_ref[...]).astype(BF16)
        xn_ref[...] = xn
        h_ref[...] = jnp.dot(xn, w_ref[...], preferred_element_type=F32)
        hp = hp_s[...]
        uext[0:HALO, :] = hp[:, :pc]
        uext[HALO:, :] = h_ref[:, :pc]
        gext[0:HALO, :] = hp[:, pc:pc + cc] * _sigmoid(hp[:, pc + cc:])
        gext[pl.ds(HALO, tm), :] = h_ref[:, pc:pc + cc] * _sigmoid(h_ref[:, pc + cc:])
        gext[pl.ds(HALO + tm, SUBLANES), :] = jnp.zeros((SUBLANES, cc), F32)
        pos = i * tm + lax.broadcasted_iota(jnp.int32, (tm, 1), 0)
        for g, win in enumerate(POOL_WINDOWS):
            cols = slice(g * GROUP, (g + 1) * GROUP)
            pooled = _pool_means(uext, pos, tm, g, win)
            mixed = jnp.dot(pooled.astype(BF16), pw_ref[g].astype(BF16),
                            preferred_element_type=F32) + pb_ref[g:g + 1, :]
            cat_ref[:, cols] = (mixed * ps_ref[:, cols]).astype(BF16)
        _shifted_copies(gext, gsh, HALO + tm)
        _tap_sum(gsh, cw_ref, [HALO - (CONV_WIDTH - 1) + k for k in range(CONV_WIDTH)], y_s, tm, cb_ref)
        y = y_s[...]
        mu = jnp.mean(y, axis=-1, keepdims=True)
        dv = y - mu
        rstd = lax.rsqrt(jnp.mean(dv * dv, axis=-1, keepdims=True) + EPS)
        ln = dv * rstd * lg_ref[...] + lb_ref[...]
        cat_ref[:, pc:] = (ln * _sigmoid(ln)).astype(BF16)
        hp_s[...] = h_ref[pl.ds(tm - HALO, HALO), :]

    small = lambda a: pl.BlockSpec(a.shape, lambda i: (0,) * a.ndim)
    return _call(
        body, name=name, grid=(t // tm,),
        in_specs=[pl.BlockSpec((tm, d), lambda i: (i, 0)), small(gamma), small(w),
                  small(pool_w), small(pool_b), small(pool_scale), small(conv_w), small(conv_b),
                  small(ln_g), small(ln_b)],
        out_specs=[pl.BlockSpec((tm, d), lambda i: (i, 0)), pl.BlockSpec((tm, hw), lambda i: (i, 0)),
                   pl.BlockSpec((tm, pc + cc), lambda i: (i, 0))],
        out_shape=[jax.ShapeDtypeStruct((t, d), BF16), jax.ShapeDtypeStruct((t, hw), F32),
                   jax.ShapeDtypeStruct((t, pc + cc), BF16)],
        scratch_shapes=[pltpu.VMEM((HALO + tm, pc), F32), pltpu.VMEM((HALO + tm + SUBLANES, cc), F32),
                        pltpu.VMEM((SUBLANES, HALO + tm, cc), F32), pltpu.VMEM((tm, cc), F32),
                        pltpu.VMEM((HALO, hw), F32)],
    )(x, gamma, w, pool_w, pool_b, pool_scale, conv_w, conv_b, ln_g, ln_b)


def _pool_conv_bwd(name, h, dcat, pool_w, pool_b, pool_scale, conv_w, conv_b, ln_g, ln_b, comm=None):
    t, hw = h.shape
    pc = len(POOL_WINDOWS) * GROUP
    cc = (hw - pc) // 2
    ng = len(POOL_WINDOWS)
    tm = _tile(t)
    per = tm // HALO
    nt = t // tm
    r2 = tm + HALO
    taps = CONV_WIDTH - 1

    def body(h_ref, hp_ref, hn_ref, dc_ref, dcn_ref, pw_ref, pb_ref, ps_ref, cw_ref, cb_ref, lg_ref, lb_ref,
             dh_ref, dpw_ref, dpb_ref, dps_ref, dcw_ref, dcb_ref, dlg_ref, dlb_ref,
             uext, gext, dcext, dqext, dycext, shifted, y_s, dg_s, dcw_acc):
        i = pl.program_id(0)

        @pl.when(i == 0)
        def _():
            for ref in (dpw_ref, dpb_ref, dps_ref, dcw_ref, dcb_ref, dlg_ref, dlb_ref, dcw_acc):
                ref[...] = jnp.zeros_like(ref)

        keep_p = (i > 0).astype(F32)
        keep_n = (i < nt - 1).astype(F32)
        hp = hp_ref[...] * keep_p
        hn = hn_ref[...] * keep_n
        uext[0:HALO, :] = hp[:, :pc]
        uext[HALO:, :] = h_ref[:, :pc]
        gext[0:HALO, :] = hp[:, pc:pc + cc] * _sigmoid(hp[:, pc + cc:])
        gext[pl.ds(HALO, tm), :] = h_ref[:, pc:pc + cc] * _sigmoid(h_ref[:, pc + cc:])
        gext[pl.ds(HALO + tm, HALO), :] = hn[:, pc:pc + cc] * _sigmoid(hn[:, pc + cc:])
        gext[pl.ds(HALO + tm + HALO, SUBLANES), :] = jnp.zeros((SUBLANES, cc), F32)
        dcext[0:tm, :] = dc_ref[...]
        dcext[pl.ds(tm, HALO), :] = dcn_ref[...] * keep_n

        pos = i * tm + lax.broadcasted_iota(jnp.int32, (tm, 1), 0)
        pos2 = i * tm + lax.broadcasted_iota(jnp.int32, (r2, 1), 0)
        for g, win in enumerate(POOL_WINDOWS):
            cols = slice(g * GROUP, (g + 1) * GROUP)
            wg = pw_ref[g].astype(BF16)
            dya = dcext[:, cols]
            dmixed = dya * ps_ref[:, cols]
            dpooled = lax.dot_general(dmixed.astype(BF16), wg, NT, preferred_element_type=F32)
            cnt2 = jnp.minimum(pos2 + 1, win).astype(F32)
            dqext[:, cols] = dpooled / cnt2
            du = -dpooled[0:tm]
            for j in range(win):
                du = du + dqext[pl.ds(j, tm), cols]
            dh_ref[:, cols] = du.astype(BF16)
            pooled = _pool_means(uext, pos, tm, g, win)
            pooled_b = pooled.astype(BF16)
            mixed = jnp.dot(pooled_b, wg, preferred_element_type=F32) + pb_ref[g:g + 1, :]
            dps_ref[:, cols] += jnp.sum(dya[0:tm] * mixed, axis=0, keepdims=True)
            dpb_ref[g:g + 1, :] += jnp.sum(dmixed[0:tm], axis=0, keepdims=True)
            dpw_ref[g] += lax.dot_general(pooled_b, dmixed[0:tm].astype(BF16), TN, preferred_element_type=F32)

        _shifted_copies(gext, shifted, HALO + tm + HALO)
        _tap_sum(shifted, cw_ref, [HALO - taps + k for k in range(CONV_WIDTH)], y_s, r2, cb_ref)
        y = y_s[...]
        mu = jnp.mean(y, axis=-1, keepdims=True)
        dv = y - mu
        rstd = lax.rsqrt(jnp.mean(dv * dv, axis=-1, keepdims=True) + EPS)
        norm = dv * rstd
        ln = norm * lg_ref[...] + lb_ref[...]
        sig = _sigmoid(ln)
        dln = dcext[:, pc:] * (sig * (1.0 + ln * (1.0 - sig)))
        dnorm = dln * lg_ref[...]
        dyc = rstd * (dnorm - jnp.mean(dnorm, axis=-1, keepdims=True)
                      - norm * jnp.mean(dnorm * norm, axis=-1, keepdims=True))
        dycext[pl.ds(0, r2), :] = dyc
        dycext[pl.ds(r2, SUBLANES), :] = jnp.zeros((SUBLANES, cc), F32)
        dlg_ref[...] += jnp.sum((dln * norm)[0:tm], axis=0, keepdims=True)
        dlb_ref[...] += jnp.sum(dln[0:tm], axis=0, keepdims=True)
        dcb_ref[...] += jnp.sum(dyc[0:tm], axis=0, keepdims=True)

        def fold(c, carry):
            r0 = pl.multiple_of(c * ROW_CHUNK, ROW_CHUNK)
            dchunk = dycext[pl.ds(r0, ROW_CHUNK), :]
            for k in range(CONV_WIDTH):
                a8, b8 = divmod(HALO - taps + k, SUBLANES)
                prod = dchunk * shifted[b8, pl.ds(r0 + SUBLANES * a8, ROW_CHUNK), :]
                part = prod[0:SUBLANES]
                for q in range(1, ROW_CHUNK // SUBLANES):
                    part = part + prod[q * SUBLANES:(q + 1) * SUBLANES]
                dcw_acc[k] += part
            return carry

        lax.fori_loop(0, tm // ROW_CHUNK, fold, 0)

        @pl.when(i == nt - 1)
        def _():
            for k in range(CONV_WIDTH):
                dcw_ref[k:k + 1, :] = jnp.sum(dcw_acc[k], axis=0, keepdims=True)

        _shifted_copies(dycext, shifted, r2)
        _tap_sum(shifted, cw_ref, [taps - k for k in range(CONV_WIDTH)], dg_s, tm)
        dg = dg_s[...]
        a = h_ref[:, pc:pc + cc]
        sg = _sigmoid(h_ref[:, pc + cc:])
        dh_ref[:, pc:pc + cc] = (dg * sg).astype(BF16)
        dh_ref[:, pc + cc:] = (dg * a * sg * (1.0 - sg)).astype(BF16)

    small = lambda a: pl.BlockSpec(a.shape, lambda i: (0,) * a.ndim)
    smalls = (pool_w, pool_b, pool_scale, conv_w, conv_b, ln_g, ln_b)
    return _call(
        body, comm=comm, name=name, grid=(nt,),
        in_specs=[pl.BlockSpec((tm, hw), lambda i: (i, 0)),
                  pl.BlockSpec((HALO, hw), lambda i: (jnp.maximum(i * per - 1, 0), 0)),
                  pl.BlockSpec((HALO, hw), lambda i: (jnp.minimum((i + 1) * per, t // HALO - 1), 0)),
                  pl.BlockSpec((tm, pc + cc), lambda i: (i, 0)),
                  pl.BlockSpec((HALO, pc + cc), lambda i: (jnp.minimum((i + 1) * per, t // HALO - 1), 0)),
                  ] + [small(a) for a in smalls],
        out_specs=[pl.BlockSpec((tm, hw), lambda i: (i, 0))] + [small(a) for a in smalls],
        out_shape=[jax.ShapeDtypeStruct((t, hw), BF16)] + [jax.ShapeDtypeStruct(a.shape, F32) for a in smalls],
        scratch_shapes=[pltpu.VMEM((HALO + tm, pc), F32), pltpu.VMEM((HALO + tm + HALO + SUBLANES, cc), F32),
                        pltpu.VMEM((r2, pc + cc), F32), pltpu.VMEM((r2, pc), F32),
                        pltpu.VMEM((r2 + SUBLANES, cc), F32), pltpu.VMEM((SUBLANES, HALO + tm + HALO, cc), F32),
                        pltpu.VMEM((r2, cc), F32), pltpu.VMEM((tm, cc), F32),
                        pltpu.VMEM((CONV_WIDTH + 1, SUBLANES, cc), F32)],
    )(h, h, h, dcat, dcat, *smalls)


SQRT_HALF = 0.7071067811865476
INV_SQRT_2PI = 0.3989422804014327


def _sgu_core(zp_ref, lg_ref, lb_ref, ws_ref, bs_ref, vo_s, tm, sc, heads):
    zp = zp_ref[...].astype(F32)
    cdf =0.5 * (1.0 + lax.erf(zp * SQRT_HALF))
    z = zp * cdf
    u = z[:, :sc]
    v = z[:, sc:]
    mu = jnp.mean(v, axis=-1, keepdims=True)
    dv = v - mu
    rstd = lax.rsqrt(jnp.mean(dv * dv, axis=-1, keepdims=True) + EPS)
    norm = dv * rstd
    vb = (norm * lg_ref[...] + lb_ref[...]).astype(BF16)
    row = lax.broadcasted_iota(jnp.int32, (GROUP, GROUP), 0)
    col = lax.broadcasted_iota(jnp.int32, (GROUP, GROUP), 1)
    mask = (col <= row).astype(F32)
    wm = [ws_ref[hd] * mask for hd in range(heads)]
    for hd in range(heads):
        cols = slice(hd * GROUP, (hd + 1) * GROUP)
        wb = wm[hd].astype(BF16)
        for n in range(tm // GROUP):
            rows = slice(n * GROUP, (n + 1) * GROUP)
            vo_s[rows, cols] = jnp.dot(wb, vb[rows, cols], preferred_element_type=F32) + bs_ref[hd]
    return (zp, cdf), u, norm, rstd, vb, wm, mask


def _sgu_fwd(name, x, gamma, w, ln_g, ln_b, w_s, b_s):
    t, d = x.shape
    two_sc = w.shape[1]
    sc = two_sc // 2
    heads = sc // GROUP
    tm = _tile(t)

    def body(x_ref, gam_ref, w_ref, lg_ref, lb_ref, ws_ref, bs_ref, xn_ref, zp_ref, q_ref, vo_s):
        xv = x_ref[...]
        r = lax.rsqrt(jnp.mean(xv * xv, axis=-1, keepdims=True) + EPS)
        xn = (xv * r * gam_ref[...]).astype(BF16)
        xn_ref[...] = xn
        zp_ref[...] = jnp.dot(xn, w_ref[...], preferred_element_type=F32).astype(BF16)
        _, u, _, _, _, _, _ = _sgu_core(zp_ref, lg_ref, lb_ref, ws_ref, bs_ref, vo_s, tm, sc, heads)
        q_ref[...] = (u * vo_s[...]).astype(BF16)

    small = lambda a: pl.BlockSpec(a.shape, lambda i: (0,) * a.ndim)
    return _call(
        body, name=name, grid=(t // tm,),
        in_specs=[pl.BlockSpec((tm, d), lambda i: (i, 0)), small(gamma), small(w),
                  small(ln_g), small(ln_b), small(w_s), small(b_s)],
        out_specs=[pl.BlockSpec((tm, d), lambda i: (i, 0)), pl.BlockSpec((tm, two_sc), lambda i: (i, 0)),
                   pl.BlockSpec((tm, sc), lambda i: (i, 0))],
        out_shape=[jax.ShapeDtypeStruct((t, d), BF16), jax.ShapeDtypeStruct((t, two_sc), BF16),
                   jax.ShapeDtypeStruct((t, sc), BF16)],
        scratch_shapes=[pltpu.VMEM((tm, sc), F32)],
    )(x, gamma, w, ln_g, ln_b, w_s, b_s)


def _sgu_bwd(name, zp, dy, w_out, ln_g, ln_b, w_s, b_s):
    t, two_sc = zp.shape
    d = dy.shape[1]
    sc = two_sc // 2
    heads = sc // GROUP
    tm = _tile(t)
    nt = t // tm

    def body(zp_ref, dy_ref, wo_ref, lg_ref, lb_ref, ws_ref, bs_ref,
             dzp_ref, dyb_ref, dlg_ref, dlb_ref, dws_ref, dbs_ref, vo_s, dvl_s, dws_acc):
        i = pl.program_id(0)

        @pl.when(i == 0)
        def _():
            dlg_ref[...] = jnp.zeros_like(dlg_ref)
            dlb_ref[...] = jnp.zeros_like(dlb_ref)
            dbs_ref[...] = jnp.zeros_like(dbs_ref)
            dws_acc[...] = jnp.zeros_like(dws_acc)

        dyb = dy_ref[...].astype(BF16)
        dyb_ref[...] = dyb
        dq = lax.dot_general(dyb, wo_ref[...], NT, preferred_element_type=F32)
        (zp, cdf), u, norm, rstd, vb, wm, mask = _sgu_core(zp_ref, lg_ref, lb_ref, ws_ref, bs_ref, vo_s, tm, sc, heads)
        du = dq * vo_s[...]
        dvo = dq * u
        dvob = dvo.astype(BF16)
        for hd in range(heads):
            cols = slice(hd * GROUP, (hd + 1) * GROUP)
            wtb = jnp.transpose(wm[hd]).astype(BF16)
            for n in range(tm // GROUP):
                rows = slice(n * GROUP, (n + 1) * GROUP)
                blk = dvob[rows, cols]
                dws_acc[hd] += lax.dot_general(blk, vb[rows, cols], NT, preferred_element_type=F32)
                dvl_s[rows, cols] = jnp.dot(wtb, blk, preferred_element_type=F32)
                dbs_ref[hd] += jnp.sum(dvo[rows, cols], axis=-1, keepdims=True)
        dvl = dvl_s[...]
        dlg_ref[...] += jnp.sum(dvl * norm, axis=0, keepdims=True)
        dlb_ref[...] += jnp.sum(dvl, axis=0, keepdims=True)
        dnorm = dvl * lg_ref[...]
        dv = rstd * (dnorm - jnp.mean(dnorm, axis=-1, keepdims=True)
                     - norm * jnp.mean(dnorm * norm, axis=-1, keepdims=True))
        dgelu = cdf + zp * (INV_SQRT_2PI * jnp.exp(-0.5 * zp * zp))
        dzp_ref[:, :sc] = (du * dgelu[:, :sc]).astype(BF16)
        dzp_ref[:, sc:] = (dv * dgelu[:, sc:]).astype(BF16)

        @pl.when(i == nt - 1)
        def _():
            for hd in range(heads):
                dws_ref[hd] = dws_acc[hd] * mask

    small = lambda a: pl.BlockSpec(a.shape, lambda i: (0,) * a.ndim)
    smalls = (ln_g, ln_b, w_s, b_s)
    return _call(
        body, name=name, grid=(nt,),
        in_specs=[pl.BlockSpec((tm, two_sc), lambda i: (i, 0)), pl.BlockSpec((tm, d), lambda i: (i, 0)),
                  small(w_out)] + [small(a) for a in smalls],
        out_specs=[pl.BlockSpec((tm, two_sc), lambda i: (i, 0)), pl.BlockSpec((tm, d), lambda i: (i, 0))]
                  + [small(a) for a in smalls],
        out_shape=[jax.ShapeDtypeStruct((t, two_sc), BF16), jax.ShapeDtypeStruct((t, d), BF16)]
                  + [jax.ShapeDtypeStruct(a.shape, F32) for a in smalls],
        scratch_shapes=[pltpu.VMEM((tm, sc), F32), pltpu.VMEM((tm, sc), F32), pltpu.VMEM(w_s.shape, F32)],
    )(zp, dy, w_out, ln_g, ln_b, w_s, b_s)


def _adamw(name, parts, w, m, v, rows):
    l_n, r_n, c_n = w.shape
    s_n = parts[0].shape[0]
    tr = min(rows, r_n)
    nr = r_n // tr
    c1 = 1.0 - ADAM_B1 ** ADAM_STEP
    c2 = 1.0 - ADAM_B2 ** ADAM_STEP

    def body(*refs):
        p_refs = refs[:l_n]
        w_ref, m_ref, v_ref, g_ref, d_ref, mo_ref, vo_ref = refs[l_n:]
        layer = pl.program_id(0)

        def update(p_ref):
            g = p_ref[0].astype(F32)
            for s in range(1, s_n):
                g = g + p_ref[s].astype(F32)
            mn = ADAM_B1 * m_ref[...] + (1.0 - ADAM_B1) * g
            vn = ADAM_B2 * v_ref[...] + (1.0 - ADAM_B2) * (g * g)
            m_hat = mn / c1
            v_hat = vn / c2
            g_ref[...] = g
            d_ref[...] = -ADAM_LR * (m_hat / (jnp.sqrt(v_hat) + ADAM_EPS) + ADAM_WD * w_ref[...])
            mo_ref[...] = mn
            vo_ref[...] = vn

        for j in range(l_n):
            pl.when(layer == j)(functools.partial(update, p_refs[j]))

    def part_spec(j):
        return pl.BlockSpec((s_n, tr, c_n), lambda l, i: (0, jnp.where(l == j, i, jnp.where(l < j, 0, nr - 1)), 0))

    blk = pl.BlockSpec((None, tr, c_n), lambda l, i: (l, i, 0))
    return _call(
        body, name=name, grid=(l_n, nr),
        in_specs=[part_spec(j) for j in range(l_n)] + [blk, blk, blk],
        out_specs=[blk] * 4,
        out_shape=[jax.ShapeDtypeStruct((l_n, r_n, c_n), F32)] * 4,
    )(*parts, w, m, v)


def _local_step(x, target, big, small, sched=None):
    t, d = x.shape
    n_layers = small["ffn1_norm"].shape[0]
    gb = {}
    gs = {}

    def row(a, l):
        return a[l:l + 1]

    def run(fn, name, *operands, **kw):
        comm = sched.plan(name, gb, gs) if sched is not None else None
        if comm is None:
            return fn(name, *operands, **kw)
        res, got = fn(name, *operands, comm=comm, **kw)
        sched.deliver(comm, got)
        return res

    saved = []
    xs = x
    for l in range(n_layers):
        rec = {"x_ffn1": xs}
        xs, rec["xn_ffn1"], rec["gu_ffn1"] = run(
            _ffn_fwd, f"ffn1_fwd_l{l}", xs, row(small["ffn1_norm"], l), big["ffn1_w_in", l], big["ffn1_w_out", l])
        rec["x_mix"] = xs
        if l % 2 == 0:
            rec["xn_mix"], rec["h"], rec["cat"] = _pool_conv_fwd(
                f"pool_conv_fwd_l{l}", xs, row(small["mix_norm"], l), big["ab_w_in"], small["pool_w"],
                small["pool_b"], small["pool_scale"], small["conv_w"], small["conv_b"],
                small["conv_ln_g"], small["conv_ln_b"])
            pre = (rec["cat"], big["ab_w_out"])
        else:
            rec["xn_mix"], rec["zp"], rec["q"] = _sgu_fwd(
                f"sgu_fwd_l{l}", xs, row(small["mix_norm"], l), big["sgu_w_in"], small["sgu_ln_g"],
                small["sgu_ln_b"], small["sgu_w"], small["sgu_b"])
            pre = (rec["q"], big["sgu_w_out"])
        last = l == n_layers - 1
        res = run(_ffn_fwd, f"ffn2_fwd_l{l}", xs, row(small["ffn2_norm"], l), big["ffn2_w_in", l],
                  big["ffn2_w_out", l], pre=pre, head=(small["final_norm"], target) if last else None)
        if last:
            rec["x_ffn2"], loss, dx, gs["final_norm"], rec["xn_ffn2"], rec["gu_ffn2"] = res
        else:
            rec["x_ffn2"], xs, rec["xn_ffn2"], rec["gu_ffn2"] = res
        saved.append(rec)

    norm_rows = {"ffn1_norm": [None] * n_layers, "mix_norm": [None] * n_layers, "ffn2_norm": [None] * n_layers}

    def ffn_backward(tag, l, dy, rec):
        gamma = row(small[f"{tag}_norm"], l)
        weights_first = (tag, l) == ("ffn1", 0)
        if weights_first:
            dyh, hh, dgu = _ffn_bwd_hidden(f"{tag}_bwd_hidden_l{l}", dy, rec[f"gu_{tag}"], big[f"{tag}_w_out", l])
        else:
            dx, dgam, dyh, hh, dgu = run(_ffn_bwd, f"{tag}_bwd_l{l}", dy, rec[f"x_{tag}"], gamma,
                                         rec[f"gu_{tag}"], big[f"{tag}_w_in", l], big[f"{tag}_w_out", l])
        gb[f"{tag}_w_out", l] = run(_matmul_tn, f"{tag}_dwout_l{l}", hh, dyh,
                                    a_split=HIDDEN_SPLIT).reshape(N_DEV, -1, d)
        gb[f"{tag}_w_in", l] = run(_matmul_tn, f"{tag}_dwin_l{l}", dgu, rec[f"xn_{tag}"],
                                   a_split=2 * HIDDEN_SPLIT).reshape(N_DEV, -1, d)
        if weights_first:
            dx, dgam = run(_matmul_nt_rms_bwd, f"{tag}_dx_l{l}", dgu, big[f"{tag}_w_in", l], dy, rec[f"x_{tag}"],
                           gamma, w_rows_are_k=True, tile=FFN_TILE)
        norm_rows[f"{tag}_norm"][l] = dgam
        return dx

    for l in reversed(range(n_layers)):
        rec = saved[l]
        dx = ffn_backward("ffn2", l, dx, rec)
        if l % 2 == 0:
            dcat, dxb = _matmul_nt(f"ab_out_bwd_l{l}", dx, big["ab_w_out"])
            gb["ab_w_out", 0] = _matmul_tn(f"ab_dwout_l{l}", rec["cat"], dxb)
            dh, gs["pool_w"], gs["pool_b"], gs["pool_scale"], gs["conv_w"], gs["conv_b"], gs["conv_ln_g"], \
                gs["conv_ln_b"] = run(
                    _pool_conv_bwd, f"pool_conv_bwd_l{l}", rec["h"], dcat, small["pool_w"], small["pool_b"],
                    small["pool_scale"], small["conv_w"], small["conv_b"], small["conv_ln_g"], small["conv_ln_b"])
            gb["ab_w_in", 0] = _matmul_tn(f"ab_dwin_l{l}", rec["xn_mix"], dh)
            dx, dgam = _matmul_nt_rms_bwd(f"ab_in_bwd_l{l}", dh, big["ab_w_in"], dx, rec["x_mix"],
                                          row(small["mix_norm"], l))
        else:
            dzp, dxb, gs["sgu_ln_g"], gs["sgu_ln_b"], gs["sgu_w"], gs["sgu_b"] = _sgu_bwd(
                f"sgu_bwd_l{l}", rec["zp"], dx, big["sgu_w_out"], small["sgu_ln_g"], small["sgu_ln_b"],
                small["sgu_w"], small["sgu_b"])
            gb["sgu_w_out", 0] = _matmul_tn(f"sgu_dwout_l{l}", rec["q"], dxb)
            gb["sgu_w_in", 0] = _matmul_tn(f"sgu_dwin_l{l}", rec["xn_mix"], dzp)
            dx, dgam = _matmul_nt_rms_bwd(f"sgu_in_bwd_l{l}", dzp, big["sgu_w_in"], dx, rec["x_mix"],
                                          row(small["mix_norm"], l))
        norm_rows["mix_norm"][l] = dgam
        dx = ffn_backward("ffn1", l, dx, rec)

    for k, rows in norm_rows.items():
        gs[k] = jnp.concatenate(rows, axis=0)
    return loss, dx, gb, gs


SHARDED_SMALL = ("conv_w", "sgu_ln_g", "sgu_ln_b")
WEIGHTS = ("ffn1_norm", "ffn1_w_in", "ffn1_w_out", "mix_norm", "ffn2_norm", "ffn2_w_in", "ffn2_w_out", "ab_w_in",
           "pool_w", "pool_b", "pool_scale", "conv_w", "conv_b", "conv_ln_g", "conv_ln_b", "ab_w_out", "sgu_w_in",
           "sgu_ln_g", "sgu_ln_b", "sgu_w", "sgu_b", "sgu_w_out", "final_norm")
LANES = 128


def _interleave_cols(g):
    n, k, c = g.shape
    return jnp.transpose(g, (1, 0, 2)).reshape(k, n * c)


def _split_cols(a):
    k, nc = a.shape
    return jnp.transpose(a.reshape(k, N_DEV, nc // N_DEV), (1, 0, 2))


def _as3(a):
    if a.ndim == 1:
        return a.reshape(1, 1, -1)
    if a.ndim == 2:
        return a.reshape(a.shape[0], 1, a.shape[1])
    return a.reshape(a.shape[0], -1, a.shape[-1])


def _pack_rows(a):
    flat = a.reshape(-1)
    pad = (-flat.shape[0]) % (8 * LANES)
    if pad:
        flat = jnp.concatenate([flat, jnp.zeros((pad,), flat.dtype)])
    return flat.reshape(-1, LANES)


FIRST_GATHER = (("ffn1_w_in", 0), ("ffn1_w_out", 0), ("conv_w", 0), ("sgu_ln_g", 0), ("sgu_ln_b", 0))
GATHER_PLAN = {
    "ffn1_fwd_l0": (("ab_w_in", 0), ("ab_w_out", 0), ("ffn2_w_in", 0), ("ffn2_w_out", 0)),
    "ffn2_fwd_l0": (("ffn1_w_in", 1), ("ffn1_w_out", 1), ("sgu_w_in", 0), ("sgu_w_out", 0)),
    "ffn1_fwd_l1": (("ffn2_w_in", 1), ("ffn2_w_out", 1)),
}
SCATTER_PLAN = {
    "ffn2_dwin_l1": (("ffn2_w_out", 1),),
    "ffn1_bwd_l1": (("ffn2_w_in", 1), ("sgu_w_out", 0)),
    "ffn1_dwout_l1": (("sgu_w_in", 0), ("sgu_ln_g", 0), ("sgu_ln_b", 0)),
    "ffn1_dwin_l1": (("ffn1_w_out", 1),),
    "ffn2_bwd_l0": (("ffn1_w_in", 1), ("pack", 0)),
    "ffn2_dwin_l0": (("ffn2_w_out", 0),),
    "pool_conv_bwd_l0": (("ffn2_w_in", 0), ("ab_w_out", 0)),
    "ffn1_dwout_l0": (("ab_w_in", 0), ("conv_w", 0)),
    "ffn1_dwin_l0": (("ffn1_w_out", 0), ("pack", 1)),
    "ffn1_dx_l0": (("ffn1_w_in", 0),),
}
PACK_GROUPS = (("sgu_w", "sgu_b"),
               ("pool_w", "pool_b", "pool_scale", "conv_b", "conv_ln_g", "conv_ln_b"),
               ("ffn1_norm", "mix_norm", "ffn2_norm", "final_norm"))
REPLICATED = tuple(n for group in PACK_GROUPS for n in group)


class _Schedule:
    def __init__(self, shards, big, small):
        self.shards, self.big, self.small = shards, big, small
        self.recv = {}
        self.packs = {}
        self.pending = {}

    def gather_comm(self, keys):
        comm = _Comm()
        for key in keys:
            comm.gather(*self.shards[key])
        self.pending[id(comm)] = ("gather", keys)
        return comm

    def scatter_comm(self, keys, gb, gs):
        comm = _Comm()
        for name, l in keys:
            if name == "pack":
                comm.gather(jnp.concatenate([_pack_rows(gs[n]) for n in PACK_GROUPS[l]], axis=0))
                continue
            if name in ("ab_w_in", "sgu_w_in"):
                send = _split_cols(gb[name, l][0])
            elif name in ("ab_w_out", "sgu_w_out"):
                send = gb[name, l][0]
                send = send.reshape(N_DEV, -1, send.shape[-1])
            elif name == "conv_w":
                send = _split_cols(gs[name][:CONV_WIDTH])
            elif name in ("sgu_ln_g", "sgu_ln_b"):
                send = gs[name].reshape(N_DEV, 1, -1)
            else:
                send = gb[name, l]
            comm.scatter(send)
        self.pending[id(comm)] = ("scatter", keys)
        return comm

    def plan(self, name, gb, gs):
        if name in GATHER_PLAN:
            return self.gather_comm(GATHER_PLAN[name])
        if name in SCATTER_PLAN:
            return self.scatter_comm(SCATTER_PLAN[name], gb, gs)
        return None

    def deliver(self, comm, got):
        kind, keys = self.pending.pop(id(comm))
        for (name, l), arr in zip(keys, got):
            if name == "pack":
                self.packs[l] = arr
            elif kind == "scatter":
                self.recv[name, l] = arr
            elif name in ("ffn1_w_in", "ffn2_w_in"):
                self.big[name, l] = arr.reshape(-1, arr.shape[-1])
            elif name in ("ffn1_w_out", "ffn2_w_out"):
                self.big[name, l] = arr.reshape(-1, arr.shape[-1])
            elif name in ("ab_w_in", "sgu_w_in"):
                self.big[name] = _interleave_cols(arr)
            elif name in ("ab_w_out", "sgu_w_out"):
                self.big[name] = arr.reshape(-1, arr.shape[-1])
            elif name == "conv_w":
                self.small[name] = jnp.pad(_interleave_cols(arr), ((0, 1), (0, 0)))
            else:
                self.small[name] = arr.reshape(1, -1)


def kernel(x, ffn1_norm, ffn1_w_in, ffn1_w_out, mix_norm, ffn2_norm, ffn2_w_in, ffn2_w_out, ab_w_in, pool_w, pool_b, pool_scale, conv_w, conv_b, conv_ln_g, conv_ln_b, ab_w_out, sgu_w_in, sgu_ln_g, sgu_ln_b, sgu_w, sgu_b, sgu_w_out, final_norm, loss_target, m_ffn1_norm, m_ffn1_w_in, m_ffn1_w_out, m_mix_norm, m_ffn2_norm, m_ffn2_w_in, m_ffn2_w_out, m_ab_w_in, m_pool_w, m_pool_b, m_pool_scale, m_conv_w, m_conv_b, m_conv_ln_g, m_conv_ln_b, m_ab_w_out, m_sgu_w_in, m_sgu_ln_g, m_sgu_ln_b, m_sgu_w, m_sgu_b, m_sgu_w_out, m_final_norm, v_ffn1_norm, v_ffn1_w_in, v_ffn1_w_out, v_mix_norm, v_ffn2_norm, v_ffn2_w_in, v_ffn2_w_out, v_ab_w_in, v_pool_w, v_pool_b, v_pool_scale, v_conv_w, v_conv_b, v_conv_ln_g, v_conv_ln_b, v_ab_w_out, v_sgu_w_in, v_sgu_ln_g, v_sgu_ln_b, v_sgu_w, v_sgu_b, v_sgu_w_out, v_final_norm):
    args = dict(locals())
    w = {n: args[n] for n in WEIGHTS}
    m = {n: args["m_" + n] for n in WEIGHTS}
    v = {n: args["v_" + n] for n in WEIGHTS}
    n_layers = ffn1_norm.shape[0]

    shards = {}
    for n in ("ffn1_w_in", "ffn2_w_in"):
        wt = jnp.swapaxes(w[n], 1, 2).astype(BF16)
        for l in range(n_layers):
            shards[n, l] = (wt, l)
    for n in ("ffn1_w_out", "ffn2_w_out"):
        wb = w[n].astype(BF16)
        for l in range(n_layers):
            shards[n, l] = (wb, l)
    for n in ("ab_w_in", "ab_w_out", "sgu_w_in", "sgu_w_out"):
        shards[n, 0] = (w[n][0].astype(BF16), None)
    shards["conv_w", 0] = (conv_w[0], None)
    shards["sgu_ln_g", 0] = (sgu_ln_g, None)
    shards["sgu_ln_b", 0] = (sgu_ln_b, None)

    big = {}
    small = {
        "ffn1_norm": ffn1_norm, "mix_norm": mix_norm, "ffn2_norm": ffn2_norm, "final_norm": final_norm.reshape(1, -1),
        "pool_w": pool_w[0], "pool_b": pool_b[0], "pool_scale": pool_scale,
        "conv_b": conv_b, "conv_ln_g": conv_ln_g, "conv_ln_b": conv_ln_b,
        "sgu_w": sgu_w[0], "sgu_b": sgu_b[0][:, :, None],
    }
    sched = _Schedule(shards, big, small)
    first = sched.gather_comm(FIRST_GATHER)
    sched.deliver(first, _exchange("gather_first", first))

    loss, grad_x, gb, gs = _local_step(x[0], loss_target[0], big, small, sched)

    last_pack = jnp.concatenate([_pack_rows(gs[n]) for n in PACK_GROUPS[-1]] + [_pack_rows(loss)], axis=0)
    last = _Comm()
    last.gather(last_pack)
    packed_all = jnp.concatenate([sched.packs[0], sched.packs[1], _exchange("reduce_last", last)[0]], axis=1)
    offsets = [0]
    for n in REPLICATED:
        offsets.append(offsets[-1] + SUBLANES * (-(-w[n].size // (SUBLANES * LANES))))
    offsets.append(offsets[-1] + SUBLANES)
    n_rows = offsets[-1]
    recv = sched.recv

    out = {}
    for n in ("ffn1_w_in", "ffn2_w_in"):
        res = _adamw(f"adamw_{n}", [recv[n, l] for l in range(n_layers)], jnp.swapaxes(w[n], 1, 2),
                     jnp.swapaxes(m[n], 1, 2), jnp.swapaxes(v[n], 1, 2), 176)
        out[n] = [jnp.swapaxes(r, 1, 2) for r in res]
    for n in ("ffn1_w_out", "ffn2_w_out"):
        out[n] = _adamw(f"adamw_{n}", [recv[n, l] for l in range(n_layers)], w[n], m[n], v[n], 176)
    for n in ("ab_w_in", "ab_w_out", "sgu_w_in", "sgu_w_out") + SHARDED_SMALL:
        w3 = _as3(w[n])
        parts = recv[n, 0].reshape((N_DEV,) + w3.shape[1:])
        res = _adamw(f"adamw_{n}", [parts], w3, _as3(m[n]), _as3(v[n]), 512)
        out[n] = [r.reshape(w[n].shape) for r in res]

    def pack_rep(src):
        tail = [jnp.zeros((offsets[-1] - offsets[-2], LANES), F32)]
        return jnp.concatenate([_pack_rows(src[n]) for n in REPLICATED] + tail, axis=0)[None]

    res = _adamw("adamw_replicated", [packed_all], pack_rep(w), pack_rep(m), pack_rep(v), n_rows)
    for i, n in enumerate(REPLICATED):
        size = w[n].size
        out[n] = [r[0, offsets[i]:offsets[i + 1]].reshape(-1)[:size].reshape(w[n].shape) for r in res]
    loss_sum = res[0][0, offsets[-2], 0]

    return (loss_sum, grad_x[None],
            *[out[n][0] for n in WEIGHTS], *[out[n][1] for n in WEIGHTS],
            *[out[n][2] for n in WEIGHTS], *[out[n][3] for n in WEIGHTS])
```
